```python
import math
import jax, jax.numpy as jnp
from jax import lax
import numpy as np

D_MODEL = 1024
BATCH = 2
SEQ = 8192
DEPTH = 2

MEM_LEN = 256
D_MIX = D_MODEL
SSD_WIDTH = D_MIX // 2
SSD_HEAD_DIM = 64
SSD_HEADS = SSD_WIDTH // SSD_HEAD_DIM
SSD_GROUPS = 2
SSD_HEADS_PER_GROUP = SSD_HEADS // SSD_GROUPS
SSD_STATE = 128
SSD_CONV = 4
SSD_CHUNK = 128
SSD_XBC = SSD_WIDTH + 2 * SSD_GROUPS * SSD_STATE

S5_WIDTH = D_MIX // 4
S5_GROUP_CH = 16
S5_GROUPS = S5_WIDTH // S5_GROUP_CH
S5_STATE = 64

RG_WIDTH = D_MIX - SSD_WIDTH - S5_WIDTH
RG_BLOCKS = 4
RG_BLOCK_DIM = RG_WIDTH // RG_BLOCKS
RG_CONV = 4
RG_C = 8.0

XA_HEADS = 4
XA_HEAD_DIM = D_MODEL // XA_HEADS
D_FF = 4 * D_MODEL

ALPHA = (2.0 * DEPTH) ** 0.25
BETA = (8.0 * DEPTH) ** -0.25
LN_EPS = 1e-5

IN_COLS = (SSD_WIDTH, SSD_XBC, SSD_HEADS, S5_WIDTH, RG_WIDTH, RG_WIDTH)
D_IN = SSD_WIDTH + SSD_XBC + SSD_HEADS + S5_WIDTH + RG_WIDTH + RG_WIDTH

kernel_name = "hybrid_ssd_s5_rglru_deepnorm"


def layer_norm(x, g, b):
    x32 = x.astype(jnp.float32)
    mu = jnp.mean(x32, axis=-1, keepdims=True)
    var = jnp.mean(jnp.square(x32 - mu), axis=-1, keepdims=True)
    return (x32 - mu) * lax.rsqrt(var + LN_EPS) * g.astype(jnp.float32) + b.astype(jnp.float32)


def causal_dwconv(x, w, b):
    k, c = w.shape
    y = lax.conv_general_dilated(x, w[:, None, :].astype(x.dtype), window_strides=(1,),
                                 padding=[(k - 1, 0)], dimension_numbers=('NWC', 'WIO', 'NWC'),
                                 feature_group_count=c)
    return y + b.astype(x.dtype)


def _lin_combine(left, right):
    a1, b1 = left
    a2, b2 = right
    return a1 * a2, a2 * b1 + b2


def linear_scan(a, b):
    return lax.associative_scan(_lin_combine, (a, b), axis=1)[1]


def ssd_mixer(z, xbc, dt_raw, conv_w, conv_b, dt_bias, a_log, d_skip, norm_w):
    bsz, seq, _ = z.shape
    nc = seq // SSD_CHUNK
    xbc = jax.nn.silu(causal_dwconv(xbc, conv_w.astype(jnp.float32), conv_b.astype(jnp.float32)))
    xs = xbc[..., :SSD_WIDTH]
    bm = xbc[..., SSD_WIDTH:SSD_WIDTH + SSD_GROUPS * SSD_STATE]
    cm = xbc[..., SSD_WIDTH + SSD_GROUPS * SSD_STATE:]
    dt = jax.nn.softplus(dt_raw + dt_bias.astype(jnp.float32))
    a = -jnp.exp(a_log.astype(jnp.float32))
    g, hg, p, n, q = SSD_GROUPS, SSD_HEADS_PER_GROUP, SSD_HEAD_DIM, SSD_STATE, SSD_CHUNK
    xh = xs.reshape(bsz, seq, SSD_HEADS, p)
    xdt = (xh * dt[..., None]).reshape(bsz, nc, q, g, hg, p)
    adt = (dt * a).reshape(bsz, nc, q, g, hg).transpose(0, 1, 3, 4, 2)
    bc = bm.reshape(bsz, nc, q, g, n)
    cc = cm.reshape(bsz, nc, q, g, n)
    a_cs = jnp.cumsum(adt, axis=-1)
    mask = jnp.tril(jnp.ones((q, q), dtype=bool))
    seg = a_cs[..., :, None] - a_cs[..., None, :]
    lmat = jnp.exp(jnp.where(mask, seg, -jnp.inf))
    cb = jnp.einsum('bclgn,bcsgn->bcgls', cc, bc)
    y_diag = jnp.einsum('bcgls,bcghls,bcsghp->bclghp', cb, lmat, xdt)
    decay_states = jnp.exp(a_cs[..., -1:] - a_cs)
    states = jnp.einsum('bclgn,bcghl,bclghp->bcghpn', bc, decay_states, xdt)
    chunk_decay = jnp.exp(a_cs[..., -1])

    def step(s, inp):
        st, dec = inp
        return s * dec[..., None, None] + st, s

    init = jnp.zeros((bsz, g, hg, p, n), jnp.float32)
    _, prev = lax.scan(step, init, (jnp.moveaxis(states, 1, 0), jnp.moveaxis(chunk_decay, 1, 0)))
    prev = jnp.moveaxis(prev, 0, 1)
    y_off = jnp.einsum('bclgn,bcghpn,bcghl->bclghp', cc, prev, jnp.exp(a_cs))
    y = (y_diag + y_off).reshape(bsz, seq, SSD_HEADS, p) + xh * d_skip.astype(jnp.float32)[:, None]
    y = y.reshape(bsz, seq, SSD_WIDTH) * jax.nn.silu(z)
    y = y * lax.rsqrt(jnp.mean(jnp.square(y), axis=-1, keepdims=True) + LN_EPS)
    return y * norm_w.astype(jnp.float32)


def s5_mixer(u, lam_re, lam_im, log_step, b_re, b_im, c_re, c_im, d_skip, glu_w, glu_b):
    bsz, seq, _ = u.shape
    f32 = jnp.float32
    ug = u.reshape(bsz, seq, S5_GROUPS, S5_GROUP_CH).astype(jnp.complex64)
    lam = lax.complex(lam_re.astype(f32), lam_im.astype(f32))
    step = jnp.exp(log_step.astype(f32))[:, None]
    lam_bar = jnp.exp(lam * step)
    bmat = lax.complex(b_re.astype(f32), b_im.astype(f32))
    b_bar = ((lam_bar - 1.0) / lam)[..., None] * bmat
    bu = jnp.einsum('gpc,blgc->blgp', b_bar, ug)
    h = linear_scan(jnp.broadcast_to(lam_bar, bu.shape), bu)
    cmat = lax.complex(c_re.astype(f32), c_im.astype(f32))
    y = jnp.real(jnp.einsum('gcp,blgp->blgc', cmat, h)).reshape(bsz, seq, S5_WIDTH)
    y = jax.nn.gelu(y + d_skip.astype(f32) * u)
    return y * jax.nn.sigmoid(jnp.einsum('blc,ce->ble', y, glu_w.astype(f32)) + glu_b.astype(f32))


def rglru_mixer(xr, gate_in, conv_w, conv_b, wa, ba, wx, bx, lam):
    bsz, seq, _ = xr.shape
    f32 = jnp.float32
    xc = causal_dwconv(xr, conv_w.astype(f32), conv_b.astype(f32))
    xh = xc.reshape(bsz, seq, RG_BLOCKS, RG_BLOCK_DIM)
    r = jax.nn.sigmoid(jnp.einsum('blhi,hij->blhj', xh, wa.astype(f32)) + ba.astype(f32)).reshape(bsz, seq, RG_WIDTH)
    i = jax.nn.sigmoid(jnp.einsum('blhi,hij->blhj', xh, wx.astype(f32)) + bx.astype(f32)).reshape(bsz, seq, RG_WIDTH)
    log_a = -RG_C * r * jax.nn.softplus(-lam.astype(f32))
    a = jnp.exp(log_a)
    mult = jnp.sqrt(-jnp.expm1(2.0 * log_a))
    h = linear_scan(a, mult * (i * xc))
    return h * jax.nn.gelu(gate_in)


def cross_attention(x, mem, wq, wk, wv, wo):
    bsz, seq, _ = x.shape
    f32 = jnp.float32
    q = jnp.einsum('bld,de->ble', x, wq.astype(f32)).reshape(bsz, seq, XA_HEADS, XA_HEAD_DIM)
    k = jnp.einsum('bmd,de->bme', mem, wk.astype(f32)).reshape(bsz, -1, XA_HEADS, XA_HEAD_DIM)
    v = jnp.einsum('bmd,de->bme', mem, wv.astype(f32)).reshape(bsz, -1, XA_HEADS, XA_HEAD_DIM)
    s = jnp.einsum('blhd,bmhd->bhlm', q, k) * (1.0 / math.sqrt(XA_HEAD_DIM))
    pr = jax.nn.softmax(s, axis=-1)
    o = jnp.einsum('bhlm,bmhd->blhd', pr, v).reshape(bsz, seq, D_MODEL)
    return jnp.einsum('ble,ed->bld', o, wo.astype(f32))


def squared_relu_mlp(x, w1, w2):
    hdn = jnp.square(jax.nn.relu(jnp.einsum('bld,df->blf', x, w1.astype(jnp.float32))))
    return jnp.einsum('blf,fd->bld', hdn, w2.astype(jnp.float32))


def setup_inputs(seed: int = 0) -> dict:
    key = jax.random.key(seed)
    ks = iter(jax.random.split(key, 64))
    f32 = jnp.float32

    def nrm(shape, scale):
        return jax.random.normal(next(ks), shape, f32) * scale

    def uni(shape, lo, hi):
        return jax.random.uniform(next(ks), shape, f32, lo, hi)

    L = DEPTH
    x = nrm((BATCH, SEQ, D_MODEL), 1.0)
    mem = nrm((BATCH, MEM_LEN, D_MODEL), 1.0)
    dt0 = jnp.exp(uni((L, SSD_HEADS), math.log(1e-3), math.log(1e-1)))
    a_rg = uni((L, RG_WIDTH), 0.9, 0.999) ** (1.0 / RG_C)
    n_idx = jnp.arange(S5_STATE, dtype=f32)
    return {
        "x": x,
        "mem": mem,
        "w_in": nrm((L, D_MODEL, D_IN), D_MODEL ** -0.5),
        "w_out": nrm((L, D_MIX, D_MODEL), BETA * D_MIX ** -0.5),
        "ssd_conv_w": nrm((L, SSD_CONV, SSD_XBC), SSD_CONV ** -0.5),
        "ssd_conv_b": nrm((L, SSD_XBC), 0.02),
        "ssd_dt_bias": dt0 + jnp.log(-jnp.expm1(-dt0)),
        "ssd_a_log": jnp.log(uni((L, SSD_HEADS), 1.0, 16.0)),
        "ssd_d": 1.0 + nrm((L, SSD_HEADS), 0.02),
        "ssd_norm_w": 1.0 + nrm((L, SSD_WIDTH), 0.02),
        "s5_lam_re": -0.5 + nrm((L, S5_GROUPS, S5_STATE), 0.01),
        "s5_lam_im": jnp.pi * n_idx + nrm((L, S5_GROUPS, S5_STATE), 0.01),
        "s5_log_step": uni((L, S5_GROUPS), math.log(1e-3), math.log(1e-1)),
        "s5_b_re": nrm((L, S5_GROUPS, S5_STATE, S5_GROUP_CH), (2.0 * S5_GROUP_CH) ** -0.5),
        "s5_b_im": nrm((L, S5_GROUPS, S5_STATE, S5_GROUP_CH), (2.0 * S5_GROUP_CH) ** -0.5),
        "s5_c_re": nrm((L, S5_GROUPS, S5_GROUP_CH, S5_STATE), (2.0 * S5_STATE) ** -0.5),
        "s5_c_im": nrm((L, S5_GROUPS, S5_GROUP_CH, S5_STATE), (2.0 * S5_STATE) ** -0.5),
        "s5_d": nrm((L, S5_WIDTH), 1.0),
        "s5_glu_w": nrm((L, S5_WIDTH, S5_WIDTH), S5_WIDTH ** -0.5),
        "s5_glu_b": nrm((L, S5_WIDTH), 0.02),
        "rg_conv_w": nrm((L, RG_CONV, RG_WIDTH), RG_CONV ** -0.5),
        "rg_conv_b": nrm((L, RG_WIDTH), 0.02),
        "rg_wa": nrm((L, RG_BLOCKS, RG_BLOCK_DIM, RG_BLOCK_DIM), RG_BLOCK_DIM ** -0.5),
        "rg_ba": nrm((L, RG_BLOCKS, RG_BLOCK_DIM), 0.02),
        "rg_wx": nrm((L, RG_BLOCKS, RG_BLOCK_DIM, RG_BLOCK_DIM), RG_BLOCK_DIM ** -0.5),
        "rg_bx": nrm((L, RG_BLOCKS, RG_BLOCK_DIM), 0.02),
        "rg_lambda": jnp.log(a_rg / (1.0 - a_rg)),
        "ln1_g": 1.0 + nrm((L, D_MODEL), 0.02),
        "ln1_b": nrm((L, D_MODEL), 0.02),
        "xa_wq": nrm((L, D_MODEL, D_MODEL), D_MODEL ** -0.5),
        "xa_wk": nrm((L, D_MODEL, D_MODEL), D_MODEL ** -0.5),
        "xa_wv": nrm((L, D_MODEL, D_MODEL), BETA * D_MODEL ** -0.5),
        "xa_wo": nrm((L, D_MODEL, D_MODEL), BETA * D_MODEL ** -0.5),
        "ln2_g": 1.0 + nrm((L, D_MODEL), 0.02),
        "ln2_b": nrm((L, D_MODEL), 0.02),
        "mlp_w1": nrm((L, D_MODEL, D_FF), BETA * D_MODEL ** -0.5),
        "mlp_w2": nrm((L, D_FF, D_MODEL), BETA * D_FF ** -0.5),
        "ln3_g": 1.0 + nrm((L, D_MODEL), 0.02),
        "ln3_b": nrm((L, D_MODEL), 0.02),
    }


def reference(x, mem, w_in, w_out, ssd_conv_w, ssd_conv_b, ssd_dt_bias, ssd_a_log, ssd_d, ssd_norm_w,
              s5_lam_re, s5_lam_im, s5_log_step, s5_b_re, s5_b_im, s5_c_re, s5_c_im, s5_d, s5_glu_w, s5_glu_b,
              rg_conv_w, rg_conv_b, rg_wa, rg_ba, rg_wx, rg_bx, rg_lambda, ln1_g, ln1_b,
              xa_wq, xa_wk, xa_wv, xa_wo, ln2_g, ln2_b, mlp_w1, mlp_w2, ln3_g, ln3_b):
    f32 = jnp.float32
    out_dtype = x.dtype
    h = x.astype(f32)
    memf = mem.astype(f32)
    split_idx = []
    acc = 0
    for w in IN_COLS[:-1]:
        acc += w
        split_idx.append(acc)
    for l in range(DEPTH):
        proj = jnp.einsum('bld,dk->blk', h, w_in[l].astype(f32))
        z, xbc, dt_raw, u_s5, x_rg, g_rg = jnp.split(proj, split_idx, axis=-1)
        y_ssd = ssd_mixer(z, xbc, dt_raw, ssd_conv_w[l], ssd_conv_b[l], ssd_dt_bias[l], ssd_a_log[l],
                          ssd_d[l], ssd_norm_w[l])
        y_s5 = s5_mixer(u_s5, s5_lam_re[l], s5_lam_im[l], s5_log_step[l], s5_b_re[l], s5_b_im[l],
                        s5_c_re[l], s5_c_im[l], s5_d[l], s5_glu_w[l], s5_glu_b[l])
        y_rg = rglru_mixer(x_rg, g_rg, rg_conv_w[l], rg_conv_b[l], rg_wa[l], rg_ba[l], rg_wx[l], rg_bx[l],
                           rg_lambda[l])
        y = jnp.concatenate([y_ssd, y_s5, y_rg], axis=-1)
        y = jnp.einsum('ble,ed->bld', y, w_out[l].astype(f32))
        h = layer_norm(ALPHA * h + y, ln1_g[l], ln1_b[l])
        h = layer_norm(ALPHA * h + cross_attention(h, memf, xa_wq[l], xa_wk[l], xa_wv[l], xa_wo[l]),
                       ln2_g[l], ln2_b[l])
        h = layer_norm(ALPHA * h + squared_relu_mlp(h, mlp_w1[l], mlp_w2[l]), ln3_g[l], ln3_b[l])
    return h.astype(out_dtype)
```

```python
import functools
import math

import jax
import jax.numpy as jnp
from jax import lax
from jax.experimental import pallas as pl
from jax.experimental.pallas import tpu as pltpu

F32 = jnp.float32
BF16 = jnp.bfloat16

D_MODEL = 1024
DEPTH = 2
SSD_WIDTH = 512
SSD_HEAD_DIM = 64
SSD_HEADS = 8
SSD_GROUPS = 2
SSD_HPG = SSD_HEADS // SSD_GROUPS
SSD_GW = SSD_HPG * SSD_HEAD_DIM
SSD_STATE = 128
SSD_XBC = 1024
CONV_K = 4
S5_WIDTH = 256
S5_GROUPS = 16
S5_GROUP_CH = 16
S5_STATE = 64
S5_NSTATE = S5_GROUPS * S5_STATE
S5_BLK = 8
RG_WIDTH = 256
RG_BLOCKS = 4
RG_C = 8.0
XA_HEADS = 4
XA_HEAD_DIM = 256
D_FF = 4096
ALPHA = (2.0 * DEPTH) ** 0.25
LN_EPS = 1e-5

LANES = 128
SUBLANES = 8
NEG_BIG = -1e30
VMEM_LIMIT = 56 * 1024 * 1024

SSD_CHUNK = 128
MIX_TQ = 256
S5_ROWS = 256
TOK_TM = 512
MLP_FC = 1024


def _dot(a, b):
    return jnp.dot(a.astype(BF16), b.astype(BF16), preferred_element_type=F32)


def _dot_nt(a, b):
    return lax.dot_general(a.astype(BF16), b.astype(BF16), (((1,), (1,)), ((), ())),
                           preferred_element_type=F32)


def _layer_norm(x, g, b):
    mu = jnp.mean(x, axis=-1, keepdims=True)
    xc = x - mu
    var = jnp.mean(xc * xc, axis=-1, keepdims=True)
    return xc * lax.rsqrt(var + LN_EPS) * g + b


def _gelu(x):
    c = math.sqrt(2.0 / math.pi)
    return 0.5 * x * (1.0 + jnp.tanh(c * (x + 0.044715 * (x * x * x))))


def _silu(x):
    return x * jax.nn.sigmoid(x)


def _softplus(x):
    return jnp.maximum(x, 0.0) + jnp.log1p(jnp.exp(-jnp.abs(x)))


def _const_spec(shape):
    nd = len(shape)
    return pl.BlockSpec(shape, lambda *_: (0,) * nd, pipeline_mode=pl.Buffered(1))


def _params(sem):
    return pltpu.CompilerParams(dimension_semantics=sem, vmem_limit_bytes=VMEM_LIMIT)


def _proj_kernel(h_ref, w_ref, o_ref):
    o_ref[...] = _dot(h_ref[...], w_ref[...]).astype(o_ref.dtype)


def _proj(h2d, w, out_dtype=F32):
    t, d = h2d.shape
    n = w.shape[1]
    tm = min(TOK_TM, t)
    return pl.pallas_call(
        _proj_kernel,
        grid=(t // tm,),
        in_specs=[pl.BlockSpec((tm, d), lambda i: (i, 0)), _const_spec((d, n))],
        out_specs=pl.BlockSpec((tm, n), lambda i: (i, 0)),
        out_shape=jax.ShapeDtypeStruct((t, n), out_dtype),
        compiler_params=_params(("parallel",)),
        name="proj",
    )(h2d, w)


def _s5_kernel(u_ref, ms_ref, my_ref, mo_ref, lvl_ref, pw_ref, d_ref, o_ref, hbuf, *, rows):
    n = S5_NSTATE

    @pl.when(pl.program_id(1) == 0)
    def _():
        hbuf[0:SUBLANES, :] = jnp.zeros((SUBLANES, 2 * n), F32)

    u = u_ref[...]
    u16 = u.astype(BF16)
    s = jnp.dot(u16, ms_ref[...], preferred_element_type=F32)
    xr = s[:, :n]
    xi = s[:, n:]
    row8 = lax.broadcasted_iota(jnp.int32, (rows, n), 0) & (SUBLANES - 1)
    for lv, dist in enumerate((1, 2, 4)):
        lr = lvl_ref[lv:lv + 1, :n]
        li = lvl_ref[lv:lv + 1, n:]
        keep = row8 >= dist
        sr = jnp.where(keep, pltpu.roll(xr, dist, 0), 0.0)
        si = jnp.where(keep, pltpu.roll(xi, dist, 0), 0.0)
        xr, xi = xr + lr * sr - li * si, xi + lr * si + li * sr
    hbuf[SUBLANES:SUBLANES + rows, :n] = xr
    hbuf[SUBLANES:SUBLANES + rows, n:] = xi

    pr = pw_ref[:, :n]
    pi = pw_ref[:, n:]

    def tile_carry(t, carry):
        cr, ci = carry
        r0 = pl.multiple_of(SUBLANES + t * SUBLANES, SUBLANES)
        tr = hbuf[pl.ds(r0, SUBLANES), 0:n]
        ti = hbuf[pl.ds(r0, SUBLANES), n:2 * n]
        tr, ti = tr + pr * cr - pi * ci, ti + pr * ci + pi * cr
        hbuf[pl.ds(r0, SUBLANES), 0:n] = tr
        hbuf[pl.ds(r0, SUBLANES), n:2 * n] = ti
        return tr[SUBLANES - 1:SUBLANES, :], ti[SUBLANES - 1:SUBLANES, :]

    c0 = (hbuf[SUBLANES - 1:SUBLANES, 0:n], hbuf[SUBLANES - 1:SUBLANES, n:2 * n])
    lax.fori_loop(0, rows // SUBLANES, tile_carry, c0)

    hprev = hbuf[SUBLANES - 1:SUBLANES - 1 + rows, :]
    y = jnp.dot(u16, my_ref[...], preferred_element_type=F32)
    y = y + jnp.dot(hprev.astype(BF16), mo_ref[...], preferred_element_type=F32)
    hbuf[0:SUBLANES, :] = hbuf[rows:rows + SUBLANES, :]
    o_ref[...] = _gelu(y + d_ref[...] * u)


def _s5_scan(u8, ms, my, mo, lvl, pw, d8):
    b, r, w = u8.shape
    rows = min(S5_ROWS, r)
    return pl.pallas_call(
        functools.partial(_s5_kernel, rows=rows),
        grid=(b, r // rows),
        in_specs=[pl.BlockSpec((None, rows, w), lambda i, j: (i, j, 0)),
                  _const_spec(ms.shape), _const_spec(my.shape), _const_spec(mo.shape),
                  _const_spec(lvl.shape), _const_spec(pw.shape), _const_spec(d8.shape)],
        out_specs=pl.BlockSpec((None, rows, w), lambda i, j: (i, j, 0)),
        out_shape=jax.ShapeDtypeStruct((b, r, w), F32),
        scratch_shapes=[pltpu.VMEM((rows + SUBLANES, 2 * S5_NSTATE), F32)],
        compiler_params=_params(("parallel", "arbitrary")),
        name="s5_scan",
    )(u8, ms, my, mo, lvl, pw, d8)


def _s5_prepare(lam_re, lam_im, log_step, b_re, b_im, c_re, c_im, d_skip):
    g, p, c, t8 = S5_GROUPS, S5_STATE, S5_GROUP_CH, S5_BLK
    lam = lax.complex(lam_re.astype(F32), lam_im.astype(F32))
    lam_dt = lam * jnp.exp(log_step.astype(F32))[:, None]
    lam_bar = jnp.exp(lam_dt)
    b_bar = ((lam_bar - 1.0) / lam)[..., None] * lax.complex(b_re.astype(F32), b_im.astype(F32))
    cmat = lax.complex(c_re.astype(F32), c_im.astype(F32))

    def lam_pow(k):
        return jnp.exp(k.astype(F32)[..., None, None] * lam_dt)

    eye = jnp.eye(g, dtype=F32)
    steps = jnp.arange(t8)
    kern = jnp.real(jnp.einsum('gcp,kgp,gpd->kgcd', cmat, lam_pow(steps), b_bar))
    lag = steps[None, :] - steps[:, None]
    kfull = jnp.where((lag >= 0)[:, :, None, None, None], kern[jnp.clip(lag, 0, t8 - 1)], 0.0)
    m_y = jnp.einsum('stgcd,gh->sgdthc', kfull, eye).reshape(t8 * g * c, t8 * g * c)
    w_s = lam_pow(t8 - 1 - steps)[..., None] * b_bar[None]
    m_s = jnp.concatenate(
        [jnp.einsum('tgpd,gh->tgdhp', part, eye).reshape(t8 * g * c, g * p)
         for part in (jnp.real(w_s), jnp.imag(w_s))], axis=1)
    v_o = cmat[None] * lam_pow(steps + 1)[:, :, None, :]
    m_o = jnp.concatenate(
        [jnp.einsum('tgcp,gh->gpthc', part, eye).reshape(g * p, t8 * g * c)
         for part in (jnp.real(v_o), -jnp.imag(v_o))], axis=0)

    def re_im(z):
        return jnp.concatenate([jnp.real(z).reshape(z.shape[0], g * p),
                                jnp.imag(z).reshape(z.shape[0], g * p)], axis=1)

    lvl = re_im(lam_pow(t8 * jnp.array([1, 2, 4, 0, 0, 0, 0, 0])))
    pw = re_im(lam_pow(t8 * (steps + 1)))
    d8 = jnp.tile(d_skip.astype(F32), t8)[None, :]
    return m_s.astype(BF16), m_y.astype(BF16), m_o.astype(BF16), lvl, pw, d8


def _expand_heads(cols, grp, head_lane):
    base = SSD_HPG * grp
    out = jnp.broadcast_to(cols[:, base:base + 1], (cols.shape[0], SSD_GW))
    for hh in range(1, SSD_HPG):
        out = jnp.where(head_lane == hh, cols[:, base + hh:base + hh + 1], out)
    return out


def _causal_conv(pad_ref, x, w_ref, b_ref, rows):
    pad_ref[SUBLANES:SUBLANES + rows, :] = x
    acc = b_ref[...] + w_ref[CONV_K - 1:CONV_K, :] * x
    for k in range(CONV_K - 1):
        off = SUBLANES - (CONV_K - 1) + k
        acc = acc + w_ref[k:k + 1, :] * pad_ref[off:off + rows, :]
    pad_ref[0:SUBLANES, :] = pad_ref[rows:rows + SUBLANES, :]
    return acc


def _mixer_kernel(h_ref, yg_ref, wz_ref, wxbc_ref, wdt_ref, wxr_ref, wgr_ref,
                  cw_ref, cb_ref, dtb_ref, aneg_ref, dvec_ref, nw_ref,
                  gluw_ref, glub_ref,
                  rcw_ref, rcb_ref, wa_ref, ba_ref, wx_ref, bx_ref, rlam_ref,
                  wout_ref, g_ref, b_ref,
                  o_ref,
                  xpad, rpad, state, ascr, bscr, rcar, ycat, *, tq, chunk):
    @pl.when(pl.program_id(1) == 0)
    def _():
        xpad[0:SUBLANES, :] = jnp.zeros((SUBLANES, SSD_XBC), F32)
        rpad[0:SUBLANES, :] = jnp.zeros((SUBLANES, RG_WIDTH), F32)
        state[...] = jnp.zeros(state.shape, F32)
        rcar[...] = jnp.zeros(rcar.shape, F32)

    hb = h_ref[...]
    h16 = hb.astype(BF16)
    z = jnp.dot(h16, wz_ref[...], preferred_element_type=F32)
    xbc = jnp.dot(h16, wxbc_ref[...], preferred_element_type=F32)
    dt_raw = jnp.dot(h16, wdt_ref[...], preferred_element_type=F32)
    xr = jnp.dot(h16, wxr_ref[...], preferred_element_type=F32)
    gr = jnp.dot(h16, wgr_ref[...], preferred_element_type=F32)

    xact = _silu(_causal_conv(xpad, xbc, cw_ref, cb_ref, tq))
    dt_all = _softplus(dt_raw + dtb_ref[...])
    q = chunk
    tri = (lax.broadcasted_iota(jnp.int32, (q, q), 0) >= lax.broadcasted_iota(jnp.int32, (q, q), 1))
    row_id = lax.broadcasted_iota(jnp.int32, (q, LANES), 0)
    head_lane = lax.broadcasted_iota(jnp.int32, (1, SSD_GW), 1) // SSD_HEAD_DIM
    for ci in range(tq // q):
        r0 = ci * q
        dt = dt_all[r0:r0 + q, :]
        cs = dt * aneg_ref[...]
        dist = 1
        while dist < q:
            cs = cs + jnp.where(row_id >= dist, pltpu.roll(cs, dist, 0), 0.0)
            dist *= 2
        cs_t = cs.T
        cs_last = cs[q - 1:q, :]
        chunk_decay = jnp.exp(cs_last)
        dec_t = jnp.exp(cs_t[:, q - 1:q] - cs_t)
        ecs = jnp.exp(cs)
        zc = z[r0:r0 + q, :]
        halves = []
        for grp in range(SSD_GROUPS):
            lo = grp * SSD_GW
            xs = xact[r0:r0 + q, lo:lo + SSD_GW]
            bm = xact[r0:r0 + q, SSD_WIDTH + grp * SSD_STATE:SSD_WIDTH + (grp + 1) * SSD_STATE]
            cm = xact[r0:r0 + q, SSD_WIDTH + (SSD_GROUPS + grp) * SSD_STATE:
                      SSD_WIDTH + (SSD_GROUPS + grp + 1) * SSD_STATE]
            cm16 = cm.astype(BF16)
            xdt16 = (xs * _expand_heads(dt, grp, head_lane)).astype(BF16)
            cb = _dot_nt(cm16, bm)
            bm_t = bm.T
            s_prev = state[grp]
            y_off = jnp.dot(cm16, s_prev.astype(BF16), preferred_element_type=F32)
            y_diag = jnp.zeros((q, SSD_GW), F32)
            st_new = jnp.zeros((SSD_STATE, SSD_GW), F32)
            for hh in range(SSD_HPG):
                hd = grp * SSD_HPG + hh
                seg = cs[:, hd:hd + 1] - cs_t[hd:hd + 1, :]
                lmat = jnp.exp(jnp.where(tri, seg, NEG_BIG))
                yd = jnp.dot((cb * lmat).astype(BF16), xdt16, preferred_element_type=F32)
                st = jnp.dot((bm_t * dec_t[hd:hd + 1, :]).astype(BF16), xdt16, preferred_element_type=F32)
                sel = head_lane == hh
                y_diag = jnp.where(sel, yd, y_diag)
                st_new = jnp.where(sel, st, st_new)
            state[grp] = s_prev * _expand_heads(chunk_decay, grp, head_lane) + st_new
            halves.append(y_diag + y_off * _expand_heads(ecs, grp, head_lane) + xs * dvec_ref[:, lo:lo + SSD_GW])
        y = jnp.concatenate(halves, axis=-1) * _silu(zc)
        y = y * lax.rsqrt(jnp.mean(y * y, axis=-1, keepdims=True) + LN_EPS) * nw_ref[...]
        ycat[r0:r0 + q, 0:SSD_WIDTH] = y.astype(BF16)

    yg = yg_ref[...]
    glu = jax.nn.sigmoid(_dot(yg, gluw_ref[...]) + glub_ref[...])
    ycat[:, SSD_WIDTH:SSD_WIDTH + S5_WIDTH] = (yg * glu).astype(BF16)

    xc = _causal_conv(rpad, xr, rcw_ref, rcb_ref, tq)
    xc16 = xc.astype(BF16)
    rgate = jax.nn.sigmoid(jnp.dot(xc16, wa_ref[...], preferred_element_type=F32) + ba_ref[...])
    igate = jax.nn.sigmoid(jnp.dot(xc16, wx_ref[...], preferred_element_type=F32) + bx_ref[...])
    log_a = (-RG_C) * rgate * _softplus(-rlam_ref[...])
    a = jnp.exp(log_a)
    bt = jnp.sqrt(1.0 - jnp.exp(2.0 * log_a)) * (igate * xc)
    row8 = lax.broadcasted_iota(jnp.int32, (tq, RG_WIDTH), 0) & (SUBLANES - 1)
    for dist in (1, 2, 4):
        keep = row8 >= dist
        a_sh = jnp.where(keep, pltpu.roll(a, dist, 0), 1.0)
        b_sh = jnp.where(keep, pltpu.roll(bt, dist, 0), 0.0)
        bt = bt + a * b_sh
        a = a * a_sh
    ascr[...] = a
    bscr[...] = bt

    def tile_carry(t, carry):
        r0 = pl.multiple_of(t * SUBLANES, SUBLANES)
        ht = bscr[pl.ds(r0, SUBLANES), :] + ascr[pl.ds(r0, SUBLANES), :] * carry
        bscr[pl.ds(r0, SUBLANES), :] = ht
        return ht[SUBLANES - 1:SUBLANES, :]

    last = lax.fori_loop(0, tq // SUBLANES, tile_carry, rcar[0:1, :])
    rcar[0:1, :] = last
    ycat[:, SSD_WIDTH + S5_WIDTH:] = (bscr[...] * _gelu(gr)).astype(BF16)

    y_all = jnp.dot(ycat[...], wout_ref[...], preferred_element_type=F32)
    o_ref[...] = _layer_norm(ALPHA * hb + y_all, g_ref[...], b_ref[...])


def _mixer(h, yg, consts):
    b, l, d = h.shape
    tq = min(MIX_TQ, l)
    chunk = min(SSD_CHUNK, tq)
    tok = lambda w: pl.BlockSpec((None, tq, w), lambda i, j: (i, j, 0))
    return pl.pallas_call(
        functools.partial(_mixer_kernel, tq=tq, chunk=chunk),
        grid=(b, l // tq),
        in_specs=[tok(d), tok(S5_WIDTH)] + [_const_spec(c.shape) for c in consts],
        out_specs=tok(d),
        out_shape=jax.ShapeDtypeStruct((b, l, d), F32),
        scratch_shapes=[
            pltpu.VMEM((tq + SUBLANES, SSD_XBC), F32),
            pltpu.VMEM((tq + SUBLANES, RG_WIDTH), F32),
            pltpu.VMEM((SSD_GROUPS, SSD_STATE, SSD_GW), F32),
            pltpu.VMEM((tq, RG_WIDTH), F32),
            pltpu.VMEM((tq, RG_WIDTH), F32),
            pltpu.VMEM((SUBLANES, RG_WIDTH), F32),
            pltpu.VMEM((tq, D_MODEL), BF16),
        ],
        compiler_params=_params(("parallel", "arbitrary")),
        name="mixer",
    )(h, yg, *consts)


def _block_diag(w):
    nb, bi, bo = w.shape
    return jnp.einsum('hij,hk->hikj', w, jnp.eye(nb, dtype=w.dtype)).reshape(nb * bi, nb * bo)


def _kv_kernel(m_ref, wk_ref, wv_ref, k_ref, v_ref):
    m16 = m_ref[...].astype(BF16)
    k_ref[...] = jnp.dot(m16, wk_ref[...], preferred_element_type=F32).astype(BF16)
    v_ref[...] = jnp.dot(m16, wv_ref[...], preferred_element_type=F32).astype(BF16)


def _kv(mem, wk, wv):
    b, m, d = mem.shape
    spec = pl.BlockSpec((None, m, d), lambda i: (i, 0, 0))
    return pl.pallas_call(
        _kv_kernel,
        grid=(b,),
        in_specs=[spec, _const_spec(wk.shape), _const_spec(wv.shape)],
        out_specs=[spec, spec],
        out_shape=[jax.ShapeDtypeStruct((b, m, d), BF16)] * 2,
        compiler_params=_params(("parallel",)),
        name="kv",
    )(mem, wk, wv)


def _xattn_kernel(h_ref, k_ref, v_ref, wq_ref, wo_ref, g_ref, b_ref, o_ref):
    hb = h_ref[...]
    qall = jnp.dot(hb.astype(BF16), wq_ref[...], preferred_element_type=F32) * (1.0 / math.sqrt(XA_HEAD_DIM))
    outs = []
    for hd in range(XA_HEADS):
        sl = slice(hd * XA_HEAD_DIM, (hd + 1) * XA_HEAD_DIM)
        s = _dot_nt(qall[:, sl], k_ref[:, sl])
        p = jnp.exp(s - jnp.max(s, axis=-1, keepdims=True))
        o = jnp.dot(p.astype(BF16), v_ref[:, sl], preferred_element_type=F32)
        outs.append(o / jnp.sum(p, axis=-1, keepdims=True))
    o = jnp.concatenate(outs, axis=-1)
    y = jnp.dot(o.astype(BF16), wo_ref[...], preferred_element_type=F32)
    o_ref[...] = _layer_norm(ALPHA * hb + y, g_ref[...], b_ref[...])


def _xattn(h, k, v, wq, wo, g, bb):
    b, l, d = h.shape
    tm = min(TOK_TM, l)
    tok = pl.BlockSpec((None, tm, d), lambda i, j: (i, j, 0))
    kvs = pl.BlockSpec((None, k.shape[1], d), lambda i, j: (i, 0, 0))
    return pl.pallas_call(
        _xattn_kernel,
        grid=(b, l // tm),
        in_specs=[tok, kvs, kvs, _const_spec(wq.shape), _const_spec(wo.shape),
                  _const_spec(g.shape), _const_spec(bb.shape)],
        out_specs=tok,
        out_shape=jax.ShapeDtypeStruct((b, l, d), F32),
        compiler_params=_params(("parallel", "parallel")),
        name="xattn",
    )(h, k, v, wq, wo, g, bb)


def _mlp_kernel(h_ref, w1_ref, w2_ref, g_ref, b_ref, o_ref):
    hb = h_ref[...]
    h16 = hb.astype(BF16)
    acc = ALPHA * hb
    for f in range(D_FF // MLP_FC):
        sl = slice(f * MLP_FC, (f + 1) * MLP_FC)
        t = jnp.maximum(jnp.dot(h16, w1_ref[:, sl], preferred_element_type=F32), 0.0)
        acc = acc + jnp.dot((t * t).astype(BF16), w2_ref[sl, :], preferred_element_type=F32)
    o_ref[...] = _layer_norm(acc, g_ref[...], b_ref[...])


def _mlp(h2d, w1, w2, g, bb):
    t, d = h2d.shape
    tm = min(TOK_TM, t)
    tok = pl.BlockSpec((tm, d), lambda i: (i, 0))
    return pl.pallas_call(
        _mlp_kernel,
        grid=(t // tm,),
        in_specs=[tok, _const_spec(w1.shape), _const_spec(w2.shape), _const_spec(g.shape), _const_spec(bb.shape)],
        out_specs=tok,
        out_shape=jax.ShapeDtypeStruct((t, d), F32),
        compiler_params=_params(("parallel",)),
        name="mlp",
    )(h2d, w1, w2, g, bb)


def _row(v):
    return v.astype(F32).reshape(1, -1)


def _pad_lanes(v2d):
    return jnp.pad(v2d, ((0, 0), (0, LANES - v2d.shape[1])))


def mixer_sublayer(h, w_in, w_out, ssd_conv_w, ssd_conv_b, ssd_dt_bias, ssd_a_log, ssd_d, ssd_norm_w,
                   s5_lam_re, s5_lam_im, s5_log_step, s5_b_re, s5_b_im, s5_c_re, s5_c_im, s5_d, s5_glu_w,
                   s5_glu_b, rg_conv_w, rg_conv_b, rg_wa, rg_ba, rg_wx, rg_bx, rg_lambda, ln_g, ln_b):
    b, l, d = h.shape
    o_z, o_xbc, o_dt = 0, SSD_WIDTH, SSD_WIDTH + SSD_XBC
    o_u = o_dt + SSD_HEADS
    o_xr = o_u + S5_WIDTH
    o_gr = o_xr + RG_WIDTH
    w16 = w_in.astype(BF16)
    u = _proj(h.reshape(b * l, d), w16[:, o_u:o_xr])
    ms, my, mo, lvl, pw, d8 = _s5_prepare(s5_lam_re, s5_lam_im, s5_log_step, s5_b_re, s5_b_im,
                                          s5_c_re, s5_c_im, s5_d)
    yg = _s5_scan(u.reshape(b, l // S5_BLK, S5_BLK * S5_WIDTH), ms, my, mo, lvl, pw, d8)
    yg = yg.reshape(b, l, S5_WIDTH)
    consts = (
        w16[:, o_z:o_xbc], w16[:, o_xbc:o_dt], _pad_lanes(w16[:, o_dt:o_u]), w16[:, o_xr:o_gr], w16[:, o_gr:],
        ssd_conv_w.astype(F32), _row(ssd_conv_b), _pad_lanes(_row(ssd_dt_bias)),
        _pad_lanes(_row(-jnp.exp(ssd_a_log.astype(F32)))),
        _row(jnp.repeat(ssd_d.astype(F32), SSD_HEAD_DIM)), _row(ssd_norm_w),
        s5_glu_w.astype(BF16), _row(s5_glu_b),
        rg_conv_w.astype(F32), _row(rg_conv_b),
        _block_diag(rg_wa.astype(F32)).astype(BF16), _row(rg_ba),
        _block_diag(rg_wx.astype(F32)).astype(BF16), _row(rg_bx), _row(rg_lambda),
        w_out.astype(BF16), _row(ln_g), _row(ln_b),
    )
    return _mixer(h, yg, consts)


def xattn_sublayer(h, mem, wq, wk, wv, wo, ln_g, ln_b):
    k, v = _kv(mem, wk.astype(BF16), wv.astype(BF16))
    return _xattn(h, k, v, wq.astype(BF16), wo.astype(BF16), _row(ln_g), _row(ln_b))


def mlp_sublayer(h, w1, w2, ln_g, ln_b):
    b, l, d = h.shape
    return _mlp(h.reshape(b * l, d), w1.astype(BF16), w2.astype(BF16), _row(ln_g), _row(ln_b)).reshape(b, l, d)


def kernel(x, mem, w_in, w_out, ssd_conv_w, ssd_conv_b, ssd_dt_bias, ssd_a_log, ssd_d, ssd_norm_w, s5_lam_re, s5_lam_im, s5_log_step, s5_b_re, s5_b_im, s5_c_re, s5_c_im, s5_d, s5_glu_w, s5_glu_b, rg_conv_w, rg_conv_b, rg_wa, rg_ba, rg_wx, rg_bx, rg_lambda, ln1_g, ln1_b, xa_wq, xa_wk, xa_wv, xa_wo, ln2_g, ln2_b, mlp_w1, mlp_w2, ln3_g, ln3_b):
    h = x.astype(F32)
    memf = mem.astype(F32)
    for l in range(DEPTH):
        h = mixer_sublayer(h, w_in[l], w_out[l], ssd_conv_w[l], ssd_conv_b[l], ssd_dt_bias[l], ssd_a_log[l],
                           ssd_d[l], ssd_norm_w[l], s5_lam_re[l], s5_lam_im[l], s5_log_step[l], s5_b_re[l],
                           s5_b_im[l], s5_c_re[l], s5_c_im[l], s5_d[l], s5_glu_w[l], s5_glu_b[l],
                           rg_conv_w[l], rg_conv_b[l], rg_wa[l], rg_ba[l], rg_wx[l], rg_bx[l], rg_lambda[l],
                           ln1_g[l], ln1_b[l])
        h = xattn_sublayer(h, memf, xa_wq[l], xa_wk[l], xa_wv[l], xa_wo[l], ln2_g[l], ln2_b[l])
        h = mlp_sublayer(h, mlp_w1[l], mlp_w2[l], ln3_g[l], ln3_b[l])
    return h.astype(x.dtype)
```

```python
import functools
import math

import jax
import jax.numpy as jnp
from jax import lax
from jax.experimental import pallas as pl
from jax.experimental.pallas import tpu as pltpu

F32 = jnp.float32
BF16 = jnp.bfloat16

D_MODEL = 1024
DEPTH = 2
SSD_WIDTH = 512
SSD_HEAD_DIM = 64
SSD_HEADS = 8
SSD_GROUPS = 2
SSD_HPG = SSD_HEADS // SSD_GROUPS
SSD_GW = SSD_HPG * SSD_HEAD_DIM
SSD_STATE = 128
SSD_XBC = 1024
CONV_K = 4
S5_WIDTH = 256
S5_GROUPS = 16
S5_GROUP_CH = 16
S5_STATE = 64
S5_NSTATE = S5_GROUPS * S5_STATE
S5_BLK = 8
RG_WIDTH = 256
RG_BLOCKS = 4
RG_C = 8.0
XA_HEADS = 4
XA_HEAD_DIM = 256
D_FF = 4096
ALPHA = (2.0 * DEPTH) ** 0.25
LN_EPS = 1e-5

LANES = 128
SUBLANES = 8
NEG_BIG = -1e30
VMEM_LIMIT = 56 * 1024 * 1024

SSD_CHUNK = 128
MIX_TQ = 256
S5_TB = 2048
TOK_TM = 512
MLP_FC = 1024


def _dot(a, b):
    return jnp.dot(a.astype(BF16), b.astype(BF16), preferred_element_type=F32)


def _dot_nt(a, b):
    return lax.dot_general(a.astype(BF16), b.astype(BF16), (((1,), (1,)), ((), ())),
                           preferred_element_type=F32)


def _layer_norm(x, g, b):
    mu = jnp.mean(x, axis=-1, keepdims=True)
    xc = x - mu
    var = jnp.mean(xc * xc, axis=-1, keepdims=True)
    return xc * lax.rsqrt(var + LN_EPS) * g + b


def _gelu(x):
    c = math.sqrt(2.0 / math.pi)
    return 0.5 * x * (1.0 + jnp.tanh(c * (x + 0.044715 * (x * x * x))))


def _silu(x):
    return x * jax.nn.sigmoid(x)


def _softplus(x):
    return jnp.maximum(x, 0.0) + jnp.log1p(jnp.exp(-jnp.abs(x)))


def _const_spec(shape):
    nd = len(shape)
    return pl.BlockSpec(shape, lambda *_: (0,) * nd, pipeline_mode=pl.Buffered(1))


def _params(sem):
    return pltpu.CompilerParams(dimension_semantics=sem, vmem_limit_bytes=VMEM_LIMIT)


def _s5_expand_operators(wc_ref, vc_ref, ct_ref, ms_scr, krev_scr, mo_scr):
    n, w, c = S5_NSTATE, S5_WIDTH, S5_GROUP_CH
    tile_rows = w // c
    grp_r = lax.broadcasted_iota(jnp.int32, (w, 2 * n), 0) // c
    grp_c = (lax.broadcasted_iota(jnp.int32, (w, 2 * n), 1) % n) // S5_STATE
    same_s = grp_r == grp_c
    grp_r = (lax.broadcasted_iota(jnp.int32, (2 * n, w), 0) % n) // S5_STATE
    lane = lax.broadcasted_iota(jnp.int32, (2 * n, w), 1)
    same_o = grp_r == lane // c
    mo0 = jnp.where(same_o, ct_ref[...], 0.0)
    kc = jnp.dot(wc_ref[...], mo0, preferred_element_type=F32, precision=lax.Precision.HIGHEST)
    same_k = (lax.broadcasted_iota(jnp.int32, (w, w), 0) // c) == (lax.broadcasted_iota(jnp.int32, (w, w), 1) // c)
    sel_row = lax.broadcasted_iota(jnp.int32, (S5_BLK * c, w), 0)
    sel_lane = lax.broadcasted_iota(jnp.int32, (S5_BLK * c, w), 1)
    vc16 = vc_ref[...].astype(BF16)
    for t in range(S5_BLK):
        rs = slice(t * w, (t + 1) * w)
        ms_scr[rs, :] = jnp.where(same_s, jnp.tile(wc_ref[t * c:(t + 1) * c, :], (tile_rows, 1)), 0.0).astype(BF16)
        krev_scr[rs, :] = jnp.where(same_k, jnp.tile(kc[t * c:(t + 1) * c, :], (tile_rows, 1)), 0.0).astype(BF16)
        pick = ((sel_row // c == t) & (sel_row % c == sel_lane % c)).astype(BF16)
        spread = jnp.dot(vc16, pick, preferred_element_type=F32)
        mo_scr[:, rs] = jnp.where(same_o, spread, 0.0).astype(BF16)


def _s5_kernel(h_ref, wu_ref, wc_ref, vc_ref, ct_ref, lvl_ref, pw_ref, d_ref, gluw_ref, glub_ref, o_ref,
               ms_scr, krev_scr, mo_scr, u_scr, hbuf, *, rows):
    n, w = S5_NSTATE, S5_WIDTH

    @pl.when((pl.program_id(0) == 0) & (pl.program_id(1) == 0))
    def _():
        _s5_expand_operators(wc_ref, vc_ref, ct_ref, ms_scr, krev_scr, mo_scr)

    @pl.when(pl.program_id(1) == 0)
    def _():
        hbuf[0:SUBLANES, :] = jnp.zeros((SUBLANES, 2 * n), F32)

    halves = w // LANES
    u_tok = jnp.dot(h_ref[...].astype(BF16), wu_ref[...], preferred_element_type=F32)
    for hf in range(halves):
        u_scr[hf] = u_tok[:, hf * LANES:(hf + 1) * LANES]
    u = jnp.concatenate([u_scr[hf, pl.ds(k, rows, stride=S5_BLK), :]
                         for k in range(S5_BLK) for hf in range(halves)], axis=-1)
    u16 = u.astype(BF16)
    s = jnp.dot(u16, ms_scr[...], preferred_element_type=F32)
    xr = s[:, :n]
    xi = s[:, n:]
    row8 = lax.broadcasted_iota(jnp.int32, (rows, n), 0) & (SUBLANES - 1)
    for lv, dist in enumerate((1, 2, 4)):
        lr = lvl_ref[lv:lv + 1, :n]
        li = lvl_ref[lv:lv + 1, n:]
        keep = row8 >= dist
        sr = jnp.where(keep, pltpu.roll(xr, dist, 0), 0.0)
        si = jnp.where(keep, pltpu.roll(xi, dist, 0), 0.0)
        xr, xi = xr + lr * sr - li * si, xi + lr * si + li * sr
    hbuf[SUBLANES:SUBLANES + rows, :n] = xr
    hbuf[SUBLANES:SUBLANES + rows, n:] = xi

    pr = pw_ref[:, :n]
    pi = pw_ref[:, n:]

    def tile_carry(t, carry):
        cr, ci = carry
        r0 = pl.multiple_of(SUBLANES + t * SUBLANES, SUBLANES)
        tr = hbuf[pl.ds(r0, SUBLANES), 0:n]
        ti = hbuf[pl.ds(r0, SUBLANES), n:2 * n]
        tr, ti = tr + pr * cr - pi * ci, ti + pr * ci + pi * cr
        hbuf[pl.ds(r0, SUBLANES), 0:n] = tr
        hbuf[pl.ds(r0, SUBLANES), n:2 * n] = ti
        return tr[SUBLANES - 1:SUBLANES, :], ti[SUBLANES - 1:SUBLANES, :]

    c0 = (hbuf[SUBLANES - 1:SUBLANES, 0:n], hbuf[SUBLANES - 1:SUBLANES, n:2 * n])
    lax.fori_loop(0, rows // SUBLANES, tile_carry, c0)

    hprev16 = hbuf[SUBLANES - 1:SUBLANES - 1 + rows, :].astype(BF16)
    hbuf[0:SUBLANES, :] = hbuf[rows:rows + SUBLANES, :]
    for t in range(S5_BLK):
        cols = slice(t * w, (t + 1) * w)
        y = jnp.dot(u16[:, :(t + 1) * w], krev_scr[(S5_BLK - 1 - t) * w:, :], preferred_element_type=F32)
        y = y + jnp.dot(hprev16, mo_scr[:, cols], preferred_element_type=F32)
        y = _gelu(y + d_ref[...] * u[:, cols])
        for hf in range(halves):
            u_scr[hf, pl.ds(t, rows, stride=S5_BLK), :] = y[:, hf * LANES:(hf + 1) * LANES]
    yg = jnp.concatenate([u_scr[hf] for hf in range(halves)], axis=-1)
    glu = jax.nn.sigmoid(_dot(yg, gluw_ref[...]) + glub_ref[...])
    o_ref[...] = (yg * glu).astype(o_ref.dtype)


def _s5_branch(h, consts):
    b, l, d = h.shape
    tb = min(S5_TB, l)
    rows = tb // S5_BLK
    n, w = S5_NSTATE, S5_WIDTH
    return pl.pallas_call(
        functools.partial(_s5_kernel, rows=rows),
        grid=(b, l // tb),
        in_specs=[pl.BlockSpec((None, tb, d), lambda i, j: (i, j, 0))] + [_const_spec(c.shape) for c in consts],
        out_specs=pl.BlockSpec((None, tb, w), lambda i, j: (i, j, 0)),
        out_shape=jax.ShapeDtypeStruct((b, l, w), BF16),
        scratch_shapes=[
            pltpu.VMEM((S5_BLK * w, 2 * n), BF16),
            pltpu.VMEM((S5_BLK * w, w), BF16),
            pltpu.VMEM((2 * n, S5_BLK * w), BF16),
            pltpu.VMEM((w // LANES, tb, LANES), F32),
            pltpu.VMEM((rows + SUBLANES, 2 * n), F32),
        ],
        compiler_params=_params(("arbitrary", "arbitrary")),
        name="s5_branch",
    )(h, *consts)


def _s5_prepare(lam_re, lam_im, log_step, b_re, b_im, c_re, c_im):
    g, p, c, t8 = S5_GROUPS, S5_STATE, S5_GROUP_CH, S5_BLK
    lr, li = lam_re.astype(F32), lam_im.astype(F32)
    dt = jnp.exp(log_step.astype(F32))[:, None]
    ar, ai = lr * dt, li * dt

    def lam_pow(k):
        k = k.astype(F32)[..., None, None]
        mag = jnp.exp(k * ar)
        return mag * jnp.cos(k * ai), mag * jnp.sin(k * ai)

    br1, bi1 = lam_pow(jnp.ones((), jnp.int32))
    xr_, xi_ = br1 - 1.0, bi1
    den = lr * lr + li * li
    cr, ci = (xr_ * lr + xi_ * li) / den, (xi_ * lr - xr_ * li) / den
    bre, bim = b_re.astype(F32), b_im.astype(F32)
    bbr = cr[..., None] * bre - ci[..., None] * bim
    bbi = cr[..., None] * bim + ci[..., None] * bre
    steps = jnp.arange(t8)
    pr, pi = lam_pow(t8 - 1 - steps)
    wsr = pr[..., None] * bbr[None] - pi[..., None] * bbi[None]
    wsi = pr[..., None] * bbi[None] + pi[..., None] * bbr[None]
    wc = jnp.concatenate([part.transpose(0, 3, 1, 2).reshape(t8 * c, g * p) for part in (wsr, wsi)], axis=1)
    cre, cim = c_re.astype(F32), c_im.astype(F32)
    pr, pi = lam_pow(steps + 1)
    vor = cre[None] * pr[:, :, None, :] - cim[None] * pi[:, :, None, :]
    voi = cre[None] * pi[:, :, None, :] + cim[None] * pr[:, :, None, :]
    vc = jnp.concatenate([part.transpose(1, 3, 0, 2).reshape(g * p, t8 * c) for part in (vor, -voi)], axis=0)
    ct = jnp.concatenate([jnp.tile(part.transpose(0, 2, 1).reshape(g * p, c), (1, g)) for part in (cre, -cim)],
                         axis=0)

    def re_im(pair):
        return jnp.concatenate([pair[0].reshape(-1, g * p), pair[1].reshape(-1, g * p)], axis=1)

    lvl = re_im(lam_pow(t8 * jnp.array([1, 2, 4, 0, 0, 0, 0, 0])))
    pw = re_im(lam_pow(t8 * (steps + 1)))
    return wc, vc, ct, lvl, pw


def _expand_heads(cols, grp, head_lane):
    base = SSD_HPG * grp
    out = jnp.broadcast_to(cols[:, base:base + 1], (cols.shape[0], SSD_GW))
    for hh in range(1, SSD_HPG):
        out = jnp.where(head_lane == hh, cols[:, base + hh:base + hh + 1], out)
    return out


def _causal_conv(pad_ref, x, w_ref, b_ref, rows):
    pad_ref[SUBLANES:SUBLANES + rows, :] = x
    acc = b_ref[...] + w_ref[CONV_K - 1:CONV_K, :] * x
    for k in range(CONV_K - 1):
        off = SUBLANES - (CONV_K - 1) + k
        acc = acc + w_ref[k:k + 1, :] * pad_ref[off:off + rows, :]
    pad_ref[0:SUBLANES, :] = pad_ref[rows:rows + SUBLANES, :]
    return acc


def _mixer_kernel(h_ref, ys5_ref, wz_ref, wxbc_ref, wdt_ref, wxr_ref, wgr_ref,
                  cw_ref, cb_ref, dtb_ref, aneg_ref, dvec_ref, nw_ref,
                  rcw_ref, rcb_ref, wa_ref, ba_ref, wx_ref, bx_ref, rlam_ref,
                  wout_ref, g_ref, b_ref,
                  o_ref,
                  xpad, rpad, state, ascr, bscr, rcar, ycat, *, tq, chunk):
    @pl.when(pl.program_id(1) == 0)
    def _():
        xpad[0:SUBLANES, :] = jnp.zeros((SUBLANES, SSD_XBC), F32)
        rpad[0:SUBLANES, :] = jnp.zeros((SUBLANES, RG_WIDTH), F32)
        state[...] = jnp.zeros(state.shape, F32)
        rcar[...] = jnp.zeros(rcar.shape, F32)

    hb = h_ref[...]
    h16 = hb.astype(BF16)
    z = jnp.dot(h16, wz_ref[...], preferred_element_type=F32)
    xbc = jnp.dot(h16, wxbc_ref[...], preferred_element_type=F32)
    dt_raw = jnp.dot(h16, wdt_ref[...], preferred_element_type=F32)
    xr = jnp.dot(h16, wxr_ref[...], preferred_element_type=F32)
    gr = jnp.dot(h16, wgr_ref[...], preferred_element_type=F32)

    xact = _silu(_causal_conv(xpad, xbc, cw_ref, cb_ref, tq))
    dt_all = _softplus(dt_raw + dtb_ref[...])
    q = chunk
    tri = (lax.broadcasted_iota(jnp.int32, (q, q), 0) >= lax.broadcasted_iota(jnp.int32, (q, q), 1))
    row_id = lax.broadcasted_iota(jnp.int32, (q, LANES), 0)
    head_lane = lax.broadcasted_iota(jnp.int32, (1, SSD_GW), 1) // SSD_HEAD_DIM
    for ci in range(tq // q):
        r0 = ci * q
        dt = dt_all[r0:r0 + q, :]
        cs = dt * aneg_ref[...]
        dist = 1
        while dist < q:
            cs = cs + jnp.where(row_id >= dist, pltpu.roll(cs, dist, 0), 0.0)
            dist *= 2
        cs_t = cs.T
        cs_last = cs[q - 1:q, :]
        chunk_decay = jnp.exp(cs_last)
        dec_t = jnp.exp(cs_t[:, q - 1:q] - cs_t)
        ecs = jnp.exp(cs)
        zc = z[r0:r0 + q, :]
        halves = []
        for grp in range(SSD_GROUPS):
            lo = grp * SSD_GW
            xs = xact[r0:r0 + q, lo:lo + SSD_GW]
            bm = xact[r0:r0 + q, SSD_WIDTH + grp * SSD_STATE:SSD_WIDTH + (grp + 1) * SSD_STATE]
            cm = xact[r0:r0 + q, SSD_WIDTH + (SSD_GROUPS + grp) * SSD_STATE:
                      SSD_WIDTH + (SSD_GROUPS + grp + 1) * SSD_STATE]
            cm16 = cm.astype(BF16)
            xdt16 = (xs * _expand_heads(dt, grp, head_lane)).astype(BF16)
            cb = _dot_nt(cm16, bm)
            bm_t = bm.T
            s_prev = state[grp]
            y_off = jnp.dot(cm16, s_prev.astype(BF16), preferred_element_type=F32)
            y_diag = jnp.zeros((q, SSD_GW), F32)
            st_new = jnp.zeros((SSD_STATE, SSD_GW), F32)
            for hh in range(SSD_HPG):
                hd = grp * SSD_HPG + hh
                seg = cs[:, hd:hd + 1] - cs_t[hd:hd + 1, :]
                lmat = jnp.exp(jnp.where(tri, seg, NEG_BIG))
                yd = jnp.dot((cb * lmat).astype(BF16), xdt16, preferred_element_type=F32)
                st = jnp.dot((bm_t * dec_t[hd:hd + 1, :]).astype(BF16), xdt16, preferred_element_type=F32)
                sel = head_lane == hh
                y_diag = jnp.where(sel, yd, y_diag)
                st_new = jnp.where(sel, st, st_new)
            state[grp] = s_prev * _expand_heads(chunk_decay, grp, head_lane) + st_new
            halves.append(y_diag + y_off * _expand_heads(ecs, grp, head_lane) + xs * dvec_ref[:, lo:lo + SSD_GW])
        y = jnp.concatenate(halves, axis=-1) * _silu(zc)
        y = y * lax.rsqrt(jnp.mean(y * y, axis=-1, keepdims=True) + LN_EPS) * nw_ref[...]
        ycat[r0:r0 + q, 0:SSD_WIDTH] = y.astype(BF16)

    ycat[:, SSD_WIDTH:SSD_WIDTH + S5_WIDTH] = ys5_ref[...]

    xc = _causal_conv(rpad, xr, rcw_ref, rcb_ref, tq)
    xc16 = xc.astype(BF16)
    rgate = jax.nn.sigmoid(jnp.dot(xc16, wa_ref[...], preferred_element_type=F32) + ba_ref[...])
    igate = jax.nn.sigmoid(jnp.dot(xc16, wx_ref[...], preferred_element_type=F32) + bx_ref[...])
    log_a = (-RG_C) * rgate * _softplus(-rlam_ref[...])
    a = jnp.exp(log_a)
    bt = jnp.sqrt(1.0 - jnp.exp(2.0 * log_a)) * (igate * xc)
    row8 = lax.broadcasted_iota(jnp.int32, (tq, RG_WIDTH), 0) & (SUBLANES - 1)
    for dist in (1, 2, 4):
        keep = row8 >= dist
        a_sh = jnp.where(keep, pltpu.roll(a, dist, 0), 1.0)
        b_sh = jnp.where(keep, pltpu.roll(bt, dist, 0), 0.0)
        bt = bt + a * b_sh
        a = a * a_sh
    ascr[...] = a
    bscr[...] = bt

    def tile_carry(t, carry):
        r0 = pl.multiple_of(t * SUBLANES, SUBLANES)
        ht = bscr[pl.ds(r0, SUBLANES), :] + ascr[pl.ds(r0, SUBLANES), :] * carry
        bscr[pl.ds(r0, SUBLANES), :] = ht
        return ht[SUBLANES - 1:SUBLANES, :]

    last = lax.fori_loop(0, tq // SUBLANES, tile_carry, rcar[0:1, :])
    rcar[0:1, :] = last
    ycat[:, SSD_WIDTH + S5_WIDTH:] = (bscr[...] * _gelu(gr)).astype(BF16)

    y_all = jnp.dot(ycat[...], wout_ref[...], preferred_element_type=F32)
    o_ref[...] = _layer_norm(ALPHA * hb + y_all, g_ref[...], b_ref[...])


def _mixer(h, ys5, consts):
    b, l, d = h.shape
    tq = min(MIX_TQ, l)
    chunk = min(SSD_CHUNK, tq)
    tok = lambda w: pl.BlockSpec((None, tq, w), lambda i, j: (i, j, 0))
    return pl.pallas_call(
        functools.partial(_mixer_kernel, tq=tq, chunk=chunk),
        grid=(b, l // tq),
        in_specs=[tok(d), tok(S5_WIDTH)] + [_const_spec(c.shape) for c in consts],
        out_specs=tok(d),
        out_shape=jax.ShapeDtypeStruct((b, l, d), F32),
        scratch_shapes=[
            pltpu.VMEM((tq + SUBLANES, SSD_XBC), F32),
            pltpu.VMEM((tq + SUBLANES, RG_WIDTH), F32),
            pltpu.VMEM((SSD_GROUPS, SSD_STATE, SSD_GW), F32),
            pltpu.VMEM((tq, RG_WIDTH), F32),
            pltpu.VMEM((tq, RG_WIDTH), F32),
            pltpu.VMEM((SUBLANES, RG_WIDTH), F32),
            pltpu.VMEM((tq, D_MODEL), BF16),
        ],
        compiler_params=_params(("parallel", "arbitrary")),
        name="mixer",
    )(h, ys5, *consts)


def _block_diag(w):
    nb, bi, bo = w.shape
    return jnp.einsum('hij,hk->hikj', w, jnp.eye(nb, dtype=w.dtype)).reshape(nb * bi, nb * bo)


def _kv_kernel(m_ref, wk_ref, wv_ref, k_ref, v_ref):
    m16 = m_ref[...].astype(BF16)
    k_ref[...] = jnp.dot(m16, wk_ref[...], preferred_element_type=F32).astype(BF16)
    v_ref[...] = jnp.dot(m16, wv_ref[...], preferred_element_type=F32).astype(BF16)


def _kv(mem, wk, wv):
    b, m, d = mem.shape
    spec = pl.BlockSpec((None, m, d), lambda i: (i, 0, 0))
    return pl.pallas_call(
        _kv_kernel,
        grid=(b,),
        in_specs=[spec, _const_spec(wk.shape), _const_spec(wv.shape)],
        out_specs=[spec, spec],
        out_shape=[jax.ShapeDtypeStruct((b, m, d), BF16)] * 2,
        compiler_params=_params(("parallel",)),
        name="kv",
    )(mem, wk, wv)


def _xattn_kernel(h_ref, k_ref, v_ref, wq_ref, wo_ref, g_ref, b_ref, o_ref):
    hb = h_ref[...]
    qall = jnp.dot(hb.astype(BF16), wq_ref[...], preferred_element_type=F32) * (1.0 / math.sqrt(XA_HEAD_DIM))
    outs = []
    for hd in range(XA_HEADS):
        sl = slice(hd * XA_HEAD_DIM, (hd + 1) * XA_HEAD_DIM)
        s = _dot_nt(qall[:, sl], k_ref[:, sl])
        p = jnp.exp(s - jnp.max(s, axis=-1, keepdims=True))
        o = jnp.dot(p.astype(BF16), v_ref[:, sl], preferred_element_type=F32)
        outs.append(o / jnp.sum(p, axis=-1, keepdims=True))
    o = jnp.concatenate(outs, axis=-1)
    y = jnp.dot(o.astype(BF16), wo_ref[...], preferred_element_type=F32)
    o_ref[...] = _layer_norm(ALPHA * hb + y, g_ref[...], b_ref[...])


def _xattn(h, k, v, wq, wo, g, bb):
    b, l, d = h.shape
    tm = min(TOK_TM, l)
    tok = pl.BlockSpec((None, tm, d), lambda i, j: (i, j, 0))
    kvs = pl.BlockSpec((None, k.shape[1], d), lambda i, j: (i, 0, 0))
    return pl.pallas_call(
        _xattn_kernel,
        grid=(b, l // tm),
        in_specs=[tok, kvs, kvs, _const_spec(wq.shape), _const_spec(wo.shape),
                  _const_spec(g.shape), _const_spec(bb.shape)],
        out_specs=tok,
        out_shape=jax.ShapeDtypeStruct((b, l, d), F32),
        compiler_params=_params(("parallel", "parallel")),
        name="xattn",
    )(h, k, v, wq, wo, g, bb)


def _mlp_kernel(h_ref, w1_ref, w2_ref, g_ref, b_ref, o_ref):
    hb = h_ref[...]
    h16 = hb.astype(BF16)
    acc = ALPHA * hb
    for f in range(D_FF // MLP_FC):
        sl = slice(f * MLP_FC, (f + 1) * MLP_FC)
        t = jnp.maximum(jnp.dot(h16, w1_ref[:, sl], preferred_element_type=F32), 0.0)
        acc = acc + jnp.dot((t * t).astype(BF16), w2_ref[sl, :], preferred_element_type=F32)
    o_ref[...] = _layer_norm(acc, g_ref[...], b_ref[...])


def _mlp(h2d, w1, w2, g, bb):
    t, d = h2d.shape
    tm = min(TOK_TM, t)
    tok = pl.BlockSpec((tm, d), lambda i: (i, 0))
    return pl.pallas_call(
        _mlp_kernel,
        grid=(t // tm,),
        in_specs=[tok, _const_spec(w1.shape), _const_spec(w2.shape), _const_spec(g.shape), _const_spec(bb.shape)],
        out_specs=tok,
        out_shape=jax.ShapeDtypeStruct((t, d), F32),
        compiler_params=_params(("parallel",)),
        name="mlp",
    )(h2d, w1, w2, g, bb)


def _row(v):
    return v.astype(F32).reshape(1, -1)


def _pad_lanes(v2d):
    return jnp.pad(v2d, ((0, 0), (0, LANES - v2d.shape[1])))


def mixer_sublayer(h, w_in, w_out, ssd_conv_w, ssd_conv_b, ssd_dt_bias, ssd_a_log, ssd_d, ssd_norm_w,
                   s5_lam_re, s5_lam_im, s5_log_step, s5_b_re, s5_b_im, s5_c_re, s5_c_im, s5_d, s5_glu_w,
                   s5_glu_b, rg_conv_w, rg_conv_b, rg_wa, rg_ba, rg_wx, rg_bx, rg_lambda, ln_g, ln_b):
    o_z, o_xbc, o_dt = 0, SSD_WIDTH, SSD_WIDTH + SSD_XBC
    o_u = o_dt + SSD_HEADS
    o_xr = o_u + S5_WIDTH
    o_gr = o_xr + RG_WIDTH
    w16 = w_in.astype(BF16)
    wc, vc, ct, lvl, pw = _s5_prepare(s5_lam_re, s5_lam_im, s5_log_step, s5_b_re, s5_b_im, s5_c_re, s5_c_im)
    ys5 = _s5_branch(h, (w16[:, o_u:o_xr], wc, vc, ct, lvl, pw, _row(s5_d),
                         s5_glu_w.astype(BF16), _row(s5_glu_b)))
    consts = (
        w16[:, o_z:o_xbc], w16[:, o_xbc:o_dt], _pad_lanes(w16[:, o_dt:o_u]), w16[:, o_xr:o_gr], w16[:, o_gr:],
        ssd_conv_w.astype(F32), _row(ssd_conv_b), _pad_lanes(_row(ssd_dt_bias)),
        _pad_lanes(_row(-jnp.exp(ssd_a_log.astype(F32)))),
        _row(jnp.repeat(ssd_d.astype(F32), SSD_HEAD_DIM)), _row(ssd_norm_w),
        rg_conv_w.astype(F32), _row(rg_conv_b),
        _block_diag(rg_wa.astype(F32)).astype(BF16), _row(rg_ba),
        _block_diag(rg_wx.astype(F32)).astype(BF16), _row(rg_bx), _row(rg_lambda),
        w_out.astype(BF16), _row(ln_g), _row(ln_b),
    )
    return _mixer(h, ys5, consts)


def xattn_sublayer(h, mem, wq, wk, wv, wo, ln_g, ln_b):
    k, v = _kv(mem, wk.astype(BF16), wv.astype(BF16))
    return _xattn(h, k, v, wq.astype(BF16), wo.astype(BF16), _row(ln_g), _row(ln_b))


def mlp_sublayer(h, w1, w2, ln_g, ln_b):
    b, l, d = h.shape
    return _mlp(h.reshape(b * l, d), w1.astype(BF16), w2.astype(BF16), _row(ln_g), _row(ln_b)).reshape(b, l, d)


def kernel(x, mem, w_in, w_out, ssd_conv_w, ssd_conv_b, ssd_dt_bias, ssd_a_log, ssd_d, ssd_norm_w, s5_lam_re, s5_lam_im, s5_log_step, s5_b_re, s5_b_im, s5_c_re, s5_c_im, s5_d, s5_glu_w, s5_glu_b, rg_conv_w, rg_conv_b, rg_wa, rg_ba, rg_wx, rg_bx, rg_lambda, ln1_g, ln1_b, xa_wq, xa_wk, xa_wv, xa_wo, ln2_g, ln2_b, mlp_w1, mlp_w2, ln3_g, ln3_b):
    h = x.astype(F32)
    memf = mem.astype(F32)
    for l in range(DEPTH):
        h = mixer_sublayer(h, w_in[l], w_out[l], ssd_conv_w[l], ssd_conv_b[l], ssd_dt_bias[l], ssd_a_log[l],
                           ssd_d[l], ssd_norm_w[l], s5_lam_re[l], s5_lam_im[l], s5_log_step[l], s5_b_re[l],
                           s5_b_im[l], s5_c_re[l], s5_c_im[l], s5_d[l], s5_glu_w[l], s5_glu_b[l],
                           rg_conv_w[l], rg_conv_b[l], rg_wa[l], rg_ba[l], rg_wx[l], rg_bx[l], rg_lambda[l],
                           ln1_g[l], ln1_b[l])
        h = xattn_sublayer(h, memf, xa_wq[l], xa_wk[l], xa_wv[l], xa_wo[l], ln2_g[l], ln2_b[l])
        h = mlp_sublayer(h, mlp_w1[l], mlp_w2[l], ln3_g[l], ln3_b[l])
    return h.astype(x.dtype)
```

```python
import collections
import functools
import math

import jax
import jax.numpy as jnp
from jax import lax
from jax.experimental import pallas as pl
from jax.experimental.pallas import tpu as pltpu

F32 = jnp.float32
BF16 = jnp.bfloat16

D_MODEL = 1024
DEPTH = 2
SSD_WIDTH = 512
SSD_HEAD_DIM = 64
SSD_HEADS = 8
SSD_GROUPS = 2
SSD_HPG = SSD_HEADS // SSD_GROUPS
SSD_GW = SSD_HPG * SSD_HEAD_DIM
SSD_STATE = 128
SSD_XBC = 1024
CONV_K = 4
S5_WIDTH = 256
S5_GROUPS = 16
S5_GROUP_CH = 16
S5_STATE = 64
S5_NSTATE = S5_GROUPS * S5_STATE
S5_BLK = 8
RG_WIDTH = 256
RG_BLOCKS = 4
RG_C = 8.0
XA_HEADS = 4
XA_HEAD_DIM = 256
D_FF = 4096
ALPHA = (2.0 * DEPTH) ** 0.25
LN_EPS = 1e-5

LANES = 128
SUBLANES = 8
MXU_TILE = 256
NEG_BIG = -1e30
VMEM_LIMIT = 56 * 1024 * 1024

SSD_CHUNK = 128
MIX_TQ = 256
S5_TB = 2048
TOK_TM = 512
MLP_FC = 1024


def _dot(a, b):
    return jnp.dot(a.astype(BF16), b.astype(BF16), preferred_element_type=F32)


def _dot_nt(a, b):
    return lax.dot_general(a.astype(BF16), b.astype(BF16), (((1,), (1,)), ((), ())),
                           preferred_element_type=F32)


def _layer_norm(x, g, b):
    mu = jnp.mean(x, axis=-1, keepdims=True)
    xc = x - mu
    var = jnp.mean(xc * xc, axis=-1, keepdims=True)
    return xc * lax.rsqrt(var + LN_EPS) * g + b


def _gelu(x):
    c = math.sqrt(2.0 / math.pi)
    return 0.5 * x * (1.0 + jnp.tanh(c * (x + 0.044715 * (x * x * x))))


def _sigmoid(x):
    return 0.5 + 0.5 * jnp.tanh(0.5 * x)


def _silu(x):
    hx = 0.5 * x
    return hx + hx * jnp.tanh(hx)


def _softplus(x):
    return jnp.maximum(x, 0.0) + jnp.log1p(jnp.exp(-jnp.abs(x)))


def _const_spec(shape):
    nd = len(shape)
    return pl.BlockSpec(shape, lambda *_: (0,) * nd, pipeline_mode=pl.Buffered(1))


def _params(sem):
    return pltpu.CompilerParams(dimension_semantics=sem, vmem_limit_bytes=VMEM_LIMIT)


def _s5_expand_operators(wc_ref, vc_ref, ct_ref, ms_scr, krev_scr, mo_scr):
    n, w, c = S5_NSTATE, S5_WIDTH, S5_GROUP_CH
    tile_rows = w // c
    grp_r = lax.broadcasted_iota(jnp.int32, (w, 2 * n), 0) // c
    grp_c = (lax.broadcasted_iota(jnp.int32, (w, 2 * n), 1) % n) // S5_STATE
    same_s = grp_r == grp_c
    grp_r = (lax.broadcasted_iota(jnp.int32, (2 * n, w), 0) % n) // S5_STATE
    lane = lax.broadcasted_iota(jnp.int32, (2 * n, w), 1)
    same_o = grp_r == lane // c
    mo0 = jnp.where(same_o, ct_ref[...], 0.0)
    kc = jnp.dot(wc_ref[...], mo0, preferred_element_type=F32, precision=lax.Precision.HIGHEST)
    same_k = (lax.broadcasted_iota(jnp.int32, (w, w), 0) // c) == (lax.broadcasted_iota(jnp.int32, (w, w), 1) // c)
    sel_row = lax.broadcasted_iota(jnp.int32, (S5_BLK * c, w), 0)
    sel_lane = lax.broadcasted_iota(jnp.int32, (S5_BLK * c, w), 1)
    vc16 = vc_ref[...].astype(BF16)
    for t in range(S5_BLK):
        rs = slice(t * w, (t + 1) * w)
        ms_scr[rs, :] = jnp.where(same_s, jnp.tile(wc_ref[t * c:(t + 1) * c, :], (tile_rows, 1)), 0.0).astype(BF16)
        krev_scr[rs, :] = jnp.where(same_k, jnp.tile(kc[t * c:(t + 1) * c, :], (tile_rows, 1)), 0.0).astype(BF16)
        pick = ((sel_row // c == t) & (sel_row % c == sel_lane % c)).astype(BF16)
        spread = jnp.dot(vc16, pick, preferred_element_type=F32)
        mo_scr[:, rs] = jnp.where(same_o, spread, 0.0).astype(BF16)


def _s5_kernel(h_ref, wu_ref, wc_ref, vc_ref, ct_ref, lvl_ref, pw_ref, d_ref, gluw_ref, glub_ref, o_ref,
               ms_scr, krev_scr, mo_scr, u_scr, hbuf, *, rows):
    n, w = S5_NSTATE, S5_WIDTH

    @pl.when((pl.program_id(0) == 0) & (pl.program_id(1) == 0))
    def _():
        _s5_expand_operators(wc_ref, vc_ref, ct_ref, ms_scr, krev_scr, mo_scr)

    @pl.when(pl.program_id(1) == 0)
    def _():
        hbuf[0:SUBLANES, :] = jnp.zeros((SUBLANES, 2 * n), F32)

    halves = w // LANES
    u_tok = jnp.dot(h_ref[...].astype(BF16), wu_ref[...], preferred_element_type=F32)
    for hf in range(halves):
        u_scr[hf] = u_tok[:, hf * LANES:(hf + 1) * LANES]
    u = jnp.concatenate([u_scr[hf, pl.ds(k, rows, stride=S5_BLK), :]
                         for k in range(S5_BLK) for hf in range(halves)], axis=-1)
    u16 = u.astype(BF16)
    s = jnp.dot(u16, ms_scr[...], preferred_element_type=F32)
    xr = s[:, :n]
    xi = s[:, n:]
    row8 = lax.broadcasted_iota(jnp.int32, (rows, n), 0) & (SUBLANES - 1)
    for lv, dist in enumerate((1, 2, 4)):
        lr = lvl_ref[lv:lv + 1, :n]
        li = lvl_ref[lv:lv + 1, n:]
        keep = row8 >= dist
        sr = jnp.where(keep, pltpu.roll(xr, dist, 0), 0.0)
        si = jnp.where(keep, pltpu.roll(xi, dist, 0), 0.0)
        xr, xi = xr + lr * sr - li * si, xi + lr * si + li * sr
    hbuf[SUBLANES:SUBLANES + rows, :n] = xr
    hbuf[SUBLANES:SUBLANES + rows, n:] = xi

    pr = pw_ref[:, :n]
    pi = pw_ref[:, n:]

    def tile_carry(t, carry):
        cr, ci = carry
        r0 = pl.multiple_of(SUBLANES + t * SUBLANES, SUBLANES)
        tr = hbuf[pl.ds(r0, SUBLANES), 0:n]
        ti = hbuf[pl.ds(r0, SUBLANES), n:2 * n]
        tr, ti = tr + pr * cr - pi * ci, ti + pr * ci + pi * cr
        hbuf[pl.ds(r0, SUBLANES), 0:n] = tr
        hbuf[pl.ds(r0, SUBLANES), n:2 * n] = ti
        return tr[SUBLANES - 1:SUBLANES, :], ti[SUBLANES - 1:SUBLANES, :]

    c0 = (hbuf[SUBLANES - 1:SUBLANES, 0:n], hbuf[SUBLANES - 1:SUBLANES, n:2 * n])
    lax.fori_loop(0, rows // SUBLANES, tile_carry, c0)

    hprev16 = hbuf[SUBLANES - 1:SUBLANES - 1 + rows, :].astype(BF16)
    hbuf[0:SUBLANES, :] = hbuf[rows:rows + SUBLANES, :]
    for t in range(S5_BLK):
        cols = slice(t * w, (t + 1) * w)
        y = jnp.dot(u16[:, :(t + 1) * w], krev_scr[(S5_BLK - 1 - t) * w:, :], preferred_element_type=F32)
        y = y + jnp.dot(hprev16, mo_scr[:, cols], preferred_element_type=F32)
        y = _gelu(y + d_ref[...] * u[:, cols])
        for hf in range(halves):
            u_scr[hf, pl.ds(t, rows, stride=S5_BLK), :] = y[:, hf * LANES:(hf + 1) * LANES]
    yg = jnp.concatenate([u_scr[hf] for hf in range(halves)], axis=-1)
    glu = _sigmoid(_dot(yg, gluw_ref[...]) + glub_ref[...])
    o_ref[...] = (yg * glu).astype(o_ref.dtype)


def _s5_branch(h, consts):
    b, l, d = h.shape
    tb = min(S5_TB, l)
    rows = tb // S5_BLK
    n, w = S5_NSTATE, S5_WIDTH
    return pl.pallas_call(
        functools.partial(_s5_kernel, rows=rows),
        grid=(b, l // tb),
        in_specs=[pl.BlockSpec((None, tb, d), lambda i, j: (i, j, 0))] + [_const_spec(c.shape) for c in consts],
        out_specs=pl.BlockSpec((None, tb, w), lambda i, j: (i, j, 0)),
        out_shape=jax.ShapeDtypeStruct((b, l, w), BF16),
        scratch_shapes=[
            pltpu.VMEM((S5_BLK * w, 2 * n), BF16),
            pltpu.VMEM((S5_BLK * w, w), BF16),
            pltpu.VMEM((2 * n, S5_BLK * w), BF16),
            pltpu.VMEM((w // LANES, tb, LANES), F32),
            pltpu.VMEM((rows + SUBLANES, 2 * n), F32),
        ],
        compiler_params=_params(("arbitrary", "arbitrary")),
        name="s5_branch",
    )(h, *consts)


def _s5_prepare(lam_re, lam_im, log_step, b_re, b_im, c_re, c_im):
    g, p, c, t8 = S5_GROUPS, S5_STATE, S5_GROUP_CH, S5_BLK
    lr, li = lam_re.astype(F32), lam_im.astype(F32)
    dt = jnp.exp(log_step.astype(F32))[:, None]
    ar, ai = lr * dt, li * dt

    def lam_pow(k):
        k = k.astype(F32)[..., None, None]
        mag = jnp.exp(k * ar)
        return mag * jnp.cos(k * ai), mag * jnp.sin(k * ai)

    br1, bi1 = lam_pow(jnp.ones((), jnp.int32))
    xr_, xi_ = br1 - 1.0, bi1
    den = lr * lr + li * li
    cr, ci = (xr_ * lr + xi_ * li) / den, (xi_ * lr - xr_ * li) / den
    bre, bim = b_re.astype(F32), b_im.astype(F32)
    bbr = cr[..., None] * bre - ci[..., None] * bim
    bbi = cr[..., None] * bim + ci[..., None] * bre
    steps = jnp.arange(t8)
    pr, pi = lam_pow(t8 - 1 - steps)
    wsr = pr[..., None] * bbr[None] - pi[..., None] * bbi[None]
    wsi = pr[..., None] * bbi[None] + pi[..., None] * bbr[None]
    wc = jnp.concatenate([part.transpose(0, 3, 1, 2).reshape(t8 * c, g * p) for part in (wsr, wsi)], axis=1)
    cre, cim = c_re.astype(F32), c_im.astype(F32)
    pr, pi = lam_pow(steps + 1)
    vor = cre[None] * pr[:, :, None, :] - cim[None] * pi[:, :, None, :]
    voi = cre[None] * pi[:, :, None, :] + cim[None] * pr[:, :, None, :]
    vc = jnp.concatenate([part.transpose(1, 3, 0, 2).reshape(g * p, t8 * c) for part in (vor, -voi)], axis=0)
    ct = jnp.concatenate([jnp.tile(part.transpose(0, 2, 1).reshape(g * p, c), (1, g)) for part in (cre, -cim)],
                         axis=0)

    def re_im(pair):
        return jnp.concatenate([pair[0].reshape(-1, g * p), pair[1].reshape(-1, g * p)], axis=1)

    lvl = re_im(lam_pow(t8 * jnp.array([1, 2, 4, 0, 0, 0, 0, 0])))
    pw = re_im(lam_pow(t8 * (steps + 1)))
    return wc, vc, ct, lvl, pw


def _expand_heads(cols, grp, head_lane):
    base = SSD_HPG * grp
    out = jnp.broadcast_to(cols[:, base:base + 1], (cols.shape[0], SSD_GW))
    for hh in range(1, SSD_HPG):
        out = jnp.where(head_lane == hh, cols[:, base + hh:base + hh + 1], out)
    return out


def _causal_conv(pad_ref, w_ref, b_ref, rows, cols):
    acc = b_ref[:, cols] + w_ref[CONV_K - 1:CONV_K, cols] * pad_ref[SUBLANES:SUBLANES + rows, cols]
    for k in range(CONV_K - 1):
        off = SUBLANES - (CONV_K - 1) + k
        acc = acc + w_ref[k:k + 1, cols] * pad_ref[off:off + rows, cols]
    return acc


_ProjBuf = collections.namedtuple("_ProjBuf", "z xpad dt rpad gr")
_MixW = collections.namedtuple(
    "_MixW", "wz wxbc wdt wxr wgr cw cb dtb aneg dvec nw rcw rcb wa ba wx bx rlam wout g b")
_MixScratch = collections.namedtuple("_MixScratch", "state ascr bscr cexp rcar")


def _col_tiles(width):
    return [slice(c, min(c + MXU_TILE, width)) for c in range(0, width, MXU_TILE)]


def _mixer_output_jobs(h_ref, ycat, w, o_ref):
    def tile_job(cols):
        def run():
            y = jnp.dot(ycat[...], w.wout[:, cols], preferred_element_type=F32)
            o_ref[:, cols] = ALPHA * h_ref[:, cols] + y
        return run

    def norm_job():
        o_ref[...] = _layer_norm(o_ref[...], w.g[...], w.b[...])

    return [tile_job(cols) for cols in _col_tiles(D_MODEL)] + [norm_job]


def _mixer_project_jobs(h_ref, h16, w, nxt, cur, first, tq):
    def cast_job():
        h16[...] = h_ref[...].astype(BF16)

    def tile_job(dst, w_ref, cols, row0):
        def run():
            dst[row0:row0 + tq, cols] = jnp.dot(h16[...], w_ref[:, cols], preferred_element_type=F32)
        return run

    def tail_job(pad_n, pad_c):
        def run():
            pad_n[0:SUBLANES, :] = jnp.where(first, 0.0, pad_c[tq:tq + SUBLANES, :])
        return run

    jobs = [cast_job]
    for dst, w_ref, row0 in ((nxt.z, w.wz, 0), (nxt.xpad, w.wxbc, SUBLANES), (nxt.dt, w.wdt, 0),
                             (nxt.rpad, w.wxr, SUBLANES), (nxt.gr, w.wgr, 0)):
        jobs += [tile_job(dst, w_ref, cols, row0) for cols in _col_tiles(dst.shape[1])]
    jobs += [tail_job(nxt.xpad, cur.xpad), tail_job(nxt.rpad, cur.rpad)]
    return jobs


def _mixer_compute(cur, ys5_ref, w, scr, ycat, first, tq, chunk, side):
    cw_ref, cb_ref, dtb_ref, aneg_ref, dvec_ref, nw_ref = w.cw, w.cb, w.dtb, w.aneg, w.dvec, w.nw
    state = scr.state
    z = cur.z
    slabs = []
    for cols in _col_tiles(SSD_XBC):
        side(1)
        slabs.append(_silu(_causal_conv(cur.xpad, cw_ref, cb_ref, tq, cols)))
    xact = jnp.concatenate(slabs, axis=-1)
    side(2)
    dt_all = _softplus(cur.dt[...] + dtb_ref[...])
    q = chunk
    tri = (lax.broadcasted_iota(jnp.int32, (q, q), 0) >= lax.broadcasted_iota(jnp.int32, (q, q), 1))
    row_id = lax.broadcasted_iota(jnp.int32, (q, LANES), 0)
    head_lane = lax.broadcasted_iota(jnp.int32, (1, SSD_GW), 1) // SSD_HEAD_DIM
    for ci in range(tq // q):
        r0 = ci * q
        dt = dt_all[r0:r0 + q, :]
        cs = dt * aneg_ref[...]
        dist = 1
        while dist < q:
            cs = cs + jnp.where(row_id >= dist, pltpu.roll(cs, dist, 0), 0.0)
            dist *= 2
        cs_t = cs.T
        cs_last = cs[q - 1:q, :]
        chunk_decay = jnp.exp(cs_last)
        dec_t = jnp.exp(cs_t[:, q - 1:q] - cs_t)
        ecs = jnp.exp(cs)
        zc = z[r0:r0 + q, :]
        halves = []
        for grp in range(SSD_GROUPS):
            side(2)
            lo = grp * SSD_GW
            xs = xact[r0:r0 + q, lo:lo + SSD_GW]
            bm = xact[r0:r0 + q, SSD_WIDTH + grp * SSD_STATE:SSD_WIDTH + (grp + 1) * SSD_STATE]
            cm = xact[r0:r0 + q, SSD_WIDTH + (SSD_GROUPS + grp) * SSD_STATE:
                      SSD_WIDTH + (SSD_GROUPS + grp + 1) * SSD_STATE]
            cm16 = cm.astype(BF16)
            xdt16 = (xs * _expand_heads(dt, grp, head_lane)).astype(BF16)
            cb = _dot_nt(cm16, bm)
            bm_t = bm.T
            s_prev = state[grp]
            if ci == 0:
                s_prev = jnp.where(first, 0.0, s_prev)
            y_off = jnp.dot(cm16, s_prev.astype(BF16), preferred_element_type=F32)
            intra, to_state, xdt_heads = [], [], []
            for hh in range(SSD_HPG):
                hd = grp * SSD_HPG + hh
                seg = cs[:, hd:hd + 1] - cs_t[hd:hd + 1, :]
                lmat = jnp.exp(jnp.where(tri, seg, NEG_BIG))
                intra.append((cb * lmat).astype(BF16))
                to_state.append((bm_t * dec_t[hd:hd + 1, :]).astype(BF16))
                xdt_heads.append(jnp.where(head_lane == hh, xdt16, jnp.zeros_like(xdt16)))
            lhs = jnp.concatenate([jnp.concatenate(intra, axis=1), jnp.concatenate(to_state, axis=1)], axis=0)
            both = jnp.dot(lhs, jnp.concatenate(xdt_heads, axis=0), preferred_element_type=F32)
            y_diag = both[:q]
            st_new = both[q:]
            state[grp] = s_prev * _expand_heads(chunk_decay, grp, head_lane) + st_new
            halves.append(y_diag + y_off * _expand_heads(ecs, grp, head_lane) + xs * dvec_ref[:, lo:lo + SSD_GW])
        y = jnp.concatenate(halves, axis=-1) * _silu(zc)
        y = y * lax.rsqrt(jnp.mean(y * y, axis=-1, keepdims=True) + LN_EPS) * nw_ref[...]
        ycat[r0:r0 + q, 0:SSD_WIDTH] = y.astype(BF16)

    ycat[:, SSD_WIDTH:SSD_WIDTH + S5_WIDTH] = ys5_ref[...]

    side(1)
    xc = _causal_conv(cur.rpad, w.rcw, w.rcb, tq, slice(0, RG_WIDTH))
    xc16 = xc.astype(BF16)
    rgate = _sigmoid(jnp.dot(xc16, w.wa[...], preferred_element_type=F32) + w.ba[...])
    igate = _sigmoid(jnp.dot(xc16, w.wx[...], preferred_element_type=F32) + w.bx[...])
    log_a = (-RG_C) * rgate * _softplus(-w.rlam[...])
    a = jnp.exp(log_a)
    bt = jnp.sqrt(1.0 - jnp.exp(2.0 * log_a)) * (igate * xc)
    row8 = lax.broadcasted_iota(jnp.int32, (tq, RG_WIDTH), 0) & (SUBLANES - 1)
    side(1)
    for dist in (1, 2, 4):
        keep = row8 >= dist
        a_sh = jnp.where(keep, pltpu.roll(a, dist, 0), 1.0)
        b_sh = jnp.where(keep, pltpu.roll(bt, dist, 0), 0.0)
        bt = bt + a * b_sh
        a = a * a_sh
    nt = tq // SUBLANES
    lane_halves = RG_WIDTH // LANES
    for hf in range(lane_halves):
        scr.ascr[hf] = a[:, hf * LANES:(hf + 1) * LANES]
        scr.bscr[hf] = bt[:, hf * LANES:(hf + 1) * LANES]
    ae = jnp.concatenate([scr.ascr[hf, pl.ds(SUBLANES - 1, nt, stride=SUBLANES), :] for hf in range(lane_halves)], -1)
    be = jnp.concatenate([scr.bscr[hf, pl.ds(SUBLANES - 1, nt, stride=SUBLANES), :] for hf in range(lane_halves)], -1)
    tile_id = lax.broadcasted_iota(jnp.int32, (nt, RG_WIDTH), 0)
    dist = 1
    while dist < nt:
        keep = tile_id >= dist
        a_sh = jnp.where(keep, pltpu.roll(ae, dist, 0), 1.0)
        b_sh = jnp.where(keep, pltpu.roll(be, dist, 0), 0.0)
        be = be + ae * b_sh
        ae = ae * a_sh
        dist *= 2
    carry_in = jnp.where(first, 0.0, scr.rcar[0:1, :])
    tile_end = be + ae * carry_in
    scr.rcar[0:1, :] = tile_end[nt - 1:nt, :]
    tile_in = jnp.where(tile_id >= 1, pltpu.roll(tile_end, 1, 0), carry_in)
    for hf in range(lane_halves):
        for k in range(SUBLANES):
            scr.cexp[hf, pl.ds(k, nt, stride=SUBLANES), :] = tile_in[:, hf * LANES:(hf + 1) * LANES]
    hscan = bt + a * jnp.concatenate([scr.cexp[hf] for hf in range(lane_halves)], axis=-1)
    ycat[:, SSD_WIDTH + S5_WIDTH:] = (hscan * _gelu(cur.gr[...])).astype(BF16)


def _mixer_kernel(hp_ref, ho_ref, ys5_ref, *rest, tq, chunk, nblk):
    nw = len(_MixW._fields)
    w = _MixW(*rest[:nw])
    o_ref = rest[nw]
    bufs = rest[nw + 1:]
    npb = len(_ProjBuf._fields)
    sets = (_ProjBuf(*bufs[:npb]), _ProjBuf(*bufs[npb:2 * npb]))
    ycats = bufs[2 * npb:2 * npb + 2]
    scr = _MixScratch(*bufs[2 * npb + 2:2 * npb + 2 + len(_MixScratch._fields)])
    h16 = bufs[-1]
    s = pl.program_id(0)

    @pl.when(s == 0)
    def _():
        for ref in bufs:
            ref[...] = jnp.zeros(ref.shape, ref.dtype)

    first_p = (s % nblk) == 0
    first_c = ((s + nblk - 1) % nblk) == 0

    def step(nxt, cur, ycat_c, ycat_o):
        jobs = (_mixer_output_jobs(ho_ref, ycat_o, w, o_ref)
                + _mixer_project_jobs(hp_ref, h16, w, nxt, cur, first_p, tq))

        def side(n):
            for _ in range(min(n, len(jobs))):
                jobs.pop(0)()

        _mixer_compute(cur, ys5_ref, w, scr, ycat_c, first_c, tq, chunk, side)
        side(len(jobs))

    @pl.when(s % 2 == 0)
    def _():
        step(sets[0], sets[1], ycats[1], ycats[0])

    @pl.when(s % 2 == 1)
    def _():
        step(sets[1], sets[0], ycats[0], ycats[1])


def _mixer(h, ys5, consts):
    b, l, d = h.shape
    tq = min(MIX_TQ, l)
    chunk = min(SSD_CHUNK, tq)
    nblk = l // tq
    total = b * nblk
    h2d = h.reshape(b * l, d)
    proj_buf = [
        pltpu.VMEM((tq, SSD_WIDTH), F32),
        pltpu.VMEM((tq + SUBLANES, SSD_XBC), F32),
        pltpu.VMEM((tq, LANES), F32),
        pltpu.VMEM((tq + SUBLANES, RG_WIDTH), F32),
        pltpu.VMEM((tq, RG_WIDTH), F32),
    ]
    rg_plane = pltpu.VMEM((RG_WIDTH // LANES, tq, LANES), F32)
    out = pl.pallas_call(
        functools.partial(_mixer_kernel, tq=tq, chunk=chunk, nblk=nblk),
        grid=(total + 2,),
        in_specs=[pl.BlockSpec((tq, d), lambda s: (jnp.minimum(s, total - 1), 0)),
                  pl.BlockSpec((tq, d), lambda s: (jnp.maximum(s - 2, 0), 0)),
                  pl.BlockSpec((tq, S5_WIDTH), lambda s: (jnp.clip(s - 1, 0, total - 1), 0))]
                 + [_const_spec(c.shape) for c in consts],
        out_specs=pl.BlockSpec((tq, d), lambda s: (jnp.maximum(s - 2, 0), 0)),
        out_shape=jax.ShapeDtypeStruct((b * l, d), F32),
        scratch_shapes=proj_buf + proj_buf + [
            pltpu.VMEM((tq, D_MODEL), BF16),
            pltpu.VMEM((tq, D_MODEL), BF16),
            pltpu.VMEM((SSD_GROUPS, SSD_STATE, SSD_GW), F32),
            rg_plane, rg_plane, rg_plane,
            pltpu.VMEM((SUBLANES, RG_WIDTH), F32),
            pltpu.VMEM((tq, d), BF16),
        ],
        compiler_params=_params(("arbitrary",)),
        name="mixer",
    )(h2d, h2d, ys5.reshape(b * l, S5_WIDTH), *consts)
    return out.reshape(b, l, d)


def _block_diag(w):
    nb, bi, bo = w.shape
    return jnp.einsum('hij,hk->hikj', w, jnp.eye(nb, dtype=w.dtype)).reshape(nb * bi, nb * bo)


def _kv_kernel(m_ref, wk_ref, wv_ref, k_ref, v_ref):
    m16 = m_ref[...].astype(BF16)
    k_ref[...] = jnp.dot(m16, wk_ref[...], preferred_element_type=F32).astype(BF16)
    v_ref[...] = jnp.dot(m16, wv_ref[...], preferred_element_type=F32).astype(BF16)


def _kv(mem, wk, wv):
    b, m, d = mem.shape
    spec = pl.BlockSpec((None, m, d), lambda i: (i, 0, 0))
    return pl.pallas_call(
        _kv_kernel,
        grid=(b,),
        in_specs=[spec, _const_spec(wk.shape), _const_spec(wv.shape)],
        out_specs=[spec, spec],
        out_shape=[jax.ShapeDtypeStruct((b, m, d), BF16)] * 2,
        compiler_params=_params(("parallel",)),
        name="kv",
    )(mem, wk, wv)


def _xattn_kernel(h_ref, k_ref, v_ref, wq_ref, wo_ref, g_ref, b_ref, o_ref):
    hb = h_ref[...]
    qall = jnp.dot(hb.astype(BF16), wq_ref[...], preferred_element_type=F32) * (1.0 / math.sqrt(XA_HEAD_DIM))
    outs = []
    for hd in range(XA_HEADS):
        sl = slice(hd * XA_HEAD_DIM, (hd + 1) * XA_HEAD_DIM)
        s = _dot_nt(qall[:, sl], k_ref[:, sl])
        p = jnp.exp(s - jnp.max(s, axis=-1, keepdims=True))
        o = jnp.dot(p.astype(BF16), v_ref[:, sl], preferred_element_type=F32)
        outs.append(o / jnp.sum(p, axis=-1, keepdims=True))
    o = jnp.concatenate(outs, axis=-1)
    y = jnp.dot(o.astype(BF16), wo_ref[...], preferred_element_type=F32)
    o_ref[...] = _layer_norm(ALPHA * hb + y, g_ref[...], b_ref[...])


def _xattn(h, k, v, wq, wo, g, bb):
    b, l, d = h.shape
    tm = min(TOK_TM, l)
    tok = pl.BlockSpec((None, tm, d), lambda i, j: (i, j, 0))
    kvs = pl.BlockSpec((None, k.shape[1], d), lambda i, j: (i, 0, 0))
    return pl.pallas_call(
        _xattn_kernel,
        grid=(b, l // tm),
        in_specs=[tok, kvs, kvs, _const_spec(wq.shape), _const_spec(wo.shape),
                  _const_spec(g.shape), _const_spec(bb.shape)],
        out_specs=tok,
        out_shape=jax.ShapeDtypeStruct((b, l, d), F32),
        compiler_params=_params(("parallel", "parallel")),
        name="xattn",
    )(h, k, v, wq, wo, g, bb)


def _mlp_kernel(h_ref, w1_ref, w2_ref, g_ref, b_ref, o_ref):
    hb = h_ref[...]
    h16 = hb.astype(BF16)
    acc = ALPHA * hb
    for f in range(D_FF // MLP_FC):
        sl = slice(f * MLP_FC, (f + 1) * MLP_FC)
        t = jnp.maximum(jnp.dot(h16, w1_ref[:, sl], preferred_element_type=F32), 0.0)
        acc = acc + jnp.dot((t * t).astype(BF16), w2_ref[sl, :], preferred_element_type=F32)
    o_ref[...] = _layer_norm(acc, g_ref[...], b_ref[...])


def _mlp(h2d, w1, w2, g, bb):
    t, d = h2d.shape
    tm = min(TOK_TM, t)
    tok = pl.BlockSpec((tm, d), lambda i: (i, 0))
    return pl.pallas_call(
        _mlp_kernel,
        grid=(t // tm,),
        in_specs=[tok, _const_spec(w1.shape), _const_spec(w2.shape), _const_spec(g.shape), _const_spec(bb.shape)],
        out_specs=tok,
        out_shape=jax.ShapeDtypeStruct((t, d), F32),
        compiler_params=_params(("parallel",)),
        name="mlp",
    )(h2d, w1, w2, g, bb)


def _row(v):
    return v.astype(F32).reshape(1, -1)


def _pad_lanes(v2d):
    return jnp.pad(v2d, ((0, 0), (0, LANES - v2d.shape[1])))


def mixer_sublayer(h, w_in, w_out, ssd_conv_w, ssd_conv_b, ssd_dt_bias, ssd_a_log, ssd_d, ssd_norm_w,
                   s5_lam_re, s5_lam_im, s5_log_step, s5_b_re, s5_b_im, s5_c_re, s5_c_im, s5_d, s5_glu_w,
                   s5_glu_b, rg_conv_w, rg_conv_b, rg_wa, rg_ba, rg_wx, rg_bx, rg_lambda, ln_g, ln_b):
    o_z, o_xbc, o_dt = 0, SSD_WIDTH, SSD_WIDTH + SSD_XBC
    o_u = o_dt + SSD_HEADS
    o_xr = o_u + S5_WIDTH
    o_gr = o_xr + RG_WIDTH
    w16 = w_in.astype(BF16)
    wc, vc, ct, lvl, pw = _s5_prepare(s5_lam_re, s5_lam_im, s5_log_step, s5_b_re, s5_b_im, s5_c_re, s5_c_im)
    ys5 = _s5_branch(h, (w16[:, o_u:o_xr], wc, vc, ct, lvl, pw, _row(s5_d),
                         s5_glu_w.astype(BF16), _row(s5_glu_b)))
    consts = (
        w16[:, o_z:o_xbc], w16[:, o_xbc:o_dt], _pad_lanes(w16[:, o_dt:o_u]), w16[:, o_xr:o_gr], w16[:, o_gr:],
        ssd_conv_w.astype(F32), _row(ssd_conv_b), _pad_lanes(_row(ssd_dt_bias)),
        _pad_lanes(_row(-jnp.exp(ssd_a_log.astype(F32)))),
        _row(jnp.repeat(ssd_d.astype(F32), SSD_HEAD_DIM)), _row(ssd_norm_w),
        rg_conv_w.astype(F32), _row(rg_conv_b),
        _block_diag(rg_wa.astype(F32)).astype(BF16), _row(rg_ba),
        _block_diag(rg_wx.astype(F32)).astype(BF16), _row(rg_bx), _row(rg_lambda),
        w_out.astype(BF16), _row(ln_g), _row(ln_b),
    )
    return _mixer(h, ys5, consts)


def xattn_sublayer(h, mem, wq, wk, wv, wo, ln_g, ln_b):
    k, v = _kv(mem, wk.astype(BF16), wv.astype(BF16))
    return _xattn(h, k, v, wq.astype(BF16), wo.astype(BF16), _row(ln_g), _row(ln_b))


def mlp_sublayer(h, w1, w2, ln_g, ln_b):
    b, l, d = h.shape
    return _mlp(h.reshape(b * l, d), w1.astype(BF16), w2.astype(BF16), _row(ln_g), _row(ln_b)).reshape(b, l, d)


def kernel(x, mem, w_in, w_out, ssd_conv_w, ssd_conv_b, ssd_dt_bias, ssd_a_log, ssd_d, ssd_norm_w, s5_lam_re, s5_lam_im, s5_log_step, s5_b_re, s5_b_im, s5_c_re, s5_c_im, s5_d, s5_glu_w, s5_glu_b, rg_conv_w, rg_conv_b, rg_wa, rg_ba, rg_wx, rg_bx, rg_lambda, ln1_g, ln1_b, xa_wq, xa_wk, xa_wv, xa_wo, ln2_g, ln2_b, mlp_w1, mlp_w2, ln3_g, ln3_b):
    h = x.astype(F32)
    memf = mem.astype(F32)
    for l in range(DEPTH):
        h = mixer_sublayer(h, w_in[l], w_out[l], ssd_conv_w[l], ssd_conv_b[l], ssd_dt_bias[l], ssd_a_log[l],
                           ssd_d[l], ssd_norm_w[l], s5_lam_re[l], s5_lam_im[l], s5_log_step[l], s5_b_re[l],
                           s5_b_im[l], s5_c_re[l], s5_c_im[l], s5_d[l], s5_glu_w[l], s5_glu_b[l],
                           rg_conv_w[l], rg_conv_b[l], rg_wa[l], rg_ba[l], rg_wx[l], rg_bx[l], rg_lambda[l],
                           ln1_g[l], ln1_b[l])
        h = xattn_sublayer(h, memf, xa_wq[l], xa_wk[l], xa_wv[l], xa_wo[l], ln2_g[l], ln2_b[l])
        h = mlp_sublayer(h, mlp_w1[l], mlp_w2[l], ln3_g[l], ln3_b[l])
    return h.astype(x.dtype)
```

```python
import collections
import functools
import math

import jax
import jax.numpy as jnp
from jax import lax
from jax.experimental import pallas as pl
from jax.experimental.pallas import tpu as pltpu

F32 = jnp.float32
BF16 = jnp.bfloat16

D_MODEL = 1024
DEPTH = 2
SSD_WIDTH = 512
SSD_HEAD_DIM = 64
SSD_HEADS = 8
SSD_GROUPS = 2
SSD_HPG = SSD_HEADS // SSD_GROUPS
SSD_GW = SSD_HPG * SSD_HEAD_DIM
SSD_STATE = 128
SSD_XBC = 1024
CONV_K = 4
S5_WIDTH = 256
S5_GROUPS = 16
S5_GROUP_CH = 16
S5_STATE = 64
S5_NSTATE = S5_GROUPS * S5_STATE
S5_BLK = 8
RG_WIDTH = 256
RG_BLOCKS = 4
RG_C = 8.0
XA_HEADS = 4
XA_HEAD_DIM = 256
D_FF = 4096
ALPHA = (2.0 * DEPTH) ** 0.25
LN_EPS = 1e-5

LANES = 128
SUBLANES = 8
MXU_TILE = 256
NEG_BIG = -1e30
VMEM_LIMIT = 56 * 1024 * 1024

SSD_CHUNK = 128
MIX_TQ = 256
S5_TB = 1024
TOK_TM = 512
MLP_FC = 1024


def _dot(a, b):
    return jnp.dot(a.astype(BF16), b.astype(BF16), preferred_element_type=F32)


def _dot_nt(a, b):
    return lax.dot_general(a.astype(BF16), b.astype(BF16), (((1,), (1,)), ((), ())),
                           preferred_element_type=F32)


def _layer_norm(x, g, b):
    mu = jnp.mean(x, axis=-1, keepdims=True)
    xc = x - mu
    var = jnp.mean(xc * xc, axis=-1, keepdims=True)
    return xc * lax.rsqrt(var + LN_EPS) * g + b


def _gelu(x):
    c = math.sqrt(2.0 / math.pi)
    return 0.5 * x * (1.0 + jnp.tanh(c * (x + 0.044715 * (x * x * x))))


def _sigmoid(x):
    return 0.5 + 0.5 * jnp.tanh(0.5 * x)


def _silu(x):
    hx = 0.5 * x
    return hx + hx * jnp.tanh(hx)


def _softplus(x):
    return jnp.maximum(x, 0.0) + jnp.log1p(jnp.exp(-jnp.abs(x)))


def _const_spec(shape):
    nd = len(shape)
    return pl.BlockSpec(shape, lambda *_: (0,) * nd, pipeline_mode=pl.Buffered(1))


def _params(sem):
    return pltpu.CompilerParams(dimension_semantics=sem, vmem_limit_bytes=VMEM_LIMIT)


def _re_lanes(p):
    return slice(2 * p * LANES, (2 * p + 1) * LANES)


def _im_lanes(p):
    return slice((2 * p + 1) * LANES, (2 * p + 2) * LANES)


def _s5_state_group(idx):
    state = (idx // (2 * LANES)) * LANES + idx % LANES
    return state // S5_STATE


def _s5_split(x):
    planes = S5_NSTATE // LANES
    return (jnp.concatenate([x[:, _re_lanes(p)] for p in range(planes)], axis=-1),
            jnp.concatenate([x[:, _im_lanes(p)] for p in range(planes)], axis=-1))


def _s5_expand_operators(wc_ref, vc_ref, ct_ref, ms_scr, krev_scr, mo_scr):
    n, w, c = S5_NSTATE, S5_WIDTH, S5_GROUP_CH
    tile_rows = w // c
    grp_r = lax.broadcasted_iota(jnp.int32, (w, 2 * n), 0) // c
    grp_c = _s5_state_group(lax.broadcasted_iota(jnp.int32, (w, 2 * n), 1))
    same_s = grp_r == grp_c
    grp_r = _s5_state_group(lax.broadcasted_iota(jnp.int32, (2 * n, w), 0))
    lane = lax.broadcasted_iota(jnp.int32, (2 * n, w), 1)
    same_o = grp_r == lane // c
    mo0 = jnp.where(same_o, ct_ref[...], 0.0)
    kc = jnp.dot(wc_ref[...], mo0, preferred_element_type=F32, precision=lax.Precision.HIGHEST)
    same_k = (lax.broadcasted_iota(jnp.int32, (w, w), 0) // c) == (lax.broadcasted_iota(jnp.int32, (w, w), 1) // c)
    sel_row = lax.broadcasted_iota(jnp.int32, (S5_BLK * c, w), 0)
    sel_lane = lax.broadcasted_iota(jnp.int32, (S5_BLK * c, w), 1)
    vc16 = vc_ref[...].astype(BF16)
    for t in range(S5_BLK):
        rs = slice(t * w, (t + 1) * w)
        ms_scr[rs, :] = jnp.where(same_s, jnp.tile(wc_ref[t * c:(t + 1) * c, :], (tile_rows, 1)), 0.0).astype(BF16)
        krev_scr[rs, :] = jnp.where(same_k, jnp.tile(kc[t * c:(t + 1) * c, :], (tile_rows, 1)), 0.0).astype(BF16)
        pick = ((sel_row // c == t) & (sel_row % c == sel_lane % c)).astype(BF16)
        spread = jnp.dot(vc16, pick, preferred_element_type=F32)
        mo_scr[:, rs] = jnp.where(same_o, spread, 0.0).astype(BF16)


_S5W = collections.namedtuple("_S5W", "wu wc vc ct lvl pw pwt d gluw glub")
_S5Ops = collections.namedtuple("_S5Ops", "ms krev mo")
_S5Buf = collections.namedtuple("_S5Buf", "u32 u16 hprev16")
_S5Scratch = collections.namedtuple("_S5Scratch", "u_tok hplane cexp carry yg")


def _s5_scan_stage(h_ref, w, ops, scr, out, first, rows, side):
    n, wd = S5_NSTATE, S5_WIDTH
    lane_halves = wd // LANES
    planes = n // LANES
    nt = rows // SUBLANES
    tb = rows * S5_BLK
    for r0 in range(0, tb, tb // 4):
        rs = slice(r0, r0 + tb // 4)
        u_tok = jnp.dot(h_ref[rs, :].astype(BF16), w.wu[...], preferred_element_type=F32)
        for hf in range(lane_halves):
            scr.u_tok[hf, rs, :] = u_tok[:, hf * LANES:(hf + 1) * LANES]
    u = jnp.concatenate([scr.u_tok[hf, pl.ds(k, rows, stride=S5_BLK), :]
                         for k in range(S5_BLK) for hf in range(lane_halves)], axis=-1)
    out.u32[...] = u
    u16 = u.astype(BF16)
    out.u16[...] = u16
    row8 = lax.broadcasted_iota(jnp.int32, (nt, SUBLANES, LANES), 1)
    for p in range(planes):
        s = jnp.dot(u16, ops.ms[:, 2 * p * LANES:(2 * p + 2) * LANES], preferred_element_type=F32)
        if p < planes - 2:
            side(1)
        xr = s[:, :LANES].reshape(nt, SUBLANES, LANES)
        xi = s[:, LANES:].reshape(nt, SUBLANES, LANES)
        for lv, dist in enumerate((1, 2, 4)):
            lr = w.lvl[lv:lv + 1, _re_lanes(p)]
            li = w.lvl[lv:lv + 1, _im_lanes(p)]
            keep = row8 >= dist
            sr = jnp.where(keep, pltpu.roll(xr, dist, 1), 0.0)
            si = jnp.where(keep, pltpu.roll(xi, dist, 1), 0.0)
            xr, xi = xr + lr * sr - li * si, xi + lr * si + li * sr
        scr.hplane[2 * p, SUBLANES:SUBLANES + rows, :] = xr.reshape(rows, LANES)
        scr.hplane[2 * p + 1, SUBLANES:SUBLANES + rows, :] = xi.reshape(rows, LANES)
    side(1)
    last = 2 * SUBLANES - 1
    er = jnp.concatenate([scr.hplane[2 * p, pl.ds(last, nt, stride=SUBLANES), :] for p in range(planes)], axis=-1)
    ei = jnp.concatenate([scr.hplane[2 * p + 1, pl.ds(last, nt, stride=SUBLANES), :] for p in range(planes)], -1)
    tile_id = lax.broadcasted_iota(jnp.int32, (nt, n), 0)
    lv, dist = 3, 1
    while dist < nt:
        lr, li = _s5_split(w.lvl[lv:lv + 1, :])
        keep = tile_id >= dist
        sr = jnp.where(keep, pltpu.roll(er, dist, 0), 0.0)
        si = jnp.where(keep, pltpu.roll(ei, dist, 0), 0.0)
        er, ei = er + lr * sr - li * si, ei + lr * si + li * sr
        lv, dist = lv + 1, dist * 2
    cr, ci = _s5_split(jnp.where(first, 0.0, scr.carry[0:1, :]))
    ptr, pti = _s5_split(w.pwt[...])
    er, ei = er + ptr * cr - pti * ci, ei + ptr * ci + pti * cr
    for p in range(planes):
        scr.carry[0:1, _re_lanes(p)] = er[nt - 1:nt, p * LANES:(p + 1) * LANES]
        scr.carry[0:1, _im_lanes(p)] = ei[nt - 1:nt, p * LANES:(p + 1) * LANES]
    tin_r = jnp.where(tile_id >= 1, pltpu.roll(er, 1, 0), cr)
    tin_i = jnp.where(tile_id >= 1, pltpu.roll(ei, 1, 0), ci)
    side(1)
    for p in range(planes):
        for k in range(SUBLANES):
            scr.cexp[2 * p, pl.ds(k, nt, stride=SUBLANES), :] = tin_r[:, p * LANES:(p + 1) * LANES]
            scr.cexp[2 * p + 1, pl.ds(k, nt, stride=SUBLANES), :] = tin_i[:, p * LANES:(p + 1) * LANES]
    for p in range(planes):
        if p % 2 == 0:
            side(1)
        pr = jnp.tile(w.pw[:, _re_lanes(p)], (nt, 1))
        pi = jnp.tile(w.pw[:, _im_lanes(p)], (nt, 1))
        tr = scr.cexp[2 * p]
        ti = scr.cexp[2 * p + 1]
        hr = scr.hplane[2 * p, SUBLANES:SUBLANES + rows, :] + pr * tr - pi * ti
        hi = scr.hplane[2 * p + 1, SUBLANES:SUBLANES + rows, :] + pr * ti + pi * tr
        scr.hplane[2 * p, SUBLANES:SUBLANES + rows, :] = hr
        scr.hplane[2 * p + 1, SUBLANES:SUBLANES + rows, :] = hi
        scr.hplane[2 * p, SUBLANES - 1:SUBLANES, :] = cr[:, p * LANES:(p + 1) * LANES]
        scr.hplane[2 * p + 1, SUBLANES - 1:SUBLANES, :] = ci[:, p * LANES:(p + 1) * LANES]
    for p in range(2 * planes):
        out.hprev16[:, p * LANES:(p + 1) * LANES] = scr.hplane[p, SUBLANES - 1:SUBLANES - 1 + rows, :].astype(BF16)


def _s5_readout_jobs(src, w, ops, scr, o_ref, rows):
    wd = S5_WIDTH
    lane_halves = wd // LANES

    def tile_job(t):
        def run():
            cols = slice(t * wd, (t + 1) * wd)
            y = jnp.dot(src.u16[:, :(t + 1) * wd], ops.krev[(S5_BLK - 1 - t) * wd:, :], preferred_element_type=F32)
            y = y + jnp.dot(src.hprev16[...], ops.mo[:, cols], preferred_element_type=F32)
            y = _gelu(y + w.d[...] * src.u32[:, cols])
            for hf in range(lane_halves):
                scr.yg[hf, pl.ds(t, rows, stride=S5_BLK), :] = y[:, hf * LANES:(hf + 1) * LANES]
        return run

    def glu_job():
        yg = jnp.concatenate([scr.yg[hf] for hf in range(lane_halves)], axis=-1)
        glu = _sigmoid(_dot(yg, w.gluw[...]) + w.glub[...])
        o_ref[...] = (yg * glu).astype(o_ref.dtype)

    return [tile_job(t) for t in range(S5_BLK)] + [glu_job]


def _s5_kernel(h_ref, *rest, rows, nblk):
    nw = len(_S5W._fields)
    w = _S5W(*rest[:nw])
    o_ref = rest[nw]
    bufs = rest[nw + 1:]
    ops = _S5Ops(*bufs[:3])
    nb = len(_S5Buf._fields)
    sets = (_S5Buf(*bufs[3:3 + nb]), _S5Buf(*bufs[3 + nb:3 + 2 * nb]))
    scr = _S5Scratch(*bufs[3 + 2 * nb:])
    s = pl.program_id(0)

    @pl.when(s == 0)
    def _():
        _s5_expand_operators(w.wc, w.vc, w.ct, *ops)
        for ref in bufs[3:]:
            ref[...] = jnp.zeros(ref.shape, ref.dtype)

    first = (s % nblk) == 0

    def step(out, src):
        jobs = _s5_readout_jobs(src, w, ops, scr, o_ref, rows)

        def side(k):
            for _ in range(min(k, len(jobs))):
                jobs.pop(0)()

        _s5_scan_stage(h_ref, w, ops, scr, out, first, rows, side)
        side(len(jobs))

    @pl.when(s % 2 == 0)
    def _():
        step(sets[0], sets[1])

    @pl.when(s % 2 == 1)
    def _():
        step(sets[1], sets[0])


def _s5_branch(h, consts):
    b, l, d = h.shape
    tb = min(S5_TB, l)
    rows = tb // S5_BLK
    nblk = l // tb
    total = b * nblk
    n, w = S5_NSTATE, S5_WIDTH
    handoff = [
        pltpu.VMEM((rows, S5_BLK * w), F32),
        pltpu.VMEM((rows, S5_BLK * w), BF16),
        pltpu.VMEM((rows, 2 * n), BF16),
    ]
    out = pl.pallas_call(
        functools.partial(_s5_kernel, rows=rows, nblk=nblk),
        grid=(total + 1,),
        in_specs=[pl.BlockSpec((tb, d), lambda s: (jnp.minimum(s, total - 1), 0))]
                 + [_const_spec(c.shape) for c in consts],
        out_specs=pl.BlockSpec((tb, w), lambda s: (jnp.maximum(s - 1, 0), 0)),
        out_shape=jax.ShapeDtypeStruct((b * l, w), BF16),
        scratch_shapes=[
            pltpu.VMEM((S5_BLK * w, 2 * n), BF16),
            pltpu.VMEM((S5_BLK * w, w), BF16),
            pltpu.VMEM((2 * n, S5_BLK * w), BF16),
        ] + handoff + handoff + [
            pltpu.VMEM((w // LANES, tb, LANES), F32),
            pltpu.VMEM((2 * n // LANES, rows + SUBLANES, LANES), F32),
            pltpu.VMEM((2 * n // LANES, rows, LANES), F32),
            pltpu.VMEM((SUBLANES, 2 * n), F32),
            pltpu.VMEM((w // LANES, tb, LANES), F32),
        ],
        compiler_params=_params(("arbitrary",)),
        name="s5_branch",
    )(h.reshape(b * l, d), *consts)
    return out.reshape(b, l, w)


def _s5_prepare(lam_re, lam_im, log_step, b_re, b_im, c_re, c_im, n_tiles):
    g, p, c, t8 = S5_GROUPS, S5_STATE, S5_GROUP_CH, S5_BLK
    lr, li = lam_re.astype(F32), lam_im.astype(F32)
    dt = jnp.exp(log_step.astype(F32))[:, None]
    ar, ai = lr * dt, li * dt

    def lam_pow(k):
        k = k.astype(F32)[..., None, None]
        mag = jnp.exp(k * ar)
        return mag * jnp.cos(k * ai), mag * jnp.sin(k * ai)

    br1, bi1 = lam_pow(jnp.ones((), jnp.int32))
    xr_, xi_ = br1 - 1.0, bi1
    den = lr * lr + li * li
    cr, ci = (xr_ * lr + xi_ * li) / den, (xi_ * lr - xr_ * li) / den
    bre, bim = b_re.astype(F32), b_im.astype(F32)
    bbr = cr[..., None] * bre - ci[..., None] * bim
    bbi = cr[..., None] * bim + ci[..., None] * bre
    steps = jnp.arange(t8)
    pr, pi = lam_pow(t8 - 1 - steps)
    wsr = pr[..., None] * bbr[None] - pi[..., None] * bbi[None]
    wsi = pr[..., None] * bbi[None] + pi[..., None] * bbr[None]

    def interleave(re, im, axis):
        shape = list(re.shape)
        shape[axis:axis + 1] = [re.shape[axis] // LANES, 1, LANES]
        both = jnp.concatenate([re.reshape(shape), im.reshape(shape)], axis=axis + 1)
        shape[axis:axis + 3] = [2 * re.shape[axis]]
        return both.reshape(shape)

    wc = interleave(*[part.transpose(0, 3, 1, 2).reshape(t8 * c, g * p) for part in (wsr, wsi)], axis=1)
    cre, cim = c_re.astype(F32), c_im.astype(F32)
    pr, pi = lam_pow(steps + 1)
    vor = cre[None] * pr[:, :, None, :] - cim[None] * pi[:, :, None, :]
    voi = cre[None] * pi[:, :, None, :] + cim[None] * pr[:, :, None, :]
    vc = interleave(*[part.transpose(1, 3, 0, 2).reshape(g * p, t8 * c) for part in (vor, -voi)], axis=0)
    ct = interleave(*[jnp.tile(part.transpose(0, 2, 1).reshape(g * p, c), (1, g)) for part in (cre, -cim)], axis=0)

    def re_im(pair):
        return interleave(pair[0].reshape(-1, g * p), pair[1].reshape(-1, g * p), axis=1)

    lvl = re_im(lam_pow(t8 * 2 ** jnp.arange(SUBLANES)))
    pw = re_im(lam_pow(t8 * (steps + 1)))
    pwt = re_im(lam_pow(t8 * SUBLANES * (jnp.arange(n_tiles) + 1)))
    return wc, vc, ct, lvl, pw, pwt


def _expand_heads(cols, grp, head_lane):
    base = SSD_HPG * grp
    out = jnp.broadcast_to(cols[:, base:base + 1], (cols.shape[0], SSD_GW))
    for hh in range(1, SSD_HPG):
        out = jnp.where(head_lane == hh, cols[:, base + hh:base + hh + 1], out)
    return out


def _causal_conv(pad_ref, w_ref, b_ref, rows, cols):
    acc = b_ref[:, cols] + w_ref[CONV_K - 1:CONV_K, cols] * pad_ref[SUBLANES:SUBLANES + rows, cols]
    for k in range(CONV_K - 1):
        off = SUBLANES - (CONV_K - 1) + k
        acc = acc + w_ref[k:k + 1, cols] * pad_ref[off:off + rows, cols]
    return acc


_ProjBuf = collections.namedtuple("_ProjBuf", "z xpad dt rpad gr")
_MixW = collections.namedtuple(
    "_MixW", "wz wxbc wdt wxr wgr cw cb dtb aneg dvec nw rcw rcb wa ba wx bx rlam wout g b")
_MixScratch = collections.namedtuple("_MixScratch", "state ascr bscr cexp rcar")


def _col_tiles(width):
    return [slice(c, min(c + MXU_TILE, width)) for c in range(0, width, MXU_TILE)]


def _mixer_output_jobs(h_ref, ycat, w, o_ref):
    def tile_job(cols):
        def run():
            y = jnp.dot(ycat[...], w.wout[:, cols], preferred_element_type=F32)
            o_ref[:, cols] = ALPHA * h_ref[:, cols] + y
        return run

    def norm_job():
        o_ref[...] = _layer_norm(o_ref[...], w.g[...], w.b[...])

    return [tile_job(cols) for cols in _col_tiles(D_MODEL)] + [norm_job]


def _mixer_project_jobs(h_ref, h16, w, nxt, cur, first, tq):
    def cast_job():
        h16[...] = h_ref[...].astype(BF16)

    def tile_job(dst, w_ref, cols, row0):
        def run():
            dst[row0:row0 + tq, cols] = jnp.dot(h16[...], w_ref[:, cols], preferred_element_type=F32)
        return run

    def tail_job(pad_n, pad_c):
        def run():
            pad_n[0:SUBLANES, :] = jnp.where(first, 0.0, pad_c[tq:tq + SUBLANES, :])
        return run

    jobs = [cast_job]
    for dst, w_ref, row0 in ((nxt.z, w.wz, 0), (nxt.xpad, w.wxbc, SUBLANES), (nxt.dt, w.wdt, 0),
                             (nxt.rpad, w.wxr, SUBLANES), (nxt.gr, w.wgr, 0)):
        jobs += [tile_job(dst, w_ref, cols, row0) for cols in _col_tiles(dst.shape[1])]
    jobs += [tail_job(nxt.xpad, cur.xpad), tail_job(nxt.rpad, cur.rpad)]
    return jobs


def _mixer_compute(cur, ys5_ref, w, scr, ycat, first, tq, chunk, side):
    cw_ref, cb_ref, dtb_ref, aneg_ref, dvec_ref, nw_ref = w.cw, w.cb, w.dtb, w.aneg, w.dvec, w.nw
    state = scr.state
    z = cur.z
    slabs = []
    for cols in _col_tiles(SSD_XBC):
        side(1)
        slabs.append(_silu(_causal_conv(cur.xpad, cw_ref, cb_ref, tq, cols)))
    xact = jnp.concatenate(slabs, axis=-1)
    side(2)
    dt_all = _softplus(cur.dt[...] + dtb_ref[...])
    q = chunk
    tri = (lax.broadcasted_iota(jnp.int32, (q, q), 0) >= lax.broadcasted_iota(jnp.int32, (q, q), 1))
    row_id = lax.broadcasted_iota(jnp.int32, (q, LANES), 0)
    head_lane = lax.broadcasted_iota(jnp.int32, (1, SSD_GW), 1) // SSD_HEAD_DIM
    for ci in range(tq // q):
        r0 = ci * q
        dt = dt_all[r0:r0 + q, :]
        cs = dt * aneg_ref[...]
        dist = 1
        while dist < q:
            cs = cs + jnp.where(row_id >= dist, pltpu.roll(cs, dist, 0), 0.0)
            dist *= 2
        cs_t = cs.T
        cs_last = cs[q - 1:q, :]
        chunk_decay = jnp.exp(cs_last)
        dec_t = jnp.exp(cs_t[:, q - 1:q] - cs_t)
        ecs = jnp.exp(cs)
        zc = z[r0:r0 + q, :]
        halves = []
        for grp in range(SSD_GROUPS):
            side(2)
            lo = grp * SSD_GW
            xs = xact[r0:r0 + q, lo:lo + SSD_GW]
            bm = xact[r0:r0 + q, SSD_WIDTH + grp * SSD_STATE:SSD_WIDTH + (grp + 1) * SSD_STATE]
            cm = xact[r0:r0 + q, SSD_WIDTH + (SSD_GROUPS + grp) * SSD_STATE:
                      SSD_WIDTH + (SSD_GROUPS + grp + 1) * SSD_STATE]
            cm16 = cm.astype(BF16)
            xdt16 = (xs * _expand_heads(dt, grp, head_lane)).astype(BF16)
            cb = _dot_nt(cm16, bm)
            bm_t = bm.T
            s_prev = state[grp]
            if ci == 0:
                s_prev = jnp.where(first, 0.0, s_prev)
            y_off = jnp.dot(cm16, s_prev.astype(BF16), preferred_element_type=F32)
            intra, to_state, xdt_heads = [], [], []
            for hh in range(SSD_HPG):
                hd = grp * SSD_HPG + hh
                seg = cs[:, hd:hd + 1] - cs_t[hd:hd + 1, :]
                lmat = jnp.exp(jnp.where(tri, seg, NEG_BIG))
                intra.append((cb * lmat).astype(BF16))
                to_state.append((bm_t * dec_t[hd:hd + 1, :]).astype(BF16))
                xdt_heads.append(jnp.where(head_lane == hh, xdt16, jnp.zeros_like(xdt16)))
            lhs = jnp.concatenate([jnp.concatenate(intra, axis=1), jnp.concatenate(to_state, axis=1)], axis=0)
            both = jnp.dot(lhs, jnp.concatenate(xdt_heads, axis=0), preferred_element_type=F32)
            y_diag = both[:q]
            st_new = both[q:]
            state[grp] = s_prev * _expand_heads(chunk_decay, grp, head_lane) + st_new
            halves.append(y_diag + y_off * _expand_heads(ecs, grp, head_lane) + xs * dvec_ref[:, lo:lo + SSD_GW])
        y = jnp.concatenate(halves, axis=-1) * _silu(zc)
        y = y * lax.rsqrt(jnp.mean(y * y, axis=-1, keepdims=True) + LN_EPS) * nw_ref[...]
        ycat[r0:r0 + q, 0:SSD_WIDTH] = y.astype(BF16)

    ycat[:, SSD_WIDTH:SSD_WIDTH + S5_WIDTH] = ys5_ref[...]

    side(1)
    xc = _causal_conv(cur.rpad, w.rcw, w.rcb, tq, slice(0, RG_WIDTH))
    xc16 = xc.astype(BF16)
    rgate = _sigmoid(jnp.dot(xc16, w.wa[...], preferred_element_type=F32) + w.ba[...])
    igate = _sigmoid(jnp.dot(xc16, w.wx[...], preferred_element_type=F32) + w.bx[...])
    log_a = (-RG_C) * rgate * _softplus(-w.rlam[...])
    a = jnp.exp(log_a)
    bt = jnp.sqrt(1.0 - jnp.exp(2.0 * log_a)) * (igate * xc)
    row8 = lax.broadcasted_iota(jnp.int32, (tq, RG_WIDTH), 0) & (SUBLANES - 1)
    side(1)
    for dist in (1, 2, 4):
        keep = row8 >= dist
        a_sh = jnp.where(keep, pltpu.roll(a, dist, 0), 1.0)
        b_sh = jnp.where(keep, pltpu.roll(bt, dist, 0), 0.0)
        bt = bt + a * b_sh
        a = a * a_sh
    nt = tq // SUBLANES
    lane_halves = RG_WIDTH // LANES
    for hf in range(lane_halves):
        scr.ascr[hf] = a[:, hf * LANES:(hf + 1) * LANES]
        scr.bscr[hf] = bt[:, hf * LANES:(hf + 1) * LANES]
    ae = jnp.concatenate([scr.ascr[hf, pl.ds(SUBLANES - 1, nt, stride=SUBLANES), :] for hf in range(lane_halves)], -1)
    be = jnp.concatenate([scr.bscr[hf, pl.ds(SUBLANES - 1, nt, stride=SUBLANES), :] for hf in range(lane_halves)], -1)
    tile_id = lax.broadcasted_iota(jnp.int32, (nt, RG_WIDTH), 0)
    dist = 1
    while dist < nt:
        keep = tile_id >= dist
        a_sh = jnp.where(keep, pltpu.roll(ae, dist, 0), 1.0)
        b_sh = jnp.where(keep, pltpu.roll(be, dist, 0), 0.0)
        be = be + ae * b_sh
        ae = ae * a_sh
        dist *= 2
    carry_in = jnp.where(first, 0.0, scr.rcar[0:1, :])
    tile_end = be + ae * carry_in
    scr.rcar[0:1, :] = tile_end[nt - 1:nt, :]
    tile_in = jnp.where(tile_id >= 1, pltpu.roll(tile_end, 1, 0), carry_in)
    for hf in range(lane_halves):
        for k in range(SUBLANES):
            scr.cexp[hf, pl.ds(k, nt, stride=SUBLANES), :] = tile_in[:, hf * LANES:(hf + 1) * LANES]
    hscan = bt + a * jnp.concatenate([scr.cexp[hf] for hf in range(lane_halves)], axis=-1)
    ycat[:, SSD_WIDTH + S5_WIDTH:] = (hscan * _gelu(cur.gr[...])).astype(BF16)


def _mixer_kernel(hp_ref, ho_ref, ys5_ref, *rest, tq, chunk, nblk):
    nw = len(_MixW._fields)
    w = _MixW(*rest[:nw])
    o_ref = rest[nw]
    bufs = rest[nw + 1:]
    npb = len(_ProjBuf._fields)
    sets = (_ProjBuf(*bufs[:npb]), _ProjBuf(*bufs[npb:2 * npb]))
    ycats = bufs[2 * npb:2 * npb + 2]
    scr = _MixScratch(*bufs[2 * npb + 2:2 * npb + 2 + len(_MixScratch._fields)])
    h16 = bufs[-1]
    s = pl.program_id(0)

    @pl.when(s == 0)
    def _():
        for ref in bufs:
            ref[...] = jnp.zeros(ref.shape, ref.dtype)

    first_p = (s % nblk) == 0
    first_c = ((s + nblk - 1) % nblk) == 0

    def step(nxt, cur, ycat_c, ycat_o):
        jobs = (_mixer_output_jobs(ho_ref, ycat_o, w, o_ref)
                + _mixer_project_jobs(hp_ref, h16, w, nxt, cur, first_p, tq))

        def side(n):
            for _ in range(min(n, len(jobs))):
                jobs.pop(0)()

        _mixer_compute(cur, ys5_ref, w, scr, ycat_c, first_c, tq, chunk, side)
        side(len(jobs))

    @pl.when(s % 2 == 0)
    def _():
        step(sets[0], sets[1], ycats[1], ycats[0])

    @pl.when(s % 2 == 1)
    def _():
        step(sets[1], sets[0], ycats[0], ycats[1])


def _mixer(h, ys5, consts):
    b, l, d = h.shape
    tq = min(MIX_TQ, l)
    chunk = min(SSD_CHUNK, tq)
    nblk = l // tq
    total = b * nblk
    h2d = h.reshape(b * l, d)
    proj_buf = [
        pltpu.VMEM((tq, SSD_WIDTH), F32),
        pltpu.VMEM((tq + SUBLANES, SSD_XBC), F32),
        pltpu.VMEM((tq, LANES), F32),
        pltpu.VMEM((tq + SUBLANES, RG_WIDTH), F32),
        pltpu.VMEM((tq, RG_WIDTH), F32),
    ]
    rg_plane = pltpu.VMEM((RG_WIDTH // LANES, tq, LANES), F32)
    out = pl.pallas_call(
        functools.partial(_mixer_kernel, tq=tq, chunk=chunk, nblk=nblk),
        grid=(total + 2,),
        in_specs=[pl.BlockSpec((tq, d), lambda s: (jnp.minimum(s, total - 1), 0)),
                  pl.BlockSpec((tq, d), lambda s: (jnp.maximum(s - 2, 0), 0)),
                  pl.BlockSpec((tq, S5_WIDTH), lambda s: (jnp.clip(s - 1, 0, total - 1), 0))]
                 + [_const_spec(c.shape) for c in consts],
        out_specs=pl.BlockSpec((tq, d), lambda s: (jnp.maximum(s - 2, 0), 0)),
        out_shape=jax.ShapeDtypeStruct((b * l, d), F32),
        scratch_shapes=proj_buf + proj_buf + [
            pltpu.VMEM((tq, D_MODEL), BF16),
            pltpu.VMEM((tq, D_MODEL), BF16),
            pltpu.VMEM((SSD_GROUPS, SSD_STATE, SSD_GW), F32),
            rg_plane, rg_plane, rg_plane,
            pltpu.VMEM((SUBLANES, RG_WIDTH), F32),
            pltpu.VMEM((tq, d), BF16),
        ],
        compiler_params=_params(("arbitrary",)),
        name="mixer",
    )(h2d, h2d, ys5.reshape(b * l, S5_WIDTH), *consts)
    return out.reshape(b, l, d)


def _block_diag(w):
    nb, bi, bo = w.shape
    return jnp.einsum('hij,hk->hikj', w, jnp.eye(nb, dtype=w.dtype)).reshape(nb * bi, nb * bo)


def _kv_kernel(m_ref, wk_ref, wv_ref, k_ref, v_ref):
    m16 = m_ref[...].astype(BF16)
    k_ref[...] = jnp.dot(m16, wk_ref[...], preferred_element_type=F32).astype(BF16)
    v_ref[...] = jnp.dot(m16, wv_ref[...], preferred_element_type=F32).astype(BF16)


def _kv(mem, wk, wv):
    b, m, d = mem.shape
    spec = pl.BlockSpec((None, m, d), lambda i: (i, 0, 0))
    return pl.pallas_call(
        _kv_kernel,
        grid=(b,),
        in_specs=[spec, _const_spec(wk.shape), _const_spec(wv.shape)],
        out_specs=[spec, spec],
        out_shape=[jax.ShapeDtypeStruct((b, m, d), BF16)] * 2,
        compiler_params=_params(("parallel",)),
        name="kv",
    )(mem, wk, wv)


def _xattn_kernel(h_ref, k_ref, v_ref, wq_ref, wo_ref, g_ref, b_ref, o_ref):
    hb = h_ref[...]
    qall = jnp.dot(hb.astype(BF16), wq_ref[...], preferred_element_type=F32) * (1.0 / math.sqrt(XA_HEAD_DIM))
    outs = []
    for hd in range(XA_HEADS):
        sl = slice(hd * XA_HEAD_DIM, (hd + 1) * XA_HEAD_DIM)
        s = _dot_nt(qall[:, sl], k_ref[:, sl])
        p = jnp.exp(s - jnp.max(s, axis=-1, keepdims=True))
        o = jnp.dot(p.astype(BF16), v_ref[:, sl], preferred_element_type=F32)
        outs.append(o / jnp.sum(p, axis=-1, keepdims=True))
    o = jnp.concatenate(outs, axis=-1)
    y = jnp.dot(o.astype(BF16), wo_ref[...], preferred_element_type=F32)
    o_ref[...] = _layer_norm(ALPHA * hb + y, g_ref[...], b_ref[...])


def _xattn(h, k, v, wq, wo, g, bb):
    b, l, d = h.shape
    tm = min(TOK_TM, l)
    tok = pl.BlockSpec((None, tm, d), lambda i, j: (i, j, 0))
    kvs = pl.BlockSpec((None, k.shape[1], d), lambda i, j: (i, 0, 0))
    return pl.pallas_call(
        _xattn_kernel,
        grid=(b, l // tm),
        in_specs=[tok, kvs, kvs, _const_spec(wq.shape), _const_spec(wo.shape),
                  _const_spec(g.shape), _const_spec(bb.shape)],
        out_specs=tok,
        out_shape=jax.ShapeDtypeStruct((b, l, d), F32),
        compiler_params=_params(("parallel", "parallel")),
        name="xattn",
    )(h, k, v, wq, wo, g, bb)


def _mlp_kernel(h_ref, w1_ref, w2_ref, g_ref, b_ref, o_ref):
    hb = h_ref[...]
    h16 = hb.astype(BF16)
    acc = ALPHA * hb
    for f in range(D_FF // MLP_FC):
        sl = slice(f * MLP_FC, (f + 1) * MLP_FC)
        t = jnp.maximum(jnp.dot(h16, w1_ref[:, sl], preferred_element_type=F32), 0.0)
        acc = acc + jnp.dot((t * t).astype(BF16), w2_ref[sl, :], preferred_element_type=F32)
    o_ref[...] = _layer_norm(acc, g_ref[...], b_ref[...])


def _mlp(h2d, w1, w2, g, bb):
    t, d = h2d.shape
    tm = min(TOK_TM, t)
    tok = pl.BlockSpec((tm, d), lambda i: (i, 0))
    return pl.pallas_call(
        _mlp_kernel,
        grid=(t // tm,),
        in_specs=[tok, _const_spec(w1.shape), _const_spec(w2.shape), _const_spec(g.shape), _const_spec(bb.shape)],
        out_specs=tok,
        out_shape=jax.ShapeDtypeStruct((t, d), F32),
        compiler_params=_params(("parallel",)),
        name="mlp",
    )(h2d, w1, w2, g, bb)


def _row(v):
    return v.astype(F32).reshape(1, -1)


def _pad_lanes(v2d):
    return jnp.pad(v2d, ((0, 0), (0, LANES - v2d.shape[1])))


def mixer_sublayer(h, w_in, w_out, ssd_conv_w, ssd_conv_b, ssd_dt_bias, ssd_a_log, ssd_d, ssd_norm_w,
                   s5_lam_re, s5_lam_im, s5_log_step, s5_b_re, s5_b_im, s5_c_re, s5_c_im, s5_d, s5_glu_w,
                   s5_glu_b, rg_conv_w, rg_conv_b, rg_wa, rg_ba, rg_wx, rg_bx, rg_lambda, ln_g, ln_b):
    o_z, o_xbc, o_dt = 0, SSD_WIDTH, SSD_WIDTH + SSD_XBC
    o_u = o_dt + SSD_HEADS
    o_xr = o_u + S5_WIDTH
    o_gr = o_xr + RG_WIDTH
    w16 = w_in.astype(BF16)
    n_tiles = min(S5_TB, h.shape[1]) // (S5_BLK * SUBLANES)
    assert n_tiles <= 2 ** (SUBLANES - 3), "S5 scan-level table holds 3 in-tile + 5 tile levels"
    s5_ops = _s5_prepare(s5_lam_re, s5_lam_im, s5_log_step, s5_b_re, s5_b_im, s5_c_re, s5_c_im, n_tiles)
    ys5 = _s5_branch(h, (w16[:, o_u:o_xr], *s5_ops, _row(s5_d), s5_glu_w.astype(BF16), _row(s5_glu_b)))
    consts = (
        w16[:, o_z:o_xbc], w16[:, o_xbc:o_dt], _pad_lanes(w16[:, o_dt:o_u]), w16[:, o_xr:o_gr], w16[:, o_gr:],
        ssd_conv_w.astype(F32), _row(ssd_conv_b), _pad_lanes(_row(ssd_dt_bias)),
        _pad_lanes(_row(-jnp.exp(ssd_a_log.astype(F32)))),
        _row(jnp.repeat(ssd_d.astype(F32), SSD_HEAD_DIM)), _row(ssd_norm_w),
        rg_conv_w.astype(F32), _row(rg_conv_b),
        _block_diag(rg_wa.astype(F32)).astype(BF16), _row(rg_ba),
        _block_diag(rg_wx.astype(F32)).astype(BF16), _row(rg_bx), _row(rg_lambda),
        w_out.astype(BF16), _row(ln_g), _row(ln_b),
    )
    return _mixer(h, ys5, consts)


def xattn_sublayer(h, mem, wq, wk, wv, wo, ln_g, ln_b):
    k, v = _kv(mem, wk.astype(BF16), wv.astype(BF16))
    return _xattn(h, k, v, wq.astype(BF16), wo.astype(BF16), _row(ln_g), _row(ln_b))


def mlp_sublayer(h, w1, w2, ln_g, ln_b):
    b, l, d = h.shape
    return _mlp(h.reshape(b * l, d), w1.astype(BF16), w2.astype(BF16), _row(ln_g), _row(ln_b)).reshape(b, l, d)


def kernel(x, mem, w_in, w_out, ssd_conv_w, ssd_conv_b, ssd_dt_bias, ssd_a_log, ssd_d, ssd_norm_w, s5_lam_re, s5_lam_im, s5_log_step, s5_b_re, s5_b_im, s5_c_re, s5_c_im, s5_d, s5_glu_w, s5_glu_b, rg_conv_w, rg_conv_b, rg_wa, rg_ba, rg_wx, rg_bx, rg_lambda, ln1_g, ln1_b, xa_wq, xa_wk, xa_wv, xa_wo, ln2_g, ln2_b, mlp_w1, mlp_w2, ln3_g, ln3_b):
    h = x.astype(F32)
    memf = mem.astype(F32)
    for l in range(DEPTH):
        h = mixer_sublayer(h, w_in[l], w_out[l], ssd_conv_w[l], ssd_conv_b[l], ssd_dt_bias[l], ssd_a_log[l],
                           ssd_d[l], ssd_norm_w[l], s5_lam_re[l], s5_lam_im[l], s5_log_step[l], s5_b_re[l],
                           s5_b_im[l], s5_c_re[l], s5_c_im[l], s5_d[l], s5_glu_w[l], s5_glu_b[l],
                           rg_conv_w[l], rg_conv_b[l], rg_wa[l], rg_ba[l], rg_wx[l], rg_bx[l], rg_lambda[l],
                           ln1_g[l], ln1_b[l])
        h = xattn_sublayer(h, memf, xa_wq[l], xa_wk[l], xa_wv[l], xa_wo[l], ln2_g[l], ln2_b[l])
        h = mlp_sublayer(h, mlp_w1[l], mlp_w2[l], ln3_g[l], ln3_b[l])
    return h.astype(x.dtype)
```

```python
import collections
import functools
import math

import jax
import jax.numpy as jnp
from jax import lax
from jax.experimental import pallas as pl
from jax.experimental.pallas import tpu as pltpu

F32 = jnp.float32
BF16 = jnp.bfloat16

D_MODEL = 1024
DEPTH = 2
SSD_WIDTH = 512
SSD_HEAD_DIM = 64
SSD_HEADS = 8
SSD_GROUPS = 2
SSD_HPG = SSD_HEADS // SSD_GROUPS
SSD_GW = SSD_HPG * SSD_HEAD_DIM
SSD_STATE = 128
SSD_XBC = 1024
CONV_K = 4
S5_WIDTH = 256
S5_GROUPS = 16
S5_GROUP_CH = 16
S5_STATE = 64
S5_NSTATE = S5_GROUPS * S5_STATE
S5_BLK = 8
RG_WIDTH = 256
RG_BLOCKS = 4
RG_C = 8.0
XA_HEADS = 4
XA_HEAD_DIM = 256
D_FF = 4096
ALPHA = (2.0 * DEPTH) ** 0.25
LN_EPS = 1e-5

LANES = 128
SUBLANES = 8
MXU_TILE = 256
NEG_BIG = -1e30
VMEM_LIMIT = 56 * 1024 * 1024

SSD_CHUNK = 128
MIX_TQ = 256
S5_TB = 2048
TOK_TM = 512
MLP_FC = 1024


def _dot(a, b):
    return jnp.dot(a.astype(BF16), b.astype(BF16), preferred_element_type=F32)


def _dot_nt(a, b):
    return lax.dot_general(a.astype(BF16), b.astype(BF16), (((1,), (1,)), ((), ())),
                           preferred_element_type=F32)


def _layer_norm(x, g, b):
    mu = jnp.mean(x, axis=-1, keepdims=True)
    xc = x - mu
    var = jnp.mean(xc * xc, axis=-1, keepdims=True)
    return xc * lax.rsqrt(var + LN_EPS) * g + b


def _gelu(x):
    c = math.sqrt(2.0 / math.pi)
    return 0.5 * x * (1.0 + jnp.tanh(c * (x + 0.044715 * (x * x * x))))


def _sigmoid(x):
    return 0.5 + 0.5 * jnp.tanh(0.5 * x)


def _silu(x):
    hx = 0.5 * x
    return hx + hx * jnp.tanh(hx)


def _softplus(x):
    return jnp.maximum(x, 0.0) + jnp.log1p(jnp.exp(-jnp.abs(x)))


def _const_spec(shape):
    nd = len(shape)
    return pl.BlockSpec(shape, lambda *_: (0,) * nd, pipeline_mode=pl.Buffered(1))


def _params(sem):
    return pltpu.CompilerParams(dimension_semantics=sem, vmem_limit_bytes=VMEM_LIMIT)


def _re_lanes(p):
    return slice(2 * p * LANES, (2 * p + 1) * LANES)


def _im_lanes(p):
    return slice((2 * p + 1) * LANES, (2 * p + 2) * LANES)


def _s5_state_group(idx):
    state = (idx // (2 * LANES)) * LANES + idx % LANES
    return state // S5_STATE


def _s5_split(x):
    planes = S5_NSTATE // LANES
    return (jnp.concatenate([x[:, _re_lanes(p)] for p in range(planes)], axis=-1),
            jnp.concatenate([x[:, _im_lanes(p)] for p in range(planes)], axis=-1))


S5_HALF_CH = S5_WIDTH // 2
S5_HALF_LANES = S5_BLK * S5_HALF_CH


def _s5_expand_operators(wc_ref, vc_ref, ct_ref, ms_scr, toep_scr, mo_scr):
    n, w, c = S5_NSTATE, S5_WIDTH, S5_GROUP_CH
    hw, hl = S5_HALF_CH, S5_HALF_LANES
    grp_half = S5_GROUPS // 2
    reps = hw // c
    grp_r = _s5_state_group(lax.broadcasted_iota(jnp.int32, (2 * n, w), 0))
    same_o = grp_r == lax.broadcasted_iota(jnp.int32, (2 * n, w), 1) // c
    mo0 = jnp.where(same_o, ct_ref[...], 0.0)
    kc = jnp.dot(wc_ref[...], mo0, preferred_element_type=F32, precision=lax.Precision.HIGHEST)
    toep_scr[...] = jnp.zeros(toep_scr.shape, toep_scr.dtype)
    ch_grp = lax.broadcasted_iota(jnp.int32, (hw, hl), 0) // c
    same_k = (lax.broadcasted_iota(jnp.int32, (hw, hw), 0) // c) == (lax.broadcasted_iota(jnp.int32, (hw, hw), 1) // c)
    pick_r = lax.broadcasted_iota(jnp.int32, (S5_BLK * c, hl), 0)
    pick_c = lax.broadcasted_iota(jnp.int32, (S5_BLK * c, hl), 1)
    pick = ((pick_r // c == pick_c // hw) & (pick_r % c == pick_c % c)).astype(BF16)
    for hf in range(2):
        lanes = slice(hf * hl, (hf + 1) * hl)
        st_grp = _s5_state_group(lax.broadcasted_iota(jnp.int32, (hw, hl), 1) + hf * hl) - hf * grp_half
        same_s = ch_grp == st_grp
        for t in range(S5_BLK):
            blk = jnp.where(same_s, jnp.tile(wc_ref[t * c:(t + 1) * c, lanes], (reps, 1)), 0.0)
            ms_scr[hf, t * hw:(t + 1) * hw, :] = blk.astype(BF16)
        st_grp = _s5_state_group(lax.broadcasted_iota(jnp.int32, (hl, hl), 0) + hf * hl) - hf * grp_half
        same_r = st_grp == (lax.broadcasted_iota(jnp.int32, (hl, hl), 1) % hw) // c
        spread = jnp.dot(vc_ref[lanes, :].astype(BF16), pick, preferred_element_type=F32)
        mo_scr[hf] = jnp.where(same_r, spread, 0.0).astype(BF16)
        for t_in in range(S5_BLK):
            for t_out in range(t_in, S5_BLK):
                k = S5_BLK - 1 - (t_out - t_in)
                blk = jnp.where(same_k, jnp.tile(kc[k * c:(k + 1) * c, hf * hw:(hf + 1) * hw], (reps, 1)), 0.0)
                toep_scr[hf, t_in * hw:(t_in + 1) * hw, t_out * hw:(t_out + 1) * hw] = blk.astype(BF16)


_S5W = collections.namedtuple("_S5W", "wu wc vc ct lvl pw pwt d gluw glub")
_S5Ops = collections.namedtuple("_S5Ops", "ms toep mo")
_S5Buf = collections.namedtuple("_S5Buf", "u32 u16 hprev16")
_S5Scratch = collections.namedtuple("_S5Scratch", "u_tok hplane cexp carry yg")


def _s5_scan_stage(h_ref, w, ops, scr, out, first, rows, side):
    n, wd = S5_NSTATE, S5_WIDTH
    lane_halves = wd // LANES
    planes = n // LANES
    nt = rows // SUBLANES
    tb = rows * S5_BLK
    for r0 in range(0, tb, tb // 4):
        rs = slice(r0, r0 + tb // 4)
        u_tok = jnp.dot(h_ref[rs, :].astype(BF16), w.wu[...], preferred_element_type=F32)
        for hf in range(lane_halves):
            scr.u_tok[hf, rs, :] = u_tok[:, hf * LANES:(hf + 1) * LANES]
    u = jnp.concatenate([scr.u_tok[hf, pl.ds(k, rows, stride=S5_BLK), :]
                         for hf in range(lane_halves) for k in range(S5_BLK)], axis=-1)
    out.u32[...] = u
    u16 = u.astype(BF16)
    out.u16[...] = u16
    row8 = lax.broadcasted_iota(jnp.int32, (nt, SUBLANES, LANES), 1)
    planes_per_half = planes // lane_halves
    for p in range(planes):
        hf, pl_in_half = divmod(p, planes_per_half)
        s = jnp.dot(u16[:, hf * S5_HALF_LANES:(hf + 1) * S5_HALF_LANES],
                    ops.ms[hf, :, 2 * pl_in_half * LANES:(2 * pl_in_half + 2) * LANES],
                    preferred_element_type=F32)
        if p < planes - 2:
            side(1)
        xr = s[:, :LANES].reshape(nt, SUBLANES, LANES)
        xi = s[:, LANES:].reshape(nt, SUBLANES, LANES)
        for lv, dist in enumerate((1, 2, 4)):
            lr = w.lvl[lv:lv + 1, _re_lanes(p)]
            li = w.lvl[lv:lv + 1, _im_lanes(p)]
            keep = row8 >= dist
            sr = jnp.where(keep, pltpu.roll(xr, dist, 1), 0.0)
            si = jnp.where(keep, pltpu.roll(xi, dist, 1), 0.0)
            xr, xi = xr + lr * sr - li * si, xi + lr * si + li * sr
        scr.hplane[2 * p, SUBLANES:SUBLANES + rows, :] = xr.reshape(rows, LANES)
        scr.hplane[2 * p + 1, SUBLANES:SUBLANES + rows, :] = xi.reshape(rows, LANES)
    side(1)
    last = 2 * SUBLANES - 1
    er = jnp.concatenate([scr.hplane[2 * p, pl.ds(last, nt, stride=SUBLANES), :] for p in range(planes)], axis=-1)
    ei = jnp.concatenate([scr.hplane[2 * p + 1, pl.ds(last, nt, stride=SUBLANES), :] for p in range(planes)], -1)
    tile_id = lax.broadcasted_iota(jnp.int32, (nt, n), 0)
    lv, dist = 3, 1
    while dist < nt:
        lr, li = _s5_split(w.lvl[lv:lv + 1, :])
        keep = tile_id >= dist
        sr = jnp.where(keep, pltpu.roll(er, dist, 0), 0.0)
        si = jnp.where(keep, pltpu.roll(ei, dist, 0), 0.0)
        er, ei = er + lr * sr - li * si, ei + lr * si + li * sr
        lv, dist = lv + 1, dist * 2
    cr, ci = _s5_split(jnp.where(first, 0.0, scr.carry[0:1, :]))
    ptr, pti = _s5_split(w.pwt[...])
    er, ei = er + ptr * cr - pti * ci, ei + ptr * ci + pti * cr
    for p in range(planes):
        scr.carry[0:1, _re_lanes(p)] = er[nt - 1:nt, p * LANES:(p + 1) * LANES]
        scr.carry[0:1, _im_lanes(p)] = ei[nt - 1:nt, p * LANES:(p + 1) * LANES]
    tin_r = jnp.where(tile_id >= 1, pltpu.roll(er, 1, 0), cr)
    tin_i = jnp.where(tile_id >= 1, pltpu.roll(ei, 1, 0), ci)
    side(1)
    for p in range(planes):
        for k in range(SUBLANES):
            scr.cexp[2 * p, pl.ds(k, nt, stride=SUBLANES), :] = tin_r[:, p * LANES:(p + 1) * LANES]
            scr.cexp[2 * p + 1, pl.ds(k, nt, stride=SUBLANES), :] = tin_i[:, p * LANES:(p + 1) * LANES]
    for p in range(planes):
        if p % 2 == 0:
            side(1)
        pr = jnp.tile(w.pw[:, _re_lanes(p)], (nt, 1))
        pi = jnp.tile(w.pw[:, _im_lanes(p)], (nt, 1))
        tr = scr.cexp[2 * p]
        ti = scr.cexp[2 * p + 1]
        hr = scr.hplane[2 * p, SUBLANES:SUBLANES + rows, :] + pr * tr - pi * ti
        hi = scr.hplane[2 * p + 1, SUBLANES:SUBLANES + rows, :] + pr * ti + pi * tr
        scr.hplane[2 * p, SUBLANES:SUBLANES + rows, :] = hr
        scr.hplane[2 * p + 1, SUBLANES:SUBLANES + rows, :] = hi
        scr.hplane[2 * p, SUBLANES - 1:SUBLANES, :] = cr[:, p * LANES:(p + 1) * LANES]
        scr.hplane[2 * p + 1, SUBLANES - 1:SUBLANES, :] = ci[:, p * LANES:(p + 1) * LANES]
    for p in range(2 * planes):
        out.hprev16[:, p * LANES:(p + 1) * LANES] = scr.hplane[p, SUBLANES - 1:SUBLANES - 1 + rows, :].astype(BF16)


def _s5_readout_jobs(src, w, ops, scr, o_ref, rows):
    hw, hl = S5_HALF_CH, S5_HALF_LANES
    lane_halves = S5_WIDTH // LANES
    steps_per_tile = MXU_TILE // hw

    def tile_job(hf, j):
        def run():
            k_rows = (j + 1) * MXU_TILE
            cols = slice(j * MXU_TILE, (j + 1) * MXU_TILE)
            y = jnp.dot(src.u16[:, hf * hl:hf * hl + k_rows], ops.toep[hf, :k_rows, cols],
                        preferred_element_type=F32)
            y = y + jnp.dot(src.hprev16[:, hf * hl:(hf + 1) * hl], ops.mo[hf, :, cols], preferred_element_type=F32)
            d_half = w.d[:, hf * hw:(hf + 1) * hw]
            y = _gelu(y + jnp.concatenate([d_half] * steps_per_tile, axis=1) * src.u32[:, hf * hl + j * MXU_TILE:
                                                                                           hf * hl + (j + 1) * MXU_TILE])
            for i in range(steps_per_tile):
                scr.yg[hf, pl.ds(j * steps_per_tile + i, rows, stride=S5_BLK), :] = y[:, i * hw:(i + 1) * hw]
        return run

    def glu_job():
        yg = jnp.concatenate([scr.yg[hf] for hf in range(lane_halves)], axis=-1)
        glu = _sigmoid(_dot(yg, w.gluw[...]) + w.glub[...])
        o_ref[...] = (yg * glu).astype(o_ref.dtype)

    return [tile_job(hf, j) for hf in range(lane_halves) for j in range(hl // MXU_TILE)] + [glu_job]


def _s5_kernel(h_ref, *rest, rows, nblk):
    nw = len(_S5W._fields)
    w = _S5W(*rest[:nw])
    o_ref = rest[nw]
    bufs = rest[nw + 1:]
    ops = _S5Ops(*bufs[:3])
    nb = len(_S5Buf._fields)
    sets = (_S5Buf(*bufs[3:3 + nb]), _S5Buf(*bufs[3 + nb:3 + 2 * nb]))
    scr = _S5Scratch(*bufs[3 + 2 * nb:])
    s = pl.program_id(0)

    @pl.when(s == 0)
    def _():
        _s5_expand_operators(w.wc, w.vc, w.ct, *ops)
        for ref in bufs[3:]:
            ref[...] = jnp.zeros(ref.shape, ref.dtype)

    first = (s % nblk) == 0

    def step(out, src):
        jobs = _s5_readout_jobs(src, w, ops, scr, o_ref, rows)

        def side(k):
            for _ in range(min(k, len(jobs))):
                jobs.pop(0)()

        _s5_scan_stage(h_ref, w, ops, scr, out, first, rows, side)
        side(len(jobs))

    @pl.when(s % 2 == 0)
    def _():
        step(sets[0], sets[1])

    @pl.when(s % 2 == 1)
    def _():
        step(sets[1], sets[0])


def _s5_branch(h, consts):
    b, l, d = h.shape
    tb = min(S5_TB, l)
    rows = tb // S5_BLK
    nblk = l // tb
    total = b * nblk
    n, w = S5_NSTATE, S5_WIDTH
    handoff = [
        pltpu.VMEM((rows, S5_BLK * w), F32),
        pltpu.VMEM((rows, S5_BLK * w), BF16),
        pltpu.VMEM((rows, 2 * n), BF16),
    ]
    out = pl.pallas_call(
        functools.partial(_s5_kernel, rows=rows, nblk=nblk),
        grid=(total + 1,),
        in_specs=[pl.BlockSpec((tb, d), lambda s: (jnp.minimum(s, total - 1), 0))]
                 + [_const_spec(c.shape) for c in consts],
        out_specs=pl.BlockSpec((tb, w), lambda s: (jnp.maximum(s - 1, 0), 0)),
        out_shape=jax.ShapeDtypeStruct((b * l, w), BF16),
        scratch_shapes=[
            pltpu.VMEM((2, S5_HALF_LANES, S5_HALF_LANES), BF16),
            pltpu.VMEM((2, S5_HALF_LANES, S5_HALF_LANES), BF16),
            pltpu.VMEM((2, S5_HALF_LANES, S5_HALF_LANES), BF16),
        ] + handoff + handoff + [
            pltpu.VMEM((w // LANES, tb, LANES), F32),
            pltpu.VMEM((2 * n // LANES, rows + SUBLANES, LANES), F32),
            pltpu.VMEM((2 * n // LANES, rows, LANES), F32),
            pltpu.VMEM((SUBLANES, 2 * n), F32),
            pltpu.VMEM((w // LANES, tb, LANES), F32),
        ],
        compiler_params=_params(("arbitrary",)),
        name="s5_branch",
    )(h.reshape(b * l, d), *consts)
    return out.reshape(b, l, w)


def _s5_prepare(lam_re, lam_im, log_step, b_re, b_im, c_re, c_im, n_tiles):
    g, p, c, t8 = S5_GROUPS, S5_STATE, S5_GROUP_CH, S5_BLK
    lr, li = lam_re.astype(F32), lam_im.astype(F32)
    dt = jnp.exp(log_step.astype(F32))[:, None]
    ar, ai = lr * dt, li * dt

    def lam_pow(k):
        k = k.astype(F32)[..., None, None]
        mag = jnp.exp(k * ar)
        return mag * jnp.cos(k * ai), mag * jnp.sin(k * ai)

    br1, bi1 = lam_pow(jnp.ones((), jnp.int32))
    xr_, xi_ = br1 - 1.0, bi1
    den = lr * lr + li * li
    cr, ci = (xr_ * lr + xi_ * li) / den, (xi_ * lr - xr_ * li) / den
    bre, bim = b_re.astype(F32), b_im.astype(F32)
    bbr = cr[..., None] * bre - ci[..., None] * bim
    bbi = cr[..., None] * bim + ci[..., None] * bre
    steps = jnp.arange(t8)
    pr, pi = lam_pow(t8 - 1 - steps)
    wsr = pr[..., None] * bbr[None] - pi[..., None] * bbi[None]
    wsi = pr[..., None] * bbi[None] + pi[..., None] * bbr[None]

    def interleave(re, im, axis):
        shape = list(re.shape)
        shape[axis:axis + 1] = [re.shape[axis] // LANES, 1, LANES]
        both = jnp.concatenate([re.reshape(shape), im.reshape(shape)], axis=axis + 1)
        shape[axis:axis + 3] = [2 * re.shape[axis]]
        return both.reshape(shape)

    wc = interleave(*[part.transpose(0, 3, 1, 2).reshape(t8 * c, g * p) for part in (wsr, wsi)], axis=1)
    cre, cim = c_re.astype(F32), c_im.astype(F32)
    pr, pi = lam_pow(steps + 1)
    vor = cre[None] * pr[:, :, None, :] - cim[None] * pi[:, :, None, :]
    voi = cre[None] * pi[:, :, None, :] + cim[None] * pr[:, :, None, :]
    vc = interleave(*[part.transpose(1, 3, 0, 2).reshape(g * p, t8 * c) for part in (vor, -voi)], axis=0)
    ct = interleave(*[jnp.tile(part.transpose(0, 2, 1).reshape(g * p, c), (1, g)) for part in (cre, -cim)], axis=0)

    def re_im(pair):
        return interleave(pair[0].reshape(-1, g * p), pair[1].reshape(-1, g * p), axis=1)

    lvl = re_im(lam_pow(t8 * 2 ** jnp.arange(SUBLANES)))
    pw = re_im(lam_pow(t8 * (steps + 1)))
    pwt = re_im(lam_pow(t8 * SUBLANES * (jnp.arange(n_tiles) + 1)))
    return wc, vc, ct, lvl, pw, pwt


def _expand_heads(cols, grp, head_lane):
    base = SSD_HPG * grp
    out = jnp.broadcast_to(cols[:, base:base + 1], (cols.shape[0], SSD_GW))
    for hh in range(1, SSD_HPG):
        out = jnp.where(head_lane == hh, cols[:, base + hh:base + hh + 1], out)
    return out


def _causal_conv(pad_ref, w_ref, b_ref, rows, cols):
    acc = b_ref[:, cols] + w_ref[CONV_K - 1:CONV_K, cols] * pad_ref[SUBLANES:SUBLANES + rows, cols]
    for k in range(CONV_K - 1):
        off = SUBLANES - (CONV_K - 1) + k
        acc = acc + w_ref[k:k + 1, cols] * pad_ref[off:off + rows, cols]
    return acc


_ProjBuf = collections.namedtuple("_ProjBuf", "z xpad dt rpad gr")
_MixW = collections.namedtuple(
    "_MixW", "wz wxbc wdt wxr wgr cw cb dtb aneg dvec nw rcw rcb wa ba wx bx rlam wout g b")
_MixScratch = collections.namedtuple("_MixScratch", "state ascr bscr cexp rcar")


def _col_tiles(width):
    return [slice(c, min(c + MXU_TILE, width)) for c in range(0, width, MXU_TILE)]


def _mixer_output_jobs(h_ref, ycat, w, o_ref):
    def tile_job(cols):
        def run():
            y = jnp.dot(ycat[...], w.wout[:, cols], preferred_element_type=F32)
            o_ref[:, cols] = ALPHA * h_ref[:, cols] + y
        return run

    def norm_job():
        o_ref[...] = _layer_norm(o_ref[...], w.g[...], w.b[...])

    return [tile_job(cols) for cols in _col_tiles(D_MODEL)] + [norm_job]


def _mixer_project_jobs(h_ref, h16, w, nxt, cur, first, tq):
    def cast_job():
        h16[...] = h_ref[...].astype(BF16)

    def tile_job(dst, w_ref, cols, row0):
        def run():
            dst[row0:row0 + tq, cols] = jnp.dot(h16[...], w_ref[:, cols], preferred_element_type=F32)
        return run

    def tail_job(pad_n, pad_c):
        def run():
            pad_n[0:SUBLANES, :] = jnp.where(first, 0.0, pad_c[tq:tq + SUBLANES, :])
        return run

    jobs = [cast_job]
    for dst, w_ref, row0 in ((nxt.z, w.wz, 0), (nxt.xpad, w.wxbc, SUBLANES), (nxt.dt, w.wdt, 0),
                             (nxt.rpad, w.wxr, SUBLANES), (nxt.gr, w.wgr, 0)):
        jobs += [tile_job(dst, w_ref, cols, row0) for cols in _col_tiles(dst.shape[1])]
    jobs += [tail_job(nxt.xpad, cur.xpad), tail_job(nxt.rpad, cur.rpad)]
    return jobs


def _mixer_compute(cur, ys5_ref, w, scr, ycat, first, tq, chunk, side):
    cw_ref, cb_ref, dtb_ref, aneg_ref, dvec_ref, nw_ref = w.cw, w.cb, w.dtb, w.aneg, w.dvec, w.nw
    state = scr.state
    z = cur.z
    slabs = []
    for cols in _col_tiles(SSD_XBC):
        side(1)
        slabs.append(_silu(_causal_conv(cur.xpad, cw_ref, cb_ref, tq, cols)))
    xact = jnp.concatenate(slabs, axis=-1)
    side(2)
    dt_all = _softplus(cur.dt[...] + dtb_ref[...])
    q = chunk
    tri = (lax.broadcasted_iota(jnp.int32, (q, q), 0) >= lax.broadcasted_iota(jnp.int32, (q, q), 1))
    row_id = lax.broadcasted_iota(jnp.int32, (q, LANES), 0)
    head_lane = lax.broadcasted_iota(jnp.int32, (1, SSD_GW), 1) // SSD_HEAD_DIM
    for ci in range(tq // q):
        r0 = ci * q
        dt = dt_all[r0:r0 + q, :]
        cs = dt * aneg_ref[...]
        dist = 1
        while dist < q:
            cs = cs + jnp.where(row_id >= dist, pltpu.roll(cs, dist, 0), 0.0)
            dist *= 2
        cs_t = cs.T
        cs_last = cs[q - 1:q, :]
        chunk_decay = jnp.exp(cs_last)
        dec_t = jnp.exp(cs_t[:, q - 1:q] - cs_t)
        ecs = jnp.exp(cs)
        zc = z[r0:r0 + q, :]
        halves = []
        for grp in range(SSD_GROUPS):
            side(2)
            lo = grp * SSD_GW
            xs = xact[r0:r0 + q, lo:lo + SSD_GW]
            bm = xact[r0:r0 + q, SSD_WIDTH + grp * SSD_STATE:SSD_WIDTH + (grp + 1) * SSD_STATE]
            cm = xact[r0:r0 + q, SSD_WIDTH + (SSD_GROUPS + grp) * SSD_STATE:
                      SSD_WIDTH + (SSD_GROUPS + grp + 1) * SSD_STATE]
            cm16 = cm.astype(BF16)
            xdt16 = (xs * _expand_heads(dt, grp, head_lane)).astype(BF16)
            cb = _dot_nt(cm16, bm)
            bm_t = bm.T
            s_prev = state[grp]
            if ci == 0:
                s_prev = jnp.where(first, 0.0, s_prev)
            y_off = jnp.dot(cm16, s_prev.astype(BF16), preferred_element_type=F32)
            intra, to_state, xdt_heads = [], [], []
            for hh in range(SSD_HPG):
                hd = grp * SSD_HPG + hh
                seg = cs[:, hd:hd + 1] - cs_t[hd:hd + 1, :]
                lmat = jnp.exp(jnp.where(tri, seg, NEG_BIG))
                intra.append((cb * lmat).astype(BF16))
                to_state.append((bm_t * dec_t[hd:hd + 1, :]).astype(BF16))
                xdt_heads.append(jnp.where(head_lane == hh, xdt16, jnp.zeros_like(xdt16)))
            lhs = jnp.concatenate([jnp.concatenate(intra, axis=1), jnp.concatenate(to_state, axis=1)], axis=0)
            both = jnp.dot(lhs, jnp.concatenate(xdt_heads, axis=0), preferred_element_type=F32)
            y_diag = both[:q]
            st_new = both[q:]
            state[grp] = s_prev * _expand_heads(chunk_decay, grp, head_lane) + st_new
            halves.append(y_diag + y_off * _expand_heads(ecs, grp, head_lane) + xs * dvec_ref[:, lo:lo + SSD_GW])
        y = jnp.concatenate(halves, axis=-1) * _silu(zc)
        y = y * lax.rsqrt(jnp.mean(y * y, axis=-1, keepdims=True) + LN_EPS) * nw_ref[...]
        ycat[r0:r0 + q, 0:SSD_WIDTH] = y.astype(BF16)

    ycat[:, SSD_WIDTH:SSD_WIDTH + S5_WIDTH] = ys5_ref[...]

    side(1)
    xc = _causal_conv(cur.rpad, w.rcw, w.rcb, tq, slice(0, RG_WIDTH))
    xc16 = xc.astype(BF16)
    rgate = _sigmoid(jnp.dot(xc16, w.wa[...], preferred_element_type=F32) + w.ba[...])
    igate = _sigmoid(jnp.dot(xc16, w.wx[...], preferred_element_type=F32) + w.bx[...])
    log_a = (-RG_C) * rgate * _softplus(-w.rlam[...])
    a = jnp.exp(log_a)
    bt = jnp.sqrt(1.0 - jnp.exp(2.0 * log_a)) * (igate * xc)
    row8 = lax.broadcasted_iota(jnp.int32, (tq, RG_WIDTH), 0) & (SUBLANES - 1)
    side(1)
    for dist in (1, 2, 4):
        keep = row8 >= dist
        a_sh = jnp.where(keep, pltpu.roll(a, dist, 0), 1.0)
        b_sh = jnp.where(keep, pltpu.roll(bt, dist, 0), 0.0)
        bt = bt + a * b_sh
        a = a * a_sh
    nt = tq // SUBLANES
    lane_halves = RG_WIDTH // LANES
    for hf in range(lane_halves):
        scr.ascr[hf] = a[:, hf * LANES:(hf + 1) * LANES]
        scr.bscr[hf] = bt[:, hf * LANES:(hf + 1) * LANES]
    ae = jnp.concatenate([scr.ascr[hf, pl.ds(SUBLANES - 1, nt, stride=SUBLANES), :] for hf in range(lane_halves)], -1)
    be = jnp.concatenate([scr.bscr[hf, pl.ds(SUBLANES - 1, nt, stride=SUBLANES), :] for hf in range(lane_halves)], -1)
    tile_id = lax.broadcasted_iota(jnp.int32, (nt, RG_WIDTH), 0)
    dist = 1
    while dist < nt:
        keep = tile_id >= dist
        a_sh = jnp.where(keep, pltpu.roll(ae, dist, 0), 1.0)
        b_sh = jnp.where(keep, pltpu.roll(be, dist, 0), 0.0)
        be = be + ae * b_sh
        ae = ae * a_sh
        dist *= 2
    carry_in = jnp.where(first, 0.0, scr.rcar[0:1, :])
    tile_end = be + ae * carry_in
    scr.rcar[0:1, :] = tile_end[nt - 1:nt, :]
    tile_in = jnp.where(tile_id >= 1, pltpu.roll(tile_end, 1, 0), carry_in)
    for hf in range(lane_halves):
        for k in range(SUBLANES):
            scr.cexp[hf, pl.ds(k, nt, stride=SUBLANES), :] = tile_in[:, hf * LANES:(hf + 1) * LANES]
    hscan = bt + a * jnp.concatenate([scr.cexp[hf] for hf in range(lane_halves)], axis=-1)
    ycat[:, SSD_WIDTH + S5_WIDTH:] = (hscan * _gelu(cur.gr[...])).astype(BF16)


def _mixer_kernel(hp_ref, ho_ref, ys5_ref, *rest, tq, chunk, nblk):
    nw = len(_MixW._fields)
    w = _MixW(*rest[:nw])
    o_ref = rest[nw]
    bufs = rest[nw + 1:]
    npb = len(_ProjBuf._fields)
    sets = (_ProjBuf(*bufs[:npb]), _ProjBuf(*bufs[npb:2 * npb]))
    ycats = bufs[2 * npb:2 * npb + 2]
    scr = _MixScratch(*bufs[2 * npb + 2:2 * npb + 2 + len(_MixScratch._fields)])
    h16 = bufs[-1]
    s = pl.program_id(0)

    @pl.when(s == 0)
    def _():
        for ref in bufs:
            ref[...] = jnp.zeros(ref.shape, ref.dtype)

    first_p = (s % nblk) == 0
    first_c = ((s + nblk - 1) % nblk) == 0

    def step(nxt, cur, ycat_c, ycat_o):
        jobs = (_mixer_output_jobs(ho_ref, ycat_o, w, o_ref)
                + _mixer_project_jobs(hp_ref, h16, w, nxt, cur, first_p, tq))

        def side(n):
            for _ in range(min(n, len(jobs))):
                jobs.pop(0)()

        _mixer_compute(cur, ys5_ref, w, scr, ycat_c, first_c, tq, chunk, side)
        side(len(jobs))

    @pl.when(s % 2 == 0)
    def _():
        step(sets[0], sets[1], ycats[1], ycats[0])

    @pl.when(s % 2 == 1)
    def _():
        step(sets[1], sets[0], ycats[0], ycats[1])


def _mixer(h, ys5, consts):
    b, l, d = h.shape
    tq = min(MIX_TQ, l)
    chunk = min(SSD_CHUNK, tq)
    nblk = l // tq
    total = b * nblk
    h2d = h.reshape(b * l, d)
    proj_buf = [
        pltpu.VMEM((tq, SSD_WIDTH), F32),
        pltpu.VMEM((tq + SUBLANES, SSD_XBC), F32),
        pltpu.VMEM((tq, LANES), F32),
        pltpu.VMEM((tq + SUBLANES, RG_WIDTH), F32),
        pltpu.VMEM((tq, RG_WIDTH), F32),
    ]
    rg_plane = pltpu.VMEM((RG_WIDTH // LANES, tq, LANES), F32)
    out = pl.pallas_call(
        functools.partial(_mixer_kernel, tq=tq, chunk=chunk, nblk=nblk),
        grid=(total + 2,),
        in_specs=[pl.BlockSpec((tq, d), lambda s: (jnp.minimum(s, total - 1), 0)),
                  pl.BlockSpec((tq, d), lambda s: (jnp.maximum(s - 2, 0), 0)),
                  pl.BlockSpec((tq, S5_WIDTH), lambda s: (jnp.clip(s - 1, 0, total - 1), 0))]
                 + [_const_spec(c.shape) for c in consts],
        out_specs=pl.BlockSpec((tq, d), lambda s: (jnp.maximum(s - 2, 0), 0)),
        out_shape=jax.ShapeDtypeStruct((b * l, d), F32),
        scratch_shapes=proj_buf + proj_buf + [
            pltpu.VMEM((tq, D_MODEL), BF16),
            pltpu.VMEM((tq, D_MODEL), BF16),
            pltpu.VMEM((SSD_GROUPS, SSD_STATE, SSD_GW), F32),
            rg_plane, rg_plane, rg_plane,
            pltpu.VMEM((SUBLANES, RG_WIDTH), F32),
            pltpu.VMEM((tq, d), BF16),
        ],
        compiler_params=_params(("arbitrary",)),
        name="mixer",
    )(h2d, h2d, ys5.reshape(b * l, S5_WIDTH), *consts)
    return out.reshape(b, l, d)


def _block_diag(w):
    nb, bi, bo = w.shape
    return jnp.einsum('hij,hk->hikj', w, jnp.eye(nb, dtype=w.dtype)).reshape(nb * bi, nb * bo)


def _kv_kernel(m_ref, wk_ref, wv_ref, k_ref, v_ref):
    m16 = m_ref[...].astype(BF16)
    k_ref[...] = jnp.dot(m16, wk_ref[...], preferred_element_type=F32).astype(BF16)
    v_ref[...] = jnp.dot(m16, wv_ref[...], preferred_element_type=F32).astype(BF16)


def _kv(mem, wk, wv):
    b, m, d = mem.shape
    spec = pl.BlockSpec((None, m, d), lambda i: (i, 0, 0))
    return pl.pallas_call(
        _kv_kernel,
        grid=(b,),
        in_specs=[spec, _const_spec(wk.shape), _const_spec(wv.shape)],
        out_specs=[spec, spec],
        out_shape=[jax.ShapeDtypeStruct((b, m, d), BF16)] * 2,
        compiler_params=_params(("parallel",)),
        name="kv",
    )(mem, wk, wv)


def _xattn_kernel(h_ref, k_ref, v_ref, wq_ref, wo_ref, g_ref, b_ref, o_ref):
    hb = h_ref[...]
    qall = jnp.dot(hb.astype(BF16), wq_ref[...], preferred_element_type=F32) * (1.0 / math.sqrt(XA_HEAD_DIM))
    outs = []
    for hd in range(XA_HEADS):
        sl = slice(hd * XA_HEAD_DIM, (hd + 1) * XA_HEAD_DIM)
        s = _dot_nt(qall[:, sl], k_ref[:, sl])
        p = jnp.exp(s - jnp.max(s, axis=-1, keepdims=True))
        o = jnp.dot(p.astype(BF16), v_ref[:, sl], preferred_element_type=F32)
        outs.append(o / jnp.sum(p, axis=-1, keepdims=True))
    o = jnp.concatenate(outs, axis=-1)
    y = jnp.dot(o.astype(BF16), wo_ref[...], preferred_element_type=F32)
    o_ref[...] = _layer_norm(ALPHA * hb + y, g_ref[...], b_ref[...])


def _xattn(h, k, v, wq, wo, g, bb):
    b, l, d = h.shape
    tm = min(TOK_TM, l)
    tok = pl.BlockSpec((None, tm, d), lambda i, j: (i, j, 0))
    kvs = pl.BlockSpec((None, k.shape[1], d), lambda i, j: (i, 0, 0))
    return pl.pallas_call(
        _xattn_kernel,
        grid=(b, l // tm),
        in_specs=[tok, kvs, kvs, _const_spec(wq.shape), _const_spec(wo.shape),
                  _const_spec(g.shape), _const_spec(bb.shape)],
        out_specs=tok,
        out_shape=jax.ShapeDtypeStruct((b, l, d), F32),
        compiler_params=_params(("parallel", "parallel")),
        name="xattn",
    )(h, k, v, wq, wo, g, bb)


def _mlp_kernel(h_ref, w1_ref, w2_ref, g_ref, b_ref, o_ref):
    hb = h_ref[...]
    h16 = hb.astype(BF16)
    acc = ALPHA * hb
    for f in range(D_FF // MLP_FC):
        sl = slice(f * MLP_FC, (f + 1) * MLP_FC)
        t = jnp.maximum(jnp.dot(h16, w1_ref[:, sl], preferred_element_type=F32), 0.0)
        acc = acc + jnp.dot((t * t).astype(BF16), w2_ref[sl, :], preferred_element_type=F32)
    o_ref[...] = _layer_norm(acc, g_ref[...], b_ref[...])


def _mlp(h2d, w1, w2, g, bb):
    t, d = h2d.shape
    tm = min(TOK_TM, t)
    tok = pl.BlockSpec((tm, d), lambda i: (i, 0))
    return pl.pallas_call(
        _mlp_kernel,
        grid=(t // tm,),
        in_specs=[tok, _const_spec(w1.shape), _const_spec(w2.shape), _const_spec(g.shape), _const_spec(bb.shape)],
        out_specs=tok,
        out_shape=jax.ShapeDtypeStruct((t, d), F32),
        compiler_params=_params(("parallel",)),
        name="mlp",
    )(h2d, w1, w2, g, bb)


def _row(v):
    return v.astype(F32).reshape(1, -1)


def _pad_lanes(v2d):
    return jnp.pad(v2d, ((0, 0), (0, LANES - v2d.shape[1])))


def mixer_sublayer(h, w_in, w_out, ssd_conv_w, ssd_conv_b, ssd_dt_bias, ssd_a_log, ssd_d, ssd_norm_w,
                   s5_lam_re, s5_lam_im, s5_log_step, s5_b_re, s5_b_im, s5_c_re, s5_c_im, s5_d, s5_glu_w,
                   s5_glu_b, rg_conv_w, rg_conv_b, rg_wa, rg_ba, rg_wx, rg_bx, rg_lambda, ln_g, ln_b):
    o_z, o_xbc, o_dt = 0, SSD_WIDTH, SSD_WIDTH + SSD_XBC
    o_u = o_dt + SSD_HEADS
    o_xr = o_u + S5_WIDTH
    o_gr = o_xr + RG_WIDTH
    w16 = w_in.astype(BF16)
    n_tiles = min(S5_TB, h.shape[1]) // (S5_BLK * SUBLANES)
    assert n_tiles <= 2 ** (SUBLANES - 3), "S5 scan-level table holds 3 in-tile + 5 tile levels"
    s5_ops = _s5_prepare(s5_lam_re, s5_lam_im, s5_log_step, s5_b_re, s5_b_im, s5_c_re, s5_c_im, n_tiles)
    ys5 = _s5_branch(h, (w16[:, o_u:o_xr], *s5_ops, _row(s5_d), s5_glu_w.astype(BF16), _row(s5_glu_b)))
    consts = (
        w16[:, o_z:o_xbc], w16[:, o_xbc:o_dt], _pad_lanes(w16[:, o_dt:o_u]), w16[:, o_xr:o_gr], w16[:, o_gr:],
        ssd_conv_w.astype(F32), _row(ssd_conv_b), _pad_lanes(_row(ssd_dt_bias)),
        _pad_lanes(_row(-jnp.exp(ssd_a_log.astype(F32)))),
        _row(jnp.repeat(ssd_d.astype(F32), SSD_HEAD_DIM)), _row(ssd_norm_w),
        rg_conv_w.astype(F32), _row(rg_conv_b),
        _block_diag(rg_wa.astype(F32)).astype(BF16), _row(rg_ba),
        _block_diag(rg_wx.astype(F32)).astype(BF16), _row(rg_bx), _row(rg_lambda),
        w_out.astype(BF16), _row(ln_g), _row(ln_b),
    )
    return _mixer(h, ys5, consts)


def xattn_sublayer(h, mem, wq, wk, wv, wo, ln_g, ln_b):
    k, v = _kv(mem, wk.astype(BF16), wv.astype(BF16))
    return _xattn(h, k, v, wq.astype(BF16), wo.astype(BF16), _row(ln_g), _row(ln_b))


def mlp_sublayer(h, w1, w2, ln_g, ln_b):
    b, l, d = h.shape
    return _mlp(h.reshape(b * l, d), w1.astype(BF16), w2.astype(BF16), _row(ln_g), _row(ln_b)).reshape(b, l, d)


def kernel(x, mem, w_in, w_out, ssd_conv_w, ssd_conv_b, ssd_dt_bias, ssd_a_log, ssd_d, ssd_norm_w, s5_lam_re, s5_lam_im, s5_log_step, s5_b_re, s5_b_im, s5_c_re, s5_c_im, s5_d, s5_glu_w, s5_glu_b, rg_conv_w, rg_conv_b, rg_wa, rg_ba, rg_wx, rg_bx, rg_lambda, ln1_g, ln1_b, xa_wq, xa_wk, xa_wv, xa_wo, ln2_g, ln2_b, mlp_w1, mlp_w2, ln3_g, ln3_b):
    h = x.astype(F32)
    memf = mem.astype(F32)
    for l in range(DEPTH):
        h = mixer_sublayer(h, w_in[l], w_out[l], ssd_conv_w[l], ssd_conv_b[l], ssd_dt_bias[l], ssd_a_log[l],
                           ssd_d[l], ssd_norm_w[l], s5_lam_re[l], s5_lam_im[l], s5_log_step[l], s5_b_re[l],
                           s5_b_im[l], s5_c_re[l], s5_c_im[l], s5_d[l], s5_glu_w[l], s5_glu_b[l],
                           rg_conv_w[l], rg_conv_b[l], rg_wa[l], rg_ba[l], rg_wx[l], rg_bx[l], rg_lambda[l],
                           ln1_g[l], ln1_b[l])
        h = xattn_sublayer(h, memf, xa_wq[l], xa_wk[l], xa_wv[l], xa_wo[l], ln2_g[l], ln2_b[l])
        h = mlp_sublayer(h, mlp_w1[l], mlp_w2[l], ln3_g[l], ln3_b[l])
    return h.astype(x.dtype)
```

```python
import collections
import functools
import math

import jax
import jax.numpy as jnp
from jax import lax
from jax.experimental import pallas as pl
from jax.experimental.pallas import tpu as pltpu

F32 = jnp.float32
BF16 = jnp.bfloat16

D_MODEL = 1024
DEPTH = 2
SSD_WIDTH = 512
SSD_HEAD_DIM = 64
SSD_HEADS = 8
SSD_GROUPS = 2
SSD_HPG = SSD_HEADS // SSD_GROUPS
SSD_GW = SSD_HPG * SSD_HEAD_DIM
SSD_STATE = 128
SSD_XBC = 1024
CONV_K = 4
S5_WIDTH = 256
S5_GROUPS = 16
S5_GROUP_CH = 16
S5_STATE = 64
S5_NSTATE = S5_GROUPS * S5_STATE
S5_BLK = 8
RG_WIDTH = 256
RG_BLOCKS = 4
RG_C = 8.0
XA_HEADS = 4
XA_HEAD_DIM = 256
D_FF = 4096
ALPHA = (2.0 * DEPTH) ** 0.25
LN_EPS = 1e-5

LANES = 128
SUBLANES = 8
MXU_TILE = 256
NEG_BIG = -1e30
VMEM_LIMIT = 56 * 1024 * 1024

SSD_CHUNK = 128
MIX_TQ = 256
S5_TB = 2048
TOK_TM = 512
MLP_FC = 1024


def _dot(a, b):
    return jnp.dot(a.astype(BF16), b.astype(BF16), preferred_element_type=F32)


def _dot_nt(a, b):
    return lax.dot_general(a.astype(BF16), b.astype(BF16), (((1,), (1,)), ((), ())),
                           preferred_element_type=F32)


def _layer_norm(x, g, b):
    mu = jnp.mean(x, axis=-1, keepdims=True)
    xc = x - mu
    var = jnp.mean(xc * xc, axis=-1, keepdims=True)
    return xc * lax.rsqrt(var + LN_EPS) * g + b


def _gelu(x):
    c = math.sqrt(2.0 / math.pi)
    return 0.5 * x * (1.0 + jnp.tanh(c * (x + 0.044715 * (x * x * x))))


def _sigmoid(x):
    return 0.5 + 0.5 * jnp.tanh(0.5 * x)


def _silu(x):
    hx = 0.5 * x
    return hx + hx * jnp.tanh(hx)


def _softplus(x):
    return jnp.maximum(x, 0.0) + jnp.log1p(jnp.exp(-jnp.abs(x)))


def _const_spec(shape):
    nd = len(shape)
    return pl.BlockSpec(shape, lambda *_: (0,) * nd, pipeline_mode=pl.Buffered(1))


def _layer_spec(stacked, layer):
    nd = stacked.ndim - 1
    return pl.BlockSpec((None,) + stacked.shape[1:], lambda *_: (layer,) + (0,) * nd, pipeline_mode=pl.Buffered(1))


def _params(sem):
    return pltpu.CompilerParams(dimension_semantics=sem, vmem_limit_bytes=VMEM_LIMIT)


def _re_lanes(p):
    return slice(2 * p * LANES, (2 * p + 1) * LANES)


def _im_lanes(p):
    return slice((2 * p + 1) * LANES, (2 * p + 2) * LANES)


def _s5_state_group(idx):
    state = (idx // (2 * LANES)) * LANES + idx % LANES
    return state // S5_STATE


def _s5_split(x):
    planes = S5_NSTATE // LANES
    return (jnp.concatenate([x[:, _re_lanes(p)] for p in range(planes)], axis=-1),
            jnp.concatenate([x[:, _im_lanes(p)] for p in range(planes)], axis=-1))


S5_HALF_CH = S5_WIDTH // 2
S5_HALF_LANES = S5_BLK * S5_HALF_CH


def _s5_expand_operators(wc_ref, vc_ref, ct_ref, ms_scr, toep_scr, mo_scr):
    n, w, c = S5_NSTATE, S5_WIDTH, S5_GROUP_CH
    hw, hl = S5_HALF_CH, S5_HALF_LANES
    grp_half = S5_GROUPS // 2
    reps = hw // c
    grp_r = _s5_state_group(lax.broadcasted_iota(jnp.int32, (2 * n, w), 0))
    same_o = grp_r == lax.broadcasted_iota(jnp.int32, (2 * n, w), 1) // c
    mo0 = jnp.where(same_o, ct_ref[...], 0.0)
    kc = jnp.dot(wc_ref[...], mo0, preferred_element_type=F32, precision=lax.Precision.HIGHEST)
    toep_scr[...] = jnp.zeros(toep_scr.shape, toep_scr.dtype)
    ch_grp = lax.broadcasted_iota(jnp.int32, (hw, hl), 0) // c
    same_k = (lax.broadcasted_iota(jnp.int32, (hw, hw), 0) // c) == (lax.broadcasted_iota(jnp.int32, (hw, hw), 1) // c)
    pick_r = lax.broadcasted_iota(jnp.int32, (S5_BLK * c, hl), 0)
    pick_c = lax.broadcasted_iota(jnp.int32, (S5_BLK * c, hl), 1)
    pick = ((pick_r // c == pick_c // hw) & (pick_r % c == pick_c % c)).astype(BF16)
    for hf in range(2):
        lanes = slice(hf * hl, (hf + 1) * hl)
        st_grp = _s5_state_group(lax.broadcasted_iota(jnp.int32, (hw, hl), 1) + hf * hl) - hf * grp_half
        same_s = ch_grp == st_grp
        for t in range(S5_BLK):
            blk = jnp.where(same_s, jnp.tile(wc_ref[t * c:(t + 1) * c, lanes], (reps, 1)), 0.0)
            ms_scr[hf, t * hw:(t + 1) * hw, :] = blk.astype(BF16)
        st_grp = _s5_state_group(lax.broadcasted_iota(jnp.int32, (hl, hl), 0) + hf * hl) - hf * grp_half
        same_r = st_grp == (lax.broadcasted_iota(jnp.int32, (hl, hl), 1) % hw) // c
        spread = jnp.dot(vc_ref[lanes, :].astype(BF16), pick, preferred_element_type=F32)
        mo_scr[hf] = jnp.where(same_r, spread, 0.0).astype(BF16)
        for t_in in range(S5_BLK):
            for t_out in range(t_in, S5_BLK):
                k = S5_BLK - 1 - (t_out - t_in)
                blk = jnp.where(same_k, jnp.tile(kc[k * c:(k + 1) * c, hf * hw:(hf + 1) * hw], (reps, 1)), 0.0)
                toep_scr[hf, t_in * hw:(t_in + 1) * hw, t_out * hw:(t_out + 1) * hw] = blk.astype(BF16)


_S5W = collections.namedtuple("_S5W", "wu wc vc ct lvl pw pwt d gluw glub")
_S5Ops = collections.namedtuple("_S5Ops", "ms toep mo")
_S5Buf = collections.namedtuple("_S5Buf", "u32 u16 hprev16")
_S5Scratch = collections.namedtuple("_S5Scratch", "u_tok hplane cexp carry yg")


def _s5_scan_stage(h_ref, w, ops, scr, out, first, rows, side):
    n, wd = S5_NSTATE, S5_WIDTH
    lane_halves = wd // LANES
    planes = n // LANES
    nt = rows // SUBLANES
    tb = rows * S5_BLK
    for r0 in range(0, tb, tb // 4):
        rs = slice(r0, r0 + tb // 4)
        u_tok = jnp.dot(h_ref[rs, :].astype(BF16), w.wu[...], preferred_element_type=F32)
        for hf in range(lane_halves):
            scr.u_tok[hf, rs, :] = u_tok[:, hf * LANES:(hf + 1) * LANES]
    u = jnp.concatenate([scr.u_tok[hf, pl.ds(k, rows, stride=S5_BLK), :]
                         for hf in range(lane_halves) for k in range(S5_BLK)], axis=-1)
    out.u32[...] = u
    u16 = u.astype(BF16)
    out.u16[...] = u16
    row8 = lax.broadcasted_iota(jnp.int32, (nt, SUBLANES, LANES), 1)
    planes_per_half = planes // lane_halves
    for p in range(planes):
        hf, pl_in_half = divmod(p, planes_per_half)
        s = jnp.dot(u16[:, hf * S5_HALF_LANES:(hf + 1) * S5_HALF_LANES],
                    ops.ms[hf, :, 2 * pl_in_half * LANES:(2 * pl_in_half + 2) * LANES],
                    preferred_element_type=F32)
        if p < planes - 2:
            side(1)
        xr = s[:, :LANES].reshape(nt, SUBLANES, LANES)
        xi = s[:, LANES:].reshape(nt, SUBLANES, LANES)
        for lv, dist in enumerate((1, 2, 4)):
            lr = w.lvl[lv:lv + 1, _re_lanes(p)]
            li = w.lvl[lv:lv + 1, _im_lanes(p)]
            keep = row8 >= dist
            sr = jnp.where(keep, pltpu.roll(xr, dist, 1), 0.0)
            si = jnp.where(keep, pltpu.roll(xi, dist, 1), 0.0)
            xr, xi = xr + lr * sr - li * si, xi + lr * si + li * sr
        scr.hplane[2 * p, SUBLANES:SUBLANES + rows, :] = xr.reshape(rows, LANES)
        scr.hplane[2 * p + 1, SUBLANES:SUBLANES + rows, :] = xi.reshape(rows, LANES)
    side(1)
    last = 2 * SUBLANES - 1
    er = jnp.concatenate([scr.hplane[2 * p, pl.ds(last, nt, stride=SUBLANES), :] for p in range(planes)], axis=-1)
    ei = jnp.concatenate([scr.hplane[2 * p + 1, pl.ds(last, nt, stride=SUBLANES), :] for p in range(planes)], -1)
    tile_id = lax.broadcasted_iota(jnp.int32, (nt, n), 0)
    lv, dist = 3, 1
    while dist < nt:
        lr, li = _s5_split(w.lvl[lv:lv + 1, :])
        keep = tile_id >= dist
        sr = jnp.where(keep, pltpu.roll(er, dist, 0), 0.0)
        si = jnp.where(keep, pltpu.roll(ei, dist, 0), 0.0)
        er, ei = er + lr * sr - li * si, ei + lr * si + li * sr
        lv, dist = lv + 1, dist * 2
    cr, ci = _s5_split(jnp.where(first, 0.0, scr.carry[0:1, :]))
    ptr, pti = _s5_split(w.pwt[...])
    er, ei = er + ptr * cr - pti * ci, ei + ptr * ci + pti * cr
    for p in range(planes):
        scr.carry[0:1, _re_lanes(p)] = er[nt - 1:nt, p * LANES:(p + 1) * LANES]
        scr.carry[0:1, _im_lanes(p)] = ei[nt - 1:nt, p * LANES:(p + 1) * LANES]
    tin_r = jnp.where(tile_id >= 1, pltpu.roll(er, 1, 0), cr)
    tin_i = jnp.where(tile_id >= 1, pltpu.roll(ei, 1, 0), ci)
    side(1)
    for p in range(planes):
        for k in range(SUBLANES):
            scr.cexp[2 * p, pl.ds(k, nt, stride=SUBLANES), :] = tin_r[:, p * LANES:(p + 1) * LANES]
            scr.cexp[2 * p + 1, pl.ds(k, nt, stride=SUBLANES), :] = tin_i[:, p * LANES:(p + 1) * LANES]
    for p in range(planes):
        if p % 2 == 0:
            side(1)
        pr = jnp.tile(w.pw[:, _re_lanes(p)], (nt, 1))
        pi = jnp.tile(w.pw[:, _im_lanes(p)], (nt, 1))
        tr = scr.cexp[2 * p]
        ti = scr.cexp[2 * p + 1]
        hr = scr.hplane[2 * p, SUBLANES:SUBLANES + rows, :] + pr * tr - pi * ti
        hi = scr.hplane[2 * p + 1, SUBLANES:SUBLANES + rows, :] + pr * ti + pi * tr
        scr.hplane[2 * p, SUBLANES:SUBLANES + rows, :] = hr
        scr.hplane[2 * p + 1, SUBLANES:SUBLANES + rows, :] = hi
        scr.hplane[2 * p, SUBLANES - 1:SUBLANES, :] = cr[:, p * LANES:(p + 1) * LANES]
        scr.hplane[2 * p + 1, SUBLANES - 1:SUBLANES, :] = ci[:, p * LANES:(p + 1) * LANES]
    for p in range(2 * planes):
        out.hprev16[:, p * LANES:(p + 1) * LANES] = scr.hplane[p, SUBLANES - 1:SUBLANES - 1 + rows, :].astype(BF16)


def _s5_readout_jobs(src, w, ops, scr, o_ref, rows):
    hw, hl = S5_HALF_CH, S5_HALF_LANES
    lane_halves = S5_WIDTH // LANES
    steps_per_tile = MXU_TILE // hw

    def tile_job(hf, j):
        def run():
            k_rows = (j + 1) * MXU_TILE
            cols = slice(j * MXU_TILE, (j + 1) * MXU_TILE)
            y = jnp.dot(src.u16[:, hf * hl:hf * hl + k_rows], ops.toep[hf, :k_rows, cols],
                        preferred_element_type=F32)
            y = y + jnp.dot(src.hprev16[:, hf * hl:(hf + 1) * hl], ops.mo[hf, :, cols], preferred_element_type=F32)
            d_half = w.d[:, hf * hw:(hf + 1) * hw]
            y = _gelu(y + jnp.concatenate([d_half] * steps_per_tile, axis=1) * src.u32[:, hf * hl + j * MXU_TILE:
                                                                                           hf * hl + (j + 1) * MXU_TILE])
            for i in range(steps_per_tile):
                scr.yg[hf, pl.ds(j * steps_per_tile + i, rows, stride=S5_BLK), :] = y[:, i * hw:(i + 1) * hw]
        return run

    def glu_job():
        yg = jnp.concatenate([scr.yg[hf] for hf in range(lane_halves)], axis=-1)
        glu = _sigmoid(_dot(yg, w.gluw[...]) + w.glub[...])
        o_ref[...] = (yg * glu).astype(o_ref.dtype)

    return [tile_job(hf, j) for hf in range(lane_halves) for j in range(hl // MXU_TILE)] + [glu_job]


def _s5_kernel(h_ref, *rest, rows, nblk):
    nw = len(_S5W._fields)
    w = _S5W(*rest[:nw])
    o_ref = rest[nw]
    bufs = rest[nw + 1:]
    ops = _S5Ops(*bufs[:3])
    nb = len(_S5Buf._fields)
    sets = (_S5Buf(*bufs[3:3 + nb]), _S5Buf(*bufs[3 + nb:3 + 2 * nb]))
    scr = _S5Scratch(*bufs[3 + 2 * nb:])
    s = pl.program_id(0)

    @pl.when(s == 0)
    def _():
        _s5_expand_operators(w.wc, w.vc, w.ct, *ops)
        for ref in bufs[3:]:
            ref[...] = jnp.zeros(ref.shape, ref.dtype)

    first = (s % nblk) == 0

    def step(out, src):
        jobs = _s5_readout_jobs(src, w, ops, scr, o_ref, rows)

        def side(k):
            for _ in range(min(k, len(jobs))):
                jobs.pop(0)()

        _s5_scan_stage(h_ref, w, ops, scr, out, first, rows, side)
        side(len(jobs))

    @pl.when(s % 2 == 0)
    def _():
        step(sets[0], sets[1])

    @pl.when(s % 2 == 1)
    def _():
        step(sets[1], sets[0])


def _s5_branch(h, consts, layer):
    b, l, d = h.shape
    tb = min(S5_TB, l)
    rows = tb // S5_BLK
    nblk = l // tb
    total = b * nblk
    n, w = S5_NSTATE, S5_WIDTH
    handoff = [
        pltpu.VMEM((rows, S5_BLK * w), F32),
        pltpu.VMEM((rows, S5_BLK * w), BF16),
        pltpu.VMEM((rows, 2 * n), BF16),
    ]
    out = pl.pallas_call(
        functools.partial(_s5_kernel, rows=rows, nblk=nblk),
        grid=(total + 1,),
        in_specs=[pl.BlockSpec((tb, d), lambda s: (jnp.minimum(s, total - 1), 0))]
                 + [_layer_spec(c, layer) for c in consts],
        out_specs=pl.BlockSpec((tb, w), lambda s: (jnp.maximum(s - 1, 0), 0)),
        out_shape=jax.ShapeDtypeStruct((b * l, w), BF16),
        scratch_shapes=[
            pltpu.VMEM((2, S5_HALF_LANES, S5_HALF_LANES), BF16),
            pltpu.VMEM((2, S5_HALF_LANES, S5_HALF_LANES), BF16),
            pltpu.VMEM((2, S5_HALF_LANES, S5_HALF_LANES), BF16),
        ] + handoff + handoff + [
            pltpu.VMEM((w // LANES, tb, LANES), F32),
            pltpu.VMEM((2 * n // LANES, rows + SUBLANES, LANES), F32),
            pltpu.VMEM((2 * n // LANES, rows, LANES), F32),
            pltpu.VMEM((SUBLANES, 2 * n), F32),
            pltpu.VMEM((w // LANES, tb, LANES), F32),
        ],
        compiler_params=_params(("arbitrary",)),
        name="s5_branch",
    )(h.reshape(b * l, d), *consts)
    return out.reshape(b, l, w)


def _s5_prepare(lam_re, lam_im, log_step, b_re, b_im, c_re, c_im, n_tiles):
    g, p, c, t8 = S5_GROUPS, S5_STATE, S5_GROUP_CH, S5_BLK
    lr, li = lam_re.astype(F32), lam_im.astype(F32)
    dt = jnp.exp(log_step.astype(F32))[:, None]
    ar, ai = lr * dt, li * dt

    def lam_pow(k):
        k = k.astype(F32)[..., None, None]
        mag = jnp.exp(k * ar)
        return mag * jnp.cos(k * ai), mag * jnp.sin(k * ai)

    br1, bi1 = lam_pow(jnp.ones((), jnp.int32))
    xr_, xi_ = br1 - 1.0, bi1
    den = lr * lr + li * li
    cr, ci = (xr_ * lr + xi_ * li) / den, (xi_ * lr - xr_ * li) / den
    bre, bim = b_re.astype(F32), b_im.astype(F32)
    bbr = cr[..., None] * bre - ci[..., None] * bim
    bbi = cr[..., None] * bim + ci[..., None] * bre
    steps = jnp.arange(t8)
    pr, pi = lam_pow(t8 - 1 - steps)
    wsr = pr[..., None] * bbr[None] - pi[..., None] * bbi[None]
    wsi = pr[..., None] * bbi[None] + pi[..., None] * bbr[None]

    def interleave(re, im, axis):
        shape = list(re.shape)
        shape[axis:axis + 1] = [re.shape[axis] // LANES, 1, LANES]
        both = jnp.concatenate([re.reshape(shape), im.reshape(shape)], axis=axis + 1)
        shape[axis:axis + 3] = [2 * re.shape[axis]]
        return both.reshape(shape)

    wc = interleave(*[part.transpose(0, 3, 1, 2).reshape(t8 * c, g * p) for part in (wsr, wsi)], axis=1)
    cre, cim = c_re.astype(F32), c_im.astype(F32)
    pr, pi = lam_pow(steps + 1)
    vor = cre[None] * pr[:, :, None, :] - cim[None] * pi[:, :, None, :]
    voi = cre[None] * pi[:, :, None, :] + cim[None] * pr[:, :, None, :]
    vc = interleave(*[part.transpose(1, 3, 0, 2).reshape(g * p, t8 * c) for part in (vor, -voi)], axis=0)
    ct = interleave(*[jnp.tile(part.transpose(0, 2, 1).reshape(g * p, c), (1, g)) for part in (cre, -cim)], axis=0)

    def re_im(pair):
        return interleave(pair[0].reshape(-1, g * p), pair[1].reshape(-1, g * p), axis=1)

    lvl = re_im(lam_pow(t8 * 2 ** jnp.arange(SUBLANES)))
    pw = re_im(lam_pow(t8 * (steps + 1)))
    pwt = re_im(lam_pow(t8 * SUBLANES * (jnp.arange(n_tiles) + 1)))
    return wc, vc, ct, lvl, pw, pwt


def _expand_heads(cols, grp, head_lane):
    base = SSD_HPG * grp
    out = jnp.broadcast_to(cols[:, base:base + 1], (cols.shape[0], SSD_GW))
    for hh in range(1, SSD_HPG):
        out = jnp.where(head_lane == hh, cols[:, base + hh:base + hh + 1], out)
    return out


def _causal_conv(pad_ref, w_ref, b_ref, rows, cols):
    acc = b_ref[:, cols] + w_ref[CONV_K - 1:CONV_K, cols] * pad_ref[SUBLANES:SUBLANES + rows, cols]
    for k in range(CONV_K - 1):
        off = SUBLANES - (CONV_K - 1) + k
        acc = acc + w_ref[k:k + 1, cols] * pad_ref[off:off + rows, cols]
    return acc


_ProjBuf = collections.namedtuple("_ProjBuf", "z xpad dt rpad gr")
_MixW = collections.namedtuple(
    "_MixW", "wz wxbc wdt wxr wgr cw cb dtb aneg dvec nw rcw rcb wa ba wx bx rlam wout g b")
_MixScratch = collections.namedtuple("_MixScratch", "state ascr bscr cexp rcar")


def _col_tiles(width):
    return [slice(c, min(c + MXU_TILE, width)) for c in range(0, width, MXU_TILE)]


def _mixer_output_jobs(h_ref, ycat, w, o_ref):
    def tile_job(cols):
        def run():
            y = jnp.dot(ycat[...], w.wout[:, cols], preferred_element_type=F32)
            o_ref[:, cols] = ALPHA * h_ref[:, cols] + y
        return run

    def norm_job():
        o_ref[...] = _layer_norm(o_ref[...], w.g[...], w.b[...])

    return [tile_job(cols) for cols in _col_tiles(D_MODEL)] + [norm_job]


def _mixer_project_jobs(h_ref, h16, w, nxt, cur, first, tq):
    def cast_job():
        h16[...] = h_ref[...].astype(BF16)

    def tile_job(dst, w_ref, cols, row0):
        def run():
            dst[row0:row0 + tq, cols] = jnp.dot(h16[...], w_ref[:, cols], preferred_element_type=F32)
        return run

    def tail_job(pad_n, pad_c):
        def run():
            pad_n[0:SUBLANES, :] = jnp.where(first, 0.0, pad_c[tq:tq + SUBLANES, :])
        return run

    jobs = [cast_job]
    for dst, w_ref, row0 in ((nxt.z, w.wz, 0), (nxt.xpad, w.wxbc, SUBLANES), (nxt.dt, w.wdt, 0),
                             (nxt.rpad, w.wxr, SUBLANES), (nxt.gr, w.wgr, 0)):
        jobs += [tile_job(dst, w_ref, cols, row0) for cols in _col_tiles(dst.shape[1])]
    jobs += [tail_job(nxt.xpad, cur.xpad), tail_job(nxt.rpad, cur.rpad)]
    return jobs


def _mixer_compute(cur, ys5_ref, w, scr, ycat, first, tq, chunk, side):
    cw_ref, cb_ref, dtb_ref, aneg_ref, dvec_ref, nw_ref = w.cw, w.cb, w.dtb, w.aneg, w.dvec, w.nw
    state = scr.state
    z = cur.z
    slabs = []
    for cols in _col_tiles(SSD_XBC):
        side(1)
        slabs.append(_silu(_causal_conv(cur.xpad, cw_ref, cb_ref, tq, cols)))
    xact = jnp.concatenate(slabs, axis=-1)
    side(2)
    dt_all = _softplus(cur.dt[...] + dtb_ref[...])
    q = chunk
    tri = (lax.broadcasted_iota(jnp.int32, (q, q), 0) >= lax.broadcasted_iota(jnp.int32, (q, q), 1))
    row_id = lax.broadcasted_iota(jnp.int32, (q, LANES), 0)
    head_lane = lax.broadcasted_iota(jnp.int32, (1, SSD_GW), 1) // SSD_HEAD_DIM
    for ci in range(tq // q):
        r0 = ci * q
        dt = dt_all[r0:r0 + q, :]
        cs = dt * aneg_ref[...]
        dist = 1
        while dist < q:
            cs = cs + jnp.where(row_id >= dist, pltpu.roll(cs, dist, 0), 0.0)
            dist *= 2
        cs_t = cs.T
        cs_last = cs[q - 1:q, :]
        chunk_decay = jnp.exp(cs_last)
        dec_t = jnp.exp(cs_t[:, q - 1:q] - cs_t)
        ecs = jnp.exp(cs)
        zc = z[r0:r0 + q, :]
        halves = []
        for grp in range(SSD_GROUPS):
            side(2)
            lo = grp * SSD_GW
            xs = xact[r0:r0 + q, lo:lo + SSD_GW]
            bm = xact[r0:r0 + q, SSD_WIDTH + grp * SSD_STATE:SSD_WIDTH + (grp + 1) * SSD_STATE]
            cm = xact[r0:r0 + q, SSD_WIDTH + (SSD_GROUPS + grp) * SSD_STATE:
                      SSD_WIDTH + (SSD_GROUPS + grp + 1) * SSD_STATE]
            cm16 = cm.astype(BF16)
            xdt16 = (xs * _expand_heads(dt, grp, head_lane)).astype(BF16)
            cb = _dot_nt(cm16, bm)
            bm_t = bm.T
            s_prev = state[grp]
            if ci == 0:
                s_prev = jnp.where(first, 0.0, s_prev)
            y_off = jnp.dot(cm16, s_prev.astype(BF16), preferred_element_type=F32)
            intra, to_state, xdt_heads = [], [], []
            for hh in range(SSD_HPG):
                hd = grp * SSD_HPG + hh
                seg = cs[:, hd:hd + 1] - cs_t[hd:hd + 1, :]
                lmat = jnp.exp(jnp.where(tri, seg, NEG_BIG))
                intra.append((cb * lmat).astype(BF16))
                to_state.append((bm_t * dec_t[hd:hd + 1, :]).astype(BF16))
                xdt_heads.append(jnp.where(head_lane == hh, xdt16, jnp.zeros_like(xdt16)))
            lhs = jnp.concatenate([jnp.concatenate(intra, axis=1), jnp.concatenate(to_state, axis=1)], axis=0)
            both = jnp.dot(lhs, jnp.concatenate(xdt_heads, axis=0), preferred_element_type=F32)
            y_diag = both[:q]
            st_new = both[q:]
            state[grp] = s_prev * _expand_heads(chunk_decay, grp, head_lane) + st_new
            halves.append(y_diag + y_off * _expand_heads(ecs, grp, head_lane) + xs * dvec_ref[:, lo:lo + SSD_GW])
        y = jnp.concatenate(halves, axis=-1) * _silu(zc)
        y = y * lax.rsqrt(jnp.mean(y * y, axis=-1, keepdims=True) + LN_EPS) * nw_ref[...]
        ycat[r0:r0 + q, 0:SSD_WIDTH] = y.astype(BF16)

    ycat[:, SSD_WIDTH:SSD_WIDTH + S5_WIDTH] = ys5_ref[...]

    side(1)
    xc = _causal_conv(cur.rpad, w.rcw, w.rcb, tq, slice(0, RG_WIDTH))
    xc16 = xc.astype(BF16)
    rgate = _sigmoid(jnp.dot(xc16, w.wa[...], preferred_element_type=F32) + w.ba[...])
    igate = _sigmoid(jnp.dot(xc16, w.wx[...], preferred_element_type=F32) + w.bx[...])
    log_a = (-RG_C) * rgate * _softplus(-w.rlam[...])
    a = jnp.exp(log_a)
    bt = jnp.sqrt(1.0 - jnp.exp(2.0 * log_a)) * (igate * xc)
    row8 = lax.broadcasted_iota(jnp.int32, (tq, RG_WIDTH), 0) & (SUBLANES - 1)
    side(1)
    for dist in (1, 2, 4):
        keep = row8 >= dist
        a_sh = jnp.where(keep, pltpu.roll(a, dist, 0), 1.0)
        b_sh = jnp.where(keep, pltpu.roll(bt, dist, 0), 0.0)
        bt = bt + a * b_sh
        a = a * a_sh
    nt = tq // SUBLANES
    lane_halves = RG_WIDTH // LANES
    for hf in range(lane_halves):
        scr.ascr[hf] = a[:, hf * LANES:(hf + 1) * LANES]
        scr.bscr[hf] = bt[:, hf * LANES:(hf + 1) * LANES]
    ae = jnp.concatenate([scr.ascr[hf, pl.ds(SUBLANES - 1, nt, stride=SUBLANES), :] for hf in range(lane_halves)], -1)
    be = jnp.concatenate([scr.bscr[hf, pl.ds(SUBLANES - 1, nt, stride=SUBLANES), :] for hf in range(lane_halves)], -1)
    tile_id = lax.broadcasted_iota(jnp.int32, (nt, RG_WIDTH), 0)
    dist = 1
    while dist < nt:
        keep = tile_id >= dist
        a_sh = jnp.where(keep, pltpu.roll(ae, dist, 0), 1.0)
        b_sh = jnp.where(keep, pltpu.roll(be, dist, 0), 0.0)
        be = be + ae * b_sh
        ae = ae * a_sh
        dist *= 2
    carry_in = jnp.where(first, 0.0, scr.rcar[0:1, :])
    tile_end = be + ae * carry_in
    scr.rcar[0:1, :] = tile_end[nt - 1:nt, :]
    tile_in = jnp.where(tile_id >= 1, pltpu.roll(tile_end, 1, 0), carry_in)
    for hf in range(lane_halves):
        for k in range(SUBLANES):
            scr.cexp[hf, pl.ds(k, nt, stride=SUBLANES), :] = tile_in[:, hf * LANES:(hf + 1) * LANES]
    hscan = bt + a * jnp.concatenate([scr.cexp[hf] for hf in range(lane_halves)], axis=-1)
    ycat[:, SSD_WIDTH + S5_WIDTH:] = (hscan * _gelu(cur.gr[...])).astype(BF16)


_MIX_W16 = ("wz", "wxbc", "wdt", "wxr", "wgr", "wout")


def _mixer_load_weights(w_in_ref, w_out_ref, w):
    o_dt = SSD_WIDTH + SSD_XBC
    o_xr = o_dt + SSD_HEADS + S5_WIDTH
    o_gr = o_xr + RG_WIDTH
    w.wz[...] = w_in_ref[:, 0:SSD_WIDTH].astype(BF16)
    w.wxbc[...] = w_in_ref[:, SSD_WIDTH:o_dt].astype(BF16)
    head_cols = lax.broadcasted_iota(jnp.int32, (1, LANES), 1) < SSD_HEADS
    w.wdt[...] = jnp.where(head_cols, w_in_ref[:, o_dt:o_dt + LANES], 0.0).astype(BF16)
    w.wxr[...] = w_in_ref[:, o_xr:o_gr].astype(BF16)
    w.wgr[...] = w_in_ref[:, o_gr:o_gr + RG_WIDTH].astype(BF16)
    w.wout[...] = w_out_ref[...].astype(BF16)


def _mixer_kernel(hp_ref, ho_ref, ys5_ref, w_in_ref, w_out_ref, *rest, tq, chunk, nblk):
    small = [f for f in _MixW._fields if f not in _MIX_W16]
    o_ref = rest[len(small)]
    bufs = rest[len(small) + 1:]
    w16 = bufs[len(bufs) - len(_MIX_W16):]
    bufs = bufs[:len(bufs) - len(_MIX_W16)]
    w = _MixW(**dict(zip(small, rest[:len(small)])), **dict(zip(_MIX_W16, w16)))
    npb = len(_ProjBuf._fields)
    sets = (_ProjBuf(*bufs[:npb]), _ProjBuf(*bufs[npb:2 * npb]))
    ycats = bufs[2 * npb:2 * npb + 2]
    scr = _MixScratch(*bufs[2 * npb + 2:2 * npb + 2 + len(_MixScratch._fields)])
    h16 = bufs[-1]
    s = pl.program_id(0)

    @pl.when(s == 0)
    def _():
        _mixer_load_weights(w_in_ref, w_out_ref, w)
        for ref in bufs:
            ref[...] = jnp.zeros(ref.shape, ref.dtype)

    first_p = (s % nblk) == 0
    first_c = ((s + nblk - 1) % nblk) == 0

    def step(nxt, cur, ycat_c, ycat_o):
        jobs = (_mixer_output_jobs(ho_ref, ycat_o, w, o_ref)
                + _mixer_project_jobs(hp_ref, h16, w, nxt, cur, first_p, tq))

        def side(n):
            for _ in range(min(n, len(jobs))):
                jobs.pop(0)()

        _mixer_compute(cur, ys5_ref, w, scr, ycat_c, first_c, tq, chunk, side)
        side(len(jobs))

    @pl.when(s % 2 == 0)
    def _():
        step(sets[0], sets[1], ycats[1], ycats[0])

    @pl.when(s % 2 == 1)
    def _():
        step(sets[1], sets[0], ycats[0], ycats[1])


def _mixer(h, ys5, w_in, w_out, consts, layer):
    b, l, d = h.shape
    tq = min(MIX_TQ, l)
    chunk = min(SSD_CHUNK, tq)
    nblk = l // tq
    total = b * nblk
    h2d = h.reshape(b * l, d)
    proj_buf = [
        pltpu.VMEM((tq, SSD_WIDTH), F32),
        pltpu.VMEM((tq + SUBLANES, SSD_XBC), F32),
        pltpu.VMEM((tq, LANES), F32),
        pltpu.VMEM((tq + SUBLANES, RG_WIDTH), F32),
        pltpu.VMEM((tq, RG_WIDTH), F32),
    ]
    rg_plane = pltpu.VMEM((RG_WIDTH // LANES, tq, LANES), F32)
    out = pl.pallas_call(
        functools.partial(_mixer_kernel, tq=tq, chunk=chunk, nblk=nblk),
        grid=(total + 2,),
        in_specs=[pl.BlockSpec((tq, d), lambda s: (jnp.minimum(s, total - 1), 0)),
                  pl.BlockSpec((tq, d), lambda s: (jnp.maximum(s - 2, 0), 0)),
                  pl.BlockSpec((tq, S5_WIDTH), lambda s: (jnp.clip(s - 1, 0, total - 1), 0))]
                 + [_layer_spec(c, layer) for c in (w_in, w_out) + tuple(consts)],
        out_specs=pl.BlockSpec((tq, d), lambda s: (jnp.maximum(s - 2, 0), 0)),
        out_shape=jax.ShapeDtypeStruct((b * l, d), F32),
        scratch_shapes=proj_buf + proj_buf + [
            pltpu.VMEM((tq, D_MODEL), BF16),
            pltpu.VMEM((tq, D_MODEL), BF16),
            pltpu.VMEM((SSD_GROUPS, SSD_STATE, SSD_GW), F32),
            rg_plane, rg_plane, rg_plane,
            pltpu.VMEM((SUBLANES, RG_WIDTH), F32),
            pltpu.VMEM((tq, d), BF16),
            pltpu.VMEM((d, SSD_WIDTH), BF16),
            pltpu.VMEM((d, SSD_XBC), BF16),
            pltpu.VMEM((d, LANES), BF16),
            pltpu.VMEM((d, RG_WIDTH), BF16),
            pltpu.VMEM((d, RG_WIDTH), BF16),
            pltpu.VMEM((d, d), BF16),
        ],
        compiler_params=_params(("arbitrary",)),
        name="mixer",
    )(h2d, h2d, ys5.reshape(b * l, S5_WIDTH), w_in, w_out, *consts)
    return out.reshape(b, l, d)


def _block_diag(w):
    nl, nb, bi, bo = w.shape
    return jnp.einsum('lhij,hk->lhikj', w, jnp.eye(nb, dtype=w.dtype)).reshape(nl, nb * bi, nb * bo)


def _kv_kernel(m_ref, wk_ref, wv_ref, k_ref, v_ref):
    m16 = m_ref[...].astype(BF16)
    k_ref[...] = jnp.dot(m16, wk_ref[...].astype(BF16), preferred_element_type=F32).astype(BF16)
    v_ref[...] = jnp.dot(m16, wv_ref[...].astype(BF16), preferred_element_type=F32).astype(BF16)


def _kv(mem, wk, wv, layer):
    b, m, d = mem.shape
    spec = pl.BlockSpec((None, m, d), lambda i: (i, 0, 0))
    return pl.pallas_call(
        _kv_kernel,
        grid=(b,),
        in_specs=[spec, _layer_spec(wk, layer), _layer_spec(wv, layer)],
        out_specs=[spec, spec],
        out_shape=[jax.ShapeDtypeStruct((b, m, d), BF16)] * 2,
        compiler_params=_params(("arbitrary",)),
        name="kv",
    )(mem, wk, wv)


def _xattn_kernel(h_ref, k_ref, v_ref, wq32_ref, wo32_ref, g_ref, b_ref, o_ref, wq_ref, wo_ref):
    @pl.when((pl.program_id(0) == 0) & (pl.program_id(1) == 0))
    def _():
        wq_ref[...] = wq32_ref[...].astype(BF16)
        wo_ref[...] = wo32_ref[...].astype(BF16)

    hb = h_ref[...]
    qall = jnp.dot(hb.astype(BF16), wq_ref[...], preferred_element_type=F32) * (1.0 / math.sqrt(XA_HEAD_DIM))
    outs = []
    for hd in range(XA_HEADS):
        sl = slice(hd * XA_HEAD_DIM, (hd + 1) * XA_HEAD_DIM)
        s = _dot_nt(qall[:, sl], k_ref[:, sl])
        p = jnp.exp(s - jnp.max(s, axis=-1, keepdims=True))
        o = jnp.dot(p.astype(BF16), v_ref[:, sl], preferred_element_type=F32)
        outs.append(o / jnp.sum(p, axis=-1, keepdims=True))
    o = jnp.concatenate(outs, axis=-1)
    y = jnp.dot(o.astype(BF16), wo_ref[...], preferred_element_type=F32)
    o_ref[...] = _layer_norm(ALPHA * hb + y, g_ref[...], b_ref[...])


def _xattn(h, k, v, wq, wo, g, bb, layer):
    b, l, d = h.shape
    tm = min(TOK_TM, l)
    tok = pl.BlockSpec((None, tm, d), lambda i, j: (i, j, 0))
    kvs = pl.BlockSpec((None, k.shape[1], d), lambda i, j: (i, 0, 0))
    return pl.pallas_call(
        _xattn_kernel,
        grid=(b, l // tm),
        in_specs=[tok, kvs, kvs] + [_layer_spec(c, layer) for c in (wq, wo, g, bb)],
        out_specs=tok,
        out_shape=jax.ShapeDtypeStruct((b, l, d), F32),
        scratch_shapes=[pltpu.VMEM((d, d), BF16), pltpu.VMEM((d, d), BF16)],
        compiler_params=_params(("arbitrary", "arbitrary")),
        name="xattn",
    )(h, k, v, wq, wo, g, bb)


def _mlp_kernel(h_ref, w1_ref, w2_ref, g_ref, b_ref, o_ref):
    hb = h_ref[...]
    h16 = hb.astype(BF16)
    acc = ALPHA * hb
    for f in range(D_FF // MLP_FC):
        sl = slice(f * MLP_FC, (f + 1) * MLP_FC)
        t = jnp.maximum(jnp.dot(h16, w1_ref[:, sl], preferred_element_type=F32), 0.0)
        acc = acc + jnp.dot((t * t).astype(BF16), w2_ref[sl, :], preferred_element_type=F32)
    o_ref[...] = _layer_norm(acc, g_ref[...], b_ref[...])


def _mlp(h2d, w1, w2, g, bb, layer):
    t, d = h2d.shape
    tm = min(TOK_TM, t)
    tok = pl.BlockSpec((tm, d), lambda i: (i, 0))
    return pl.pallas_call(
        _mlp_kernel,
        grid=(t // tm,),
        in_specs=[tok] + [_layer_spec(c, layer) for c in (w1, w2, g, bb)],
        out_specs=tok,
        out_shape=jax.ShapeDtypeStruct((t, d), F32),
        compiler_params=_params(("parallel",)),
        name="mlp",
    )(h2d, w1, w2, g, bb)


def _rows(v):
    return v.astype(F32).reshape(v.shape[0], 1, -1)


def _pad_lanes(v3d):
    return jnp.pad(v3d, ((0, 0), (0, 0), (0, LANES - v3d.shape[2])))


def prepare_params(seq_len, w_in, w_out, ssd_conv_w, ssd_conv_b, ssd_dt_bias, ssd_a_log, ssd_d, ssd_norm_w,
                   s5_lam_re, s5_lam_im, s5_log_step, s5_b_re, s5_b_im, s5_c_re, s5_c_im, s5_d, s5_glu_w, s5_glu_b,
                   rg_conv_w, rg_conv_b, rg_wa, rg_ba, rg_wx, rg_bx, rg_lambda, ln1_g, ln1_b,
                   xa_wq, xa_wk, xa_wv, xa_wo, ln2_g, ln2_b, mlp_w1, mlp_w2, ln3_g, ln3_b):
    o_u = SSD_WIDTH + SSD_XBC + SSD_HEADS
    n_tiles = min(S5_TB, seq_len) // (S5_BLK * SUBLANES)
    assert n_tiles <= 2 ** (SUBLANES - 3), "S5 scan-level table holds 3 in-tile + 5 tile levels"
    s5_ops = jax.vmap(functools.partial(_s5_prepare, n_tiles=n_tiles))(
        s5_lam_re, s5_lam_im, s5_log_step, s5_b_re, s5_b_im, s5_c_re, s5_c_im)
    s5 = (w_in[:, :, o_u:o_u + S5_WIDTH].astype(BF16), *s5_ops, _rows(s5_d), s5_glu_w.astype(BF16), _rows(s5_glu_b))
    mix_small = dict(
        cw=ssd_conv_w.astype(F32), cb=_rows(ssd_conv_b), dtb=_pad_lanes(_rows(ssd_dt_bias)),
        aneg=_pad_lanes(_rows(-jnp.exp(ssd_a_log.astype(F32)))),
        dvec=_rows(jnp.repeat(ssd_d.astype(F32), SSD_HEAD_DIM, axis=1)), nw=_rows(ssd_norm_w),
        rcw=rg_conv_w.astype(F32), rcb=_rows(rg_conv_b),
        wa=_block_diag(rg_wa.astype(F32)).astype(BF16), ba=_rows(rg_ba.reshape(rg_ba.shape[0], -1)),
        wx=_block_diag(rg_wx.astype(F32)).astype(BF16), bx=_rows(rg_bx.reshape(rg_bx.shape[0], -1)),
        rlam=_rows(rg_lambda), g=_rows(ln1_g), b=_rows(ln1_b))
    mix = (w_in.astype(F32), w_out.astype(F32), tuple(mix_small[f] for f in _MixW._fields if f not in _MIX_W16))
    xa = (xa_wk.astype(F32), xa_wv.astype(F32), xa_wq.astype(F32), xa_wo.astype(F32), _rows(ln2_g), _rows(ln2_b))
    mlp = (mlp_w1.astype(BF16), mlp_w2.astype(BF16), _rows(ln3_g), _rows(ln3_b))
    return dict(s5=s5, mix=mix, xa=xa, mlp=mlp)


def mixer_sublayer(h, prm, layer):
    ys5 = _s5_branch(h, prm["s5"], layer)
    w_in, w_out, consts = prm["mix"]
    return _mixer(h, ys5, w_in, w_out, consts, layer)


def xattn_sublayer(h, mem, prm, layer):
    wk, wv, wq, wo, g, bb = prm["xa"]
    k, v = _kv(mem, wk, wv, layer)
    return _xattn(h, k, v, wq, wo, g, bb, layer)


def mlp_sublayer(h, prm, layer):
    b, l, d = h.shape
    return _mlp(h.reshape(b * l, d), *prm["mlp"], layer).reshape(b, l, d)


def kernel(x, mem, w_in, w_out, ssd_conv_w, ssd_conv_b, ssd_dt_bias, ssd_a_log, ssd_d, ssd_norm_w, s5_lam_re, s5_lam_im, s5_log_step, s5_b_re, s5_b_im, s5_c_re, s5_c_im, s5_d, s5_glu_w, s5_glu_b, rg_conv_w, rg_conv_b, rg_wa, rg_ba, rg_wx, rg_bx, rg_lambda, ln1_g, ln1_b, xa_wq, xa_wk, xa_wv, xa_wo, ln2_g, ln2_b, mlp_w1, mlp_w2, ln3_g, ln3_b):
    h = x.astype(F32)
    memf = mem.astype(F32)
    prm = prepare_params(x.shape[1], w_in, w_out, ssd_conv_w, ssd_conv_b, ssd_dt_bias, ssd_a_log, ssd_d, ssd_norm_w,
                         s5_lam_re, s5_lam_im, s5_log_step, s5_b_re, s5_b_im, s5_c_re, s5_c_im, s5_d, s5_glu_w,
                         s5_glu_b, rg_conv_w, rg_conv_b, rg_wa, rg_ba, rg_wx, rg_bx, rg_lambda, ln1_g, ln1_b,
                         xa_wq, xa_wk, xa_wv, xa_wo, ln2_g, ln2_b, mlp_w1, mlp_w2, ln3_g, ln3_b)
    for layer in range(DEPTH):
        h = mixer_sublayer(h, prm, layer)
        h = xattn_sublayer(h, memf, prm, layer)
        h = mlp_sublayer(h, prm, layer)
    return h.astype(x.dtype)
```

```python
import collections
import functools
import math

import jax
import jax.numpy as jnp
from jax import lax
from jax.experimental import pallas as pl
from jax.experimental.pallas import tpu as pltpu

F32 = jnp.float32
BF16 = jnp.bfloat16

D_MODEL = 1024
DEPTH = 2
SSD_WIDTH = 512
SSD_HEAD_DIM = 64
SSD_HEADS = 8
SSD_GROUPS = 2
SSD_HPG = SSD_HEADS // SSD_GROUPS
SSD_GW = SSD_HPG * SSD_HEAD_DIM
SSD_STATE = 128
SSD_XBC = 1024
CONV_K = 4
S5_WIDTH = 256
S5_GROUPS = 16
S5_GROUP_CH = 16
S5_STATE = 64
S5_NSTATE = S5_GROUPS * S5_STATE
S5_BLK = 8
RG_WIDTH = 256
RG_BLOCKS = 4
RG_C = 8.0
XA_HEADS = 4
XA_HEAD_DIM = 256
D_FF = 4096
ALPHA = (2.0 * DEPTH) ** 0.25
LN_EPS = 1e-5

LANES = 128
SUBLANES = 8
MXU_TILE = 256
NEG_BIG = -1e30
VMEM_LIMIT = 56 * 1024 * 1024

SSD_CHUNK = 128
MIX_TQ = 256
S5_TB = 2048
TOK_TM = 512
MLP_FC = 1024


def _dot(a, b):
    return jnp.dot(a.astype(BF16), b.astype(BF16), preferred_element_type=F32)


def _dot_nt(a, b):
    return lax.dot_general(a.astype(BF16), b.astype(BF16), (((1,), (1,)), ((), ())),
                           preferred_element_type=F32)


def _layer_norm(x, g, b):
    mu = jnp.mean(x, axis=-1, keepdims=True)
    xc = x - mu
    var = jnp.mean(xc * xc, axis=-1, keepdims=True)
    return xc * lax.rsqrt(var + LN_EPS) * g + b


def _gelu(x):
    c = math.sqrt(2.0 / math.pi)
    return 0.5 * x * (1.0 + jnp.tanh(c * (x + 0.044715 * (x * x * x))))


def _sigmoid(x):
    return 0.5 + 0.5 * jnp.tanh(0.5 * x)


def _silu(x):
    hx = 0.5 * x
    return hx + hx * jnp.tanh(hx)


def _softplus(x):
    return jnp.maximum(x, 0.0) + jnp.log1p(jnp.exp(-jnp.abs(x)))


def _const_spec(shape):
    nd = len(shape)
    return pl.BlockSpec(shape, lambda *_: (0,) * nd, pipeline_mode=pl.Buffered(1))


def _layer_spec(stacked, layer):
    nd = stacked.ndim - 1
    return pl.BlockSpec((None,) + stacked.shape[1:], lambda *_: (layer,) + (0,) * nd, pipeline_mode=pl.Buffered(1))


def _params(sem):
    return pltpu.CompilerParams(dimension_semantics=sem, vmem_limit_bytes=VMEM_LIMIT)


def _re_lanes(p):
    return slice(2 * p * LANES, (2 * p + 1) * LANES)


def _im_lanes(p):
    return slice((2 * p + 1) * LANES, (2 * p + 2) * LANES)


def _s5_state_group(idx):
    state = (idx // (2 * LANES)) * LANES + idx % LANES
    return state // S5_STATE


def _s5_split(x):
    planes = S5_NSTATE // LANES
    return (jnp.concatenate([x[:, _re_lanes(p)] for p in range(planes)], axis=-1),
            jnp.concatenate([x[:, _im_lanes(p)] for p in range(planes)], axis=-1))


S5_HALF_CH = S5_WIDTH // 2
S5_HALF_LANES = S5_BLK * S5_HALF_CH


def _s5_expand_operators(wc_ref, vc_ref, ct_ref, ms_scr, toep_scr, mo_scr):
    n, w, c = S5_NSTATE, S5_WIDTH, S5_GROUP_CH
    hw, hl = S5_HALF_CH, S5_HALF_LANES
    grp_half = S5_GROUPS // 2
    reps = hw // c
    grp_r = _s5_state_group(lax.broadcasted_iota(jnp.int32, (2 * n, w), 0))
    same_o = grp_r == lax.broadcasted_iota(jnp.int32, (2 * n, w), 1) // c
    mo0 = jnp.where(same_o, ct_ref[...], 0.0)
    kc = jnp.dot(wc_ref[...], mo0, preferred_element_type=F32, precision=lax.Precision.HIGHEST)
    toep_scr[...] = jnp.zeros(toep_scr.shape, toep_scr.dtype)
    ch_grp = lax.broadcasted_iota(jnp.int32, (hw, hl), 0) // c
    same_k = (lax.broadcasted_iota(jnp.int32, (hw, hw), 0) // c) == (lax.broadcasted_iota(jnp.int32, (hw, hw), 1) // c)
    pick_r = lax.broadcasted_iota(jnp.int32, (S5_BLK * c, hl), 0)
    pick_c = lax.broadcasted_iota(jnp.int32, (S5_BLK * c, hl), 1)
    pick = ((pick_r // c == pick_c // hw) & (pick_r % c == pick_c % c)).astype(BF16)
    for hf in range(2):
        lanes = slice(hf * hl, (hf + 1) * hl)
        st_grp = _s5_state_group(lax.broadcasted_iota(jnp.int32, (hw, hl), 1) + hf * hl) - hf * grp_half
        same_s = ch_grp == st_grp
        for t in range(S5_BLK):
            blk = jnp.where(same_s, jnp.tile(wc_ref[t * c:(t + 1) * c, lanes], (reps, 1)), 0.0)
            ms_scr[hf, t * hw:(t + 1) * hw, :] = blk.astype(BF16)
        st_grp = _s5_state_group(lax.broadcasted_iota(jnp.int32, (hl, hl), 0) + hf * hl) - hf * grp_half
        same_r = st_grp == (lax.broadcasted_iota(jnp.int32, (hl, hl), 1) % hw) // c
        spread = jnp.dot(vc_ref[lanes, :].astype(BF16), pick, preferred_element_type=F32)
        mo_scr[hf] = jnp.where(same_r, spread, 0.0).astype(BF16)
        for t_in in range(S5_BLK):
            for t_out in range(t_in, S5_BLK):
                k = S5_BLK - 1 - (t_out - t_in)
                blk = jnp.where(same_k, jnp.tile(kc[k * c:(k + 1) * c, hf * hw:(hf + 1) * hw], (reps, 1)), 0.0)
                toep_scr[hf, t_in * hw:(t_in + 1) * hw, t_out * hw:(t_out + 1) * hw] = blk.astype(BF16)


_S5W = collections.namedtuple("_S5W", "wu wc vc ct lvl pw pwt d gluw glub")
_S5Ops = collections.namedtuple("_S5Ops", "ms toep mo")
_S5Buf = collections.namedtuple("_S5Buf", "u32 u16 hprev16")
_S5Scratch = collections.namedtuple("_S5Scratch", "u_tok hplane cexp carry yg")


def _s5_scan_stage(h_ref, w, ops, scr, out, first, rows, side):
    n, wd = S5_NSTATE, S5_WIDTH
    lane_halves = wd // LANES
    planes = n // LANES
    nt = rows // SUBLANES
    tb = rows * S5_BLK
    for r0 in range(0, tb, tb // 4):
        rs = slice(r0, r0 + tb // 4)
        u_tok = jnp.dot(h_ref[rs, :].astype(BF16), w.wu[...], preferred_element_type=F32)
        for hf in range(lane_halves):
            scr.u_tok[hf, rs, :] = u_tok[:, hf * LANES:(hf + 1) * LANES]
    u = jnp.concatenate([scr.u_tok[hf, pl.ds(k, rows, stride=S5_BLK), :]
                         for hf in range(lane_halves) for k in range(S5_BLK)], axis=-1)
    out.u32[...] = u
    u16 = u.astype(BF16)
    out.u16[...] = u16
    row8 = lax.broadcasted_iota(jnp.int32, (nt, SUBLANES, LANES), 1)
    planes_per_half = planes // lane_halves
    for p in range(planes):
        hf, pl_in_half = divmod(p, planes_per_half)
        s = jnp.dot(u16[:, hf * S5_HALF_LANES:(hf + 1) * S5_HALF_LANES],
                    ops.ms[hf, :, 2 * pl_in_half * LANES:(2 * pl_in_half + 2) * LANES],
                    preferred_element_type=F32)
        if p < planes - 2:
            side(1)
        xr = s[:, :LANES].reshape(nt, SUBLANES, LANES)
        xi = s[:, LANES:].reshape(nt, SUBLANES, LANES)
        for lv, dist in enumerate((1, 2, 4)):
            lr = w.lvl[lv:lv + 1, _re_lanes(p)]
            li = w.lvl[lv:lv + 1, _im_lanes(p)]
            keep = row8 >= dist
            sr = jnp.where(keep, pltpu.roll(xr, dist, 1), 0.0)
            si = jnp.where(keep, pltpu.roll(xi, dist, 1), 0.0)
            xr, xi = xr + lr * sr - li * si, xi + lr * si + li * sr
        scr.hplane[2 * p, SUBLANES:SUBLANES + rows, :] = xr.reshape(rows, LANES)
        scr.hplane[2 * p + 1, SUBLANES:SUBLANES + rows, :] = xi.reshape(rows, LANES)
    side(1)
    last = 2 * SUBLANES - 1
    er = jnp.concatenate([scr.hplane[2 * p, pl.ds(last, nt, stride=SUBLANES), :] for p in range(planes)], axis=-1)
    ei = jnp.concatenate([scr.hplane[2 * p + 1, pl.ds(last, nt, stride=SUBLANES), :] for p in range(planes)], -1)
    tile_id = lax.broadcasted_iota(jnp.int32, (nt, n), 0)
    lv, dist = 3, 1
    while dist < nt:
        lr, li = _s5_split(w.lvl[lv:lv + 1, :])
        keep = tile_id >= dist
        sr = jnp.where(keep, pltpu.roll(er, dist, 0), 0.0)
        si = jnp.where(keep, pltpu.roll(ei, dist, 0), 0.0)
        er, ei = er + lr * sr - li * si, ei + lr * si + li * sr
        lv, dist = lv + 1, dist * 2
    cr, ci = _s5_split(jnp.where(first, 0.0, scr.carry[0:1, :]))
    ptr, pti = _s5_split(w.pwt[...])
    er, ei = er + ptr * cr - pti * ci, ei + ptr * ci + pti * cr
    for p in range(planes):
        scr.carry[0:1, _re_lanes(p)] = er[nt - 1:nt, p * LANES:(p + 1) * LANES]
        scr.carry[0:1, _im_lanes(p)] = ei[nt - 1:nt, p * LANES:(p + 1) * LANES]
    tin_r = jnp.where(tile_id >= 1, pltpu.roll(er, 1, 0), cr)
    tin_i = jnp.where(tile_id >= 1, pltpu.roll(ei, 1, 0), ci)
    side(1)
    for p in range(planes):
        for k in range(SUBLANES):
            scr.cexp[2 * p, pl.ds(k, nt, stride=SUBLANES), :] = tin_r[:, p * LANES:(p + 1) * LANES]
            scr.cexp[2 * p + 1, pl.ds(k, nt, stride=SUBLANES), :] = tin_i[:, p * LANES:(p + 1) * LANES]
    for p in range(planes):
        if p % 2 == 0:
            side(1)
        pr = jnp.tile(w.pw[:, _re_lanes(p)], (nt, 1))
        pi = jnp.tile(w.pw[:, _im_lanes(p)], (nt, 1))
        tr = scr.cexp[2 * p]
        ti = scr.cexp[2 * p + 1]
        hr = scr.hplane[2 * p, SUBLANES:SUBLANES + rows, :] + pr * tr - pi * ti
        hi = scr.hplane[2 * p + 1, SUBLANES:SUBLANES + rows, :] + pr * ti + pi * tr
        scr.hplane[2 * p, SUBLANES:SUBLANES + rows, :] = hr
        scr.hplane[2 * p + 1, SUBLANES:SUBLANES + rows, :] = hi
        scr.hplane[2 * p, SUBLANES - 1:SUBLANES, :] = cr[:, p * LANES:(p + 1) * LANES]
        scr.hplane[2 * p + 1, SUBLANES - 1:SUBLANES, :] = ci[:, p * LANES:(p + 1) * LANES]
    for p in range(2 * planes):
        out.hprev16[:, p * LANES:(p + 1) * LANES] = scr.hplane[p, SUBLANES - 1:SUBLANES - 1 + rows, :].astype(BF16)


def _s5_readout_jobs(src, w, ops, scr, o_ref, rows):
    hw, hl = S5_HALF_CH, S5_HALF_LANES
    lane_halves = S5_WIDTH // LANES
    steps_per_tile = MXU_TILE // hw

    def tile_job(hf, j):
        def run():
            k_rows = (j + 1) * MXU_TILE
            cols = slice(j * MXU_TILE, (j + 1) * MXU_TILE)
            y = jnp.dot(src.u16[:, hf * hl:hf * hl + k_rows], ops.toep[hf, :k_rows, cols],
                        preferred_element_type=F32)
            y = y + jnp.dot(src.hprev16[:, hf * hl:(hf + 1) * hl], ops.mo[hf, :, cols], preferred_element_type=F32)
            d_half = w.d[:, hf * hw:(hf + 1) * hw]
            y = _gelu(y + jnp.concatenate([d_half] * steps_per_tile, axis=1) * src.u32[:, hf * hl + j * MXU_TILE:
                                                                                           hf * hl + (j + 1) * MXU_TILE])
            for i in range(steps_per_tile):
                scr.yg[hf, pl.ds(j * steps_per_tile + i, rows, stride=S5_BLK), :] = y[:, i * hw:(i + 1) * hw]
        return run

    def glu_job():
        yg = jnp.concatenate([scr.yg[hf] for hf in range(lane_halves)], axis=-1)
        glu = _sigmoid(_dot(yg, w.gluw[...]) + w.glub[...])
        o_ref[...] = (yg * glu).astype(o_ref.dtype)

    return [tile_job(hf, j) for hf in range(lane_halves) for j in range(hl // MXU_TILE)] + [glu_job]


def _s5_kernel(h_ref, *rest, rows, nblk):
    nw = len(_S5W._fields)
    w = _S5W(*rest[:nw])
    o_ref = rest[nw]
    bufs = rest[nw + 1:]
    ops = _S5Ops(*bufs[:3])
    nb = len(_S5Buf._fields)
    sets = (_S5Buf(*bufs[3:3 + nb]), _S5Buf(*bufs[3 + nb:3 + 2 * nb]))
    scr = _S5Scratch(*bufs[3 + 2 * nb:])
    s = pl.program_id(0)

    @pl.when(s == 0)
    def _():
        _s5_expand_operators(w.wc, w.vc, w.ct, *ops)
        for ref in bufs[3:]:
            ref[...] = jnp.zeros(ref.shape, ref.dtype)

    first = (s % nblk) == 0

    def step(out, src):
        jobs = _s5_readout_jobs(src, w, ops, scr, o_ref, rows)

        def side(k):
            for _ in range(min(k, len(jobs))):
                jobs.pop(0)()

        _s5_scan_stage(h_ref, w, ops, scr, out, first, rows, side)
        side(len(jobs))

    @pl.when(s % 2 == 0)
    def _():
        step(sets[0], sets[1])

    @pl.when(s % 2 == 1)
    def _():
        step(sets[1], sets[0])


def _s5_branch(h, consts, layer):
    b, l, d = h.shape
    tb = min(S5_TB, l)
    rows = tb // S5_BLK
    nblk = l // tb
    total = b * nblk
    n, w = S5_NSTATE, S5_WIDTH
    handoff = [
        pltpu.VMEM((rows, S5_BLK * w), F32),
        pltpu.VMEM((rows, S5_BLK * w), BF16),
        pltpu.VMEM((rows, 2 * n), BF16),
    ]
    out = pl.pallas_call(
        functools.partial(_s5_kernel, rows=rows, nblk=nblk),
        grid=(total + 1,),
        in_specs=[pl.BlockSpec((tb, d), lambda s: (jnp.minimum(s, total - 1), 0))]
                 + [_layer_spec(c, layer) for c in consts],
        out_specs=pl.BlockSpec((tb, w), lambda s: (jnp.maximum(s - 1, 0), 0)),
        out_shape=jax.ShapeDtypeStruct((b * l, w), BF16),
        scratch_shapes=[
            pltpu.VMEM((2, S5_HALF_LANES, S5_HALF_LANES), BF16),
            pltpu.VMEM((2, S5_HALF_LANES, S5_HALF_LANES), BF16),
            pltpu.VMEM((2, S5_HALF_LANES, S5_HALF_LANES), BF16),
        ] + handoff + handoff + [
            pltpu.VMEM((w // LANES, tb, LANES), F32),
            pltpu.VMEM((2 * n // LANES, rows + SUBLANES, LANES), F32),
            pltpu.VMEM((2 * n // LANES, rows, LANES), F32),
            pltpu.VMEM((SUBLANES, 2 * n), F32),
            pltpu.VMEM((w // LANES, tb, LANES), F32),
        ],
        compiler_params=_params(("arbitrary",)),
        name="s5_branch",
    )(h.reshape(b * l, d), *consts)
    return out.reshape(b, l, w)


def _s5_prepare(lam_re, lam_im, log_step, b_re, b_im, c_re, c_im, n_tiles):
    g, p, c, t8 = S5_GROUPS, S5_STATE, S5_GROUP_CH, S5_BLK
    lr, li = lam_re.astype(F32), lam_im.astype(F32)
    dt = jnp.exp(log_step.astype(F32))[:, None]
    ar, ai = lr * dt, li * dt

    def lam_pow(k):
        k = k.astype(F32)[..., None, None]
        mag = jnp.exp(k * ar)
        return mag * jnp.cos(k * ai), mag * jnp.sin(k * ai)

    br1, bi1 = lam_pow(jnp.ones((), jnp.int32))
    xr_, xi_ = br1 - 1.0, bi1
    den = lr * lr + li * li
    cr, ci = (xr_ * lr + xi_ * li) / den, (xi_ * lr - xr_ * li) / den
    bre, bim = b_re.astype(F32), b_im.astype(F32)
    bbr = cr[..., None] * bre - ci[..., None] * bim
    bbi = cr[..., None] * bim + ci[..., None] * bre
    steps = jnp.arange(t8)
    pr, pi = lam_pow(t8 - 1 - steps)
    wsr = pr[..., None] * bbr[None] - pi[..., None] * bbi[None]
    wsi = pr[..., None] * bbi[None] + pi[..., None] * bbr[None]

    def interleave(re, im, axis):
        shape = list(re.shape)
        shape[axis:axis + 1] = [re.shape[axis] // LANES, 1, LANES]
        both = jnp.concatenate([re.reshape(shape), im.reshape(shape)], axis=axis + 1)
        shape[axis:axis + 3] = [2 * re.shape[axis]]
        return both.reshape(shape)

    wc = interleave(*[part.transpose(0, 3, 1, 2).reshape(t8 * c, g * p) for part in (wsr, wsi)], axis=1)
    cre, cim = c_re.astype(F32), c_im.astype(F32)
    pr, pi = lam_pow(steps + 1)
    vor = cre[None] * pr[:, :, None, :] - cim[None] * pi[:, :, None, :]
    voi = cre[None] * pi[:, :, None, :] + cim[None] * pr[:, :, None, :]
    vc = interleave(*[part.transpose(1, 3, 0, 2).reshape(g * p, t8 * c) for part in (vor, -voi)], axis=0)
    ct = interleave(*[jnp.tile(part.transpose(0, 2, 1).reshape(g * p, c), (1, g)) for part in (cre, -cim)], axis=0)

    def re_im(pair):
        return interleave(pair[0].reshape(-1, g * p), pair[1].reshape(-1, g * p), axis=1)

    lvl = re_im(lam_pow(t8 * 2 ** jnp.arange(SUBLANES)))
    pw = re_im(lam_pow(t8 * (steps + 1)))
    pwt = re_im(lam_pow(t8 * SUBLANES * (jnp.arange(n_tiles) + 1)))
    return wc, vc, ct, lvl, pw, pwt


def _expand_heads(cols, grp, head_lane):
    base = SSD_HPG * grp
    out = jnp.broadcast_to(cols[:, base:base + 1], (cols.shape[0], SSD_GW))
    for hh in range(1, SSD_HPG):
        out = jnp.where(head_lane == hh, cols[:, base + hh:base + hh + 1], out)
    return out


def _causal_conv(pad_ref, w_ref, b_ref, rows, cols):
    acc = b_ref[:, cols] + w_ref[CONV_K - 1:CONV_K, cols] * pad_ref[SUBLANES:SUBLANES + rows, cols]
    for k in range(CONV_K - 1):
        off = SUBLANES - (CONV_K - 1) + k
        acc = acc + w_ref[k:k + 1, cols] * pad_ref[off:off + rows, cols]
    return acc


_ProjBuf = collections.namedtuple("_ProjBuf", "z xpad dt rpad gr")
_MixW = collections.namedtuple(
    "_MixW", "wz wxbc wdt wxr wgr cw cb dtb aneg dvec nw rcw rcb wa ba wx bx rlam wout g b")
_MixScratch = collections.namedtuple("_MixScratch", "state ascr bscr cexp rcar")


def _col_tiles(width):
    return [slice(c, min(c + MXU_TILE, width)) for c in range(0, width, MXU_TILE)]


def _mixer_output_jobs(h_ref, ycat, w, o_ref):
    def tile_job(cols):
        def run():
            y = jnp.dot(ycat[...], w.wout[:, cols], preferred_element_type=F32)
            o_ref[:, cols] = ALPHA * h_ref[:, cols] + y
        return run

    def norm_job():
        o_ref[...] = _layer_norm(o_ref[...], w.g[...], w.b[...])

    return [tile_job(cols) for cols in _col_tiles(D_MODEL)] + [norm_job]


def _mixer_project_jobs(h_ref, h16, w, nxt, cur, first, tq):
    def cast_job():
        h16[...] = h_ref[...].astype(BF16)

    def tile_job(dst, w_ref, cols, row0):
        def run():
            dst[row0:row0 + tq, cols] = jnp.dot(h16[...], w_ref[:, cols], preferred_element_type=F32)
        return run

    def tail_job(pad_n, pad_c):
        def run():
            pad_n[0:SUBLANES, :] = jnp.where(first, 0.0, pad_c[tq:tq + SUBLANES, :])
        return run

    jobs = [cast_job]
    for dst, w_ref, row0 in ((nxt.z, w.wz, 0), (nxt.xpad, w.wxbc, SUBLANES), (nxt.dt, w.wdt, 0),
                             (nxt.rpad, w.wxr, SUBLANES), (nxt.gr, w.wgr, 0)):
        jobs += [tile_job(dst, w_ref, cols, row0) for cols in _col_tiles(dst.shape[1])]
    jobs += [tail_job(nxt.xpad, cur.xpad), tail_job(nxt.rpad, cur.rpad)]
    return jobs


def _mixer_compute(cur, ys5_ref, w, scr, ycat, first, tq, chunk, side):
    cw_ref, cb_ref, dtb_ref, aneg_ref, dvec_ref, nw_ref = w.cw, w.cb, w.dtb, w.aneg, w.dvec, w.nw
    state = scr.state
    z = cur.z
    slabs = []
    for cols in _col_tiles(SSD_XBC):
        side(1)
        slabs.append(_silu(_causal_conv(cur.xpad, cw_ref, cb_ref, tq, cols)))
    xact = jnp.concatenate(slabs, axis=-1)
    side(2)
    dt_all = _softplus(cur.dt[...] + dtb_ref[...])
    q = chunk
    tri = (lax.broadcasted_iota(jnp.int32, (q, q), 0) >= lax.broadcasted_iota(jnp.int32, (q, q), 1))
    row_id = lax.broadcasted_iota(jnp.int32, (q, LANES), 0)
    head_lane = lax.broadcasted_iota(jnp.int32, (1, SSD_GW), 1) // SSD_HEAD_DIM
    for ci in range(tq // q):
        r0 = ci * q
        dt = dt_all[r0:r0 + q, :]
        cs = dt * aneg_ref[...]
        dist = 1
        while dist < q:
            cs = cs + jnp.where(row_id >= dist, pltpu.roll(cs, dist, 0), 0.0)
            dist *= 2
        cs_t = cs.T
        cs_last = cs[q - 1:q, :]
        chunk_decay = jnp.exp(cs_last)
        dec_t = jnp.exp(cs_t[:, q - 1:q] - cs_t)
        ecs = jnp.exp(cs)
        zc = z[r0:r0 + q, :]
        halves = []
        for grp in range(SSD_GROUPS):
            side(2)
            lo = grp * SSD_GW
            xs = xact[r0:r0 + q, lo:lo + SSD_GW]
            bm = xact[r0:r0 + q, SSD_WIDTH + grp * SSD_STATE:SSD_WIDTH + (grp + 1) * SSD_STATE]
            cm = xact[r0:r0 + q, SSD_WIDTH + (SSD_GROUPS + grp) * SSD_STATE:
                      SSD_WIDTH + (SSD_GROUPS + grp + 1) * SSD_STATE]
            cm16 = cm.astype(BF16)
            xdt16 = (xs * _expand_heads(dt, grp, head_lane)).astype(BF16)
            cb = _dot_nt(cm16, bm)
            bm_t = bm.T
            s_prev = state[grp]
            if ci == 0:
                s_prev = jnp.where(first, 0.0, s_prev)
            y_off = jnp.dot(cm16, s_prev.astype(BF16), preferred_element_type=F32)
            intra, to_state, xdt_heads = [], [], []
            for hh in range(SSD_HPG):
                hd = grp * SSD_HPG + hh
                seg = cs[:, hd:hd + 1] - cs_t[hd:hd + 1, :]
                lmat = jnp.exp(jnp.where(tri, seg, NEG_BIG))
                intra.append((cb * lmat).astype(BF16))
                to_state.append((bm_t * dec_t[hd:hd + 1, :]).astype(BF16))
                xdt_heads.append(jnp.where(head_lane == hh, xdt16, jnp.zeros_like(xdt16)))
            lhs = jnp.concatenate([jnp.concatenate(intra, axis=1), jnp.concatenate(to_state, axis=1)], axis=0)
            both = jnp.dot(lhs, jnp.concatenate(xdt_heads, axis=0), preferred_element_type=F32)
            y_diag = both[:q]
            st_new = both[q:]
            state[grp] = s_prev * _expand_heads(chunk_decay, grp, head_lane) + st_new
            halves.append(y_diag + y_off * _expand_heads(ecs, grp, head_lane) + xs * dvec_ref[:, lo:lo + SSD_GW])
        y = jnp.concatenate(halves, axis=-1) * _silu(zc)
        y = y * lax.rsqrt(jnp.mean(y * y, axis=-1, keepdims=True) + LN_EPS) * nw_ref[...]
        ycat[r0:r0 + q, 0:SSD_WIDTH] = y.astype(BF16)

    ycat[:, SSD_WIDTH:SSD_WIDTH + S5_WIDTH] = ys5_ref[...]
    _mixer_rglru(cur, w, scr, ycat, first, tq, side)


def _mixer_rglru(cur, w, scr, ycat, first, tq, side):
    side(1)
    xc = _causal_conv(cur.rpad, w.rcw, w.rcb, tq, slice(0, RG_WIDTH))
    xc16 = xc.astype(BF16)
    rgate = _sigmoid(jnp.dot(xc16, w.wa[...], preferred_element_type=F32) + w.ba[...])
    igate = _sigmoid(jnp.dot(xc16, w.wx[...], preferred_element_type=F32) + w.bx[...])
    log_a = (-RG_C) * rgate * _softplus(-w.rlam[...])
    a = jnp.exp(log_a)
    bt = jnp.sqrt(1.0 - jnp.exp(2.0 * log_a)) * (igate * xc)
    row8 = lax.broadcasted_iota(jnp.int32, (tq, RG_WIDTH), 0) & (SUBLANES - 1)
    side(1)
    for dist in (1, 2, 4):
        keep = row8 >= dist
        a_sh = jnp.where(keep, pltpu.roll(a, dist, 0), 1.0)
        b_sh = jnp.where(keep, pltpu.roll(bt, dist, 0), 0.0)
        bt = bt + a * b_sh
        a = a * a_sh
    nt = tq // SUBLANES
    lane_halves = RG_WIDTH // LANES
    for hf in range(lane_halves):
        scr.ascr[hf] = a[:, hf * LANES:(hf + 1) * LANES]
        scr.bscr[hf] = bt[:, hf * LANES:(hf + 1) * LANES]
    ae = jnp.concatenate([scr.ascr[hf, pl.ds(SUBLANES - 1, nt, stride=SUBLANES), :] for hf in range(lane_halves)], -1)
    be = jnp.concatenate([scr.bscr[hf, pl.ds(SUBLANES - 1, nt, stride=SUBLANES), :] for hf in range(lane_halves)], -1)
    tile_id = lax.broadcasted_iota(jnp.int32, (nt, RG_WIDTH), 0)
    dist = 1
    while dist < nt:
        keep = tile_id >= dist
        a_sh = jnp.where(keep, pltpu.roll(ae, dist, 0), 1.0)
        b_sh = jnp.where(keep, pltpu.roll(be, dist, 0), 0.0)
        be = be + ae * b_sh
        ae = ae * a_sh
        dist *= 2
    carry_in = jnp.where(first, 0.0, scr.rcar[0:1, :])
    tile_end = be + ae * carry_in
    scr.rcar[0:1, :] = tile_end[nt - 1:nt, :]
    tile_in = jnp.where(tile_id >= 1, pltpu.roll(tile_end, 1, 0), carry_in)
    for hf in range(lane_halves):
        for k in range(SUBLANES):
            scr.cexp[hf, pl.ds(k, nt, stride=SUBLANES), :] = tile_in[:, hf * LANES:(hf + 1) * LANES]
    hscan = bt + a * jnp.concatenate([scr.cexp[hf] for hf in range(lane_halves)], axis=-1)
    ycat[:, SSD_WIDTH + S5_WIDTH:] = (hscan * _gelu(cur.gr[...])).astype(BF16)


_MIX_W16 = ("wz", "wxbc", "wdt", "wxr", "wgr", "wout")


def _mixer_load_weights(w_in_ref, w_out_ref, w):
    o_dt = SSD_WIDTH + SSD_XBC
    o_xr = o_dt + SSD_HEADS + S5_WIDTH
    o_gr = o_xr + RG_WIDTH
    w.wz[...] = w_in_ref[:, 0:SSD_WIDTH].astype(BF16)
    w.wxbc[...] = w_in_ref[:, SSD_WIDTH:o_dt].astype(BF16)
    head_cols = lax.broadcasted_iota(jnp.int32, (1, LANES), 1) < SSD_HEADS
    w.wdt[...] = jnp.where(head_cols, w_in_ref[:, o_dt:o_dt + LANES], 0.0).astype(BF16)
    w.wxr[...] = w_in_ref[:, o_xr:o_gr].astype(BF16)
    w.wgr[...] = w_in_ref[:, o_gr:o_gr + RG_WIDTH].astype(BF16)
    w.wout[...] = w_out_ref[...].astype(BF16)


def _mixer_kernel(hp_ref, ho_ref, ys5_ref, w_in_ref, w_out_ref, *rest, tq, chunk, nblk):
    small = [f for f in _MixW._fields if f not in _MIX_W16]
    o_ref = rest[len(small)]
    bufs = rest[len(small) + 1:]
    w16 = bufs[len(bufs) - len(_MIX_W16):]
    bufs = bufs[:len(bufs) - len(_MIX_W16)]
    w = _MixW(**dict(zip(small, rest[:len(small)])), **dict(zip(_MIX_W16, w16)))
    npb = len(_ProjBuf._fields)
    sets = (_ProjBuf(*bufs[:npb]), _ProjBuf(*bufs[npb:2 * npb]))
    ycats = bufs[2 * npb:2 * npb + 2]
    scr = _MixScratch(*bufs[2 * npb + 2:2 * npb + 2 + len(_MixScratch._fields)])
    h16 = bufs[-1]
    s = pl.program_id(0)

    @pl.when(s == 0)
    def _():
        _mixer_load_weights(w_in_ref, w_out_ref, w)
        for ref in bufs:
            ref[...] = jnp.zeros(ref.shape, ref.dtype)

    first_p = (s % nblk) == 0
    first_c = ((s + nblk - 1) % nblk) == 0

    def step(nxt, cur, ycat_c, ycat_o):
        jobs = (_mixer_output_jobs(ho_ref, ycat_o, w, o_ref)
                + _mixer_project_jobs(hp_ref, h16, w, nxt, cur, first_p, tq))

        def side(n):
            for _ in range(min(n, len(jobs))):
                jobs.pop(0)()

        _mixer_compute(cur, ys5_ref, w, scr, ycat_c, first_c, tq, chunk, side)
        side(len(jobs))

    @pl.when(s % 2 == 0)
    def _():
        step(sets[0], sets[1], ycats[1], ycats[0])

    @pl.when(s % 2 == 1)
    def _():
        step(sets[1], sets[0], ycats[0], ycats[1])


def _mixer(h, ys5, w_in, w_out, consts, layer):
    b, l, d = h.shape
    tq = min(MIX_TQ, l)
    chunk = min(SSD_CHUNK, tq)
    nblk = l // tq
    total = b * nblk
    h2d = h.reshape(b * l, d)
    proj_buf = [
        pltpu.VMEM((tq, SSD_WIDTH), F32),
        pltpu.VMEM((tq + SUBLANES, SSD_XBC), F32),
        pltpu.VMEM((tq, LANES), F32),
        pltpu.VMEM((tq + SUBLANES, RG_WIDTH), F32),
        pltpu.VMEM((tq, RG_WIDTH), F32),
    ]
    rg_plane = pltpu.VMEM((RG_WIDTH // LANES, tq, LANES), F32)
    out = pl.pallas_call(
        functools.partial(_mixer_kernel, tq=tq, chunk=chunk, nblk=nblk),
        grid=(total + 2,),
        in_specs=[pl.BlockSpec((tq, d), lambda s: (jnp.minimum(s, total - 1), 0)),
                  pl.BlockSpec((tq, d), lambda s: (jnp.maximum(s - 2, 0), 0)),
                  pl.BlockSpec((tq, S5_WIDTH), lambda s: (jnp.clip(s - 1, 0, total - 1), 0))]
                 + [_layer_spec(c, layer) for c in (w_in, w_out) + tuple(consts)],
        out_specs=pl.BlockSpec((tq, d), lambda s: (jnp.maximum(s - 2, 0), 0)),
        out_shape=jax.ShapeDtypeStruct((b * l, d), F32),
        scratch_shapes=proj_buf + proj_buf + [
            pltpu.VMEM((tq, D_MODEL), BF16),
            pltpu.VMEM((tq, D_MODEL), BF16),
            pltpu.VMEM((SSD_GROUPS, SSD_STATE, SSD_GW), F32),
            rg_plane, rg_plane, rg_plane,
            pltpu.VMEM((SUBLANES, RG_WIDTH), F32),
            pltpu.VMEM((tq, d), BF16),
            pltpu.VMEM((d, SSD_WIDTH), BF16),
            pltpu.VMEM((d, SSD_XBC), BF16),
            pltpu.VMEM((d, LANES), BF16),
            pltpu.VMEM((d, RG_WIDTH), BF16),
            pltpu.VMEM((d, RG_WIDTH), BF16),
            pltpu.VMEM((d, d), BF16),
        ],
        compiler_params=_params(("arbitrary",)),
        name="mixer",
    )(h2d, h2d, ys5.reshape(b * l, S5_WIDTH), w_in, w_out, *consts)
    return out.reshape(b, l, d)


def _block_diag(w):
    nl, nb, bi, bo = w.shape
    return jnp.einsum('lhij,hk->lhikj', w, jnp.eye(nb, dtype=w.dtype)).reshape(nl, nb * bi, nb * bo)


def _kv_kernel(m_ref, wk_ref, wv_ref, k_ref, v_ref):
    m16 = m_ref[...].astype(BF16)
    k_ref[...] = jnp.dot(m16, wk_ref[...].astype(BF16), preferred_element_type=F32).astype(BF16)
    v_ref[...] = jnp.dot(m16, wv_ref[...].astype(BF16), preferred_element_type=F32).astype(BF16)


def _kv(mem, wk, wv, layer):
    b, m, d = mem.shape
    spec = pl.BlockSpec((None, m, d), lambda i: (i, 0, 0))
    return pl.pallas_call(
        _kv_kernel,
        grid=(b,),
        in_specs=[spec, _layer_spec(wk, layer), _layer_spec(wv, layer)],
        out_specs=[spec, spec],
        out_shape=[jax.ShapeDtypeStruct((b, m, d), BF16)] * 2,
        compiler_params=_params(("arbitrary",)),
        name="kv",
    )(mem, wk, wv)


def _xattn_kernel(hq_ref, ho_ref, k_ref, v_ref, wq32_ref, wo32_ref, g_ref, b_ref, o_ref,
                  wq_ref, wo_ref, h16, q0, q1, a0, a1):
    s = pl.program_id(0)

    @pl.when(s == 0)
    def _():
        wq_ref[...] = wq32_ref[...].astype(BF16)
        wo_ref[...] = wo32_ref[...].astype(BF16)
        for ref in (q0, q1, a0, a1):
            ref[...] = jnp.zeros(ref.shape, ref.dtype)

    scale = 1.0 / math.sqrt(XA_HEAD_DIM)

    def step(q_nxt, q_cur, att_cur, att_old):
        def cast_job():
            h16[...] = hq_ref[...].astype(BF16)

        def q_job(cols):
            def run():
                q = jnp.dot(h16[...], wq_ref[:, cols], preferred_element_type=F32) * scale
                q_nxt[:, cols] = q.astype(BF16)
            return run

        def out_job(cols):
            def run():
                y = jnp.dot(att_old[...], wo_ref[:, cols], preferred_element_type=F32)
                o_ref[:, cols] = ALPHA * ho_ref[:, cols] + y
            return run

        def norm_job():
            o_ref[...] = _layer_norm(o_ref[...], g_ref[...], b_ref[...])

        tiles = _col_tiles(D_MODEL)
        jobs = [out_job(c) for c in tiles] + [norm_job, cast_job] + [q_job(c) for c in tiles]

        def side(n):
            for _ in range(min(n, len(jobs))):
                jobs.pop(0)()

        heads = [slice(hd * XA_HEAD_DIM, (hd + 1) * XA_HEAD_DIM) for hd in range(XA_HEADS)]
        scores = [lax.dot_general(q_cur[:, sl], k_ref[:, sl], (((1,), (1,)), ((), ())), preferred_element_type=F32)
                  for sl in heads]
        for hd, sl in enumerate(heads):
            side(3 if hd < 2 else 2)
            p = jnp.exp(scores[hd] - jnp.max(scores[hd], axis=-1, keepdims=True))
            o = jnp.dot(p.astype(BF16), v_ref[:, sl], preferred_element_type=F32)
            att_cur[:, sl] = (o / jnp.sum(p, axis=-1, keepdims=True)).astype(BF16)
        side(len(jobs))

    @pl.when(s % 2 == 0)
    def _():
        step(q0, q1, a1, a0)

    @pl.when(s % 2 == 1)
    def _():
        step(q1, q0, a0, a1)


def _xattn(h, k, v, wq, wo, g, bb, layer):
    b, l, d = h.shape
    tm = min(TOK_TM, l)
    nblk = l // tm
    total = b * nblk
    h2d = h.reshape(b * l, d)
    kvs = pl.BlockSpec((None, k.shape[1], d), lambda s: (jnp.clip(s - 1, 0, total - 1) // nblk, 0, 0))
    act16 = pltpu.VMEM((tm, d), BF16)
    out = pl.pallas_call(
        _xattn_kernel,
        grid=(total + 2,),
        in_specs=[pl.BlockSpec((tm, d), lambda s: (jnp.minimum(s, total - 1), 0)),
                  pl.BlockSpec((tm, d), lambda s: (jnp.maximum(s - 2, 0), 0)),
                  kvs, kvs] + [_layer_spec(c, layer) for c in (wq, wo, g, bb)],
        out_specs=pl.BlockSpec((tm, d), lambda s: (jnp.maximum(s - 2, 0), 0)),
        out_shape=jax.ShapeDtypeStruct((b * l, d), F32),
        scratch_shapes=[pltpu.VMEM((d, d), BF16), pltpu.VMEM((d, d), BF16),
                        act16,
                        act16, act16,
                        act16, act16],
        compiler_params=_params(("arbitrary",)),
        name="xattn",
    )(h2d, h2d, k, v, wq, wo, g, bb)
    return out.reshape(b, l, d)


def _mlp_kernel(h_ref, w1_ref, w2_ref, g_ref, b_ref, o_ref):
    hb = h_ref[...]
    h16 = hb.astype(BF16)
    acc = ALPHA * hb
    for f in range(D_FF // MLP_FC):
        sl = slice(f * MLP_FC, (f + 1) * MLP_FC)
        t = jnp.maximum(jnp.dot(h16, w1_ref[:, sl], preferred_element_type=F32), 0.0)
        acc = acc + jnp.dot((t * t).astype(BF16), w2_ref[sl, :], preferred_element_type=F32)
    o_ref[...] = _layer_norm(acc, g_ref[...], b_ref[...])


def _mlp(h2d, w1, w2, g, bb, layer):
    t, d = h2d.shape
    tm = min(TOK_TM, t)
    tok = pl.BlockSpec((tm, d), lambda i: (i, 0))
    return pl.pallas_call(
        _mlp_kernel,
        grid=(t // tm,),
        in_specs=[tok] + [_layer_spec(c, layer) for c in (w1, w2, g, bb)],
        out_specs=tok,
        out_shape=jax.ShapeDtypeStruct((t, d), F32),
        compiler_params=_params(("parallel",)),
        name="mlp",
    )(h2d, w1, w2, g, bb)


def _rows(v):
    return v.astype(F32).reshape(v.shape[0], 1, -1)


def _pad_lanes(v3d):
    return jnp.pad(v3d, ((0, 0), (0, 0), (0, LANES - v3d.shape[2])))


def prepare_params(seq_len, w_in, w_out, ssd_conv_w, ssd_conv_b, ssd_dt_bias, ssd_a_log, ssd_d, ssd_norm_w,
                   s5_lam_re, s5_lam_im, s5_log_step, s5_b_re, s5_b_im, s5_c_re, s5_c_im, s5_d, s5_glu_w, s5_glu_b,
                   rg_conv_w, rg_conv_b, rg_wa, rg_ba, rg_wx, rg_bx, rg_lambda, ln1_g, ln1_b,
                   xa_wq, xa_wk, xa_wv, xa_wo, ln2_g, ln2_b, mlp_w1, mlp_w2, ln3_g, ln3_b):
    o_u = SSD_WIDTH + SSD_XBC + SSD_HEADS
    n_tiles = min(S5_TB, seq_len) // (S5_BLK * SUBLANES)
    assert n_tiles <= 2 ** (SUBLANES - 3), "S5 scan-level table holds 3 in-tile + 5 tile levels"
    s5_ops = jax.vmap(functools.partial(_s5_prepare, n_tiles=n_tiles))(
        s5_lam_re, s5_lam_im, s5_log_step, s5_b_re, s5_b_im, s5_c_re, s5_c_im)
    s5 = (w_in[:, :, o_u:o_u + S5_WIDTH].astype(BF16), *s5_ops, _rows(s5_d), s5_glu_w.astype(BF16), _rows(s5_glu_b))
    mix_small = dict(
        cw=ssd_conv_w.astype(F32), cb=_rows(ssd_conv_b), dtb=_pad_lanes(_rows(ssd_dt_bias)),
        aneg=_pad_lanes(_rows(-jnp.exp(ssd_a_log.astype(F32)))),
        dvec=_rows(jnp.repeat(ssd_d.astype(F32), SSD_HEAD_DIM, axis=1)), nw=_rows(ssd_norm_w),
        rcw=rg_conv_w.astype(F32), rcb=_rows(rg_conv_b),
        wa=_block_diag(rg_wa.astype(F32)).astype(BF16), ba=_rows(rg_ba.reshape(rg_ba.shape[0], -1)),
        wx=_block_diag(rg_wx.astype(F32)).astype(BF16), bx=_rows(rg_bx.reshape(rg_bx.shape[0], -1)),
        rlam=_rows(rg_lambda), g=_rows(ln1_g), b=_rows(ln1_b))
    mix = (w_in.astype(F32), w_out.astype(F32), tuple(mix_small[f] for f in _MixW._fields if f not in _MIX_W16))
    xa = (xa_wk.astype(F32), xa_wv.astype(F32), xa_wq.astype(F32), xa_wo.astype(F32), _rows(ln2_g), _rows(ln2_b))
    mlp = (mlp_w1.astype(BF16), mlp_w2.astype(BF16), _rows(ln3_g), _rows(ln3_b))
    return dict(s5=s5, mix=mix, xa=xa, mlp=mlp)


def mixer_sublayer(h, prm, layer):
    ys5 = _s5_branch(h, prm["s5"], layer)
    w_in, w_out, consts = prm["mix"]
    return _mixer(h, ys5, w_in, w_out, consts, layer)


def xattn_sublayer(h, mem, prm, layer):
    wk, wv, wq, wo, g, bb = prm["xa"]
    k, v = _kv(mem, wk, wv, layer)
    return _xattn(h, k, v, wq, wo, g, bb, layer)


def mlp_sublayer(h, prm, layer):
    b, l, d = h.shape
    return _mlp(h.reshape(b * l, d), *prm["mlp"], layer).reshape(b, l, d)


def kernel(x, mem, w_in, w_out, ssd_conv_w, ssd_conv_b, ssd_dt_bias, ssd_a_log, ssd_d, ssd_norm_w, s5_lam_re, s5_lam_im, s5_log_step, s5_b_re, s5_b_im, s5_c_re, s5_c_im, s5_d, s5_glu_w, s5_glu_b, rg_conv_w, rg_conv_b, rg_wa, rg_ba, rg_wx, rg_bx, rg_lambda, ln1_g, ln1_b, xa_wq, xa_wk, xa_wv, xa_wo, ln2_g, ln2_b, mlp_w1, mlp_w2, ln3_g, ln3_b):
    h = x.astype(F32)
    memf = mem.astype(F32)
    prm = prepare_params(x.shape[1], w_in, w_out, ssd_conv_w, ssd_conv_b, ssd_dt_bias, ssd_a_log, ssd_d, ssd_norm_w,
                         s5_lam_re, s5_lam_im, s5_log_step, s5_b_re, s5_b_im, s5_c_re, s5_c_im, s5_d, s5_glu_w,
                         s5_glu_b, rg_conv_w, rg_conv_b, rg_wa, rg_ba, rg_wx, rg_bx, rg_lambda, ln1_g, ln1_b,
                         xa_wq, xa_wk, xa_wv, xa_wo, ln2_g, ln2_b, mlp_w1, mlp_w2, ln3_g, ln3_b)
    for layer in range(DEPTH):
        h = mixer_sublayer(h, prm, layer)
        h = xattn_sublayer(h, memf, prm, layer)
        h = mlp_sublayer(h, prm, layer)
    return h.astype(x.dtype)
```

```python
import collections
import functools
import math

import jax
import jax.numpy as jnp
from jax import lax
from jax.experimental import pallas as pl
from jax.experimental.pallas import tpu as pltpu

F32 = jnp.float32
BF16 = jnp.bfloat16

D_MODEL = 1024
DEPTH = 2
SSD_WIDTH = 512
SSD_HEAD_DIM = 64
SSD_HEADS = 8
SSD_GROUPS = 2
SSD_HPG = SSD_HEADS // SSD_GROUPS
SSD_GW = SSD_HPG * SSD_HEAD_DIM
SSD_STATE = 128
SSD_XBC = 1024
CONV_K = 4
S5_WIDTH = 256
S5_GROUPS = 16
S5_GROUP_CH = 16
S5_STATE = 64
S5_NSTATE = S5_GROUPS * S5_STATE
S5_BLK = 8
RG_WIDTH = 256
RG_BLOCKS = 4
RG_C = 8.0
XA_HEADS = 4
XA_HEAD_DIM = 256
D_FF = 4096
ALPHA = (2.0 * DEPTH) ** 0.25
LN_EPS = 1e-5

LANES = 128
SUBLANES = 8
MXU_TILE = 256
NEG_BIG = -1e30
VMEM_LIMIT = 56 * 1024 * 1024

SSD_CHUNK = 128
MIX_TQ = 256
S5_TB = 2048
TOK_TM = 512
MLP_FC = 1024


def _dot(a, b):
    return jnp.dot(a.astype(BF16), b.astype(BF16), preferred_element_type=F32)


def _dot_nt(a, b):
    return lax.dot_general(a.astype(BF16), b.astype(BF16), (((1,), (1,)), ((), ())),
                           preferred_element_type=F32)


def _layer_norm(x, g, b):
    mu = jnp.mean(x, axis=-1, keepdims=True)
    xc = x - mu
    var = jnp.mean(xc * xc, axis=-1, keepdims=True)
    return xc * lax.rsqrt(var + LN_EPS) * g + b


def _gelu(x):
    c = math.sqrt(2.0 / math.pi)
    return 0.5 * x * (1.0 + jnp.tanh(c * (x + 0.044715 * (x * x * x))))


def _sigmoid(x):
    return 0.5 + 0.5 * jnp.tanh(0.5 * x)


def _silu(x):
    hx = 0.5 * x
    return hx + hx * jnp.tanh(hx)


def _softplus(x):
    return jnp.maximum(x, 0.0) + jnp.log1p(jnp.exp(-jnp.abs(x)))


def _const_spec(shape):
    nd = len(shape)
    return pl.BlockSpec(shape, lambda *_: (0,) * nd, pipeline_mode=pl.Buffered(1))


def _layer_spec(stacked, layer):
    nd = stacked.ndim - 1
    return pl.BlockSpec((None,) + stacked.shape[1:], lambda *_: (layer,) + (0,) * nd, pipeline_mode=pl.Buffered(1))


def _params(sem):
    return pltpu.CompilerParams(dimension_semantics=sem, vmem_limit_bytes=VMEM_LIMIT)


def _re_lanes(p):
    return slice(2 * p * LANES, (2 * p + 1) * LANES)


def _im_lanes(p):
    return slice((2 * p + 1) * LANES, (2 * p + 2) * LANES)


def _s5_state_group(idx):
    state = (idx // (2 * LANES)) * LANES + idx % LANES
    return state // S5_STATE


def _s5_split(x):
    planes = S5_NSTATE // LANES
    return (jnp.concatenate([x[:, _re_lanes(p)] for p in range(planes)], axis=-1),
            jnp.concatenate([x[:, _im_lanes(p)] for p in range(planes)], axis=-1))


S5_HALF_CH = S5_WIDTH // 2
S5_HALF_LANES = S5_BLK * S5_HALF_CH


def _s5_expand_operators(wc_ref, vc_ref, ct_ref, ms_scr, toep_scr, mo_scr):
    n, w, c = S5_NSTATE, S5_WIDTH, S5_GROUP_CH
    hw, hl = S5_HALF_CH, S5_HALF_LANES
    grp_half = S5_GROUPS // 2
    reps = hw // c
    grp_r = _s5_state_group(lax.broadcasted_iota(jnp.int32, (2 * n, w), 0))
    same_o = grp_r == lax.broadcasted_iota(jnp.int32, (2 * n, w), 1) // c
    mo0 = jnp.where(same_o, ct_ref[...], 0.0)
    kc = jnp.dot(wc_ref[...], mo0, preferred_element_type=F32, precision=lax.Precision.HIGHEST)
    toep_scr[...] = jnp.zeros(toep_scr.shape, toep_scr.dtype)
    ch_grp = lax.broadcasted_iota(jnp.int32, (hw, hl), 0) // c
    same_k = (lax.broadcasted_iota(jnp.int32, (hw, hw), 0) // c) == (lax.broadcasted_iota(jnp.int32, (hw, hw), 1) // c)
    pick_r = lax.broadcasted_iota(jnp.int32, (S5_BLK * c, hl), 0)
    pick_c = lax.broadcasted_iota(jnp.int32, (S5_BLK * c, hl), 1)
    pick = ((pick_r // c == pick_c // hw) & (pick_r % c == pick_c % c)).astype(BF16)
    for hf in range(2):
        lanes = slice(hf * hl, (hf + 1) * hl)
        st_grp = _s5_state_group(lax.broadcasted_iota(jnp.int32, (hw, hl), 1) + hf * hl) - hf * grp_half
        same_s = ch_grp == st_grp
        for t in range(S5_BLK):
            blk = jnp.where(same_s, jnp.tile(wc_ref[t * c:(t + 1) * c, lanes], (reps, 1)), 0.0)
            ms_scr[hf, t * hw:(t + 1) * hw, :] = blk.astype(BF16)
        st_grp = _s5_state_group(lax.broadcasted_iota(jnp.int32, (hl, hl), 0) + hf * hl) - hf * grp_half
        same_r = st_grp == (lax.broadcasted_iota(jnp.int32, (hl, hl), 1) % hw) // c
        spread = jnp.dot(vc_ref[lanes, :].astype(BF16), pick, preferred_element_type=F32)
        mo_scr[hf] = jnp.where(same_r, spread, 0.0).astype(BF16)
        for t_in in range(S5_BLK):
            for t_out in range(t_in, S5_BLK):
                k = S5_BLK - 1 - (t_out - t_in)
                blk = jnp.where(same_k, jnp.tile(kc[k * c:(k + 1) * c, hf * hw:(hf + 1) * hw], (reps, 1)), 0.0)
                toep_scr[hf, t_in * hw:(t_in + 1) * hw, t_out * hw:(t_out + 1) * hw] = blk.astype(BF16)


_S5W = collections.namedtuple("_S5W", "wu wc vc ct lvl pw pwt d gluw glub")
_S5Ops = collections.namedtuple("_S5Ops", "ms toep mo")
_S5Buf = collections.namedtuple("_S5Buf", "u32 u16 hprev16")
_S5Scratch = collections.namedtuple("_S5Scratch", "u_tok hplane cexp carry yg")


def _s5_scan_stage(h_ref, w, ops, scr, out, first, rows, side):
    n, wd = S5_NSTATE, S5_WIDTH
    lane_halves = wd // LANES
    planes = n // LANES
    nt = rows // SUBLANES
    tb = rows * S5_BLK
    for r0 in range(0, tb, tb // 4):
        rs = slice(r0, r0 + tb // 4)
        u_tok = jnp.dot(h_ref[rs, :].astype(BF16), w.wu[...], preferred_element_type=F32)
        for hf in range(lane_halves):
            scr.u_tok[hf, rs, :] = u_tok[:, hf * LANES:(hf + 1) * LANES]
    u = jnp.concatenate([scr.u_tok[hf, pl.ds(k, rows, stride=S5_BLK), :]
                         for hf in range(lane_halves) for k in range(S5_BLK)], axis=-1)
    out.u32[...] = u
    u16 = u.astype(BF16)
    out.u16[...] = u16
    row8 = lax.broadcasted_iota(jnp.int32, (nt, SUBLANES, LANES), 1)
    planes_per_half = planes // lane_halves
    for p in range(planes):
        hf, pl_in_half = divmod(p, planes_per_half)
        s = jnp.dot(u16[:, hf * S5_HALF_LANES:(hf + 1) * S5_HALF_LANES],
                    ops.ms[hf, :, 2 * pl_in_half * LANES:(2 * pl_in_half + 2) * LANES],
                    preferred_element_type=F32)
        if p < planes - 2:
            side(1)
        xr = s[:, :LANES].reshape(nt, SUBLANES, LANES)
        xi = s[:, LANES:].reshape(nt, SUBLANES, LANES)
        for lv, dist in enumerate((1, 2, 4)):
            lr = w.lvl[lv:lv + 1, _re_lanes(p)]
            li = w.lvl[lv:lv + 1, _im_lanes(p)]
            keep = row8 >= dist
            sr = jnp.where(keep, pltpu.roll(xr, dist, 1), 0.0)
            si = jnp.where(keep, pltpu.roll(xi, dist, 1), 0.0)
            xr, xi = xr + lr * sr - li * si, xi + lr * si + li * sr
        scr.hplane[2 * p, SUBLANES:SUBLANES + rows, :] = xr.reshape(rows, LANES)
        scr.hplane[2 * p + 1, SUBLANES:SUBLANES + rows, :] = xi.reshape(rows, LANES)
    side(1)
    last = 2 * SUBLANES - 1
    er = jnp.concatenate([scr.hplane[2 * p, pl.ds(last, nt, stride=SUBLANES), :] for p in range(planes)], axis=-1)
    ei = jnp.concatenate([scr.hplane[2 * p + 1, pl.ds(last, nt, stride=SUBLANES), :] for p in range(planes)], -1)
    tile_id = lax.broadcasted_iota(jnp.int32, (nt, n), 0)
    lv, dist = 3, 1
    while dist < nt:
        lr, li = _s5_split(w.lvl[lv:lv + 1, :])
        keep = tile_id >= dist
        sr = jnp.where(keep, pltpu.roll(er, dist, 0), 0.0)
        si = jnp.where(keep, pltpu.roll(ei, dist, 0), 0.0)
        er, ei = er + lr * sr - li * si, ei + lr * si + li * sr
        lv, dist = lv + 1, dist * 2
    cr, ci = _s5_split(jnp.where(first, 0.0, scr.carry[0:1, :]))
    ptr, pti = _s5_split(w.pwt[...])
    er, ei = er + ptr * cr - pti * ci, ei + ptr * ci + pti * cr
    for p in range(planes):
        scr.carry[0:1, _re_lanes(p)] = er[nt - 1:nt, p * LANES:(p + 1) * LANES]
        scr.carry[0:1, _im_lanes(p)] = ei[nt - 1:nt, p * LANES:(p + 1) * LANES]
    tin_r = jnp.where(tile_id >= 1, pltpu.roll(er, 1, 0), cr)
    tin_i = jnp.where(tile_id >= 1, pltpu.roll(ei, 1, 0), ci)
    side(1)
    for p in range(planes):
        for k in range(SUBLANES):
            scr.cexp[2 * p, pl.ds(k, nt, stride=SUBLANES), :] = tin_r[:, p * LANES:(p + 1) * LANES]
            scr.cexp[2 * p + 1, pl.ds(k, nt, stride=SUBLANES), :] = tin_i[:, p * LANES:(p + 1) * LANES]
    for p in range(planes):
        if p % 2 == 0:
            side(1)
        pr = jnp.tile(w.pw[:, _re_lanes(p)], (nt, 1))
        pi = jnp.tile(w.pw[:, _im_lanes(p)], (nt, 1))
        tr = scr.cexp[2 * p]
        ti = scr.cexp[2 * p + 1]
        hr = scr.hplane[2 * p, SUBLANES:SUBLANES + rows, :] + pr * tr - pi * ti
        hi = scr.hplane[2 * p + 1, SUBLANES:SUBLANES + rows, :] + pr * ti + pi * tr
        scr.hplane[2 * p, SUBLANES:SUBLANES + rows, :] = hr
        scr.hplane[2 * p + 1, SUBLANES:SUBLANES + rows, :] = hi
        scr.hplane[2 * p, SUBLANES - 1:SUBLANES, :] = cr[:, p * LANES:(p + 1) * LANES]
        scr.hplane[2 * p + 1, SUBLANES - 1:SUBLANES, :] = ci[:, p * LANES:(p + 1) * LANES]
    for p in range(2 * planes):
        out.hprev16[:, p * LANES:(p + 1) * LANES] = scr.hplane[p, SUBLANES - 1:SUBLANES - 1 + rows, :].astype(BF16)


def _s5_readout_jobs(src, w, ops, scr, o_ref, rows):
    hw, hl = S5_HALF_CH, S5_HALF_LANES
    lane_halves = S5_WIDTH // LANES
    steps_per_tile = MXU_TILE // hw

    def tile_job(hf, j):
        def run():
            k_rows = (j + 1) * MXU_TILE
            cols = slice(j * MXU_TILE, (j + 1) * MXU_TILE)
            y = jnp.dot(src.u16[:, hf * hl:hf * hl + k_rows], ops.toep[hf, :k_rows, cols],
                        preferred_element_type=F32)
            y = y + jnp.dot(src.hprev16[:, hf * hl:(hf + 1) * hl], ops.mo[hf, :, cols], preferred_element_type=F32)
            d_half = w.d[:, hf * hw:(hf + 1) * hw]
            y = _gelu(y + jnp.concatenate([d_half] * steps_per_tile, axis=1) * src.u32[:, hf * hl + j * MXU_TILE:
                                                                                           hf * hl + (j + 1) * MXU_TILE])
            for i in range(steps_per_tile):
                scr.yg[hf, pl.ds(j * steps_per_tile + i, rows, stride=S5_BLK), :] = y[:, i * hw:(i + 1) * hw]
        return run

    def glu_job():
        yg = jnp.concatenate([scr.yg[hf] for hf in range(lane_halves)], axis=-1)
        glu = _sigmoid(_dot(yg, w.gluw[...]) + w.glub[...])
        o_ref[...] = (yg * glu).astype(o_ref.dtype)

    return [tile_job(hf, j) for hf in range(lane_halves) for j in range(hl // MXU_TILE)] + [glu_job]


def _s5_kernel(h_ref, *rest, rows, nblk):
    nw = len(_S5W._fields)
    w = _S5W(*rest[:nw])
    o_ref = rest[nw]
    bufs = rest[nw + 1:]
    ops = _S5Ops(*bufs[:3])
    nb = len(_S5Buf._fields)
    scr = _S5Scratch(*bufs[3 + nb:])
    s = pl.program_id(0)
    parity = s % 2

    @pl.when(s == 0)
    def _():
        _s5_expand_operators(w.wc, w.vc, w.ct, *ops)
        for ref in bufs[3:]:
            ref[...] = jnp.zeros(ref.shape, ref.dtype)

    first = (s % nblk) == 0

    def step(out, src):
        jobs = _s5_readout_jobs(src, w, ops, scr, o_ref, rows)

        def side(k):
            for _ in range(min(k, len(jobs))):
                jobs.pop(0)()

        _s5_scan_stage(h_ref, w, ops, scr, out, first, rows, side)
        side(len(jobs))

    step(_S5Buf(*(r.at[parity] for r in bufs[3:3 + nb])), _S5Buf(*(r.at[1 - parity] for r in bufs[3:3 + nb])))


def _s5_branch(h, consts, layer):
    b, l, d = h.shape
    tb = min(S5_TB, l)
    rows = tb // S5_BLK
    nblk = l // tb
    total = b * nblk
    n, w = S5_NSTATE, S5_WIDTH
    handoff = [
        pltpu.VMEM((2, rows, S5_BLK * w), F32),
        pltpu.VMEM((2, rows, S5_BLK * w), BF16),
        pltpu.VMEM((2, rows, 2 * n), BF16),
    ]
    out = pl.pallas_call(
        functools.partial(_s5_kernel, rows=rows, nblk=nblk),
        grid=(total + 1,),
        in_specs=[pl.BlockSpec((tb, d), lambda s: (jnp.minimum(s, total - 1), 0))]
                 + [_layer_spec(c, layer) for c in consts],
        out_specs=pl.BlockSpec((tb, w), lambda s: (jnp.maximum(s - 1, 0), 0)),
        out_shape=jax.ShapeDtypeStruct((b * l, w), BF16),
        scratch_shapes=[
            pltpu.VMEM((2, S5_HALF_LANES, S5_HALF_LANES), BF16),
            pltpu.VMEM((2, S5_HALF_LANES, S5_HALF_LANES), BF16),
            pltpu.VMEM((2, S5_HALF_LANES, S5_HALF_LANES), BF16),
        ] + handoff + [
            pltpu.VMEM((w // LANES, tb, LANES), F32),
            pltpu.VMEM((2 * n // LANES, rows + SUBLANES, LANES), F32),
            pltpu.VMEM((2 * n // LANES, rows, LANES), F32),
            pltpu.VMEM((SUBLANES, 2 * n), F32),
            pltpu.VMEM((w // LANES, tb, LANES), F32),
        ],
        compiler_params=_params(("arbitrary",)),
        name="s5_branch",
    )(h.reshape(b * l, d), *consts)
    return out.reshape(b, l, w)


def _s5_prepare(lam_re, lam_im, log_step, b_re, b_im, c_re, c_im, n_tiles):
    g, p, c, t8 = S5_GROUPS, S5_STATE, S5_GROUP_CH, S5_BLK
    lr, li = lam_re.astype(F32), lam_im.astype(F32)
    dt = jnp.exp(log_step.astype(F32))[:, None]
    ar, ai = lr * dt, li * dt

    def lam_pow(k):
        k = k.astype(F32)[..., None, None]
        mag = jnp.exp(k * ar)
        return mag * jnp.cos(k * ai), mag * jnp.sin(k * ai)

    br1, bi1 = lam_pow(jnp.ones((), jnp.int32))
    xr_, xi_ = br1 - 1.0, bi1
    den = lr * lr + li * li
    cr, ci = (xr_ * lr + xi_ * li) / den, (xi_ * lr - xr_ * li) / den
    bre, bim = b_re.astype(F32), b_im.astype(F32)
    bbr = cr[..., None] * bre - ci[..., None] * bim
    bbi = cr[..., None] * bim + ci[..., None] * bre
    steps = jnp.arange(t8)
    pr, pi = lam_pow(t8 - 1 - steps)
    wsr = pr[..., None] * bbr[None] - pi[..., None] * bbi[None]
    wsi = pr[..., None] * bbi[None] + pi[..., None] * bbr[None]

    def interleave(re, im, axis):
        shape = list(re.shape)
        shape[axis:axis + 1] = [re.shape[axis] // LANES, 1, LANES]
        both = jnp.concatenate([re.reshape(shape), im.reshape(shape)], axis=axis + 1)
        shape[axis:axis + 3] = [2 * re.shape[axis]]
        return both.reshape(shape)

    wc = interleave(*[part.transpose(0, 3, 1, 2).reshape(t8 * c, g * p) for part in (wsr, wsi)], axis=1)
    cre, cim = c_re.astype(F32), c_im.astype(F32)
    pr, pi = lam_pow(steps + 1)
    vor = cre[None] * pr[:, :, None, :] - cim[None] * pi[:, :, None, :]
    voi = cre[None] * pi[:, :, None, :] + cim[None] * pr[:, :, None, :]
    vc = interleave(*[part.transpose(1, 3, 0, 2).reshape(g * p, t8 * c) for part in (vor, -voi)], axis=0)
    ct = interleave(*[jnp.tile(part.transpose(0, 2, 1).reshape(g * p, c), (1, g)) for part in (cre, -cim)], axis=0)

    def re_im(pair):
        return interleave(pair[0].reshape(-1, g * p), pair[1].reshape(-1, g * p), axis=1)

    lvl = re_im(lam_pow(t8 * 2 ** jnp.arange(SUBLANES)))
    pw = re_im(lam_pow(t8 * (steps + 1)))
    pwt = re_im(lam_pow(t8 * SUBLANES * (jnp.arange(n_tiles) + 1)))
    return wc, vc, ct, lvl, pw, pwt


def _expand_heads(cols, grp, head_lane):
    base = SSD_HPG * grp
    out = jnp.broadcast_to(cols[:, base:base + 1], (cols.shape[0], SSD_GW))
    for hh in range(1, SSD_HPG):
        out = jnp.where(head_lane == hh, cols[:, base + hh:base + hh + 1], out)
    return out


def _causal_conv(pad_ref, w_ref, b_ref, rows, cols):
    acc = b_ref[:, cols] + w_ref[CONV_K - 1:CONV_K, cols] * pad_ref[SUBLANES:SUBLANES + rows, cols]
    for k in range(CONV_K - 1):
        off = SUBLANES - (CONV_K - 1) + k
        acc = acc + w_ref[k:k + 1, cols] * pad_ref[off:off + rows, cols]
    return acc


_ProjBuf = collections.namedtuple("_ProjBuf", "z xpad dt rpad gr")
_MixW = collections.namedtuple(
    "_MixW", "wz wxbc wdt wxr wgr cw cb dtb aneg dvec nw rcw rcb wa ba wx bx rlam wout g b")
_MixScratch = collections.namedtuple("_MixScratch", "state ascr bscr cexp rcar")


def _col_tiles(width):
    return [slice(c, min(c + MXU_TILE, width)) for c in range(0, width, MXU_TILE)]


def _mixer_output_jobs(h_ref, ycat, w, o_ref):
    def tile_job(cols):
        def run():
            y = jnp.dot(ycat[...], w.wout[:, cols], preferred_element_type=F32)
            o_ref[:, cols] = ALPHA * h_ref[:, cols] + y
        return run

    def norm_job():
        o_ref[...] = _layer_norm(o_ref[...], w.g[...], w.b[...])

    return [tile_job(cols) for cols in _col_tiles(D_MODEL)] + [norm_job]


def _mixer_project_jobs(h_ref, h16, w, nxt, cur, first, tq):
    def cast_job():
        h16[...] = h_ref[...].astype(BF16)

    def tile_job(dst, w_ref, cols, row0):
        def run():
            dst[row0:row0 + tq, cols] = jnp.dot(h16[...], w_ref[:, cols], preferred_element_type=F32)
        return run

    def tail_job(pad_n, pad_c):
        def run():
            pad_n[0:SUBLANES, :] = jnp.where(first, 0.0, pad_c[tq:tq + SUBLANES, :])
        return run

    jobs = [cast_job]
    for dst, w_ref, row0 in ((nxt.z, w.wz, 0), (nxt.xpad, w.wxbc, SUBLANES), (nxt.dt, w.wdt, 0),
                             (nxt.rpad, w.wxr, SUBLANES), (nxt.gr, w.wgr, 0)):
        jobs += [tile_job(dst, w_ref, cols, row0) for cols in _col_tiles(dst.shape[1])]
    jobs += [tail_job(nxt.xpad, cur.xpad), tail_job(nxt.rpad, cur.rpad)]
    return jobs


def _mixer_compute(cur, ys5_ref, w, scr, ycat, first, tq, chunk, side):
    cw_ref, cb_ref, dtb_ref, aneg_ref, dvec_ref, nw_ref = w.cw, w.cb, w.dtb, w.aneg, w.dvec, w.nw
    state = scr.state
    z = cur.z
    slabs = []
    for cols in _col_tiles(SSD_XBC):
        side(1)
        slabs.append(_silu(_causal_conv(cur.xpad, cw_ref, cb_ref, tq, cols)))
    xact = jnp.concatenate(slabs, axis=-1)
    side(2)
    dt_all = _softplus(cur.dt[...] + dtb_ref[...])
    q = chunk
    tri = (lax.broadcasted_iota(jnp.int32, (q, q), 0) >= lax.broadcasted_iota(jnp.int32, (q, q), 1))
    row_id = lax.broadcasted_iota(jnp.int32, (q, LANES), 0)
    head_lane = lax.broadcasted_iota(jnp.int32, (1, SSD_GW), 1) // SSD_HEAD_DIM
    for ci in range(tq // q):
        r0 = ci * q
        dt = dt_all[r0:r0 + q, :]
        cs = dt * aneg_ref[...]
        dist = 1
        while dist < q:
            cs = cs + jnp.where(row_id >= dist, pltpu.roll(cs, dist, 0), 0.0)
            dist *= 2
        cs_t = cs.T
        cs_last = cs[q - 1:q, :]
        chunk_decay = jnp.exp(cs_last)
        dec_t = jnp.exp(cs_t[:, q - 1:q] - cs_t)
        ecs = jnp.exp(cs)
        zc = z[r0:r0 + q, :]
        halves = []
        for grp in range(SSD_GROUPS):
            side(2)
            lo = grp * SSD_GW
            xs = xact[r0:r0 + q, lo:lo + SSD_GW]
            bm = xact[r0:r0 + q, SSD_WIDTH + grp * SSD_STATE:SSD_WIDTH + (grp + 1) * SSD_STATE]
            cm = xact[r0:r0 + q, SSD_WIDTH + (SSD_GROUPS + grp) * SSD_STATE:
                      SSD_WIDTH + (SSD_GROUPS + grp + 1) * SSD_STATE]
            cm16 = cm.astype(BF16)
            xdt16 = (xs * _expand_heads(dt, grp, head_lane)).astype(BF16)
            cb = _dot_nt(cm16, bm)
            bm_t = bm.T
            s_prev = state[grp]
            if ci == 0:
                s_prev = jnp.where(first, 0.0, s_prev)
            y_off = jnp.dot(cm16, s_prev.astype(BF16), preferred_element_type=F32)
            intra, to_state, xdt_heads = [], [], []
            for hh in range(SSD_HPG):
                hd = grp * SSD_HPG + hh
                seg = cs[:, hd:hd + 1] - cs_t[hd:hd + 1, :]
                lmat = jnp.exp(jnp.where(tri, seg, NEG_BIG))
                intra.append((cb * lmat).astype(BF16))
                to_state.append((bm_t * dec_t[hd:hd + 1, :]).astype(BF16))
                xdt_heads.append(jnp.where(head_lane == hh, xdt16, jnp.zeros_like(xdt16)))
            lhs = jnp.concatenate([jnp.concatenate(intra, axis=1), jnp.concatenate(to_state, axis=1)], axis=0)
            both = jnp.dot(lhs, jnp.concatenate(xdt_heads, axis=0), preferred_element_type=F32)
            y_diag = both[:q]
            st_new = both[q:]
            state[grp] = s_prev * _expand_heads(chunk_decay, grp, head_lane) + st_new
            halves.append(y_diag + y_off * _expand_heads(ecs, grp, head_lane) + xs * dvec_ref[:, lo:lo + SSD_GW])
        y = jnp.concatenate(halves, axis=-1) * _silu(zc)
        y = y * lax.rsqrt(jnp.mean(y * y, axis=-1, keepdims=True) + LN_EPS) * nw_ref[...]
        ycat[r0:r0 + q, 0:SSD_WIDTH] = y.astype(BF16)

    ycat[:, SSD_WIDTH:SSD_WIDTH + S5_WIDTH] = ys5_ref[...]
    _mixer_rglru(cur, w, scr, ycat, first, tq, side)


def _mixer_rglru(cur, w, scr, ycat, first, tq, side):
    side(1)
    xc = _causal_conv(cur.rpad, w.rcw, w.rcb, tq, slice(0, RG_WIDTH))
    xc16 = xc.astype(BF16)
    rgate = _sigmoid(jnp.dot(xc16, w.wa[...], preferred_element_type=F32) + w.ba[...])
    igate = _sigmoid(jnp.dot(xc16, w.wx[...], preferred_element_type=F32) + w.bx[...])
    log_a = (-RG_C) * rgate * _softplus(-w.rlam[...])
    a = jnp.exp(log_a)
    bt = jnp.sqrt(1.0 - jnp.exp(2.0 * log_a)) * (igate * xc)
    row8 = lax.broadcasted_iota(jnp.int32, (tq, RG_WIDTH), 0) & (SUBLANES - 1)
    side(1)
    for dist in (1, 2, 4):
        keep = row8 >= dist
        a_sh = jnp.where(keep, pltpu.roll(a, dist, 0), 1.0)
        b_sh = jnp.where(keep, pltpu.roll(bt, dist, 0), 0.0)
        bt = bt + a * b_sh
        a = a * a_sh
    nt = tq // SUBLANES
    lane_halves = RG_WIDTH // LANES
    for hf in range(lane_halves):
        scr.ascr[hf] = a[:, hf * LANES:(hf + 1) * LANES]
        scr.bscr[hf] = bt[:, hf * LANES:(hf + 1) * LANES]
    ae = jnp.concatenate([scr.ascr[hf, pl.ds(SUBLANES - 1, nt, stride=SUBLANES), :] for hf in range(lane_halves)], -1)
    be = jnp.concatenate([scr.bscr[hf, pl.ds(SUBLANES - 1, nt, stride=SUBLANES), :] for hf in range(lane_halves)], -1)
    tile_id = lax.broadcasted_iota(jnp.int32, (nt, RG_WIDTH), 0)
    dist = 1
    while dist < nt:
        keep = tile_id >= dist
        a_sh = jnp.where(keep, pltpu.roll(ae, dist, 0), 1.0)
        b_sh = jnp.where(keep, pltpu.roll(be, dist, 0), 0.0)
        be = be + ae * b_sh
        ae = ae * a_sh
        dist *= 2
    carry_in = jnp.where(first, 0.0, scr.rcar[0:1, :])
    tile_end = be + ae * carry_in
    scr.rcar[0:1, :] = tile_end[nt - 1:nt, :]
    tile_in = jnp.where(tile_id >= 1, pltpu.roll(tile_end, 1, 0), carry_in)
    for hf in range(lane_halves):
        for k in range(SUBLANES):
            scr.cexp[hf, pl.ds(k, nt, stride=SUBLANES), :] = tile_in[:, hf * LANES:(hf + 1) * LANES]
    hscan = bt + a * jnp.concatenate([scr.cexp[hf] for hf in range(lane_halves)], axis=-1)
    ycat[:, SSD_WIDTH + S5_WIDTH:] = (hscan * _gelu(cur.gr[...])).astype(BF16)


_MIX_W16 = ("wz", "wxbc", "wdt", "wxr", "wgr", "wout")


def _mixer_load_weights(w_in_ref, w_out_ref, w):
    o_dt = SSD_WIDTH + SSD_XBC
    o_xr = o_dt + SSD_HEADS + S5_WIDTH
    o_gr = o_xr + RG_WIDTH
    w.wz[...] = w_in_ref[:, 0:SSD_WIDTH].astype(BF16)
    w.wxbc[...] = w_in_ref[:, SSD_WIDTH:o_dt].astype(BF16)
    head_cols = lax.broadcasted_iota(jnp.int32, (1, LANES), 1) < SSD_HEADS
    w.wdt[...] = jnp.where(head_cols, w_in_ref[:, o_dt:o_dt + LANES], 0.0).astype(BF16)
    w.wxr[...] = w_in_ref[:, o_xr:o_gr].astype(BF16)
    w.wgr[...] = w_in_ref[:, o_gr:o_gr + RG_WIDTH].astype(BF16)
    w.wout[...] = w_out_ref[...].astype(BF16)


def _mixer_kernel(hp_ref, ho_ref, ys5_ref, w_in_ref, w_out_ref, *rest, tq, chunk, nblk):
    small = [f for f in _MixW._fields if f not in _MIX_W16]
    o_ref = rest[len(small)]
    bufs = rest[len(small) + 1:]
    w16 = bufs[len(bufs) - len(_MIX_W16):]
    bufs = bufs[:len(bufs) - len(_MIX_W16)]
    w = _MixW(**dict(zip(small, rest[:len(small)])), **dict(zip(_MIX_W16, w16)))
    npb = len(_ProjBuf._fields)
    sets = (_ProjBuf(*bufs[:npb]), _ProjBuf(*bufs[npb:2 * npb]))
    ycats = bufs[2 * npb:2 * npb + 2]
    scr = _MixScratch(*bufs[2 * npb + 2:2 * npb + 2 + len(_MixScratch._fields)])
    h16 = bufs[-1]
    s = pl.program_id(0)

    @pl.when(s == 0)
    def _():
        _mixer_load_weights(w_in_ref, w_out_ref, w)
        for ref in bufs:
            ref[...] = jnp.zeros(ref.shape, ref.dtype)

    first_p = (s % nblk) == 0
    first_c = ((s + nblk - 1) % nblk) == 0

    def step(nxt, cur, ycat_c, ycat_o):
        jobs = (_mixer_output_jobs(ho_ref, ycat_o, w, o_ref)
                + _mixer_project_jobs(hp_ref, h16, w, nxt, cur, first_p, tq))

        def side(n):
            for _ in range(min(n, len(jobs))):
                jobs.pop(0)()

        _mixer_compute(cur, ys5_ref, w, scr, ycat_c, first_c, tq, chunk, side)
        side(len(jobs))

    @pl.when(s % 2 == 0)
    def _():
        step(sets[0], sets[1], ycats[1], ycats[0])

    @pl.when(s % 2 == 1)
    def _():
        step(sets[1], sets[0], ycats[0], ycats[1])


def _mixer(h, ys5, w_in, w_out, consts, layer):
    b, l, d = h.shape
    tq = min(MIX_TQ, l)
    chunk = min(SSD_CHUNK, tq)
    nblk = l // tq
    total = b * nblk
    h2d = h.reshape(b * l, d)
    proj_buf = [
        pltpu.VMEM((tq, SSD_WIDTH), F32),
        pltpu.VMEM((tq + SUBLANES, SSD_XBC), F32),
        pltpu.VMEM((tq, LANES), F32),
        pltpu.VMEM((tq + SUBLANES, RG_WIDTH), F32),
        pltpu.VMEM((tq, RG_WIDTH), F32),
    ]
    rg_plane = pltpu.VMEM((RG_WIDTH // LANES, tq, LANES), F32)
    out = pl.pallas_call(
        functools.partial(_mixer_kernel, tq=tq, chunk=chunk, nblk=nblk),
        grid=(total + 2,),
        in_specs=[pl.BlockSpec((tq, d), lambda s: (jnp.minimum(s, total - 1), 0)),
                  pl.BlockSpec((tq, d), lambda s: (jnp.maximum(s - 2, 0), 0)),
                  pl.BlockSpec((tq, S5_WIDTH), lambda s: (jnp.clip(s - 1, 0, total - 1), 0))]
                 + [_layer_spec(c, layer) for c in (w_in, w_out) + tuple(consts)],
        out_specs=pl.BlockSpec((tq, d), lambda s: (jnp.maximum(s - 2, 0), 0)),
        out_shape=jax.ShapeDtypeStruct((b * l, d), F32),
        scratch_shapes=proj_buf + proj_buf + [
            pltpu.VMEM((tq, D_MODEL), BF16),
            pltpu.VMEM((tq, D_MODEL), BF16),
            pltpu.VMEM((SSD_GROUPS, SSD_STATE, SSD_GW), F32),
            rg_plane, rg_plane, rg_plane,
            pltpu.VMEM((SUBLANES, RG_WIDTH), F32),
            pltpu.VMEM((tq, d), BF16),
            pltpu.VMEM((d, SSD_WIDTH), BF16),
            pltpu.VMEM((d, SSD_XBC), BF16),
            pltpu.VMEM((d, LANES), BF16),
            pltpu.VMEM((d, RG_WIDTH), BF16),
            pltpu.VMEM((d, RG_WIDTH), BF16),
            pltpu.VMEM((d, d), BF16),
        ],
        compiler_params=_params(("arbitrary",)),
        name="mixer",
    )(h2d, h2d, ys5.reshape(b * l, S5_WIDTH), w_in, w_out, *consts)
    return out.reshape(b, l, d)


def _block_diag(w):
    nl, nb, bi, bo = w.shape
    return jnp.einsum('lhij,hk->lhikj', w, jnp.eye(nb, dtype=w.dtype)).reshape(nl, nb * bi, nb * bo)


def _kv_kernel(m_ref, wk_ref, wv_ref, k_ref, v_ref):
    m16 = m_ref[...].astype(BF16)
    k_ref[...] = jnp.dot(m16, wk_ref[...].astype(BF16), preferred_element_type=F32).astype(BF16)
    v_ref[...] = jnp.dot(m16, wv_ref[...].astype(BF16), preferred_element_type=F32).astype(BF16)


def _kv(mem, wk, wv, layer):
    b, m, d = mem.shape
    spec = pl.BlockSpec((None, m, d), lambda i: (i, 0, 0))
    return pl.pallas_call(
        _kv_kernel,
        grid=(b,),
        in_specs=[spec, _layer_spec(wk, layer), _layer_spec(wv, layer)],
        out_specs=[spec, spec],
        out_shape=[jax.ShapeDtypeStruct((b, m, d), BF16)] * 2,
        compiler_params=_params(("arbitrary",)),
        name="kv",
    )(mem, wk, wv)


def _xattn_kernel(h_ref, k_ref, v_ref, wq32_ref, wo32_ref, g_ref, b_ref, o_ref, wq_ref, wo_ref):
    @pl.when((pl.program_id(0) == 0) & (pl.program_id(1) == 0))
    def _():
        wq_ref[...] = wq32_ref[...].astype(BF16)
        wo_ref[...] = wo32_ref[...].astype(BF16)

    hb = h_ref[...]
    qall = jnp.dot(hb.astype(BF16), wq_ref[...], preferred_element_type=F32) * (1.0 / math.sqrt(XA_HEAD_DIM))
    outs = []
    for hd in range(XA_HEADS):
        sl = slice(hd * XA_HEAD_DIM, (hd + 1) * XA_HEAD_DIM)
        s = _dot_nt(qall[:, sl], k_ref[:, sl])
        p = jnp.exp(s - jnp.max(s, axis=-1, keepdims=True))
        o = jnp.dot(p.astype(BF16), v_ref[:, sl], preferred_element_type=F32)
        outs.append(o / jnp.sum(p, axis=-1, keepdims=True))
    o = jnp.concatenate(outs, axis=-1)
    y = jnp.dot(o.astype(BF16), wo_ref[...], preferred_element_type=F32)
    o_ref[...] = _layer_norm(ALPHA * hb + y, g_ref[...], b_ref[...])


def _xattn(h, k, v, wq, wo, g, bb, layer):
    b, l, d = h.shape
    tm = min(TOK_TM, l)
    tok = pl.BlockSpec((None, tm, d), lambda i, j: (i, j, 0))
    kvs = pl.BlockSpec((None, k.shape[1], d), lambda i, j: (i, 0, 0))
    return pl.pallas_call(
        _xattn_kernel,
        grid=(b, l // tm),
        in_specs=[tok, kvs, kvs] + [_layer_spec(c, layer) for c in (wq, wo, g, bb)],
        out_specs=tok,
        out_shape=jax.ShapeDtypeStruct((b, l, d), F32),
        scratch_shapes=[pltpu.VMEM((d, d), BF16), pltpu.VMEM((d, d), BF16)],
        compiler_params=_params(("arbitrary", "arbitrary")),
        name="xattn",
    )(h, k, v, wq, wo, g, bb)


def _mlp_kernel(h_ref, w1_ref, w2_ref, g_ref, b_ref, o_ref):
    hb = h_ref[...]
    h16 = hb.astype(BF16)
    acc = ALPHA * hb
    for f in range(D_FF // MLP_FC):
        sl = slice(f * MLP_FC, (f + 1) * MLP_FC)
        t = jnp.maximum(jnp.dot(h16, w1_ref[:, sl], preferred_element_type=F32), 0.0)
        acc = acc + jnp.dot((t * t).astype(BF16), w2_ref[sl, :], preferred_element_type=F32)
    o_ref[...] = _layer_norm(acc, g_ref[...], b_ref[...])


def _mlp(h2d, w1, w2, g, bb, layer):
    t, d = h2d.shape
    tm = min(TOK_TM, t)
    tok = pl.BlockSpec((tm, d), lambda i: (i, 0))
    return pl.pallas_call(
        _mlp_kernel,
        grid=(t // tm,),
        in_specs=[tok] + [_layer_spec(c, layer) for c in (w1, w2, g, bb)],
        out_specs=tok,
        out_shape=jax.ShapeDtypeStruct((t, d), F32),
        compiler_params=_params(("parallel",)),
        name="mlp",
    )(h2d, w1, w2, g, bb)


def _rows(v):
    return v.astype(F32).reshape(v.shape[0], 1, -1)


def _pad_lanes(v3d):
    return jnp.pad(v3d, ((0, 0), (0, 0), (0, LANES - v3d.shape[2])))


def prepare_params(seq_len, w_in, w_out, ssd_conv_w, ssd_conv_b, ssd_dt_bias, ssd_a_log, ssd_d, ssd_norm_w,
                   s5_lam_re, s5_lam_im, s5_log_step, s5_b_re, s5_b_im, s5_c_re, s5_c_im, s5_d, s5_glu_w, s5_glu_b,
                   rg_conv_w, rg_conv_b, rg_wa, rg_ba, rg_wx, rg_bx, rg_lambda, ln1_g, ln1_b,
                   xa_wq, xa_wk, xa_wv, xa_wo, ln2_g, ln2_b, mlp_w1, mlp_w2, ln3_g, ln3_b):
    o_u = SSD_WIDTH + SSD_XBC + SSD_HEADS
    n_tiles = min(S5_TB, seq_len) // (S5_BLK * SUBLANES)
    assert n_tiles <= 2 ** (SUBLANES - 3), "S5 scan-level table holds 3 in-tile + 5 tile levels"
    s5_ops = jax.vmap(functools.partial(_s5_prepare, n_tiles=n_tiles))(
        s5_lam_re, s5_lam_im, s5_log_step, s5_b_re, s5_b_im, s5_c_re, s5_c_im)
    s5 = (w_in[:, :, o_u:o_u + S5_WIDTH].astype(BF16), *s5_ops, _rows(s5_d), s5_glu_w.astype(BF16), _rows(s5_glu_b))
    mix_small = dict(
        cw=ssd_conv_w.astype(F32), cb=_rows(ssd_conv_b), dtb=_pad_lanes(_rows(ssd_dt_bias)),
        aneg=_pad_lanes(_rows(-jnp.exp(ssd_a_log.astype(F32)))),
        dvec=_rows(jnp.repeat(ssd_d.astype(F32), SSD_HEAD_DIM, axis=1)), nw=_rows(ssd_norm_w),
        rcw=rg_conv_w.astype(F32), rcb=_rows(rg_conv_b),
        wa=_block_diag(rg_wa.astype(F32)).astype(BF16), ba=_rows(rg_ba.reshape(rg_ba.shape[0], -1)),
        wx=_block_diag(rg_wx.astype(F32)).astype(BF16), bx=_rows(rg_bx.reshape(rg_bx.shape[0], -1)),
        rlam=_rows(rg_lambda), g=_rows(ln1_g), b=_rows(ln1_b))
    mix = (w_in.astype(F32), w_out.astype(F32), tuple(mix_small[f] for f in _MixW._fields if f not in _MIX_W16))
    xa = (xa_wk.astype(F32), xa_wv.astype(F32), xa_wq.astype(F32), xa_wo.astype(F32), _rows(ln2_g), _rows(ln2_b))
    mlp = (mlp_w1.astype(BF16), mlp_w2.astype(BF16), _rows(ln3_g), _rows(ln3_b))
    return dict(s5=s5, mix=mix, xa=xa, mlp=mlp)


def mixer_sublayer(h, prm, layer):
    ys5 = _s5_branch(h, prm["s5"], layer)
    w_in, w_out, consts = prm["mix"]
    return _mixer(h, ys5, w_in, w_out, consts, layer)


def xattn_sublayer(h, mem, prm, layer):
    wk, wv, wq, wo, g, bb = prm["xa"]
    k, v = _kv(mem, wk, wv, layer)
    return _xattn(h, k, v, wq, wo, g, bb, layer)


def mlp_sublayer(h, prm, layer):
    b, l, d = h.shape
    return _mlp(h.reshape(b * l, d), *prm["mlp"], layer).reshape(b, l, d)


def kernel(x, mem, w_in, w_out, ssd_conv_w, ssd_conv_b, ssd_dt_bias, ssd_a_log, ssd_d, ssd_norm_w, s5_lam_re, s5_lam_im, s5_log_step, s5_b_re, s5_b_im, s5_c_re, s5_c_im, s5_d, s5_glu_w, s5_glu_b, rg_conv_w, rg_conv_b, rg_wa, rg_ba, rg_wx, rg_bx, rg_lambda, ln1_g, ln1_b, xa_wq, xa_wk, xa_wv, xa_wo, ln2_g, ln2_b, mlp_w1, mlp_w2, ln3_g, ln3_b):
    h = x.astype(F32)
    memf = mem.astype(F32)
    prm = prepare_params(x.shape[1], w_in, w_out, ssd_conv_w, ssd_conv_b, ssd_dt_bias, ssd_a_log, ssd_d, ssd_norm_w,
                         s5_lam_re, s5_lam_im, s5_log_step, s5_b_re, s5_b_im, s5_c_re, s5_c_im, s5_d, s5_glu_w,
                         s5_glu_b, rg_conv_w, rg_conv_b, rg_wa, rg_ba, rg_wx, rg_bx, rg_lambda, ln1_g, ln1_b,
                         xa_wq, xa_wk, xa_wv, xa_wo, ln2_g, ln2_b, mlp_w1, mlp_w2, ln3_g, ln3_b)
    for layer in range(DEPTH):
        h = mixer_sublayer(h, prm, layer)
        h = xattn_sublayer(h, memf, prm, layer)
        h = mlp_sublayer(h, prm, layer)
    return h.astype(x.dtype)
```

```python
import collections
import functools
import math

import jax
import jax.numpy as jnp
from jax import lax
from jax.experimental import pallas as pl
from jax.experimental.pallas import tpu as pltpu

F32 = jnp.float32
BF16 = jnp.bfloat16

D_MODEL = 1024
DEPTH = 2
SSD_WIDTH = 512
SSD_HEAD_DIM = 64
SSD_HEADS = 8
SSD_GROUPS = 2
SSD_HPG = SSD_HEADS // SSD_GROUPS
SSD_GW = SSD_HPG * SSD_HEAD_DIM
SSD_STATE = 128
SSD_XBC = 1024
CONV_K = 4
S5_WIDTH = 256
S5_GROUPS = 16
S5_GROUP_CH = 16
S5_STATE = 64
S5_NSTATE = S5_GROUPS * S5_STATE
S5_BLK = 8
RG_WIDTH = 256
RG_BLOCKS = 4
RG_C = 8.0
XA_HEADS = 4
XA_HEAD_DIM = 256
D_FF = 4096
ALPHA = (2.0 * DEPTH) ** 0.25
LN_EPS = 1e-5

LANES = 128
SUBLANES = 8
MXU_TILE = 256
NEG_BIG = -1e30
VMEM_LIMIT = 56 * 1024 * 1024

SSD_CHUNK = 128
MIX_TQ = 256
S5_TB = 2048
TOK_TM = 512
MLP_FC = 1024


def _dot(a, b):
    return jnp.dot(a.astype(BF16), b.astype(BF16), preferred_element_type=F32)


def _dot_nt(a, b):
    return lax.dot_general(a.astype(BF16), b.astype(BF16), (((1,), (1,)), ((), ())),
                           preferred_element_type=F32)


def _layer_norm(x, g, b):
    mu = jnp.mean(x, axis=-1, keepdims=True)
    xc = x - mu
    var = jnp.mean(xc * xc, axis=-1, keepdims=True)
    return xc * lax.rsqrt(var + LN_EPS) * g + b


def _gelu(x):
    c = math.sqrt(2.0 / math.pi)
    return 0.5 * x * (1.0 + jnp.tanh(c * (x + 0.044715 * (x * x * x))))


def _sigmoid(x):
    return 0.5 + 0.5 * jnp.tanh(0.5 * x)


def _silu(x):
    hx = 0.5 * x
    return hx + hx * jnp.tanh(hx)


def _softplus(x):
    return jnp.maximum(x, 0.0) + jnp.log1p(jnp.exp(-jnp.abs(x)))


def _const_spec(shape):
    nd = len(shape)
    return pl.BlockSpec(shape, lambda *_: (0,) * nd, pipeline_mode=pl.Buffered(1))


def _layer_spec(stacked, layer):
    nd = stacked.ndim - 1
    return pl.BlockSpec((None,) + stacked.shape[1:], lambda *_: (layer,) + (0,) * nd, pipeline_mode=pl.Buffered(1))


def _params(sem):
    return pltpu.CompilerParams(dimension_semantics=sem, vmem_limit_bytes=VMEM_LIMIT)


def _re_lanes(p):
    return slice(2 * p * LANES, (2 * p + 1) * LANES)


def _im_lanes(p):
    return slice((2 * p + 1) * LANES, (2 * p + 2) * LANES)


def _s5_state_group(idx):
    state = (idx // (2 * LANES)) * LANES + idx % LANES
    return state // S5_STATE


def _s5_split(x):
    planes = S5_NSTATE // LANES
    return (jnp.concatenate([x[:, _re_lanes(p)] for p in range(planes)], axis=-1),
            jnp.concatenate([x[:, _im_lanes(p)] for p in range(planes)], axis=-1))


S5_HALF_CH = S5_WIDTH // 2
S5_HALF_LANES = S5_BLK * S5_HALF_CH


def _s5_expand_operators(wc_ref, vc_ref, ct_ref, ms_scr, toep_scr, mo_scr):
    n, w, c = S5_NSTATE, S5_WIDTH, S5_GROUP_CH
    hw, hl = S5_HALF_CH, S5_HALF_LANES
    grp_half = S5_GROUPS // 2
    reps = hw // c
    grp_r = _s5_state_group(lax.broadcasted_iota(jnp.int32, (2 * n, w), 0))
    same_o = grp_r == lax.broadcasted_iota(jnp.int32, (2 * n, w), 1) // c
    mo0 = jnp.where(same_o, ct_ref[...], 0.0)
    kc = jnp.dot(wc_ref[...], mo0, preferred_element_type=F32, precision=lax.Precision.HIGHEST)
    toep_scr[...] = jnp.zeros(toep_scr.shape, toep_scr.dtype)
    ch_grp = lax.broadcasted_iota(jnp.int32, (hw, hl), 0) // c
    same_k = (lax.broadcasted_iota(jnp.int32, (hw, hw), 0) // c) == (lax.broadcasted_iota(jnp.int32, (hw, hw), 1) // c)
    pick_r = lax.broadcasted_iota(jnp.int32, (S5_BLK * c, hl), 0)
    pick_c = lax.broadcasted_iota(jnp.int32, (S5_BLK * c, hl), 1)
    pick = ((pick_r // c == pick_c // hw) & (pick_r % c == pick_c % c)).astype(BF16)
    for hf in range(2):
        lanes = slice(hf * hl, (hf + 1) * hl)
        st_grp = _s5_state_group(lax.broadcasted_iota(jnp.int32, (hw, hl), 1) + hf * hl) - hf * grp_half
        same_s = ch_grp == st_grp
        for t in range(S5_BLK):
            blk = jnp.where(same_s, jnp.tile(wc_ref[t * c:(t + 1) * c, lanes], (reps, 1)), 0.0)
            ms_scr[hf, t * hw:(t + 1) * hw, :] = blk.astype(BF16)
        st_grp = _s5_state_group(lax.broadcasted_iota(jnp.int32, (hl, hl), 0) + hf * hl) - hf * grp_half
        same_r = st_grp == (lax.broadcasted_iota(jnp.int32, (hl, hl), 1) % hw) // c
        spread = jnp.dot(vc_ref[lanes, :].astype(BF16), pick, preferred_element_type=F32)
        mo_scr[hf] = jnp.where(same_r, spread, 0.0).astype(BF16)
        for t_in in range(S5_BLK):
            for t_out in range(t_in, S5_BLK):
                k = S5_BLK - 1 - (t_out - t_in)
                blk = jnp.where(same_k, jnp.tile(kc[k * c:(k + 1) * c, hf * hw:(hf + 1) * hw], (reps, 1)), 0.0)
                toep_scr[hf, t_in * hw:(t_in + 1) * hw, t_out * hw:(t_out + 1) * hw] = blk.astype(BF16)


_S5W = collections.namedtuple("_S5W", "wu wc vc ct lvl pw pwt d gluw glub")
_S5Ops = collections.namedtuple("_S5Ops", "ms toep mo")
_S5Buf = collections.namedtuple("_S5Buf", "u32 u16 hprev16")
_S5Scratch = collections.namedtuple("_S5Scratch", "u_tok hplane cexp carry yg")


def _s5_scan_stage(h_ref, w, ops, scr, out, first, rows, side):
    n, wd = S5_NSTATE, S5_WIDTH
    lane_halves = wd // LANES
    planes = n // LANES
    nt = rows // SUBLANES
    tb = rows * S5_BLK
    for r0 in range(0, tb, tb // 4):
        rs = slice(r0, r0 + tb // 4)
        u_tok = jnp.dot(h_ref[rs, :].astype(BF16), w.wu[...], preferred_element_type=F32)
        for hf in range(lane_halves):
            scr.u_tok[hf, rs, :] = u_tok[:, hf * LANES:(hf + 1) * LANES]
    u = jnp.concatenate([scr.u_tok[hf, pl.ds(k, rows, stride=S5_BLK), :]
                         for hf in range(lane_halves) for k in range(S5_BLK)], axis=-1)
    out.u32[...] = u
    u16 = u.astype(BF16)
    out.u16[...] = u16
    row8 = lax.broadcasted_iota(jnp.int32, (nt, SUBLANES, LANES), 1)
    planes_per_half = planes // lane_halves
    for p in range(planes):
        hf, pl_in_half = divmod(p, planes_per_half)
        s = jnp.dot(u16[:, hf * S5_HALF_LANES:(hf + 1) * S5_HALF_LANES],
                    ops.ms[hf, :, 2 * pl_in_half * LANES:(2 * pl_in_half + 2) * LANES],
                    preferred_element_type=F32)
        if p < planes - 2:
            side(1)
        xr = s[:, :LANES].reshape(nt, SUBLANES, LANES)
        xi = s[:, LANES:].reshape(nt, SUBLANES, LANES)
        for lv, dist in enumerate((1, 2, 4)):
            lr = w.lvl[lv:lv + 1, _re_lanes(p)]
            li = w.lvl[lv:lv + 1, _im_lanes(p)]
            keep = row8 >= dist
            sr = jnp.where(keep, pltpu.roll(xr, dist, 1), 0.0)
            si = jnp.where(keep, pltpu.roll(xi, dist, 1), 0.0)
            xr, xi = xr + lr * sr - li * si, xi + lr * si + li * sr
        scr.hplane[2 * p, SUBLANES:SUBLANES + rows, :] = xr.reshape(rows, LANES)
        scr.hplane[2 * p + 1, SUBLANES:SUBLANES + rows, :] = xi.reshape(rows, LANES)
    side(1)
    last = 2 * SUBLANES - 1
    er = jnp.concatenate([scr.hplane[2 * p, pl.ds(last, nt, stride=SUBLANES), :] for p in range(planes)], axis=-1)
    ei = jnp.concatenate([scr.hplane[2 * p + 1, pl.ds(last, nt, stride=SUBLANES), :] for p in range(planes)], -1)
    tile_id = lax.broadcasted_iota(jnp.int32, (nt, n), 0)
    lv, dist = 3, 1
    while dist < nt:
        lr, li = _s5_split(w.lvl[lv:lv + 1, :])
        keep = tile_id >= dist
        sr = jnp.where(keep, pltpu.roll(er, dist, 0), 0.0)
        si = jnp.where(keep, pltpu.roll(ei, dist, 0), 0.0)
        er, ei = er + lr * sr - li * si, ei + lr * si + li * sr
        lv, dist = lv + 1, dist * 2
    cr, ci = _s5_split(jnp.where(first, 0.0, scr.carry[0:1, :]))
    ptr, pti = _s5_split(w.pwt[...])
    er, ei = er + ptr * cr - pti * ci, ei + ptr * ci + pti * cr
    for p in range(planes):
        scr.carry[0:1, _re_lanes(p)] = er[nt - 1:nt, p * LANES:(p + 1) * LANES]
        scr.carry[0:1, _im_lanes(p)] = ei[nt - 1:nt, p * LANES:(p + 1) * LANES]
    tin_r = jnp.where(tile_id >= 1, pltpu.roll(er, 1, 0), cr)
    tin_i = jnp.where(tile_id >= 1, pltpu.roll(ei, 1, 0), ci)
    side(1)
    for p in range(planes):
        for k in range(SUBLANES):
            scr.cexp[2 * p, pl.ds(k, nt, stride=SUBLANES), :] = tin_r[:, p * LANES:(p + 1) * LANES]
            scr.cexp[2 * p + 1, pl.ds(k, nt, stride=SUBLANES), :] = tin_i[:, p * LANES:(p + 1) * LANES]
    for p in range(planes):
        if p % 2 == 0:
            side(1)
        pr = jnp.tile(w.pw[:, _re_lanes(p)], (nt, 1))
        pi = jnp.tile(w.pw[:, _im_lanes(p)], (nt, 1))
        tr = scr.cexp[2 * p]
        ti = scr.cexp[2 * p + 1]
        hr = scr.hplane[2 * p, SUBLANES:SUBLANES + rows, :] + pr * tr - pi * ti
        hi = scr.hplane[2 * p + 1, SUBLANES:SUBLANES + rows, :] + pr * ti + pi * tr
        scr.hplane[2 * p, SUBLANES:SUBLANES + rows, :] = hr
        scr.hplane[2 * p + 1, SUBLANES:SUBLANES + rows, :] = hi
        scr.hplane[2 * p, SUBLANES - 1:SUBLANES, :] = cr[:, p * LANES:(p + 1) * LANES]
        scr.hplane[2 * p + 1, SUBLANES - 1:SUBLANES, :] = ci[:, p * LANES:(p + 1) * LANES]
    for p in range(2 * planes):
        out.hprev16[:, p * LANES:(p + 1) * LANES] = scr.hplane[p, SUBLANES - 1:SUBLANES - 1 + rows, :].astype(BF16)


def _s5_readout_jobs(src, w, ops, scr, o_ref, rows):
    hw, hl = S5_HALF_CH, S5_HALF_LANES
    lane_halves = S5_WIDTH // LANES
    steps_per_tile = MXU_TILE // hw

    def tile_job(hf, j):
        def run():
            k_rows = (j + 1) * MXU_TILE
            cols = slice(j * MXU_TILE, (j + 1) * MXU_TILE)
            y = jnp.dot(src.u16[:, hf * hl:hf * hl + k_rows], ops.toep[hf, :k_rows, cols],
                        preferred_element_type=F32)
            y = y + jnp.dot(src.hprev16[:, hf * hl:(hf + 1) * hl], ops.mo[hf, :, cols], preferred_element_type=F32)
            d_half = w.d[:, hf * hw:(hf + 1) * hw]
            y = _gelu(y + jnp.concatenate([d_half] * steps_per_tile, axis=1) * src.u32[:, hf * hl + j * MXU_TILE:
                                                                                           hf * hl + (j + 1) * MXU_TILE])
            for i in range(steps_per_tile):
                scr.yg[hf, pl.ds(j * steps_per_tile + i, rows, stride=S5_BLK), :] = y[:, i * hw:(i + 1) * hw]
        return run

    def glu_job():
        yg = jnp.concatenate([scr.yg[hf] for hf in range(lane_halves)], axis=-1)
        glu = _sigmoid(_dot(yg, w.gluw[...]) + w.glub[...])
        o_ref[...] = (yg * glu).astype(o_ref.dtype)

    return [tile_job(hf, j) for hf in range(lane_halves) for j in range(hl // MXU_TILE)] + [glu_job]


def _s5_kernel(h_ref, *rest, rows, nblk):
    nw = len(_S5W._fields)
    w = _S5W(*rest[:nw])
    o_ref = rest[nw]
    bufs = rest[nw + 1:]
    ops = _S5Ops(*bufs[:3])
    nb = len(_S5Buf._fields)
    scr = _S5Scratch(*bufs[3 + nb:])
    s = pl.program_id(0)
    parity = s % 2

    @pl.when(s == 0)
    def _():
        _s5_expand_operators(w.wc, w.vc, w.ct, *ops)
        for ref in bufs[3:]:
            ref[...] = jnp.zeros(ref.shape, ref.dtype)

    first = (s % nblk) == 0

    def step(out, src):
        jobs = _s5_readout_jobs(src, w, ops, scr, o_ref, rows)

        def side(k):
            for _ in range(min(k, len(jobs))):
                jobs.pop(0)()

        _s5_scan_stage(h_ref, w, ops, scr, out, first, rows, side)
        side(len(jobs))

    step(_S5Buf(*(r.at[parity] for r in bufs[3:3 + nb])), _S5Buf(*(r.at[1 - parity] for r in bufs[3:3 + nb])))


def _s5_branch(h, consts, layer):
    b, l, d = h.shape
    tb = min(S5_TB, l)
    rows = tb // S5_BLK
    nblk = l // tb
    total = b * nblk
    n, w = S5_NSTATE, S5_WIDTH
    handoff = [
        pltpu.VMEM((2, rows, S5_BLK * w), F32),
        pltpu.VMEM((2, rows, S5_BLK * w), BF16),
        pltpu.VMEM((2, rows, 2 * n), BF16),
    ]
    out = pl.pallas_call(
        functools.partial(_s5_kernel, rows=rows, nblk=nblk),
        grid=(total + 1,),
        in_specs=[pl.BlockSpec((tb, d), lambda s: (jnp.minimum(s, total - 1), 0))]
                 + [_layer_spec(c, layer) for c in consts],
        out_specs=pl.BlockSpec((tb, w), lambda s: (jnp.maximum(s - 1, 0), 0)),
        out_shape=jax.ShapeDtypeStruct((b * l, w), BF16),
        scratch_shapes=[
            pltpu.VMEM((2, S5_HALF_LANES, S5_HALF_LANES), BF16),
            pltpu.VMEM((2, S5_HALF_LANES, S5_HALF_LANES), BF16),
            pltpu.VMEM((2, S5_HALF_LANES, S5_HALF_LANES), BF16),
        ] + handoff + [
            pltpu.VMEM((w // LANES, tb, LANES), F32),
            pltpu.VMEM((2 * n // LANES, rows + SUBLANES, LANES), F32),
            pltpu.VMEM((2 * n // LANES, rows, LANES), F32),
            pltpu.VMEM((SUBLANES, 2 * n), F32),
            pltpu.VMEM((w // LANES, tb, LANES), F32),
        ],
        compiler_params=_params(("arbitrary",)),
        name="s5_branch",
    )(h.reshape(b * l, d), *consts)
    return out.reshape(b, l, w)


def _s5_prepare(lam_re, lam_im, log_step, b_re, b_im, c_re, c_im, n_tiles):
    g, p, c, t8 = S5_GROUPS, S5_STATE, S5_GROUP_CH, S5_BLK
    lr, li = lam_re.astype(F32), lam_im.astype(F32)
    dt = jnp.exp(log_step.astype(F32))[:, None]
    ar, ai = lr * dt, li * dt

    def lam_pow(k):
        k = k.astype(F32)[..., None, None]
        mag = jnp.exp(k * ar)
        return mag * jnp.cos(k * ai), mag * jnp.sin(k * ai)

    br1, bi1 = lam_pow(jnp.ones((), jnp.int32))
    xr_, xi_ = br1 - 1.0, bi1
    den = lr * lr + li * li
    cr, ci = (xr_ * lr + xi_ * li) / den, (xi_ * lr - xr_ * li) / den
    bre, bim = b_re.astype(F32), b_im.astype(F32)
    bbr = cr[..., None] * bre - ci[..., None] * bim
    bbi = cr[..., None] * bim + ci[..., None] * bre
    steps = jnp.arange(t8)
    pr, pi = lam_pow(t8 - 1 - steps)
    wsr = pr[..., None] * bbr[None] - pi[..., None] * bbi[None]
    wsi = pr[..., None] * bbi[None] + pi[..., None] * bbr[None]

    def interleave(re, im, axis):
        shape = list(re.shape)
        shape[axis:axis + 1] = [re.shape[axis] // LANES, 1, LANES]
        both = jnp.concatenate([re.reshape(shape), im.reshape(shape)], axis=axis + 1)
        shape[axis:axis + 3] = [2 * re.shape[axis]]
        return both.reshape(shape)

    wc = interleave(*[part.transpose(0, 3, 1, 2).reshape(t8 * c, g * p) for part in (wsr, wsi)], axis=1)
    cre, cim = c_re.astype(F32), c_im.astype(F32)
    pr, pi = lam_pow(steps + 1)
    vor = cre[None] * pr[:, :, None, :] - cim[None] * pi[:, :, None, :]
    voi = cre[None] * pi[:, :, None, :] + cim[None] * pr[:, :, None, :]
    vc = interleave(*[part.transpose(1, 3, 0, 2).reshape(g * p, t8 * c) for part in (vor, -voi)], axis=0)
    ct = interleave(*[jnp.tile(part.transpose(0, 2, 1).reshape(g * p, c), (1, g)) for part in (cre, -cim)], axis=0)

    def re_im(pair):
        return interleave(pair[0].reshape(-1, g * p), pair[1].reshape(-1, g * p), axis=1)

    lvl = re_im(lam_pow(t8 * 2 ** jnp.arange(SUBLANES)))
    pw = re_im(lam_pow(t8 * (steps + 1)))
    pwt = re_im(lam_pow(t8 * SUBLANES * (jnp.arange(n_tiles) + 1)))
    return wc, vc, ct, lvl, pw, pwt


def _expand_heads(cols, grp, head_lane):
    base = SSD_HPG * grp
    out = jnp.broadcast_to(cols[:, base:base + 1], (cols.shape[0], SSD_GW))
    for hh in range(1, SSD_HPG):
        out = jnp.where(head_lane == hh, cols[:, base + hh:base + hh + 1], out)
    return out


def _causal_conv(pad_ref, w_ref, b_ref, rows, cols):
    acc = b_ref[:, cols] + w_ref[CONV_K - 1:CONV_K, cols] * pad_ref[SUBLANES:SUBLANES + rows, cols]
    for k in range(CONV_K - 1):
        off = SUBLANES - (CONV_K - 1) + k
        acc = acc + w_ref[k:k + 1, cols] * pad_ref[off:off + rows, cols]
    return acc


_ProjBuf = collections.namedtuple("_ProjBuf", "z xpad dt rpad gr")
_MixW = collections.namedtuple(
    "_MixW", "wz wxbc wdt wxr wgr cw cb dtb aneg dvec nw rcw rcb wa ba wx bx rlam wout g b")
_MixScratch = collections.namedtuple("_MixScratch", "state ascr bscr cexp rcar")


def _col_tiles(width):
    return [slice(c, min(c + MXU_TILE, width)) for c in range(0, width, MXU_TILE)]


def _mixer_output_jobs(h_ref, ycat, w, o_ref):
    def tile_job(cols):
        def run():
            y = jnp.dot(ycat[...], w.wout[:, cols], preferred_element_type=F32)
            o_ref[:, cols] = ALPHA * h_ref[:, cols] + y
        return run

    def norm_job():
        o_ref[...] = _layer_norm(o_ref[...], w.g[...], w.b[...])

    return [tile_job(cols) for cols in _col_tiles(D_MODEL)] + [norm_job]


def _mixer_project_jobs(h_ref, h16, h_keep, w, nxt, cur, first, tq):
    def cast_job():
        hb = h_ref[...]
        h_keep[...] = hb
        h16[...] = hb.astype(BF16)

    def tile_job(dst, w_ref, cols, row0):
        def run():
            dst[row0:row0 + tq, cols] = jnp.dot(h16[...], w_ref[:, cols], preferred_element_type=F32)
        return run

    def tail_job(pad_n, pad_c):
        def run():
            pad_n[0:SUBLANES, :] = jnp.where(first, 0.0, pad_c[tq:tq + SUBLANES, :])
        return run

    jobs = [cast_job]
    for dst, w_ref, row0 in ((nxt.z, w.wz, 0), (nxt.xpad, w.wxbc, SUBLANES), (nxt.dt, w.wdt, 0),
                             (nxt.rpad, w.wxr, SUBLANES), (nxt.gr, w.wgr, 0)):
        jobs += [tile_job(dst, w_ref, cols, row0) for cols in _col_tiles(dst.shape[1])]
    jobs += [tail_job(nxt.xpad, cur.xpad), tail_job(nxt.rpad, cur.rpad)]
    return jobs


def _mixer_compute(cur, ys5_ref, w, scr, ycat, first, tq, chunk, side):
    cw_ref, cb_ref, dtb_ref, aneg_ref, dvec_ref, nw_ref = w.cw, w.cb, w.dtb, w.aneg, w.dvec, w.nw
    state = scr.state
    z = cur.z
    slabs = []
    for cols in _col_tiles(SSD_XBC):
        side(1)
        slabs.append(_silu(_causal_conv(cur.xpad, cw_ref, cb_ref, tq, cols)))
    xact = jnp.concatenate(slabs, axis=-1)
    side(2)
    dt_all = _softplus(cur.dt[...] + dtb_ref[...])
    q = chunk
    tri = (lax.broadcasted_iota(jnp.int32, (q, q), 0) >= lax.broadcasted_iota(jnp.int32, (q, q), 1))
    row_id = lax.broadcasted_iota(jnp.int32, (q, LANES), 0)
    head_lane = lax.broadcasted_iota(jnp.int32, (1, SSD_GW), 1) // SSD_HEAD_DIM
    for ci in range(tq // q):
        r0 = ci * q
        dt = dt_all[r0:r0 + q, :]
        cs = dt * aneg_ref[...]
        dist = 1
        while dist < q:
            cs = cs + jnp.where(row_id >= dist, pltpu.roll(cs, dist, 0), 0.0)
            dist *= 2
        cs_t = cs.T
        cs_last = cs[q - 1:q, :]
        chunk_decay = jnp.exp(cs_last)
        dec_t = jnp.exp(cs_t[:, q - 1:q] - cs_t)
        ecs = jnp.exp(cs)
        zc = z[r0:r0 + q, :]
        halves = []
        for grp in range(SSD_GROUPS):
            side(2)
            lo = grp * SSD_GW
            xs = xact[r0:r0 + q, lo:lo + SSD_GW]
            bm = xact[r0:r0 + q, SSD_WIDTH + grp * SSD_STATE:SSD_WIDTH + (grp + 1) * SSD_STATE]
            cm = xact[r0:r0 + q, SSD_WIDTH + (SSD_GROUPS + grp) * SSD_STATE:
                      SSD_WIDTH + (SSD_GROUPS + grp + 1) * SSD_STATE]
            cm16 = cm.astype(BF16)
            xdt16 = (xs * _expand_heads(dt, grp, head_lane)).astype(BF16)
            cb = _dot_nt(cm16, bm)
            bm_t = bm.T
            s_prev = state[grp]
            if ci == 0:
                s_prev = jnp.where(first, 0.0, s_prev)
            y_off = jnp.dot(cm16, s_prev.astype(BF16), preferred_element_type=F32)
            intra, to_state, xdt_heads = [], [], []
            for hh in range(SSD_HPG):
                hd = grp * SSD_HPG + hh
                seg = cs[:, hd:hd + 1] - cs_t[hd:hd + 1, :]
                lmat = jnp.exp(jnp.where(tri, seg, NEG_BIG))
                intra.append((cb * lmat).astype(BF16))
                to_state.append((bm_t * dec_t[hd:hd + 1, :]).astype(BF16))
                xdt_heads.append(jnp.where(head_lane == hh, xdt16, jnp.zeros_like(xdt16)))
            lhs = jnp.concatenate([jnp.concatenate(intra, axis=1), jnp.concatenate(to_state, axis=1)], axis=0)
            both = jnp.dot(lhs, jnp.concatenate(xdt_heads, axis=0), preferred_element_type=F32)
            y_diag = both[:q]
            st_new = both[q:]
            state[grp] = s_prev * _expand_heads(chunk_decay, grp, head_lane) + st_new
            halves.append(y_diag + y_off * _expand_heads(ecs, grp, head_lane) + xs * dvec_ref[:, lo:lo + SSD_GW])
        y = jnp.concatenate(halves, axis=-1) * _silu(zc)
        y = y * lax.rsqrt(jnp.mean(y * y, axis=-1, keepdims=True) + LN_EPS) * nw_ref[...]
        ycat[r0:r0 + q, 0:SSD_WIDTH] = y.astype(BF16)

    ycat[:, SSD_WIDTH:SSD_WIDTH + S5_WIDTH] = ys5_ref[...]
    _mixer_rglru(cur, w, scr, ycat, first, tq, side)


def _mixer_rglru(cur, w, scr, ycat, first, tq, side):
    side(1)
    xc = _causal_conv(cur.rpad, w.rcw, w.rcb, tq, slice(0, RG_WIDTH))
    xc16 = xc.astype(BF16)
    rgate = _sigmoid(jnp.dot(xc16, w.wa[...], preferred_element_type=F32) + w.ba[...])
    igate = _sigmoid(jnp.dot(xc16, w.wx[...], preferred_element_type=F32) + w.bx[...])
    log_a = (-RG_C) * rgate * _softplus(-w.rlam[...])
    a = jnp.exp(log_a)
    bt = jnp.sqrt(1.0 - jnp.exp(2.0 * log_a)) * (igate * xc)
    row8 = lax.broadcasted_iota(jnp.int32, (tq, RG_WIDTH), 0) & (SUBLANES - 1)
    side(1)
    for dist in (1, 2, 4):
        keep = row8 >= dist
        a_sh = jnp.where(keep, pltpu.roll(a, dist, 0), 1.0)
        b_sh = jnp.where(keep, pltpu.roll(bt, dist, 0), 0.0)
        bt = bt + a * b_sh
        a = a * a_sh
    nt = tq // SUBLANES
    lane_halves = RG_WIDTH // LANES
    for hf in range(lane_halves):
        scr.ascr[hf] = a[:, hf * LANES:(hf + 1) * LANES]
        scr.bscr[hf] = bt[:, hf * LANES:(hf + 1) * LANES]
    ae = jnp.concatenate([scr.ascr[hf, pl.ds(SUBLANES - 1, nt, stride=SUBLANES), :] for hf in range(lane_halves)], -1)
    be = jnp.concatenate([scr.bscr[hf, pl.ds(SUBLANES - 1, nt, stride=SUBLANES), :] for hf in range(lane_halves)], -1)
    tile_id = lax.broadcasted_iota(jnp.int32, (nt, RG_WIDTH), 0)
    dist = 1
    while dist < nt:
        keep = tile_id >= dist
        a_sh = jnp.where(keep, pltpu.roll(ae, dist, 0), 1.0)
        b_sh = jnp.where(keep, pltpu.roll(be, dist, 0), 0.0)
        be = be + ae * b_sh
        ae = ae * a_sh
        dist *= 2
    carry_in = jnp.where(first, 0.0, scr.rcar[0:1, :])
    tile_end = be + ae * carry_in
    scr.rcar[0:1, :] = tile_end[nt - 1:nt, :]
    tile_in = jnp.where(tile_id >= 1, pltpu.roll(tile_end, 1, 0), carry_in)
    for hf in range(lane_halves):
        for k in range(SUBLANES):
            scr.cexp[hf, pl.ds(k, nt, stride=SUBLANES), :] = tile_in[:, hf * LANES:(hf + 1) * LANES]
    hscan = bt + a * jnp.concatenate([scr.cexp[hf] for hf in range(lane_halves)], axis=-1)
    ycat[:, SSD_WIDTH + S5_WIDTH:] = (hscan * _gelu(cur.gr[...])).astype(BF16)


_MIX_W16 = ("wz", "wxbc", "wdt", "wxr", "wgr", "wout")


def _mixer_load_weights(w_in_ref, w_out_ref, w):
    o_dt = SSD_WIDTH + SSD_XBC
    o_xr = o_dt + SSD_HEADS + S5_WIDTH
    o_gr = o_xr + RG_WIDTH
    w.wz[...] = w_in_ref[:, 0:SSD_WIDTH].astype(BF16)
    w.wxbc[...] = w_in_ref[:, SSD_WIDTH:o_dt].astype(BF16)
    head_cols = lax.broadcasted_iota(jnp.int32, (1, LANES), 1) < SSD_HEADS
    w.wdt[...] = jnp.where(head_cols, w_in_ref[:, o_dt:o_dt + LANES], 0.0).astype(BF16)
    w.wxr[...] = w_in_ref[:, o_xr:o_gr].astype(BF16)
    w.wgr[...] = w_in_ref[:, o_gr:o_gr + RG_WIDTH].astype(BF16)
    w.wout[...] = w_out_ref[...].astype(BF16)


def _mixer_kernel(hp_ref, ys5_ref, w_in_ref, w_out_ref, *rest, tq, chunk, nblk):
    small = [f for f in _MixW._fields if f not in _MIX_W16]
    o_ref = rest[len(small)]
    bufs = rest[len(small) + 1:]
    w16 = bufs[len(bufs) - len(_MIX_W16):]
    bufs = bufs[:len(bufs) - len(_MIX_W16)]
    w = _MixW(**dict(zip(small, rest[:len(small)])), **dict(zip(_MIX_W16, w16)))
    npb = len(_ProjBuf._fields)
    sets = (_ProjBuf(*bufs[:npb]), _ProjBuf(*bufs[npb:2 * npb]))
    ycats = bufs[2 * npb:2 * npb + 2]
    scr = _MixScratch(*bufs[2 * npb + 2:2 * npb + 2 + len(_MixScratch._fields)])
    h16 = bufs[-1]
    s = pl.program_id(0)
    h_keep = bufs[-2]
    h_in = h_keep.at[s % 3]
    h_out = h_keep.at[(s + 1) % 3]

    @pl.when(s == 0)
    def _():
        _mixer_load_weights(w_in_ref, w_out_ref, w)
        for ref in bufs:
            ref[...] = jnp.zeros(ref.shape, ref.dtype)

    first_p = (s % nblk) == 0
    first_c = ((s + nblk - 1) % nblk) == 0

    def step(nxt, cur, ycat_c, ycat_o):
        jobs = (_mixer_output_jobs(h_out, ycat_o, w, o_ref)
                + _mixer_project_jobs(hp_ref, h16, h_in, w, nxt, cur, first_p, tq))

        def side(n):
            for _ in range(min(n, len(jobs))):
                jobs.pop(0)()

        _mixer_compute(cur, ys5_ref, w, scr, ycat_c, first_c, tq, chunk, side)
        side(len(jobs))

    @pl.when(s % 2 == 0)
    def _():
        step(sets[0], sets[1], ycats[1], ycats[0])

    @pl.when(s % 2 == 1)
    def _():
        step(sets[1], sets[0], ycats[0], ycats[1])


def _mixer(h, ys5, w_in, w_out, consts, layer):
    b, l, d = h.shape
    tq = min(MIX_TQ, l)
    chunk = min(SSD_CHUNK, tq)
    nblk = l // tq
    total = b * nblk
    h2d = h.reshape(b * l, d)
    proj_buf = [
        pltpu.VMEM((tq, SSD_WIDTH), F32),
        pltpu.VMEM((tq + SUBLANES, SSD_XBC), F32),
        pltpu.VMEM((tq, LANES), F32),
        pltpu.VMEM((tq + SUBLANES, RG_WIDTH), F32),
        pltpu.VMEM((tq, RG_WIDTH), F32),
    ]
    rg_plane = pltpu.VMEM((RG_WIDTH // LANES, tq, LANES), F32)
    out = pl.pallas_call(
        functools.partial(_mixer_kernel, tq=tq, chunk=chunk, nblk=nblk),
        grid=(total + 2,),
        in_specs=[pl.BlockSpec((tq, d), lambda s: (jnp.minimum(s, total - 1), 0)),
                  pl.BlockSpec((tq, S5_WIDTH), lambda s: (jnp.clip(s - 1, 0, total - 1), 0))]
                 + [_layer_spec(c, layer) for c in (w_in, w_out) + tuple(consts)],
        out_specs=pl.BlockSpec((tq, d), lambda s: (jnp.maximum(s - 2, 0), 0)),
        out_shape=jax.ShapeDtypeStruct((b * l, d), F32),
        scratch_shapes=proj_buf + proj_buf + [
            pltpu.VMEM((tq, D_MODEL), BF16),
            pltpu.VMEM((tq, D_MODEL), BF16),
            pltpu.VMEM((SSD_GROUPS, SSD_STATE, SSD_GW), F32),
            rg_plane, rg_plane, rg_plane,
            pltpu.VMEM((SUBLANES, RG_WIDTH), F32),
            pltpu.VMEM((3, tq, d), F32),
            pltpu.VMEM((tq, d), BF16),
            pltpu.VMEM((d, SSD_WIDTH), BF16),
            pltpu.VMEM((d, SSD_XBC), BF16),
            pltpu.VMEM((d, LANES), BF16),
            pltpu.VMEM((d, RG_WIDTH), BF16),
            pltpu.VMEM((d, RG_WIDTH), BF16),
            pltpu.VMEM((d, d), BF16),
        ],
        compiler_params=_params(("arbitrary",)),
        name="mixer",
    )(h2d, ys5.reshape(b * l, S5_WIDTH), w_in, w_out, *consts)
    return out.reshape(b, l, d)


def _block_diag(w):
    nl, nb, bi, bo = w.shape
    return jnp.einsum('lhij,hk->lhikj', w, jnp.eye(nb, dtype=w.dtype)).reshape(nl, nb * bi, nb * bo)


def _kv_kernel(m_ref, wk_ref, wv_ref, k_ref, v_ref):
    m16 = m_ref[...].astype(BF16)
    k_ref[...] = jnp.dot(m16, wk_ref[...].astype(BF16), preferred_element_type=F32).astype(BF16)
    v_ref[...] = jnp.dot(m16, wv_ref[...].astype(BF16), preferred_element_type=F32).astype(BF16)


def _kv(mem, wk, wv, layer):
    b, m, d = mem.shape
    spec = pl.BlockSpec((None, m, d), lambda i: (i, 0, 0))
    return pl.pallas_call(
        _kv_kernel,
        grid=(b,),
        in_specs=[spec, _layer_spec(wk, layer), _layer_spec(wv, layer)],
        out_specs=[spec, spec],
        out_shape=[jax.ShapeDtypeStruct((b, m, d), BF16)] * 2,
        compiler_params=_params(("arbitrary",)),
        name="kv",
    )(mem, wk, wv)


def _xattn_kernel(h_ref, k_ref, v_ref, wq32_ref, wo32_ref, g_ref, b_ref, o_ref, wq_ref, wo_ref):
    @pl.when((pl.program_id(0) == 0) & (pl.program_id(1) == 0))
    def _():
        wq_ref[...] = wq32_ref[...].astype(BF16)
        wo_ref[...] = wo32_ref[...].astype(BF16)

    hb = h_ref[...]
    qall = jnp.dot(hb.astype(BF16), wq_ref[...], preferred_element_type=F32) * (1.0 / math.sqrt(XA_HEAD_DIM))
    outs = []
    for hd in range(XA_HEADS):
        sl = slice(hd * XA_HEAD_DIM, (hd + 1) * XA_HEAD_DIM)
        s = _dot_nt(qall[:, sl], k_ref[:, sl])
        p = jnp.exp(s - jnp.max(s, axis=-1, keepdims=True))
        o = jnp.dot(p.astype(BF16), v_ref[:, sl], preferred_element_type=F32)
        outs.append(o / jnp.sum(p, axis=-1, keepdims=True))
    o = jnp.concatenate(outs, axis=-1)
    y = jnp.dot(o.astype(BF16), wo_ref[...], preferred_element_type=F32)
    o_ref[...] = _layer_norm(ALPHA * hb + y, g_ref[...], b_ref[...])


def _xattn(h, k, v, wq, wo, g, bb, layer):
    b, l, d = h.shape
    tm = min(TOK_TM, l)
    tok = pl.BlockSpec((None, tm, d), lambda i, j: (i, j, 0))
    kvs = pl.BlockSpec((None, k.shape[1], d), lambda i, j: (i, 0, 0))
    return pl.pallas_call(
        _xattn_kernel,
        grid=(b, l // tm),
        in_specs=[tok, kvs, kvs] + [_layer_spec(c, layer) for c in (wq, wo, g, bb)],
        out_specs=tok,
        out_shape=jax.ShapeDtypeStruct((b, l, d), F32),
        scratch_shapes=[pltpu.VMEM((d, d), BF16), pltpu.VMEM((d, d), BF16)],
        compiler_params=_params(("arbitrary", "arbitrary")),
        name="xattn",
    )(h, k, v, wq, wo, g, bb)


def _mlp_kernel(h_ref, w1_ref, w2_ref, g_ref, b_ref, o_ref):
    hb = h_ref[...]
    h16 = hb.astype(BF16)
    acc = ALPHA * hb
    for f in range(D_FF // MLP_FC):
        sl = slice(f * MLP_FC, (f + 1) * MLP_FC)
        t = jnp.maximum(jnp.dot(h16, w1_ref[:, sl], preferred_element_type=F32), 0.0)
        acc = acc + jnp.dot((t * t).astype(BF16), w2_ref[sl, :], preferred_element_type=F32)
    o_ref[...] = _layer_norm(acc, g_ref[...], b_ref[...])


def _mlp(h2d, w1, w2, g, bb, layer):
    t, d = h2d.shape
    tm = min(TOK_TM, t)
    tok = pl.BlockSpec((tm, d), lambda i: (i, 0))
    return pl.pallas_call(
        _mlp_kernel,
        grid=(t // tm,),
        in_specs=[tok] + [_layer_spec(c, layer) for c in (w1, w2, g, bb)],
        out_specs=tok,
        out_shape=jax.ShapeDtypeStruct((t, d), F32),
        compiler_params=_params(("parallel",)),
        name="mlp",
    )(h2d, w1, w2, g, bb)


def _rows(v):
    return v.astype(F32).reshape(v.shape[0], 1, -1)


def _pad_lanes(v3d):
    return jnp.pad(v3d, ((0, 0), (0, 0), (0, LANES - v3d.shape[2])))


def prepare_params(seq_len, w_in, w_out, ssd_conv_w, ssd_conv_b, ssd_dt_bias, ssd_a_log, ssd_d, ssd_norm_w,
                   s5_lam_re, s5_lam_im, s5_log_step, s5_b_re, s5_b_im, s5_c_re, s5_c_im, s5_d, s5_glu_w, s5_glu_b,
                   rg_conv_w, rg_conv_b, rg_wa, rg_ba, rg_wx, rg_bx, rg_lambda, ln1_g, ln1_b,
                   xa_wq, xa_wk, xa_wv, xa_wo, ln2_g, ln2_b, mlp_w1, mlp_w2, ln3_g, ln3_b):
    o_u = SSD_WIDTH + SSD_XBC + SSD_HEADS
    n_tiles = min(S5_TB, seq_len) // (S5_BLK * SUBLANES)
    assert n_tiles <= 2 ** (SUBLANES - 3), "S5 scan-level table holds 3 in-tile + 5 tile levels"
    s5_ops = jax.vmap(functools.partial(_s5_prepare, n_tiles=n_tiles))(
        s5_lam_re, s5_lam_im, s5_log_step, s5_b_re, s5_b_im, s5_c_re, s5_c_im)
    s5 = (w_in[:, :, o_u:o_u + S5_WIDTH].astype(BF16), *s5_ops, _rows(s5_d), s5_glu_w.astype(BF16), _rows(s5_glu_b))
    mix_small = dict(
        cw=ssd_conv_w.astype(F32), cb=_rows(ssd_conv_b), dtb=_pad_lanes(_rows(ssd_dt_bias)),
        aneg=_pad_lanes(_rows(-jnp.exp(ssd_a_log.astype(F32)))),
        dvec=_rows(jnp.repeat(ssd_d.astype(F32), SSD_HEAD_DIM, axis=1)), nw=_rows(ssd_norm_w),
        rcw=rg_conv_w.astype(F32), rcb=_rows(rg_conv_b),
        wa=_block_diag(rg_wa.astype(F32)).astype(BF16), ba=_rows(rg_ba.reshape(rg_ba.shape[0], -1)),
        wx=_block_diag(rg_wx.astype(F32)).astype(BF16), bx=_rows(rg_bx.reshape(rg_bx.shape[0], -1)),
        rlam=_rows(rg_lambda), g=_rows(ln1_g), b=_rows(ln1_b))
    mix = (w_in.astype(F32), w_out.astype(F32), tuple(mix_small[f] for f in _MixW._fields if f not in _MIX_W16))
    xa = (xa_wk.astype(F32), xa_wv.astype(F32), xa_wq.astype(F32), xa_wo.astype(F32), _rows(ln2_g), _rows(ln2_b))
    mlp = (mlp_w1.astype(BF16), mlp_w2.astype(BF16), _rows(ln3_g), _rows(ln3_b))
    return dict(s5=s5, mix=mix, xa=xa, mlp=mlp)


def mixer_sublayer(h, prm, layer):
    ys5 = _s5_branch(h, prm["s5"], layer)
    w_in, w_out, consts = prm["mix"]
    return _mixer(h, ys5, w_in, w_out, consts, layer)


def xattn_sublayer(h, mem, prm, layer):
    wk, wv, wq, wo, g, bb = prm["xa"]
    k, v = _kv(mem, wk, wv, layer)
    return _xattn(h, k, v, wq, wo, g, bb, layer)


def mlp_sublayer(h, prm, layer):
    b, l, d = h.shape
    return _mlp(h.reshape(b * l, d), *prm["mlp"], layer).reshape(b, l, d)


def kernel(x, mem, w_in, w_out, ssd_conv_w, ssd_conv_b, ssd_dt_bias, ssd_a_log, ssd_d, ssd_norm_w, s5_lam_re, s5_lam_im, s5_log_step, s5_b_re, s5_b_im, s5_c_re, s5_c_im, s5_d, s5_glu_w, s5_glu_b, rg_conv_w, rg_conv_b, rg_wa, rg_ba, rg_wx, rg_bx, rg_lambda, ln1_g, ln1_b, xa_wq, xa_wk, xa_wv, xa_wo, ln2_g, ln2_b, mlp_w1, mlp_w2, ln3_g, ln3_b):
    h = x.astype(F32)
    memf = mem.astype(F32)
    prm = prepare_params(x.shape[1], w_in, w_out, ssd_conv_w, ssd_conv_b, ssd_dt_bias, ssd_a_log, ssd_d, ssd_norm_w,
                         s5_lam_re, s5_lam_im, s5_log_step, s5_b_re, s5_b_im, s5_c_re, s5_c_im, s5_d, s5_glu_w,
                         s5_glu_b, rg_conv_w, rg_conv_b, rg_wa, rg_ba, rg_wx, rg_bx, rg_lambda, ln1_g, ln1_b,
                         xa_wq, xa_wk, xa_wv, xa_wo, ln2_g, ln2_b, mlp_w1, mlp_w2, ln3_g, ln3_b)
    for layer in range(DEPTH):
        h = mixer_sublayer(h, prm, layer)
        h = xattn_sublayer(h, memf, prm, layer)
        h = mlp_sublayer(h, prm, layer)
    return h.astype(x.dtype)
```

```python
import collections
import functools
import math

import jax
import jax.numpy as jnp
from jax import lax
from jax.experimental import pallas as pl
from jax.experimental.pallas import tpu as pltpu

F32 = jnp.float32
BF16 = jnp.bfloat16

D_MODEL = 1024
DEPTH = 2
SSD_WIDTH = 512
SSD_HEAD_DIM = 64
SSD_HEADS = 8
SSD_GROUPS = 2
SSD_HPG = SSD_HEADS // SSD_GROUPS
SSD_GW = SSD_HPG * SSD_HEAD_DIM
SSD_STATE = 128
SSD_XBC = 1024
CONV_K = 4
S5_WIDTH = 256
S5_GROUPS = 16
S5_GROUP_CH = 16
S5_STATE = 64
S5_NSTATE = S5_GROUPS * S5_STATE
S5_BLK = 8
RG_WIDTH = 256
RG_BLOCKS = 4
RG_C = 8.0
XA_HEADS = 4
XA_HEAD_DIM = 256
D_FF = 4096
ALPHA = (2.0 * DEPTH) ** 0.25
LN_EPS = 1e-5

LANES = 128
SUBLANES = 8
MXU_TILE = 256
NEG_BIG = -1e30
VMEM_LIMIT = 56 * 1024 * 1024

SSD_CHUNK = 128
MIX_TQ = 256
S5_TB = 2048
TOK_TM = 512
MLP_TM = 1024
MLP_FC = 1024


def _dot(a, b):
    return jnp.dot(a.astype(BF16), b.astype(BF16), preferred_element_type=F32)


def _dot_nt(a, b):
    return lax.dot_general(a.astype(BF16), b.astype(BF16), (((1,), (1,)), ((), ())),
                           preferred_element_type=F32)


def _layer_norm(x, g, b):
    mu = jnp.mean(x, axis=-1, keepdims=True)
    xc = x - mu
    var = jnp.mean(xc * xc, axis=-1, keepdims=True)
    return xc * lax.rsqrt(var + LN_EPS) * g + b


def _gelu(x):
    c = math.sqrt(2.0 / math.pi)
    return 0.5 * x * (1.0 + jnp.tanh(c * (x + 0.044715 * (x * x * x))))


def _sigmoid(x):
    return 0.5 + 0.5 * jnp.tanh(0.5 * x)


def _silu(x):
    hx = 0.5 * x
    return hx + hx * jnp.tanh(hx)


def _softplus(x):
    return jnp.maximum(x, 0.0) + jnp.log1p(jnp.exp(-jnp.abs(x)))


def _const_spec(shape):
    nd = len(shape)
    return pl.BlockSpec(shape, lambda *_: (0,) * nd, pipeline_mode=pl.Buffered(1))


def _layer_spec(stacked, layer):
    nd = stacked.ndim - 1
    return pl.BlockSpec((None,) + stacked.shape[1:], lambda *_: (layer,) + (0,) * nd, pipeline_mode=pl.Buffered(1))


def _params(sem):
    return pltpu.CompilerParams(dimension_semantics=sem, vmem_limit_bytes=VMEM_LIMIT)


def _re_lanes(p):
    return slice(2 * p * LANES, (2 * p + 1) * LANES)


def _im_lanes(p):
    return slice((2 * p + 1) * LANES, (2 * p + 2) * LANES)


def _s5_state_group(idx):
    state = (idx // (2 * LANES)) * LANES + idx % LANES
    return state // S5_STATE


def _s5_split(x):
    planes = S5_NSTATE // LANES
    return (jnp.concatenate([x[:, _re_lanes(p)] for p in range(planes)], axis=-1),
            jnp.concatenate([x[:, _im_lanes(p)] for p in range(planes)], axis=-1))


S5_HALF_CH = S5_WIDTH // 2
S5_HALF_LANES = S5_BLK * S5_HALF_CH


def _s5_expand_operators(wc_ref, vc_ref, ct_ref, ms_scr, toep_scr, mo_scr):
    n, w, c = S5_NSTATE, S5_WIDTH, S5_GROUP_CH
    hw, hl = S5_HALF_CH, S5_HALF_LANES
    grp_half = S5_GROUPS // 2
    reps = hw // c
    grp_r = _s5_state_group(lax.broadcasted_iota(jnp.int32, (2 * n, w), 0))
    same_o = grp_r == lax.broadcasted_iota(jnp.int32, (2 * n, w), 1) // c
    mo0 = jnp.where(same_o, ct_ref[...], 0.0)
    kc = jnp.dot(wc_ref[...], mo0, preferred_element_type=F32, precision=lax.Precision.HIGHEST)
    toep_scr[...] = jnp.zeros(toep_scr.shape, toep_scr.dtype)
    ch_grp = lax.broadcasted_iota(jnp.int32, (hw, hl), 0) // c
    same_k = (lax.broadcasted_iota(jnp.int32, (hw, hw), 0) // c) == (lax.broadcasted_iota(jnp.int32, (hw, hw), 1) // c)
    pick_r = lax.broadcasted_iota(jnp.int32, (S5_BLK * c, hl), 0)
    pick_c = lax.broadcasted_iota(jnp.int32, (S5_BLK * c, hl), 1)
    pick = ((pick_r // c == pick_c // hw) & (pick_r % c == pick_c % c)).astype(BF16)
    for hf in range(2):
        lanes = slice(hf * hl, (hf + 1) * hl)
        st_grp = _s5_state_group(lax.broadcasted_iota(jnp.int32, (hw, hl), 1) + hf * hl) - hf * grp_half
        same_s = ch_grp == st_grp
        for t in range(S5_BLK):
            blk = jnp.where(same_s, jnp.tile(wc_ref[t * c:(t + 1) * c, lanes], (reps, 1)), 0.0)
            ms_scr[hf, t * hw:(t + 1) * hw, :] = blk.astype(BF16)
        st_grp = _s5_state_group(lax.broadcasted_iota(jnp.int32, (hl, hl), 0) + hf * hl) - hf * grp_half
        same_r = st_grp == (lax.broadcasted_iota(jnp.int32, (hl, hl), 1) % hw) // c
        spread = jnp.dot(vc_ref[lanes, :].astype(BF16), pick, preferred_element_type=F32)
        mo_scr[hf] = jnp.where(same_r, spread, 0.0).astype(BF16)
        for t_in in range(S5_BLK):
            for t_out in range(t_in, S5_BLK):
                k = S5_BLK - 1 - (t_out - t_in)
                blk = jnp.where(same_k, jnp.tile(kc[k * c:(k + 1) * c, hf * hw:(hf + 1) * hw], (reps, 1)), 0.0)
                toep_scr[hf, t_in * hw:(t_in + 1) * hw, t_out * hw:(t_out + 1) * hw] = blk.astype(BF16)


_S5W = collections.namedtuple("_S5W", "wu wc vc ct lvl pw pwt d gluw glub")
_S5Ops = collections.namedtuple("_S5Ops", "ms toep mo")
_S5Buf = collections.namedtuple("_S5Buf", "u32 u16 hprev16")
_S5Scratch = collections.namedtuple("_S5Scratch", "u_tok hplane cexp carry yg")


def _s5_scan_stage(h_ref, w, ops, scr, out, first, rows, side):
    n, wd = S5_NSTATE, S5_WIDTH
    lane_halves = wd // LANES
    planes = n // LANES
    nt = rows // SUBLANES
    tb = rows * S5_BLK
    for r0 in range(0, tb, tb // 4):
        rs = slice(r0, r0 + tb // 4)
        u_tok = jnp.dot(h_ref[rs, :].astype(BF16), w.wu[...], preferred_element_type=F32)
        for hf in range(lane_halves):
            scr.u_tok[hf, rs, :] = u_tok[:, hf * LANES:(hf + 1) * LANES]
    u = jnp.concatenate([scr.u_tok[hf, pl.ds(k, rows, stride=S5_BLK), :]
                         for hf in range(lane_halves) for k in range(S5_BLK)], axis=-1)
    out.u32[...] = u
    u16 = u.astype(BF16)
    out.u16[...] = u16
    row8 = lax.broadcasted_iota(jnp.int32, (nt, SUBLANES, LANES), 1)
    planes_per_half = planes // lane_halves
    for p in range(planes):
        hf, pl_in_half = divmod(p, planes_per_half)
        s = jnp.dot(u16[:, hf * S5_HALF_LANES:(hf + 1) * S5_HALF_LANES],
                    ops.ms[hf, :, 2 * pl_in_half * LANES:(2 * pl_in_half + 2) * LANES],
                    preferred_element_type=F32)
        if p < planes - 2:
            side(1)
        xr = s[:, :LANES].reshape(nt, SUBLANES, LANES)
        xi = s[:, LANES:].reshape(nt, SUBLANES, LANES)
        for lv, dist in enumerate((1, 2, 4)):
            lr = w.lvl[lv:lv + 1, _re_lanes(p)]
            li = w.lvl[lv:lv + 1, _im_lanes(p)]
            keep = row8 >= dist
            sr = jnp.where(keep, pltpu.roll(xr, dist, 1), 0.0)
            si = jnp.where(keep, pltpu.roll(xi, dist, 1), 0.0)
            xr, xi = xr + lr * sr - li * si, xi + lr * si + li * sr
        scr.hplane[2 * p, SUBLANES:SUBLANES + rows, :] = xr.reshape(rows, LANES)
        scr.hplane[2 * p + 1, SUBLANES:SUBLANES + rows, :] = xi.reshape(rows, LANES)
    side(1)
    last = 2 * SUBLANES - 1
    er = jnp.concatenate([scr.hplane[2 * p, pl.ds(last, nt, stride=SUBLANES), :] for p in range(planes)], axis=-1)
    ei = jnp.concatenate([scr.hplane[2 * p + 1, pl.ds(last, nt, stride=SUBLANES), :] for p in range(planes)], -1)
    tile_id = lax.broadcasted_iota(jnp.int32, (nt, n), 0)
    lv, dist = 3, 1
    while dist < nt:
        lr, li = _s5_split(w.lvl[lv:lv + 1, :])
        keep = tile_id >= dist
        sr = jnp.where(keep, pltpu.roll(er, dist, 0), 0.0)
        si = jnp.where(keep, pltpu.roll(ei, dist, 0), 0.0)
        er, ei = er + lr * sr - li * si, ei + lr * si + li * sr
        lv, dist = lv + 1, dist * 2
    cr, ci = _s5_split(jnp.where(first, 0.0, scr.carry[0:1, :]))
    ptr, pti = _s5_split(w.pwt[...])
    er, ei = er + ptr * cr - pti * ci, ei + ptr * ci + pti * cr
    for p in range(planes):
        scr.carry[0:1, _re_lanes(p)] = er[nt - 1:nt, p * LANES:(p + 1) * LANES]
        scr.carry[0:1, _im_lanes(p)] = ei[nt - 1:nt, p * LANES:(p + 1) * LANES]
    tin_r = jnp.where(tile_id >= 1, pltpu.roll(er, 1, 0), cr)
    tin_i = jnp.where(tile_id >= 1, pltpu.roll(ei, 1, 0), ci)
    side(1)
    for p in range(planes):
        for k in range(SUBLANES):
            scr.cexp[2 * p, pl.ds(k, nt, stride=SUBLANES), :] = tin_r[:, p * LANES:(p + 1) * LANES]
            scr.cexp[2 * p + 1, pl.ds(k, nt, stride=SUBLANES), :] = tin_i[:, p * LANES:(p + 1) * LANES]
    for p in range(planes):
        if p % 2 == 0:
            side(1)
        pr = jnp.tile(w.pw[:, _re_lanes(p)], (nt, 1))
        pi = jnp.tile(w.pw[:, _im_lanes(p)], (nt, 1))
        tr = scr.cexp[2 * p]
        ti = scr.cexp[2 * p + 1]
        hr = scr.hplane[2 * p, SUBLANES:SUBLANES + rows, :] + pr * tr - pi * ti
        hi = scr.hplane[2 * p + 1, SUBLANES:SUBLANES + rows, :] + pr * ti + pi * tr
        scr.hplane[2 * p, SUBLANES:SUBLANES + rows, :] = hr
        scr.hplane[2 * p + 1, SUBLANES:SUBLANES + rows, :] = hi
        scr.hplane[2 * p, SUBLANES - 1:SUBLANES, :] = cr[:, p * LANES:(p + 1) * LANES]
        scr.hplane[2 * p + 1, SUBLANES - 1:SUBLANES, :] = ci[:, p * LANES:(p + 1) * LANES]
    for p in range(2 * planes):
        out.hprev16[:, p * LANES:(p + 1) * LANES] = scr.hplane[p, SUBLANES - 1:SUBLANES - 1 + rows, :].astype(BF16)


def _s5_readout_jobs(src, w, ops, scr, o_ref, rows):
    hw, hl = S5_HALF_CH, S5_HALF_LANES
    lane_halves = S5_WIDTH // LANES
    steps_per_tile = MXU_TILE // hw

    def tile_job(hf, j):
        def run():
            k_rows = (j + 1) * MXU_TILE
            cols = slice(j * MXU_TILE, (j + 1) * MXU_TILE)
            y = jnp.dot(src.u16[:, hf * hl:hf * hl + k_rows], ops.toep[hf, :k_rows, cols],
                        preferred_element_type=F32)
            y = y + jnp.dot(src.hprev16[:, hf * hl:(hf + 1) * hl], ops.mo[hf, :, cols], preferred_element_type=F32)
            d_half = w.d[:, hf * hw:(hf + 1) * hw]
            y = _gelu(y + jnp.concatenate([d_half] * steps_per_tile, axis=1) * src.u32[:, hf * hl + j * MXU_TILE:
                                                                                           hf * hl + (j + 1) * MXU_TILE])
            for i in range(steps_per_tile):
                scr.yg[hf, pl.ds(j * steps_per_tile + i, rows, stride=S5_BLK), :] = y[:, i * hw:(i + 1) * hw]
        return run

    def glu_job():
        yg = jnp.concatenate([scr.yg[hf] for hf in range(lane_halves)], axis=-1)
        glu = _sigmoid(_dot(yg, w.gluw[...]) + w.glub[...])
        o_ref[...] = (yg * glu).astype(o_ref.dtype)

    return [tile_job(hf, j) for hf in range(lane_halves) for j in range(hl // MXU_TILE)] + [glu_job]


def _s5_kernel(h_ref, *rest, rows, nblk):
    nw = len(_S5W._fields)
    w = _S5W(*rest[:nw])
    o_ref = rest[nw]
    bufs = rest[nw + 1:]
    ops = _S5Ops(*bufs[:3])
    nb = len(_S5Buf._fields)
    scr = _S5Scratch(*bufs[3 + nb:])
    s = pl.program_id(0)
    parity = s % 2

    @pl.when(s == 0)
    def _():
        _s5_expand_operators(w.wc, w.vc, w.ct, *ops)
        for ref in bufs[3:]:
            ref[...] = jnp.zeros(ref.shape, ref.dtype)

    first = (s % nblk) == 0

    def step(out, src):
        jobs = _s5_readout_jobs(src, w, ops, scr, o_ref, rows)

        def side(k):
            for _ in range(min(k, len(jobs))):
                jobs.pop(0)()

        _s5_scan_stage(h_ref, w, ops, scr, out, first, rows, side)
        side(len(jobs))

    step(_S5Buf(*(r.at[parity] for r in bufs[3:3 + nb])), _S5Buf(*(r.at[1 - parity] for r in bufs[3:3 + nb])))


def _s5_branch(h, consts, layer):
    b, l, d = h.shape
    tb = min(S5_TB, l)
    rows = tb // S5_BLK
    nblk = l // tb
    total = b * nblk
    n, w = S5_NSTATE, S5_WIDTH
    handoff = [
        pltpu.VMEM((2, rows, S5_BLK * w), F32),
        pltpu.VMEM((2, rows, S5_BLK * w), BF16),
        pltpu.VMEM((2, rows, 2 * n), BF16),
    ]
    out = pl.pallas_call(
        functools.partial(_s5_kernel, rows=rows, nblk=nblk),
        grid=(total + 1,),
        in_specs=[pl.BlockSpec((tb, d), lambda s: (jnp.minimum(s, total - 1), 0))]
                 + [_layer_spec(c, layer) for c in consts],
        out_specs=pl.BlockSpec((tb, w), lambda s: (jnp.maximum(s - 1, 0), 0)),
        out_shape=jax.ShapeDtypeStruct((b * l, w), BF16),
        scratch_shapes=[
            pltpu.VMEM((2, S5_HALF_LANES, S5_HALF_LANES), BF16),
            pltpu.VMEM((2, S5_HALF_LANES, S5_HALF_LANES), BF16),
            pltpu.VMEM((2, S5_HALF_LANES, S5_HALF_LANES), BF16),
        ] + handoff + [
            pltpu.VMEM((w // LANES, tb, LANES), F32),
            pltpu.VMEM((2 * n // LANES, rows + SUBLANES, LANES), F32),
            pltpu.VMEM((2 * n // LANES, rows, LANES), F32),
            pltpu.VMEM((SUBLANES, 2 * n), F32),
            pltpu.VMEM((w // LANES, tb, LANES), F32),
        ],
        compiler_params=_params(("arbitrary",)),
        name="s5_branch",
    )(h.reshape(b * l, d), *consts)
    return out.reshape(b, l, w)


def _s5_prepare(lam_re, lam_im, log_step, b_re, b_im, c_re, c_im, n_tiles):
    g, p, c, t8 = S5_GROUPS, S5_STATE, S5_GROUP_CH, S5_BLK
    lr, li = lam_re.astype(F32), lam_im.astype(F32)
    dt = jnp.exp(log_step.astype(F32))[:, None]
    ar, ai = lr * dt, li * dt

    def lam_pow(k):
        k = k.astype(F32)[..., None, None]
        mag = jnp.exp(k * ar)
        return mag * jnp.cos(k * ai), mag * jnp.sin(k * ai)

    br1, bi1 = lam_pow(jnp.ones((), jnp.int32))
    xr_, xi_ = br1 - 1.0, bi1
    den = lr * lr + li * li
    cr, ci = (xr_ * lr + xi_ * li) / den, (xi_ * lr - xr_ * li) / den
    bre, bim = b_re.astype(F32), b_im.astype(F32)
    bbr = cr[..., None] * bre - ci[..., None] * bim
    bbi = cr[..., None] * bim + ci[..., None] * bre
    steps = jnp.arange(t8)
    pr, pi = lam_pow(t8 - 1 - steps)
    wsr = pr[..., None] * bbr[None] - pi[..., None] * bbi[None]
    wsi = pr[..., None] * bbi[None] + pi[..., None] * bbr[None]

    def interleave(re, im, axis):
        shape = list(re.shape)
        shape[axis:axis + 1] = [re.shape[axis] // LANES, 1, LANES]
        both = jnp.concatenate([re.reshape(shape), im.reshape(shape)], axis=axis + 1)
        shape[axis:axis + 3] = [2 * re.shape[axis]]
        return both.reshape(shape)

    wc = interleave(*[part.transpose(0, 3, 1, 2).reshape(t8 * c, g * p) for part in (wsr, wsi)], axis=1)
    cre, cim = c_re.astype(F32), c_im.astype(F32)
    pr, pi = lam_pow(steps + 1)
    vor = cre[None] * pr[:, :, None, :] - cim[None] * pi[:, :, None, :]
    voi = cre[None] * pi[:, :, None, :] + cim[None] * pr[:, :, None, :]
    vc = interleave(*[part.transpose(1, 3, 0, 2).reshape(g * p, t8 * c) for part in (vor, -voi)], axis=0)
    ct = interleave(*[jnp.tile(part.transpose(0, 2, 1).reshape(g * p, c), (1, g)) for part in (cre, -cim)], axis=0)

    def re_im(pair):
        return interleave(pair[0].reshape(-1, g * p), pair[1].reshape(-1, g * p), axis=1)

    lvl = re_im(lam_pow(t8 * 2 ** jnp.arange(SUBLANES)))
    pw = re_im(lam_pow(t8 * (steps + 1)))
    pwt = re_im(lam_pow(t8 * SUBLANES * (jnp.arange(n_tiles) + 1)))
    return wc, vc, ct, lvl, pw, pwt


def _expand_heads(cols, grp, head_lane):
    base = SSD_HPG * grp
    out = jnp.broadcast_to(cols[:, base:base + 1], (cols.shape[0], SSD_GW))
    for hh in range(1, SSD_HPG):
        out = jnp.where(head_lane == hh, cols[:, base + hh:base + hh + 1], out)
    return out


def _causal_conv(pad_ref, w_ref, b_ref, rows, cols):
    acc = b_ref[:, cols] + w_ref[CONV_K - 1:CONV_K, cols] * pad_ref[SUBLANES:SUBLANES + rows, cols]
    for k in range(CONV_K - 1):
        off = SUBLANES - (CONV_K - 1) + k
        acc = acc + w_ref[k:k + 1, cols] * pad_ref[off:off + rows, cols]
    return acc


_ProjBuf = collections.namedtuple("_ProjBuf", "z xpad dt rpad gr")
_MixW = collections.namedtuple(
    "_MixW", "wz wxbc wdt wxr wgr cw cb dtb aneg dvec nw rcw rcb wa ba wx bx rlam wout g b")
_MixScratch = collections.namedtuple("_MixScratch", "state ascr bscr cexp rcar")


def _col_tiles(width):
    return [slice(c, min(c + MXU_TILE, width)) for c in range(0, width, MXU_TILE)]


def _mixer_output_jobs(h_ref, ycat, w, o_ref):
    def tile_job(cols):
        def run():
            y = jnp.dot(ycat[...], w.wout[:, cols], preferred_element_type=F32)
            o_ref[:, cols] = ALPHA * h_ref[:, cols] + y
        return run

    def norm_job():
        o_ref[...] = _layer_norm(o_ref[...], w.g[...], w.b[...])

    return [tile_job(cols) for cols in _col_tiles(D_MODEL)] + [norm_job]


def _mixer_project_jobs(h_ref, h16, w, nxt, cur, first, tq):
    def cast_job():
        h16[...] = h_ref[...].astype(BF16)

    def tile_job(dst, w_ref, cols, row0):
        def run():
            dst[row0:row0 + tq, cols] = jnp.dot(h16[...], w_ref[:, cols], preferred_element_type=F32)
        return run

    def tail_job(pad_n, pad_c):
        def run():
            pad_n[0:SUBLANES, :] = jnp.where(first, 0.0, pad_c[tq:tq + SUBLANES, :])
        return run

    jobs = [cast_job]
    for dst, w_ref, row0 in ((nxt.z, w.wz, 0), (nxt.xpad, w.wxbc, SUBLANES), (nxt.dt, w.wdt, 0),
                             (nxt.rpad, w.wxr, SUBLANES), (nxt.gr, w.wgr, 0)):
        jobs += [tile_job(dst, w_ref, cols, row0) for cols in _col_tiles(dst.shape[1])]
    jobs += [tail_job(nxt.xpad, cur.xpad), tail_job(nxt.rpad, cur.rpad)]
    return jobs


def _mixer_compute(cur, ys5_ref, w, scr, ycat, first, tq, chunk, side):
    cw_ref, cb_ref, dtb_ref, aneg_ref, dvec_ref, nw_ref = w.cw, w.cb, w.dtb, w.aneg, w.dvec, w.nw
    state = scr.state
    z = cur.z
    slabs = []
    for cols in _col_tiles(SSD_XBC):
        side(1)
        slabs.append(_silu(_causal_conv(cur.xpad, cw_ref, cb_ref, tq, cols)))
    xact = jnp.concatenate(slabs, axis=-1)
    side(2)
    dt_all = _softplus(cur.dt[...] + dtb_ref[...])
    q = chunk
    tri = (lax.broadcasted_iota(jnp.int32, (q, q), 0) >= lax.broadcasted_iota(jnp.int32, (q, q), 1))
    row_id = lax.broadcasted_iota(jnp.int32, (q, LANES), 0)
    head_lane = lax.broadcasted_iota(jnp.int32, (1, SSD_GW), 1) // SSD_HEAD_DIM
    for ci in range(tq // q):
        r0 = ci * q
        dt = dt_all[r0:r0 + q, :]
        cs = dt * aneg_ref[...]
        dist = 1
        while dist < q:
            cs = cs + jnp.where(row_id >= dist, pltpu.roll(cs, dist, 0), 0.0)
            dist *= 2
        cs_t = cs.T
        cs_last = cs[q - 1:q, :]
        chunk_decay = jnp.exp(cs_last)
        dec_t = jnp.exp(cs_t[:, q - 1:q] - cs_t)
        ecs = jnp.exp(cs)
        zc = z[r0:r0 + q, :]
        halves = []
        for grp in range(SSD_GROUPS):
            side(2)
            lo = grp * SSD_GW
            xs = xact[r0:r0 + q, lo:lo + SSD_GW]
            bm = xact[r0:r0 + q, SSD_WIDTH + grp * SSD_STATE:SSD_WIDTH + (grp + 1) * SSD_STATE]
            cm = xact[r0:r0 + q, SSD_WIDTH + (SSD_GROUPS + grp) * SSD_STATE:
                      SSD_WIDTH + (SSD_GROUPS + grp + 1) * SSD_STATE]
            cm16 = cm.astype(BF16)
            xdt16 = (xs * _expand_heads(dt, grp, head_lane)).astype(BF16)
            cb = _dot_nt(cm16, bm)
            bm_t = bm.T
            s_prev = state[grp]
            if ci == 0:
                s_prev = jnp.where(first, 0.0, s_prev)
            y_off = jnp.dot(cm16, s_prev.astype(BF16), preferred_element_type=F32)
            intra, to_state, xdt_heads = [], [], []
            for hh in range(SSD_HPG):
                hd = grp * SSD_HPG + hh
                seg = cs[:, hd:hd + 1] - cs_t[hd:hd + 1, :]
                lmat = jnp.exp(jnp.where(tri, seg, NEG_BIG))
                intra.append((cb * lmat).astype(BF16))
                to_state.append((bm_t * dec_t[hd:hd + 1, :]).astype(BF16))
                xdt_heads.append(jnp.where(head_lane == hh, xdt16, jnp.zeros_like(xdt16)))
            lhs = jnp.concatenate([jnp.concatenate(intra, axis=1), jnp.concatenate(to_state, axis=1)], axis=0)
            both = jnp.dot(lhs, jnp.concatenate(xdt_heads, axis=0), preferred_element_type=F32)
            y_diag = both[:q]
            st_new = both[q:]
            state[grp] = s_prev * _expand_heads(chunk_decay, grp, head_lane) + st_new
            halves.append(y_diag + y_off * _expand_heads(ecs, grp, head_lane) + xs * dvec_ref[:, lo:lo + SSD_GW])
        y = jnp.concatenate(halves, axis=-1) * _silu(zc)
        y = y * lax.rsqrt(jnp.mean(y * y, axis=-1, keepdims=True) + LN_EPS) * nw_ref[...]
        ycat[r0:r0 + q, 0:SSD_WIDTH] = y.astype(BF16)

    ycat[:, SSD_WIDTH:SSD_WIDTH + S5_WIDTH] = ys5_ref[...]
    _mixer_rglru(cur, w, scr, ycat, first, tq, side)


def _mixer_rglru(cur, w, scr, ycat, first, tq, side):
    side(1)
    xc = _causal_conv(cur.rpad, w.rcw, w.rcb, tq, slice(0, RG_WIDTH))
    xc16 = xc.astype(BF16)
    rgate = _sigmoid(jnp.dot(xc16, w.wa[...], preferred_element_type=F32) + w.ba[...])
    igate = _sigmoid(jnp.dot(xc16, w.wx[...], preferred_element_type=F32) + w.bx[...])
    log_a = (-RG_C) * rgate * _softplus(-w.rlam[...])
    a = jnp.exp(log_a)
    bt = jnp.sqrt(1.0 - jnp.exp(2.0 * log_a)) * (igate * xc)
    row8 = lax.broadcasted_iota(jnp.int32, (tq, RG_WIDTH), 0) & (SUBLANES - 1)
    side(1)
    for dist in (1, 2, 4):
        keep = row8 >= dist
        a_sh = jnp.where(keep, pltpu.roll(a, dist, 0), 1.0)
        b_sh = jnp.where(keep, pltpu.roll(bt, dist, 0), 0.0)
        bt = bt + a * b_sh
        a = a * a_sh
    nt = tq // SUBLANES
    lane_halves = RG_WIDTH // LANES
    for hf in range(lane_halves):
        scr.ascr[hf] = a[:, hf * LANES:(hf + 1) * LANES]
        scr.bscr[hf] = bt[:, hf * LANES:(hf + 1) * LANES]
    ae = jnp.concatenate([scr.ascr[hf, pl.ds(SUBLANES - 1, nt, stride=SUBLANES), :] for hf in range(lane_halves)], -1)
    be = jnp.concatenate([scr.bscr[hf, pl.ds(SUBLANES - 1, nt, stride=SUBLANES), :] for hf in range(lane_halves)], -1)
    tile_id = lax.broadcasted_iota(jnp.int32, (nt, RG_WIDTH), 0)
    dist = 1
    while dist < nt:
        keep = tile_id >= dist
        a_sh = jnp.where(keep, pltpu.roll(ae, dist, 0), 1.0)
        b_sh = jnp.where(keep, pltpu.roll(be, dist, 0), 0.0)
        be = be + ae * b_sh
        ae = ae * a_sh
        dist *= 2
    carry_in = jnp.where(first, 0.0, scr.rcar[0:1, :])
    tile_end = be + ae * carry_in
    scr.rcar[0:1, :] = tile_end[nt - 1:nt, :]
    tile_in = jnp.where(tile_id >= 1, pltpu.roll(tile_end, 1, 0), carry_in)
    for hf in range(lane_halves):
        for k in range(SUBLANES):
            scr.cexp[hf, pl.ds(k, nt, stride=SUBLANES), :] = tile_in[:, hf * LANES:(hf + 1) * LANES]
    hscan = bt + a * jnp.concatenate([scr.cexp[hf] for hf in range(lane_halves)], axis=-1)
    ycat[:, SSD_WIDTH + S5_WIDTH:] = (hscan * _gelu(cur.gr[...])).astype(BF16)


_MIX_W16 = ("wz", "wxbc", "wdt", "wxr", "wgr", "wout")


def _mixer_load_weights(w_in_ref, w_out_ref, w):
    o_dt = SSD_WIDTH + SSD_XBC
    o_xr = o_dt + SSD_HEADS + S5_WIDTH
    o_gr = o_xr + RG_WIDTH
    w.wz[...] = w_in_ref[:, 0:SSD_WIDTH].astype(BF16)
    w.wxbc[...] = w_in_ref[:, SSD_WIDTH:o_dt].astype(BF16)
    head_cols = lax.broadcasted_iota(jnp.int32, (1, LANES), 1) < SSD_HEADS
    w.wdt[...] = jnp.where(head_cols, w_in_ref[:, o_dt:o_dt + LANES], 0.0).astype(BF16)
    w.wxr[...] = w_in_ref[:, o_xr:o_gr].astype(BF16)
    w.wgr[...] = w_in_ref[:, o_gr:o_gr + RG_WIDTH].astype(BF16)
    w.wout[...] = w_out_ref[...].astype(BF16)


def _mixer_kernel(hp_ref, ho_ref, ys5_ref, w_in_ref, w_out_ref, *rest, tq, chunk, nblk):
    small = [f for f in _MixW._fields if f not in _MIX_W16]
    o_ref = rest[len(small)]
    bufs = rest[len(small) + 1:]
    w16 = bufs[len(bufs) - len(_MIX_W16):]
    bufs = bufs[:len(bufs) - len(_MIX_W16)]
    w = _MixW(**dict(zip(small, rest[:len(small)])), **dict(zip(_MIX_W16, w16)))
    npb = len(_ProjBuf._fields)
    sets = (_ProjBuf(*bufs[:npb]), _ProjBuf(*bufs[npb:2 * npb]))
    ycats = bufs[2 * npb:2 * npb + 2]
    scr = _MixScratch(*bufs[2 * npb + 2:2 * npb + 2 + len(_MixScratch._fields)])
    h16 = bufs[-1]
    s = pl.program_id(0)

    @pl.when(s == 0)
    def _():
        _mixer_load_weights(w_in_ref, w_out_ref, w)
        for ref in bufs:
            ref[...] = jnp.zeros(ref.shape, ref.dtype)

    first_p = (s % nblk) == 0
    first_c = ((s + nblk - 1) % nblk) == 0

    def step(nxt, cur, ycat_c, ycat_o):
        jobs = (_mixer_output_jobs(ho_ref, ycat_o, w, o_ref)
                + _mixer_project_jobs(hp_ref, h16, w, nxt, cur, first_p, tq))

        def side(n):
            for _ in range(min(n, len(jobs))):
                jobs.pop(0)()

        _mixer_compute(cur, ys5_ref, w, scr, ycat_c, first_c, tq, chunk, side)
        side(len(jobs))

    @pl.when(s % 2 == 0)
    def _():
        step(sets[0], sets[1], ycats[1], ycats[0])

    @pl.when(s % 2 == 1)
    def _():
        step(sets[1], sets[0], ycats[0], ycats[1])


def _mixer(h, ys5, w_in, w_out, consts, layer):
    b, l, d = h.shape
    tq = min(MIX_TQ, l)
    chunk = min(SSD_CHUNK, tq)
    nblk = l // tq
    total = b * nblk
    h2d = h.reshape(b * l, d)
    proj_buf = [
        pltpu.VMEM((tq, SSD_WIDTH), F32),
        pltpu.VMEM((tq + SUBLANES, SSD_XBC), F32),
        pltpu.VMEM((tq, LANES), F32),
        pltpu.VMEM((tq + SUBLANES, RG_WIDTH), F32),
        pltpu.VMEM((tq, RG_WIDTH), F32),
    ]
    rg_plane = pltpu.VMEM((RG_WIDTH // LANES, tq, LANES), F32)
    out = pl.pallas_call(
        functools.partial(_mixer_kernel, tq=tq, chunk=chunk, nblk=nblk),
        grid=(total + 2,),
        in_specs=[pl.BlockSpec((tq, d), lambda s: (jnp.minimum(s, total - 1), 0)),
                  pl.BlockSpec((tq, d), lambda s: (jnp.maximum(s - 2, 0), 0)),
                  pl.BlockSpec((tq, S5_WIDTH), lambda s: (jnp.clip(s - 1, 0, total - 1), 0))]
                 + [_layer_spec(c, layer) for c in (w_in, w_out) + tuple(consts)],
        out_specs=pl.BlockSpec((tq, d), lambda s: (jnp.maximum(s - 2, 0), 0)),
        out_shape=jax.ShapeDtypeStruct((b * l, d), F32),
        scratch_shapes=proj_buf + proj_buf + [
            pltpu.VMEM((tq, D_MODEL), BF16),
            pltpu.VMEM((tq, D_MODEL), BF16),
            pltpu.VMEM((SSD_GROUPS, SSD_STATE, SSD_GW), F32),
            rg_plane, rg_plane, rg_plane,
            pltpu.VMEM((SUBLANES, RG_WIDTH), F32),
            pltpu.VMEM((tq, d), BF16),
            pltpu.VMEM((d, SSD_WIDTH), BF16),
            pltpu.VMEM((d, SSD_XBC), BF16),
            pltpu.VMEM((d, LANES), BF16),
            pltpu.VMEM((d, RG_WIDTH), BF16),
            pltpu.VMEM((d, RG_WIDTH), BF16),
            pltpu.VMEM((d, d), BF16),
        ],
        compiler_params=_params(("arbitrary",)),
        name="mixer",
    )(h2d, h2d, ys5.reshape(b * l, S5_WIDTH), w_in, w_out, *consts)
    return out.reshape(b, l, d)


def _block_diag(w):
    nl, nb, bi, bo = w.shape
    return jnp.einsum('lhij,hk->lhikj', w, jnp.eye(nb, dtype=w.dtype)).reshape(nl, nb * bi, nb * bo)


def _xattn_kernel(h_ref, m_ref, wk32_ref, wv32_ref, wq32_ref, wo32_ref, g_ref, b_ref, o_ref,
                  wq_ref, wo_ref, k_ref, v_ref):
    @pl.when((pl.program_id(0) == 0) & (pl.program_id(1) == 0))
    def _():
        wq_ref[...] = wq32_ref[...].astype(BF16)
        wo_ref[...] = wo32_ref[...].astype(BF16)

    @pl.when(pl.program_id(1) == 0)
    def _():
        m16 = m_ref[...].astype(BF16)
        k_ref[...] = jnp.dot(m16, wk32_ref[...].astype(BF16), preferred_element_type=F32).astype(BF16)
        v_ref[...] = jnp.dot(m16, wv32_ref[...].astype(BF16), preferred_element_type=F32).astype(BF16)

    hb = h_ref[...]
    qall = jnp.dot(hb.astype(BF16), wq_ref[...], preferred_element_type=F32) * (1.0 / math.sqrt(XA_HEAD_DIM))
    outs = []
    for hd in range(XA_HEADS):
        sl = slice(hd * XA_HEAD_DIM, (hd + 1) * XA_HEAD_DIM)
        s = _dot_nt(qall[:, sl], k_ref[:, sl])
        p = jnp.exp(s - jnp.max(s, axis=-1, keepdims=True))
        o = jnp.dot(p.astype(BF16), v_ref[:, sl], preferred_element_type=F32)
        outs.append(o / jnp.sum(p, axis=-1, keepdims=True))
    o = jnp.concatenate(outs, axis=-1)
    y = jnp.dot(o.astype(BF16), wo_ref[...], preferred_element_type=F32)
    o_ref[...] = _layer_norm(ALPHA * hb + y, g_ref[...], b_ref[...])


def _xattn(h, mem, wk, wv, wq, wo, g, bb, layer):
    b, l, d = h.shape
    m = mem.shape[1]
    tm = min(TOK_TM, l)
    tok = pl.BlockSpec((None, tm, d), lambda i, j: (i, j, 0))
    mems = pl.BlockSpec((None, m, d), lambda i, j: (i, 0, 0))
    return pl.pallas_call(
        _xattn_kernel,
        grid=(b, l // tm),
        in_specs=[tok, mems] + [_layer_spec(c, layer) for c in (wk, wv, wq, wo, g, bb)],
        out_specs=tok,
        out_shape=jax.ShapeDtypeStruct((b, l, d), F32),
        scratch_shapes=[pltpu.VMEM((d, d), BF16), pltpu.VMEM((d, d), BF16),
                        pltpu.VMEM((m, d), BF16), pltpu.VMEM((m, d), BF16)],
        compiler_params=_params(("arbitrary", "arbitrary")),
        name="xattn",
    )(h, mem, wk, wv, wq, wo, g, bb)


def _mlp_kernel(h_ref, w1_ref, w2_ref, g_ref, b_ref, o_ref):
    hb = h_ref[...]
    h16 = hb.astype(BF16)
    acc = ALPHA * hb
    for f in range(D_FF // MLP_FC):
        sl = slice(f * MLP_FC, (f + 1) * MLP_FC)
        t = jnp.maximum(jnp.dot(h16, w1_ref[:, sl], preferred_element_type=F32), 0.0)
        acc = acc + jnp.dot((t * t).astype(BF16), w2_ref[sl, :], preferred_element_type=F32)
    o_ref[...] = _layer_norm(acc, g_ref[...], b_ref[...])


def _mlp(h2d, w1, w2, g, bb, layer):
    t, d = h2d.shape
    tm = min(MLP_TM, t)
    tok = pl.BlockSpec((tm, d), lambda i: (i, 0))
    return pl.pallas_call(
        _mlp_kernel,
        grid=(t // tm,),
        in_specs=[tok] + [_layer_spec(c, layer) for c in (w1, w2, g, bb)],
        out_specs=tok,
        out_shape=jax.ShapeDtypeStruct((t, d), F32),
        compiler_params=_params(("parallel",)),
        name="mlp",
    )(h2d, w1, w2, g, bb)


def _rows(v):
    return v.astype(F32).reshape(v.shape[0], 1, -1)


def _pad_lanes(v3d):
    return jnp.pad(v3d, ((0, 0), (0, 0), (0, LANES - v3d.shape[2])))


def prepare_params(seq_len, w_in, w_out, ssd_conv_w, ssd_conv_b, ssd_dt_bias, ssd_a_log, ssd_d, ssd_norm_w,
                   s5_lam_re, s5_lam_im, s5_log_step, s5_b_re, s5_b_im, s5_c_re, s5_c_im, s5_d, s5_glu_w, s5_glu_b,
                   rg_conv_w, rg_conv_b, rg_wa, rg_ba, rg_wx, rg_bx, rg_lambda, ln1_g, ln1_b,
                   xa_wq, xa_wk, xa_wv, xa_wo, ln2_g, ln2_b, mlp_w1, mlp_w2, ln3_g, ln3_b):
    o_u = SSD_WIDTH + SSD_XBC + SSD_HEADS
    n_tiles = min(S5_TB, seq_len) // (S5_BLK * SUBLANES)
    assert n_tiles <= 2 ** (SUBLANES - 3), "S5 scan-level table holds 3 in-tile + 5 tile levels"
    s5_ops = jax.vmap(functools.partial(_s5_prepare, n_tiles=n_tiles))(
        s5_lam_re, s5_lam_im, s5_log_step, s5_b_re, s5_b_im, s5_c_re, s5_c_im)
    s5 = (w_in[:, :, o_u:o_u + S5_WIDTH].astype(BF16), *s5_ops, _rows(s5_d), s5_glu_w.astype(BF16), _rows(s5_glu_b))
    mix_small = dict(
        cw=ssd_conv_w.astype(F32), cb=_rows(ssd_conv_b), dtb=_pad_lanes(_rows(ssd_dt_bias)),
        aneg=_pad_lanes(_rows(-jnp.exp(ssd_a_log.astype(F32)))),
        dvec=_rows(jnp.repeat(ssd_d.astype(F32), SSD_HEAD_DIM, axis=1)), nw=_rows(ssd_norm_w),
        rcw=rg_conv_w.astype(F32), rcb=_rows(rg_conv_b),
        wa=_block_diag(rg_wa.astype(F32)).astype(BF16), ba=_rows(rg_ba.reshape(rg_ba.shape[0], -1)),
        wx=_block_diag(rg_wx.astype(F32)).astype(BF16), bx=_rows(rg_bx.reshape(rg_bx.shape[0], -1)),
        rlam=_rows(rg_lambda), g=_rows(ln1_g), b=_rows(ln1_b))
    mix = (w_in.astype(F32), w_out.astype(F32), tuple(mix_small[f] for f in _MixW._fields if f not in _MIX_W16))
    xa = (xa_wk.astype(F32), xa_wv.astype(F32), xa_wq.astype(F32), xa_wo.astype(F32), _rows(ln2_g), _rows(ln2_b))
    mlp = (mlp_w1.astype(BF16), mlp_w2.astype(BF16), _rows(ln3_g), _rows(ln3_b))
    return dict(s5=s5, mix=mix, xa=xa, mlp=mlp)


def mixer_sublayer(h, prm, layer):
    ys5 = _s5_branch(h, prm["s5"], layer)
    w_in, w_out, consts = prm["mix"]
    return _mixer(h, ys5, w_in, w_out, consts, layer)


def xattn_sublayer(h, mem, prm, layer):
    return _xattn(h, mem, *prm["xa"], layer)


def mlp_sublayer(h, prm, layer):
    b, l, d = h.shape
    return _mlp(h.reshape(b * l, d), *prm["mlp"], layer).reshape(b, l, d)


def kernel(x, mem, w_in, w_out, ssd_conv_w, ssd_conv_b, ssd_dt_bias, ssd_a_log, ssd_d, ssd_norm_w, s5_lam_re, s5_lam_im, s5_log_step, s5_b_re, s5_b_im, s5_c_re, s5_c_im, s5_d, s5_glu_w, s5_glu_b, rg_conv_w, rg_conv_b, rg_wa, rg_ba, rg_wx, rg_bx, rg_lambda, ln1_g, ln1_b, xa_wq, xa_wk, xa_wv, xa_wo, ln2_g, ln2_b, mlp_w1, mlp_w2, ln3_g, ln3_b):
    h = x.astype(F32)
    memf = mem.astype(F32)
    prm = prepare_params(x.shape[1], w_in, w_out, ssd_conv_w, ssd_conv_b, ssd_dt_bias, ssd_a_log, ssd_d, ssd_norm_w,
                         s5_lam_re, s5_lam_im, s5_log_step, s5_b_re, s5_b_im, s5_c_re, s5_c_im, s5_d, s5_glu_w,
                         s5_glu_b, rg_conv_w, rg_conv_b, rg_wa, rg_ba, rg_wx, rg_bx, rg_lambda, ln1_g, ln1_b,
                         xa_wq, xa_wk, xa_wv, xa_wo, ln2_g, ln2_b, mlp_w1, mlp_w2, ln3_g, ln3_b)
    for layer in range(DEPTH):
        h = mixer_sublayer(h, prm, layer)
        h = xattn_sublayer(h, memf, prm, layer)
        h = mlp_sublayer(h, prm, layer)
    return h.astype(x.dtype)
```

```python
import collections
import functools
import math

import jax
import jax.numpy as jnp
from jax import lax
from jax.experimental import pallas as pl
from jax.experimental.pallas import tpu as pltpu

F32 = jnp.float32
BF16 = jnp.bfloat16

D_MODEL = 1024
DEPTH = 2
SSD_WIDTH = 512
SSD_HEAD_DIM = 64
SSD_HEADS = 8
SSD_GROUPS = 2
SSD_HPG = SSD_HEADS // SSD_GROUPS
SSD_GW = SSD_HPG * SSD_HEAD_DIM
SSD_STATE = 128
SSD_XBC = 1024
CONV_K = 4
S5_WIDTH = 256
S5_GROUPS = 16
S5_GROUP_CH = 16
S5_STATE = 64
S5_NSTATE = S5_GROUPS * S5_STATE
S5_BLK = 8
RG_WIDTH = 256
RG_BLOCKS = 4
RG_C = 8.0
XA_HEADS = 4
XA_HEAD_DIM = 256
D_FF = 4096
ALPHA = (2.0 * DEPTH) ** 0.25
LN_EPS = 1e-5

LANES = 128
SUBLANES = 8
MXU_TILE = 256
NEG_BIG = -1e30
VMEM_LIMIT = 56 * 1024 * 1024

SSD_CHUNK = 128
MIX_TQ = 256
S5_TB = 2048
TOK_TM = 1024
MLP_TM = 1024
MLP_FC = 1024


def _dot(a, b):
    return jnp.dot(a.astype(BF16), b.astype(BF16), preferred_element_type=F32)


def _dot_nt(a, b):
    return lax.dot_general(a.astype(BF16), b.astype(BF16), (((1,), (1,)), ((), ())),
                           preferred_element_type=F32)


def _layer_norm(x, g, b):
    mu = jnp.mean(x, axis=-1, keepdims=True)
    xc = x - mu
    var = jnp.mean(xc * xc, axis=-1, keepdims=True)
    return xc * lax.rsqrt(var + LN_EPS) * g + b


def _gelu(x):
    c = math.sqrt(2.0 / math.pi)
    return 0.5 * x * (1.0 + jnp.tanh(c * (x + 0.044715 * (x * x * x))))


def _sigmoid(x):
    return 0.5 + 0.5 * jnp.tanh(0.5 * x)


def _silu(x):
    hx = 0.5 * x
    return hx + hx * jnp.tanh(hx)


def _softplus(x):
    return jnp.maximum(x, 0.0) + jnp.log1p(jnp.exp(-jnp.abs(x)))


def _const_spec(shape):
    nd = len(shape)
    return pl.BlockSpec(shape, lambda *_: (0,) * nd, pipeline_mode=pl.Buffered(1))


def _layer_spec(stacked, layer):
    nd = stacked.ndim - 1
    return pl.BlockSpec((None,) + stacked.shape[1:], lambda *_: (layer,) + (0,) * nd, pipeline_mode=pl.Buffered(1))


def _params(sem):
    return pltpu.CompilerParams(dimension_semantics=sem, vmem_limit_bytes=VMEM_LIMIT)


def _re_lanes(p):
    return slice(2 * p * LANES, (2 * p + 1) * LANES)


def _im_lanes(p):
    return slice((2 * p + 1) * LANES, (2 * p + 2) * LANES)


def _s5_state_group(idx):
    state = (idx // (2 * LANES)) * LANES + idx % LANES
    return state // S5_STATE


def _s5_split(x):
    planes = S5_NSTATE // LANES
    return (jnp.concatenate([x[:, _re_lanes(p)] for p in range(planes)], axis=-1),
            jnp.concatenate([x[:, _im_lanes(p)] for p in range(planes)], axis=-1))


S5_HALF_CH = S5_WIDTH // 2
S5_HALF_LANES = S5_BLK * S5_HALF_CH


def _s5_expand_operators(wc_ref, vc_ref, ct_ref, ms_scr, toep_scr, mo_scr):
    n, w, c = S5_NSTATE, S5_WIDTH, S5_GROUP_CH
    hw, hl = S5_HALF_CH, S5_HALF_LANES
    grp_half = S5_GROUPS // 2
    reps = hw // c
    grp_r = _s5_state_group(lax.broadcasted_iota(jnp.int32, (2 * n, w), 0))
    same_o = grp_r == lax.broadcasted_iota(jnp.int32, (2 * n, w), 1) // c
    mo0 = jnp.where(same_o, ct_ref[...], 0.0)
    kc = jnp.dot(wc_ref[...], mo0, preferred_element_type=F32, precision=lax.Precision.HIGHEST)
    toep_scr[...] = jnp.zeros(toep_scr.shape, toep_scr.dtype)
    ch_grp = lax.broadcasted_iota(jnp.int32, (hw, hl), 0) // c
    same_k = (lax.broadcasted_iota(jnp.int32, (hw, hw), 0) // c) == (lax.broadcasted_iota(jnp.int32, (hw, hw), 1) // c)
    pick_r = lax.broadcasted_iota(jnp.int32, (S5_BLK * c, hl), 0)
    pick_c = lax.broadcasted_iota(jnp.int32, (S5_BLK * c, hl), 1)
    pick = ((pick_r // c == pick_c // hw) & (pick_r % c == pick_c % c)).astype(BF16)
    for hf in range(2):
        lanes = slice(hf * hl, (hf + 1) * hl)
        st_grp = _s5_state_group(lax.broadcasted_iota(jnp.int32, (hw, hl), 1) + hf * hl) - hf * grp_half
        same_s = ch_grp == st_grp
        for t in range(S5_BLK):
            blk = jnp.where(same_s, jnp.tile(wc_ref[t * c:(t + 1) * c, lanes], (reps, 1)), 0.0)
            ms_scr[hf, t * hw:(t + 1) * hw, :] = blk.astype(BF16)
        st_grp = _s5_state_group(lax.broadcasted_iota(jnp.int32, (hl, hl), 0) + hf * hl) - hf * grp_half
        same_r = st_grp == (lax.broadcasted_iota(jnp.int32, (hl, hl), 1) % hw) // c
        spread = jnp.dot(vc_ref[lanes, :].astype(BF16), pick, preferred_element_type=F32)
        mo_scr[hf] = jnp.where(same_r, spread, 0.0).astype(BF16)
        for t_in in range(S5_BLK):
            for t_out in range(t_in, S5_BLK):
                k = S5_BLK - 1 - (t_out - t_in)
                blk = jnp.where(same_k, jnp.tile(kc[k * c:(k + 1) * c, hf * hw:(hf + 1) * hw], (reps, 1)), 0.0)
                toep_scr[hf, t_in * hw:(t_in + 1) * hw, t_out * hw:(t_out + 1) * hw] = blk.astype(BF16)


_S5W = collections.namedtuple("_S5W", "wu wc vc ct lvl pw pwt d gluw glub")
_S5Ops = collections.namedtuple("_S5Ops", "ms toep mo")
_S5Buf = collections.namedtuple("_S5Buf", "u32 u16 hprev16")
_S5Scratch = collections.namedtuple("_S5Scratch", "u_tok hplane cexp carry yg")


def _s5_scan_stage(h_ref, w, ops, scr, out, first, rows, side):
    n, wd = S5_NSTATE, S5_WIDTH
    lane_halves = wd // LANES
    planes = n // LANES
    nt = rows // SUBLANES
    tb = rows * S5_BLK
    for r0 in range(0, tb, tb // 4):
        rs = slice(r0, r0 + tb // 4)
        u_tok = jnp.dot(h_ref[rs, :].astype(BF16), w.wu[...], preferred_element_type=F32)
        for hf in range(lane_halves):
            scr.u_tok[hf, rs, :] = u_tok[:, hf * LANES:(hf + 1) * LANES]
    u = jnp.concatenate([scr.u_tok[hf, pl.ds(k, rows, stride=S5_BLK), :]
                         for hf in range(lane_halves) for k in range(S5_BLK)], axis=-1)
    out.u32[...] = u
    u16 = u.astype(BF16)
    out.u16[...] = u16
    row8 = lax.broadcasted_iota(jnp.int32, (nt, SUBLANES, LANES), 1)
    planes_per_half = planes // lane_halves
    for p in range(planes):
        hf, pl_in_half = divmod(p, planes_per_half)
        s = jnp.dot(u16[:, hf * S5_HALF_LANES:(hf + 1) * S5_HALF_LANES],
                    ops.ms[hf, :, 2 * pl_in_half * LANES:(2 * pl_in_half + 2) * LANES],
                    preferred_element_type=F32)
        if p < planes - 2:
            side(1)
        xr = s[:, :LANES].reshape(nt, SUBLANES, LANES)
        xi = s[:, LANES:].reshape(nt, SUBLANES, LANES)
        for lv, dist in enumerate((1, 2, 4)):
            lr = w.lvl[lv:lv + 1, _re_lanes(p)]
            li = w.lvl[lv:lv + 1, _im_lanes(p)]
            keep = row8 >= dist
            sr = jnp.where(keep, pltpu.roll(xr, dist, 1), 0.0)
            si = jnp.where(keep, pltpu.roll(xi, dist, 1), 0.0)
            xr, xi = xr + lr * sr - li * si, xi + lr * si + li * sr
        scr.hplane[2 * p, SUBLANES:SUBLANES + rows, :] = xr.reshape(rows, LANES)
        scr.hplane[2 * p + 1, SUBLANES:SUBLANES + rows, :] = xi.reshape(rows, LANES)
    side(1)
    last = 2 * SUBLANES - 1
    er = jnp.concatenate([scr.hplane[2 * p, pl.ds(last, nt, stride=SUBLANES), :] for p in range(planes)], axis=-1)
    ei = jnp.concatenate([scr.hplane[2 * p + 1, pl.ds(last, nt, stride=SUBLANES), :] for p in range(planes)], -1)
    tile_id = lax.broadcasted_iota(jnp.int32, (nt, n), 0)
    lv, dist = 3, 1
    while dist < nt:
        lr, li = _s5_split(w.lvl[lv:lv + 1, :])
        keep = tile_id >= dist
        sr = jnp.where(keep, pltpu.roll(er, dist, 0), 0.0)
        si = jnp.where(keep, pltpu.roll(ei, dist, 0), 0.0)
        er, ei = er + lr * sr - li * si, ei + lr * si + li * sr
        lv, dist = lv + 1, dist * 2
    cr, ci = _s5_split(jnp.where(first, 0.0, scr.carry[0:1, :]))
    ptr, pti = _s5_split(w.pwt[...])
    er, ei = er + ptr * cr - pti * ci, ei + ptr * ci + pti * cr
    for p in range(planes):
        scr.carry[0:1, _re_lanes(p)] = er[nt - 1:nt, p * LANES:(p + 1) * LANES]
        scr.carry[0:1, _im_lanes(p)] = ei[nt - 1:nt, p * LANES:(p + 1) * LANES]
    tin_r = jnp.where(tile_id >= 1, pltpu.roll(er, 1, 0), cr)
    tin_i = jnp.where(tile_id >= 1, pltpu.roll(ei, 1, 0), ci)
    side(1)
    for p in range(planes):
        for k in range(SUBLANES):
            scr.cexp[2 * p, pl.ds(k, nt, stride=SUBLANES), :] = tin_r[:, p * LANES:(p + 1) * LANES]
            scr.cexp[2 * p + 1, pl.ds(k, nt, stride=SUBLANES), :] = tin_i[:, p * LANES:(p + 1) * LANES]
    for p in range(planes):
        if p % 2 == 0:
            side(1)
        pr = jnp.tile(w.pw[:, _re_lanes(p)], (nt, 1))
        pi = jnp.tile(w.pw[:, _im_lanes(p)], (nt, 1))
        tr = scr.cexp[2 * p]
        ti = scr.cexp[2 * p + 1]
        hr = scr.hplane[2 * p, SUBLANES:SUBLANES + rows, :] + pr * tr - pi * ti
        hi = scr.hplane[2 * p + 1, SUBLANES:SUBLANES + rows, :] + pr * ti + pi * tr
        scr.hplane[2 * p, SUBLANES:SUBLANES + rows, :] = hr
        scr.hplane[2 * p + 1, SUBLANES:SUBLANES + rows, :] = hi
        scr.hplane[2 * p, SUBLANES - 1:SUBLANES, :] = cr[:, p * LANES:(p + 1) * LANES]
        scr.hplane[2 * p + 1, SUBLANES - 1:SUBLANES, :] = ci[:, p * LANES:(p + 1) * LANES]
    for p in range(2 * planes):
        out.hprev16[:, p * LANES:(p + 1) * LANES] = scr.hplane[p, SUBLANES - 1:SUBLANES - 1 + rows, :].astype(BF16)


def _s5_readout_jobs(src, w, ops, scr, o_ref, rows):
    hw, hl = S5_HALF_CH, S5_HALF_LANES
    lane_halves = S5_WIDTH // LANES
    steps_per_tile = MXU_TILE // hw

    def tile_job(hf, j):
        def run():
            k_rows = (j + 1) * MXU_TILE
            cols = slice(j * MXU_TILE, (j + 1) * MXU_TILE)
            y = jnp.dot(src.u16[:, hf * hl:hf * hl + k_rows], ops.toep[hf, :k_rows, cols],
                        preferred_element_type=F32)
            y = y + jnp.dot(src.hprev16[:, hf * hl:(hf + 1) * hl], ops.mo[hf, :, cols], preferred_element_type=F32)
            d_half = w.d[:, hf * hw:(hf + 1) * hw]
            y = _gelu(y + jnp.concatenate([d_half] * steps_per_tile, axis=1) * src.u32[:, hf * hl + j * MXU_TILE:
                                                                                           hf * hl + (j + 1) * MXU_TILE])
            for i in range(steps_per_tile):
                scr.yg[hf, pl.ds(j * steps_per_tile + i, rows, stride=S5_BLK), :] = y[:, i * hw:(i + 1) * hw]
        return run

    def glu_job():
        yg = jnp.concatenate([scr.yg[hf] for hf in range(lane_halves)], axis=-1)
        glu = _sigmoid(_dot(yg, w.gluw[...]) + w.glub[...])
        o_ref[...] = (yg * glu).astype(o_ref.dtype)

    return [tile_job(hf, j) for hf in range(lane_halves) for j in range(hl // MXU_TILE)] + [glu_job]


def _s5_kernel(h_ref, *rest, rows, nblk):
    nw = len(_S5W._fields)
    w = _S5W(*rest[:nw])
    o_ref = rest[nw]
    bufs = rest[nw + 1:]
    ops = _S5Ops(*bufs[:3])
    nb = len(_S5Buf._fields)
    scr = _S5Scratch(*bufs[3 + nb:])
    s = pl.program_id(0)
    parity = s % 2

    @pl.when(s == 0)
    def _():
        _s5_expand_operators(w.wc, w.vc, w.ct, *ops)
        for ref in bufs[3:]:
            ref[...] = jnp.zeros(ref.shape, ref.dtype)

    first = (s % nblk) == 0

    def step(out, src):
        jobs = _s5_readout_jobs(src, w, ops, scr, o_ref, rows)

        def side(k):
            for _ in range(min(k, len(jobs))):
                jobs.pop(0)()

        _s5_scan_stage(h_ref, w, ops, scr, out, first, rows, side)
        side(len(jobs))

    step(_S5Buf(*(r.at[parity] for r in bufs[3:3 + nb])), _S5Buf(*(r.at[1 - parity] for r in bufs[3:3 + nb])))


def _s5_branch(h, consts, layer):
    b, l, d = h.shape
    tb = min(S5_TB, l)
    rows = tb // S5_BLK
    nblk = l // tb
    total = b * nblk
    n, w = S5_NSTATE, S5_WIDTH
    handoff = [
        pltpu.VMEM((2, rows, S5_BLK * w), F32),
        pltpu.VMEM((2, rows, S5_BLK * w), BF16),
        pltpu.VMEM((2, rows, 2 * n), BF16),
    ]
    out = pl.pallas_call(
        functools.partial(_s5_kernel, rows=rows, nblk=nblk),
        grid=(total + 1,),
        in_specs=[pl.BlockSpec((tb, d), lambda s: (jnp.minimum(s, total - 1), 0))]
                 + [_layer_spec(c, layer) for c in consts],
        out_specs=pl.BlockSpec((tb, w), lambda s: (jnp.maximum(s - 1, 0), 0)),
        out_shape=jax.ShapeDtypeStruct((b * l, w), BF16),
        scratch_shapes=[
            pltpu.VMEM((2, S5_HALF_LANES, S5_HALF_LANES), BF16),
            pltpu.VMEM((2, S5_HALF_LANES, S5_HALF_LANES), BF16),
            pltpu.VMEM((2, S5_HALF_LANES, S5_HALF_LANES), BF16),
        ] + handoff + [
            pltpu.VMEM((w // LANES, tb, LANES), F32),
            pltpu.VMEM((2 * n // LANES, rows + SUBLANES, LANES), F32),
            pltpu.VMEM((2 * n // LANES, rows, LANES), F32),
            pltpu.VMEM((SUBLANES, 2 * n), F32),
            pltpu.VMEM((w // LANES, tb, LANES), F32),
        ],
        compiler_params=_params(("arbitrary",)),
        name="s5_branch",
    )(h.reshape(b * l, d), *consts)
    return out.reshape(b, l, w)


def _s5_prepare(lam_re, lam_im, log_step, b_re, b_im, c_re, c_im, n_tiles):
    g, p, c, t8 = S5_GROUPS, S5_STATE, S5_GROUP_CH, S5_BLK
    lr, li = lam_re.astype(F32), lam_im.astype(F32)
    dt = jnp.exp(log_step.astype(F32))[:, None]
    ar, ai = lr * dt, li * dt

    def lam_pow(k):
        k = k.astype(F32)[..., None, None]
        mag = jnp.exp(k * ar)
        return mag * jnp.cos(k * ai), mag * jnp.sin(k * ai)

    br1, bi1 = lam_pow(jnp.ones((), jnp.int32))
    xr_, xi_ = br1 - 1.0, bi1
    den = lr * lr + li * li
    cr, ci = (xr_ * lr + xi_ * li) / den, (xi_ * lr - xr_ * li) / den
    bre, bim = b_re.astype(F32), b_im.astype(F32)
    bbr = cr[..., None] * bre - ci[..., None] * bim
    bbi = cr[..., None] * bim + ci[..., None] * bre
    steps = jnp.arange(t8)
    pr, pi = lam_pow(t8 - 1 - steps)
    wsr = pr[..., None] * bbr[None] - pi[..., None] * bbi[None]
    wsi = pr[..., None] * bbi[None] + pi[..., None] * bbr[None]

    def interleave(re, im, axis):
        shape = list(re.shape)
        shape[axis:axis + 1] = [re.shape[axis] // LANES, 1, LANES]
        both = jnp.concatenate([re.reshape(shape), im.reshape(shape)], axis=axis + 1)
        shape[axis:axis + 3] = [2 * re.shape[axis]]
        return both.reshape(shape)

    wc = interleave(*[part.transpose(0, 3, 1, 2).reshape(t8 * c, g * p) for part in (wsr, wsi)], axis=1)
    cre, cim = c_re.astype(F32), c_im.astype(F32)
    pr, pi = lam_pow(steps + 1)
    vor = cre[None] * pr[:, :, None, :] - cim[None] * pi[:, :, None, :]
    voi = cre[None] * pi[:, :, None, :] + cim[None] * pr[:, :, None, :]
    vc = interleave(*[part.transpose(1, 3, 0, 2).reshape(g * p, t8 * c) for part in (vor, -voi)], axis=0)
    ct = interleave(*[jnp.tile(part.transpose(0, 2, 1).reshape(g * p, c), (1, g)) for part in (cre, -cim)], axis=0)

    def re_im(pair):
        return interleave(pair[0].reshape(-1, g * p), pair[1].reshape(-1, g * p), axis=1)

    lvl = re_im(lam_pow(t8 * 2 ** jnp.arange(SUBLANES)))
    pw = re_im(lam_pow(t8 * (steps + 1)))
    pwt = re_im(lam_pow(t8 * SUBLANES * (jnp.arange(n_tiles) + 1)))
    return wc, vc, ct, lvl, pw, pwt


def _expand_heads(cols, grp, head_lane):
    base = SSD_HPG * grp
    out = jnp.broadcast_to(cols[:, base:base + 1], (cols.shape[0], SSD_GW))
    for hh in range(1, SSD_HPG):
        out = jnp.where(head_lane == hh, cols[:, base + hh:base + hh + 1], out)
    return out


def _causal_conv(pad_ref, w_ref, b_ref, rows, cols):
    acc = b_ref[:, cols] + w_ref[CONV_K - 1:CONV_K, cols] * pad_ref[SUBLANES:SUBLANES + rows, cols]
    for k in range(CONV_K - 1):
        off = SUBLANES - (CONV_K - 1) + k
        acc = acc + w_ref[k:k + 1, cols] * pad_ref[off:off + rows, cols]
    return acc


_ProjBuf = collections.namedtuple("_ProjBuf", "z xpad dt rpad gr")
_MixW = collections.namedtuple(
    "_MixW", "wz wxbc wdt wxr wgr cw cb dtb aneg dvec nw rcw rcb wa ba wx bx rlam wout g b")
_MixScratch = collections.namedtuple("_MixScratch", "state ascr bscr cexp rcar")


def _col_tiles(width):
    return [slice(c, min(c + MXU_TILE, width)) for c in range(0, width, MXU_TILE)]


def _mixer_output_jobs(h_ref, ycat, w, o_ref):
    def tile_job(cols):
        def run():
            y = jnp.dot(ycat[...], w.wout[:, cols], preferred_element_type=F32)
            o_ref[:, cols] = ALPHA * h_ref[:, cols] + y
        return run

    def norm_job():
        o_ref[...] = _layer_norm(o_ref[...], w.g[...], w.b[...])

    return [tile_job(cols) for cols in _col_tiles(D_MODEL)] + [norm_job]


def _mixer_project_jobs(h_ref, h16, w, nxt, cur, first, tq):
    def cast_job():
        h16[...] = h_ref[...].astype(BF16)

    def tile_job(dst, w_ref, cols, row0):
        def run():
            dst[row0:row0 + tq, cols] = jnp.dot(h16[...], w_ref[:, cols], preferred_element_type=F32)
        return run

    def tail_job(pad_n, pad_c):
        def run():
            pad_n[0:SUBLANES, :] = jnp.where(first, 0.0, pad_c[tq:tq + SUBLANES, :])
        return run

    jobs = [cast_job]
    for dst, w_ref, row0 in ((nxt.z, w.wz, 0), (nxt.xpad, w.wxbc, SUBLANES), (nxt.dt, w.wdt, 0),
                             (nxt.rpad, w.wxr, SUBLANES), (nxt.gr, w.wgr, 0)):
        jobs += [tile_job(dst, w_ref, cols, row0) for cols in _col_tiles(dst.shape[1])]
    jobs += [tail_job(nxt.xpad, cur.xpad), tail_job(nxt.rpad, cur.rpad)]
    return jobs


def _mixer_compute(cur, ys5_ref, w, scr, ycat, first, tq, chunk, side):
    cw_ref, cb_ref, dtb_ref, aneg_ref, dvec_ref, nw_ref = w.cw, w.cb, w.dtb, w.aneg, w.dvec, w.nw
    state = scr.state
    z = cur.z
    slabs = []
    for cols in _col_tiles(SSD_XBC):
        side(1)
        slabs.append(_silu(_causal_conv(cur.xpad, cw_ref, cb_ref, tq, cols)))
    xact = jnp.concatenate(slabs, axis=-1)
    side(2)
    dt_all = _softplus(cur.dt[...] + dtb_ref[...])
    q = chunk
    tri = (lax.broadcasted_iota(jnp.int32, (q, q), 0) >= lax.broadcasted_iota(jnp.int32, (q, q), 1))
    row_id = lax.broadcasted_iota(jnp.int32, (q, LANES), 0)
    head_lane = lax.broadcasted_iota(jnp.int32, (1, SSD_GW), 1) // SSD_HEAD_DIM
    for ci in range(tq // q):
        r0 = ci * q
        dt = dt_all[r0:r0 + q, :]
        cs = dt * aneg_ref[...]
        dist = 1
        while dist < q:
            cs = cs + jnp.where(row_id >= dist, pltpu.roll(cs, dist, 0), 0.0)
            dist *= 2
        cs_t = cs.T
        cs_last = cs[q - 1:q, :]
        chunk_decay = jnp.exp(cs_last)
        dec_t = jnp.exp(cs_t[:, q - 1:q] - cs_t)
        ecs = jnp.exp(cs)
        zc = z[r0:r0 + q, :]
        halves = []
        for grp in range(SSD_GROUPS):
            side(2)
            lo = grp * SSD_GW
            xs = xact[r0:r0 + q, lo:lo + SSD_GW]
            bm = xact[r0:r0 + q, SSD_WIDTH + grp * SSD_STATE:SSD_WIDTH + (grp + 1) * SSD_STATE]
            cm = xact[r0:r0 + q, SSD_WIDTH + (SSD_GROUPS + grp) * SSD_STATE:
                      SSD_WIDTH + (SSD_GROUPS + grp + 1) * SSD_STATE]
            cm16 = cm.astype(BF16)
            xdt16 = (xs * _expand_heads(dt, grp, head_lane)).astype(BF16)
            cb = _dot_nt(cm16, bm)
            bm_t = bm.T
            s_prev = state[grp]
            if ci == 0:
                s_prev = jnp.where(first, 0.0, s_prev)
            y_off = jnp.dot(cm16, s_prev.astype(BF16), preferred_element_type=F32)
            intra, to_state, xdt_heads = [], [], []
            for hh in range(SSD_HPG):
                hd = grp * SSD_HPG + hh
                seg = cs[:, hd:hd + 1] - cs_t[hd:hd + 1, :]
                lmat = jnp.exp(jnp.where(tri, seg, NEG_BIG))
                intra.append((cb * lmat).astype(BF16))
                to_state.append((bm_t * dec_t[hd:hd + 1, :]).astype(BF16))
                xdt_heads.append(jnp.where(head_lane == hh, xdt16, jnp.zeros_like(xdt16)))
            lhs = jnp.concatenate([jnp.concatenate(intra, axis=1), jnp.concatenate(to_state, axis=1)], axis=0)
            both = jnp.dot(lhs, jnp.concatenate(xdt_heads, axis=0), preferred_element_type=F32)
            y_diag = both[:q]
            st_new = both[q:]
            state[grp] = s_prev * _expand_heads(chunk_decay, grp, head_lane) + st_new
            halves.append(y_diag + y_off * _expand_heads(ecs, grp, head_lane) + xs * dvec_ref[:, lo:lo + SSD_GW])
        y = jnp.concatenate(halves, axis=-1) * _silu(zc)
        y = y * lax.rsqrt(jnp.mean(y * y, axis=-1, keepdims=True) + LN_EPS) * nw_ref[...]
        ycat[r0:r0 + q, 0:SSD_WIDTH] = y.astype(BF16)

    ycat[:, SSD_WIDTH:SSD_WIDTH + S5_WIDTH] = ys5_ref[...]
    _mixer_rglru(cur, w, scr, ycat, first, tq, side)


def _mixer_rglru(cur, w, scr, ycat, first, tq, side):
    side(1)
    xc = _causal_conv(cur.rpad, w.rcw, w.rcb, tq, slice(0, RG_WIDTH))
    xc16 = xc.astype(BF16)
    rgate = _sigmoid(jnp.dot(xc16, w.wa[...], preferred_element_type=F32) + w.ba[...])
    igate = _sigmoid(jnp.dot(xc16, w.wx[...], preferred_element_type=F32) + w.bx[...])
    log_a = (-RG_C) * rgate * _softplus(-w.rlam[...])
    a = jnp.exp(log_a)
    bt = jnp.sqrt(1.0 - jnp.exp(2.0 * log_a)) * (igate * xc)
    row8 = lax.broadcasted_iota(jnp.int32, (tq, RG_WIDTH), 0) & (SUBLANES - 1)
    side(1)
    for dist in (1, 2, 4):
        keep = row8 >= dist
        a_sh = jnp.where(keep, pltpu.roll(a, dist, 0), 1.0)
        b_sh = jnp.where(keep, pltpu.roll(bt, dist, 0), 0.0)
        bt = bt + a * b_sh
        a = a * a_sh
    nt = tq // SUBLANES
    lane_halves = RG_WIDTH // LANES
    for hf in range(lane_halves):
        scr.ascr[hf] = a[:, hf * LANES:(hf + 1) * LANES]
        scr.bscr[hf] = bt[:, hf * LANES:(hf + 1) * LANES]
    ae = jnp.concatenate([scr.ascr[hf, pl.ds(SUBLANES - 1, nt, stride=SUBLANES), :] for hf in range(lane_halves)], -1)
    be = jnp.concatenate([scr.bscr[hf, pl.ds(SUBLANES - 1, nt, stride=SUBLANES), :] for hf in range(lane_halves)], -1)
    tile_id = lax.broadcasted_iota(jnp.int32, (nt, RG_WIDTH), 0)
    dist = 1
    while dist < nt:
        keep = tile_id >= dist
        a_sh = jnp.where(keep, pltpu.roll(ae, dist, 0), 1.0)
        b_sh = jnp.where(keep, pltpu.roll(be, dist, 0), 0.0)
        be = be + ae * b_sh
        ae = ae * a_sh
        dist *= 2
    carry_in = jnp.where(first, 0.0, scr.rcar[0:1, :])
    tile_end = be + ae * carry_in
    scr.rcar[0:1, :] = tile_end[nt - 1:nt, :]
    tile_in = jnp.where(tile_id >= 1, pltpu.roll(tile_end, 1, 0), carry_in)
    for hf in range(lane_halves):
        for k in range(SUBLANES):
            scr.cexp[hf, pl.ds(k, nt, stride=SUBLANES), :] = tile_in[:, hf * LANES:(hf + 1) * LANES]
    hscan = bt + a * jnp.concatenate([scr.cexp[hf] for hf in range(lane_halves)], axis=-1)
    ycat[:, SSD_WIDTH + S5_WIDTH:] = (hscan * _gelu(cur.gr[...])).astype(BF16)


_MIX_W16 = ("wz", "wxbc", "wdt", "wxr", "wgr", "wout")


def _mixer_load_weights(w_in_ref, w_out_ref, w):
    o_dt = SSD_WIDTH + SSD_XBC
    o_xr = o_dt + SSD_HEADS + S5_WIDTH
    o_gr = o_xr + RG_WIDTH
    w.wz[...] = w_in_ref[:, 0:SSD_WIDTH].astype(BF16)
    w.wxbc[...] = w_in_ref[:, SSD_WIDTH:o_dt].astype(BF16)
    head_cols = lax.broadcasted_iota(jnp.int32, (1, LANES), 1) < SSD_HEADS
    w.wdt[...] = jnp.where(head_cols, w_in_ref[:, o_dt:o_dt + LANES], 0.0).astype(BF16)
    w.wxr[...] = w_in_ref[:, o_xr:o_gr].astype(BF16)
    w.wgr[...] = w_in_ref[:, o_gr:o_gr + RG_WIDTH].astype(BF16)
    w.wout[...] = w_out_ref[...].astype(BF16)


def _mixer_kernel(hp_ref, ho_ref, ys5_ref, w_in_ref, w_out_ref, *rest, tq, chunk, nblk):
    small = [f for f in _MixW._fields if f not in _MIX_W16]
    o_ref = rest[len(small)]
    bufs = rest[len(small) + 1:]
    w16 = bufs[len(bufs) - len(_MIX_W16):]
    bufs = bufs[:len(bufs) - len(_MIX_W16)]
    w = _MixW(**dict(zip(small, rest[:len(small)])), **dict(zip(_MIX_W16, w16)))
    npb = len(_ProjBuf._fields)
    sets = (_ProjBuf(*bufs[:npb]), _ProjBuf(*bufs[npb:2 * npb]))
    ycats = bufs[2 * npb:2 * npb + 2]
    scr = _MixScratch(*bufs[2 * npb + 2:2 * npb + 2 + len(_MixScratch._fields)])
    h16 = bufs[-1]
    s = pl.program_id(0)

    @pl.when(s == 0)
    def _():
        _mixer_load_weights(w_in_ref, w_out_ref, w)
        for ref in bufs:
            ref[...] = jnp.zeros(ref.shape, ref.dtype)

    first_p = (s % nblk) == 0
    first_c = ((s + nblk - 1) % nblk) == 0

    def step(nxt, cur, ycat_c, ycat_o):
        jobs = (_mixer_output_jobs(ho_ref, ycat_o, w, o_ref)
                + _mixer_project_jobs(hp_ref, h16, w, nxt, cur, first_p, tq))

        def side(n):
            for _ in range(min(n, len(jobs))):
                jobs.pop(0)()

        _mixer_compute(cur, ys5_ref, w, scr, ycat_c, first_c, tq, chunk, side)
        side(len(jobs))

    @pl.when(s % 2 == 0)
    def _():
        step(sets[0], sets[1], ycats[1], ycats[0])

    @pl.when(s % 2 == 1)
    def _():
        step(sets[1], sets[0], ycats[0], ycats[1])


def _mixer(h, ys5, w_in, w_out, consts, layer):
    b, l, d = h.shape
    tq = min(MIX_TQ, l)
    chunk = min(SSD_CHUNK, tq)
    nblk = l // tq
    total = b * nblk
    h2d = h.reshape(b * l, d)
    proj_buf = [
        pltpu.VMEM((tq, SSD_WIDTH), F32),
        pltpu.VMEM((tq + SUBLANES, SSD_XBC), F32),
        pltpu.VMEM((tq, LANES), F32),
        pltpu.VMEM((tq + SUBLANES, RG_WIDTH), F32),
        pltpu.VMEM((tq, RG_WIDTH), F32),
    ]
    rg_plane = pltpu.VMEM((RG_WIDTH // LANES, tq, LANES), F32)
    out = pl.pallas_call(
        functools.partial(_mixer_kernel, tq=tq, chunk=chunk, nblk=nblk),
        grid=(total + 2,),
        in_specs=[pl.BlockSpec((tq, d), lambda s: (jnp.minimum(s, total - 1), 0)),
                  pl.BlockSpec((tq, d), lambda s: (jnp.maximum(s - 2, 0), 0)),
                  pl.BlockSpec((tq, S5_WIDTH), lambda s: (jnp.clip(s - 1, 0, total - 1), 0))]
                 + [_layer_spec(c, layer) for c in (w_in, w_out) + tuple(consts)],
        out_specs=pl.BlockSpec((tq, d), lambda s: (jnp.maximum(s - 2, 0), 0)),
        out_shape=jax.ShapeDtypeStruct((b * l, d), F32),
        scratch_shapes=proj_buf + proj_buf + [
            pltpu.VMEM((tq, D_MODEL), BF16),
            pltpu.VMEM((tq, D_MODEL), BF16),
            pltpu.VMEM((SSD_GROUPS, SSD_STATE, SSD_GW), F32),
            rg_plane, rg_plane, rg_plane,
            pltpu.VMEM((SUBLANES, RG_WIDTH), F32),
            pltpu.VMEM((tq, d), BF16),
            pltpu.VMEM((d, SSD_WIDTH), BF16),
            pltpu.VMEM((d, SSD_XBC), BF16),
            pltpu.VMEM((d, LANES), BF16),
            pltpu.VMEM((d, RG_WIDTH), BF16),
            pltpu.VMEM((d, RG_WIDTH), BF16),
            pltpu.VMEM((d, d), BF16),
        ],
        compiler_params=_params(("arbitrary",)),
        name="mixer",
    )(h2d, h2d, ys5.reshape(b * l, S5_WIDTH), w_in, w_out, *consts)
    return out.reshape(b, l, d)


def _block_diag(w):
    nl, nb, bi, bo = w.shape
    return jnp.einsum('lhij,hk->lhikj', w, jnp.eye(nb, dtype=w.dtype)).reshape(nl, nb * bi, nb * bo)


def _xattn_kernel(h_ref, m_ref, wk32_ref, wv32_ref, wq32_ref, wo32_ref, g_ref, b_ref, o_ref,
                  wq_ref, wo_ref, k_ref, v_ref):
    @pl.when((pl.program_id(0) == 0) & (pl.program_id(1) == 0))
    def _():
        wq_ref[...] = wq32_ref[...].astype(BF16)
        wo_ref[...] = wo32_ref[...].astype(BF16)

    @pl.when(pl.program_id(1) == 0)
    def _():
        m16 = m_ref[...].astype(BF16)
        k_ref[...] = jnp.dot(m16, wk32_ref[...].astype(BF16), preferred_element_type=F32).astype(BF16)
        v_ref[...] = jnp.dot(m16, wv32_ref[...].astype(BF16), preferred_element_type=F32).astype(BF16)

    hb = h_ref[...]
    qall = jnp.dot(hb.astype(BF16), wq_ref[...], preferred_element_type=F32) * (1.0 / math.sqrt(XA_HEAD_DIM))
    outs = []
    for hd in range(XA_HEADS):
        sl = slice(hd * XA_HEAD_DIM, (hd + 1) * XA_HEAD_DIM)
        s = _dot_nt(qall[:, sl], k_ref[:, sl])
        p = jnp.exp(s - jnp.max(s, axis=-1, keepdims=True))
        o = jnp.dot(p.astype(BF16), v_ref[:, sl], preferred_element_type=F32)
        outs.append(o / jnp.sum(p, axis=-1, keepdims=True))
    o16 = jnp.concatenate(outs, axis=-1).astype(BF16)
    half = o16.shape[0] // 2
    for rows in (slice(0, half), slice(half, 2 * half)):
        y = jnp.dot(o16[rows], wo_ref[...], preferred_element_type=F32)
        o_ref[rows, :] = _layer_norm(ALPHA * hb[rows] + y, g_ref[...], b_ref[...])


def _xattn(h, mem, wk, wv, wq, wo, g, bb, layer):
    b, l, d = h.shape
    m = mem.shape[1]
    tm = min(TOK_TM, l)
    tok = pl.BlockSpec((None, tm, d), lambda i, j: (i, j, 0))
    mems = pl.BlockSpec((None, m, d), lambda i, j: (i, 0, 0))
    return pl.pallas_call(
        _xattn_kernel,
        grid=(b, l // tm),
        in_specs=[tok, mems] + [_layer_spec(c, layer) for c in (wk, wv, wq, wo, g, bb)],
        out_specs=tok,
        out_shape=jax.ShapeDtypeStruct((b, l, d), F32),
        scratch_shapes=[pltpu.VMEM((d, d), BF16), pltpu.VMEM((d, d), BF16),
                        pltpu.VMEM((m, d), BF16), pltpu.VMEM((m, d), BF16)],
        compiler_params=_params(("arbitrary", "arbitrary")),
        name="xattn",
    )(h, mem, wk, wv, wq, wo, g, bb)


def _mlp_kernel(h_ref, w1_ref, w2_ref, g_ref, b_ref, o_ref):
    hb = h_ref[...]
    h16 = hb.astype(BF16)
    acc = ALPHA * hb
    for f in range(D_FF // MLP_FC):
        sl = slice(f * MLP_FC, (f + 1) * MLP_FC)
        t = jnp.maximum(jnp.dot(h16, w1_ref[:, sl], preferred_element_type=F32), 0.0)
        acc = acc + jnp.dot((t * t).astype(BF16), w2_ref[sl, :], preferred_element_type=F32)
    o_ref[...] = _layer_norm(acc, g_ref[...], b_ref[...])


def _mlp(h2d, w1, w2, g, bb, layer):
    t, d = h2d.shape
    tm = min(MLP_TM, t)
    tok = pl.BlockSpec((tm, d), lambda i: (i, 0))
    return pl.pallas_call(
        _mlp_kernel,
        grid=(t // tm,),
        in_specs=[tok] + [_layer_spec(c, layer) for c in (w1, w2, g, bb)],
        out_specs=tok,
        out_shape=jax.ShapeDtypeStruct((t, d), F32),
        compiler_params=_params(("parallel",)),
        name="mlp",
    )(h2d, w1, w2, g, bb)


def _rows(v):
    return v.astype(F32).reshape(v.shape[0], 1, -1)


def _pad_lanes(v3d):
    return jnp.pad(v3d, ((0, 0), (0, 0), (0, LANES - v3d.shape[2])))


def prepare_params(seq_len, w_in, w_out, ssd_conv_w, ssd_conv_b, ssd_dt_bias, ssd_a_log, ssd_d, ssd_norm_w,
                   s5_lam_re, s5_lam_im, s5_log_step, s5_b_re, s5_b_im, s5_c_re, s5_c_im, s5_d, s5_glu_w, s5_glu_b,
                   rg_conv_w, rg_conv_b, rg_wa, rg_ba, rg_wx, rg_bx, rg_lambda, ln1_g, ln1_b,
                   xa_wq, xa_wk, xa_wv, xa_wo, ln2_g, ln2_b, mlp_w1, mlp_w2, ln3_g, ln3_b):
    o_u = SSD_WIDTH + SSD_XBC + SSD_HEADS
    n_tiles = min(S5_TB, seq_len) // (S5_BLK * SUBLANES)
    assert n_tiles <= 2 ** (SUBLANES - 3), "S5 scan-level table holds 3 in-tile + 5 tile levels"
    s5_ops = jax.vmap(functools.partial(_s5_prepare, n_tiles=n_tiles))(
        s5_lam_re, s5_lam_im, s5_log_step, s5_b_re, s5_b_im, s5_c_re, s5_c_im)
    s5 = (w_in[:, :, o_u:o_u + S5_WIDTH].astype(BF16), *s5_ops, _rows(s5_d), s5_glu_w.astype(BF16), _rows(s5_glu_b))
    mix_small = dict(
        cw=ssd_conv_w.astype(F32), cb=_rows(ssd_conv_b), dtb=_pad_lanes(_rows(ssd_dt_bias)),
        aneg=_pad_lanes(_rows(-jnp.exp(ssd_a_log.astype(F32)))),
        dvec=_rows(jnp.repeat(ssd_d.astype(F32), SSD_HEAD_DIM, axis=1)), nw=_rows(ssd_norm_w),
        rcw=rg_conv_w.astype(F32), rcb=_rows(rg_conv_b),
        wa=_block_diag(rg_wa.astype(F32)).astype(BF16), ba=_rows(rg_ba.reshape(rg_ba.shape[0], -1)),
        wx=_block_diag(rg_wx.astype(F32)).astype(BF16), bx=_rows(rg_bx.reshape(rg_bx.shape[0], -1)),
        rlam=_rows(rg_lambda), g=_rows(ln1_g), b=_rows(ln1_b))
    mix = (w_in.astype(F32), w_out.astype(F32), tuple(mix_small[f] for f in _MixW._fields if f not in _MIX_W16))
    xa = (xa_wk.astype(F32), xa_wv.astype(F32), xa_wq.astype(F32), xa_wo.astype(F32), _rows(ln2_g), _rows(ln2_b))
    mlp = (mlp_w1.astype(BF16), mlp_w2.astype(BF16), _rows(ln3_g), _rows(ln3_b))
    return dict(s5=s5, mix=mix, xa=xa, mlp=mlp)


def mixer_sublayer(h, prm, layer):
    ys5 = _s5_branch(h, prm["s5"], layer)
    w_in, w_out, consts = prm["mix"]
    return _mixer(h, ys5, w_in, w_out, consts, layer)


def xattn_sublayer(h, mem, prm, layer):
    return _xattn(h, mem, *prm["xa"], layer)


def mlp_sublayer(h, prm, layer):
    b, l, d = h.shape
    return _mlp(h.reshape(b * l, d), *prm["mlp"], layer).reshape(b, l, d)


def kernel(x, mem, w_in, w_out, ssd_conv_w, ssd_conv_b, ssd_dt_bias, ssd_a_log, ssd_d, ssd_norm_w, s5_lam_re, s5_lam_im, s5_log_step, s5_b_re, s5_b_im, s5_c_re, s5_c_im, s5_d, s5_glu_w, s5_glu_b, rg_conv_w, rg_conv_b, rg_wa, rg_ba, rg_wx, rg_bx, rg_lambda, ln1_g, ln1_b, xa_wq, xa_wk, xa_wv, xa_wo, ln2_g, ln2_b, mlp_w1, mlp_w2, ln3_g, ln3_b):
    h = x.astype(F32)
    memf = mem.astype(F32)
    prm = prepare_params(x.shape[1], w_in, w_out, ssd_conv_w, ssd_conv_b, ssd_dt_bias, ssd_a_log, ssd_d, ssd_norm_w,
                         s5_lam_re, s5_lam_im, s5_log_step, s5_b_re, s5_b_im, s5_c_re, s5_c_im, s5_d, s5_glu_w,
                         s5_glu_b, rg_conv_w, rg_conv_b, rg_wa, rg_ba, rg_wx, rg_bx, rg_lambda, ln1_g, ln1_b,
                         xa_wq, xa_wk, xa_wv, xa_wo, ln2_g, ln2_b, mlp_w1, mlp_w2, ln3_g, ln3_b)
    for layer in range(DEPTH):
        h = mixer_sublayer(h, prm, layer)
        h = xattn_sublayer(h, memf, prm, layer)
        h = mlp_sublayer(h, prm, layer)
    return h.astype(x.dtype)
```

```python
import collections
import functools
import math

import jax
import jax.numpy as jnp
from jax import lax
from jax.experimental import pallas as pl
from jax.experimental.pallas import tpu as pltpu

F32 = jnp.float32
BF16 = jnp.bfloat16

D_MODEL = 1024
DEPTH = 2
SSD_WIDTH = 512
SSD_HEAD_DIM = 64
SSD_HEADS = 8
SSD_GROUPS = 2
SSD_HPG = SSD_HEADS // SSD_GROUPS
SSD_GW = SSD_HPG * SSD_HEAD_DIM
SSD_STATE = 128
SSD_XBC = 1024
CONV_K = 4
S5_WIDTH = 256
S5_GROUPS = 16
S5_GROUP_CH = 16
S5_STATE = 64
S5_NSTATE = S5_GROUPS * S5_STATE
S5_BLK = 8
RG_WIDTH = 256
RG_BLOCKS = 4
RG_C = 8.0
XA_HEADS = 4
XA_HEAD_DIM = 256
D_FF = 4096
ALPHA = (2.0 * DEPTH) ** 0.25
LN_EPS = 1e-5

LANES = 128
SUBLANES = 8
MXU_TILE = 256
NEG_BIG = -1e30
VMEM_LIMIT = 56 * 1024 * 1024

SSD_CHUNK = 128
MIX_TQ = 256
S5_TB = 2048
TOK_TM = 1024
MLP_TM = 1024
MLP_FC = 1024


def _dot(a, b):
    return jnp.dot(a.astype(BF16), b.astype(BF16), preferred_element_type=F32)


def _dot_nt(a, b):
    return lax.dot_general(a.astype(BF16), b.astype(BF16), (((1,), (1,)), ((), ())),
                           preferred_element_type=F32)


def _layer_norm(x, g, b):
    mu = jnp.mean(x, axis=-1, keepdims=True)
    xc = x - mu
    var = jnp.mean(xc * xc, axis=-1, keepdims=True)
    return xc * lax.rsqrt(var + LN_EPS) * g + b


def _gelu(x):
    c = math.sqrt(2.0 / math.pi)
    return 0.5 * x * (1.0 + jnp.tanh(c * (x + 0.044715 * (x * x * x))))


def _sigmoid(x):
    return 0.5 + 0.5 * jnp.tanh(0.5 * x)


def _silu(x):
    hx = 0.5 * x
    return hx + hx * jnp.tanh(hx)


def _softplus(x):
    return jnp.maximum(x, 0.0) + jnp.log1p(jnp.exp(-jnp.abs(x)))


def _const_spec(shape):
    nd = len(shape)
    return pl.BlockSpec(shape, lambda *_: (0,) * nd, pipeline_mode=pl.Buffered(1))


def _layer_spec(stacked, layer):
    nd = stacked.ndim - 1
    return pl.BlockSpec((None,) + stacked.shape[1:], lambda *_: (layer,) + (0,) * nd, pipeline_mode=pl.Buffered(1))


def _params(sem):
    return pltpu.CompilerParams(dimension_semantics=sem, vmem_limit_bytes=VMEM_LIMIT)


def _re_lanes(p):
    return slice(2 * p * LANES, (2 * p + 1) * LANES)


def _im_lanes(p):
    return slice((2 * p + 1) * LANES, (2 * p + 2) * LANES)


def _s5_state_group(idx):
    state = (idx // (2 * LANES)) * LANES + idx % LANES
    return state // S5_STATE


def _s5_split(x):
    planes = S5_NSTATE // LANES
    return (jnp.concatenate([x[:, _re_lanes(p)] for p in range(planes)], axis=-1),
            jnp.concatenate([x[:, _im_lanes(p)] for p in range(planes)], axis=-1))


S5_HALF_CH = S5_WIDTH // 2
S5_HALF_LANES = S5_BLK * S5_HALF_CH


def _s5_expand_operators(wc_ref, vc_ref, ct_ref, ms_scr, toep_scr, mo_scr):
    n, w, c = S5_NSTATE, S5_WIDTH, S5_GROUP_CH
    hw, hl = S5_HALF_CH, S5_HALF_LANES
    grp_half = S5_GROUPS // 2
    reps = hw // c
    grp_r = _s5_state_group(lax.broadcasted_iota(jnp.int32, (2 * n, w), 0))
    same_o = grp_r == lax.broadcasted_iota(jnp.int32, (2 * n, w), 1) // c
    mo0 = jnp.where(same_o, ct_ref[...], 0.0)
    kc = jnp.dot(wc_ref[...], mo0, preferred_element_type=F32, precision=lax.Precision.HIGHEST)
    toep_scr[...] = jnp.zeros(toep_scr.shape, toep_scr.dtype)
    ch_grp = lax.broadcasted_iota(jnp.int32, (hw, hl), 0) // c
    same_k = (lax.broadcasted_iota(jnp.int32, (hw, hw), 0) // c) == (lax.broadcasted_iota(jnp.int32, (hw, hw), 1) // c)
    pick_r = lax.broadcasted_iota(jnp.int32, (S5_BLK * c, hl), 0)
    pick_c = lax.broadcasted_iota(jnp.int32, (S5_BLK * c, hl), 1)
    pick = ((pick_r // c == pick_c // hw) & (pick_r % c == pick_c % c)).astype(BF16)
    for hf in range(2):
        lanes = slice(hf * hl, (hf + 1) * hl)
        st_grp = _s5_state_group(lax.broadcasted_iota(jnp.int32, (hw, hl), 1) + hf * hl) - hf * grp_half
        same_s = ch_grp == st_grp
        for t in range(S5_BLK):
            blk = jnp.where(same_s, jnp.tile(wc_ref[t * c:(t + 1) * c, lanes], (reps, 1)), 0.0)
            ms_scr[hf, t * hw:(t + 1) * hw, :] = blk.astype(BF16)
        st_grp = _s5_state_group(lax.broadcasted_iota(jnp.int32, (hl, hl), 0) + hf * hl) - hf * grp_half
        same_r = st_grp == (lax.broadcasted_iota(jnp.int32, (hl, hl), 1) % hw) // c
        spread = jnp.dot(vc_ref[lanes, :].astype(BF16), pick, preferred_element_type=F32)
        mo_scr[hf] = jnp.where(same_r, spread, 0.0).astype(BF16)
        for t_in in range(S5_BLK):
            for t_out in range(t_in, S5_BLK):
                k = S5_BLK - 1 - (t_out - t_in)
                blk = jnp.where(same_k, jnp.tile(kc[k * c:(k + 1) * c, hf * hw:(hf + 1) * hw], (reps, 1)), 0.0)
                toep_scr[hf, t_in * hw:(t_in + 1) * hw, t_out * hw:(t_out + 1) * hw] = blk.astype(BF16)


_S5W = collections.namedtuple("_S5W", "wu wc vc ct lvl pw pwt d gluw glub")
_S5Ops = collections.namedtuple("_S5Ops", "ms toep mo")
_S5Buf = collections.namedtuple("_S5Buf", "u32 u16 hprev16")
_S5Scratch = collections.namedtuple("_S5Scratch", "u_tok hplane cexp carry yg")


def _s5_scan_stage(h_ref, w, ops, scr, out, first, rows, side):
    n, wd = S5_NSTATE, S5_WIDTH
    lane_halves = wd // LANES
    planes = n // LANES
    nt = rows // SUBLANES
    tb = rows * S5_BLK
    for r0 in range(0, tb, tb // 4):
        rs = slice(r0, r0 + tb // 4)
        u_tok = jnp.dot(h_ref[rs, :].astype(BF16), w.wu[...], preferred_element_type=F32)
        for hf in range(lane_halves):
            scr.u_tok[hf, rs, :] = u_tok[:, hf * LANES:(hf + 1) * LANES]
    u = jnp.concatenate([scr.u_tok[hf, pl.ds(k, rows, stride=S5_BLK), :]
                         for hf in range(lane_halves) for k in range(S5_BLK)], axis=-1)
    out.u32[...] = u
    u16 = u.astype(BF16)
    out.u16[...] = u16
    row8 = lax.broadcasted_iota(jnp.int32, (nt, SUBLANES, LANES), 1)
    planes_per_half = planes // lane_halves
    for p in range(planes):
        hf, pl_in_half = divmod(p, planes_per_half)
        s = jnp.dot(u16[:, hf * S5_HALF_LANES:(hf + 1) * S5_HALF_LANES],
                    ops.ms[hf, :, 2 * pl_in_half * LANES:(2 * pl_in_half + 2) * LANES],
                    preferred_element_type=F32)
        if p < planes - 2:
            side(1)
        xr = s[:, :LANES].reshape(nt, SUBLANES, LANES)
        xi = s[:, LANES:].reshape(nt, SUBLANES, LANES)
        for lv, dist in enumerate((1, 2, 4)):
            lr = w.lvl[lv:lv + 1, _re_lanes(p)]
            li = w.lvl[lv:lv + 1, _im_lanes(p)]
            keep = row8 >= dist
            sr = jnp.where(keep, pltpu.roll(xr, dist, 1), 0.0)
            si = jnp.where(keep, pltpu.roll(xi, dist, 1), 0.0)
            xr, xi = xr + lr * sr - li * si, xi + lr * si + li * sr
        scr.hplane[2 * p, SUBLANES:SUBLANES + rows, :] = xr.reshape(rows, LANES)
        scr.hplane[2 * p + 1, SUBLANES:SUBLANES + rows, :] = xi.reshape(rows, LANES)
    side(1)
    last = 2 * SUBLANES - 1
    er = jnp.concatenate([scr.hplane[2 * p, pl.ds(last, nt, stride=SUBLANES), :] for p in range(planes)], axis=-1)
    ei = jnp.concatenate([scr.hplane[2 * p + 1, pl.ds(last, nt, stride=SUBLANES), :] for p in range(planes)], -1)
    tile_id = lax.broadcasted_iota(jnp.int32, (nt, n), 0)
    lv, dist = 3, 1
    while dist < nt:
        lr, li = _s5_split(w.lvl[lv:lv + 1, :])
        keep = tile_id >= dist
        sr = jnp.where(keep, pltpu.roll(er, dist, 0), 0.0)
        si = jnp.where(keep, pltpu.roll(ei, dist, 0), 0.0)
        er, ei = er + lr * sr - li * si, ei + lr * si + li * sr
        lv, dist = lv + 1, dist * 2
    cr, ci = _s5_split(jnp.where(first, 0.0, scr.carry[0:1, :]))
    ptr, pti = _s5_split(w.pwt[...])
    er, ei = er + ptr * cr - pti * ci, ei + ptr * ci + pti * cr
    for p in range(planes):
        scr.carry[0:1, _re_lanes(p)] = er[nt - 1:nt, p * LANES:(p + 1) * LANES]
        scr.carry[0:1, _im_lanes(p)] = ei[nt - 1:nt, p * LANES:(p + 1) * LANES]
    tin_r = jnp.where(tile_id >= 1, pltpu.roll(er, 1, 0), cr)
    tin_i = jnp.where(tile_id >= 1, pltpu.roll(ei, 1, 0), ci)
    side(1)
    for p in range(planes):
        for k in range(SUBLANES):
            scr.cexp[2 * p, pl.ds(k, nt, stride=SUBLANES), :] = tin_r[:, p * LANES:(p + 1) * LANES]
            scr.cexp[2 * p + 1, pl.ds(k, nt, stride=SUBLANES), :] = tin_i[:, p * LANES:(p + 1) * LANES]
    for p in range(planes):
        if p % 2 == 0:
            side(1)
        pr = jnp.tile(w.pw[:, _re_lanes(p)], (nt, 1))
        pi = jnp.tile(w.pw[:, _im_lanes(p)], (nt, 1))
        tr = scr.cexp[2 * p]
        ti = scr.cexp[2 * p + 1]
        hr = scr.hplane[2 * p, SUBLANES:SUBLANES + rows, :] + pr * tr - pi * ti
        hi = scr.hplane[2 * p + 1, SUBLANES:SUBLANES + rows, :] + pr * ti + pi * tr
        scr.hplane[2 * p, SUBLANES:SUBLANES + rows, :] = hr
        scr.hplane[2 * p + 1, SUBLANES:SUBLANES + rows, :] = hi
        scr.hplane[2 * p, SUBLANES - 1:SUBLANES, :] = cr[:, p * LANES:(p + 1) * LANES]
        scr.hplane[2 * p + 1, SUBLANES - 1:SUBLANES, :] = ci[:, p * LANES:(p + 1) * LANES]
    for p in range(2 * planes):
        out.hprev16[:, p * LANES:(p + 1) * LANES] = scr.hplane[p, SUBLANES - 1:SUBLANES - 1 + rows, :].astype(BF16)


def _s5_readout_jobs(src, w, ops, scr, o_ref, rows):
    hw, hl = S5_HALF_CH, S5_HALF_LANES
    lane_halves = S5_WIDTH // LANES
    steps_per_tile = MXU_TILE // hw

    def tile_job(hf, j):
        def run():
            k_rows = (j + 1) * MXU_TILE
            cols = slice(j * MXU_TILE, (j + 1) * MXU_TILE)
            y = jnp.dot(src.u16[:, hf * hl:hf * hl + k_rows], ops.toep[hf, :k_rows, cols],
                        preferred_element_type=F32)
            y = y + jnp.dot(src.hprev16[:, hf * hl:(hf + 1) * hl], ops.mo[hf, :, cols], preferred_element_type=F32)
            d_half = w.d[:, hf * hw:(hf + 1) * hw]
            y = _gelu(y + jnp.concatenate([d_half] * steps_per_tile, axis=1) * src.u32[:, hf * hl + j * MXU_TILE:
                                                                                           hf * hl + (j + 1) * MXU_TILE])
            for i in range(steps_per_tile):
                scr.yg[hf, pl.ds(j * steps_per_tile + i, rows, stride=S5_BLK), :] = y[:, i * hw:(i + 1) * hw]
        return run

    def glu_job():
        yg = jnp.concatenate([scr.yg[hf] for hf in range(lane_halves)], axis=-1)
        glu = _sigmoid(_dot(yg, w.gluw[...]) + w.glub[...])
        o_ref[...] = (yg * glu).astype(o_ref.dtype)

    return [tile_job(hf, j) for hf in range(lane_halves) for j in range(hl // MXU_TILE)] + [glu_job]


def _s5_kernel(h_ref, *rest, rows, nblk):
    nw = len(_S5W._fields)
    w = _S5W(*rest[:nw])
    o_ref = rest[nw]
    bufs = rest[nw + 1:]
    ops = _S5Ops(*bufs[:3])
    nb = len(_S5Buf._fields)
    scr = _S5Scratch(*bufs[3 + nb:])
    s = pl.program_id(0)
    parity = s % 2

    @pl.when(s == 0)
    def _():
        _s5_expand_operators(w.wc, w.vc, w.ct, *ops)
        for ref in bufs[3:]:
            ref[...] = jnp.zeros(ref.shape, ref.dtype)

    first = (s % nblk) == 0

    out = _S5Buf(*(r.at[parity] for r in bufs[3:3 + nb]))
    src = _S5Buf(*(r.at[1 - parity] for r in bufs[3:3 + nb]))

    def step(scan, readout):
        jobs = _s5_readout_jobs(src, w, ops, scr, o_ref, rows) if readout else []

        def side(k):
            for _ in range(min(k, len(jobs))):
                jobs.pop(0)()

        if scan:
            _s5_scan_stage(h_ref, w, ops, scr, out, first, rows, side)
        side(len(jobs))

    last = pl.num_programs(0) - 1

    @pl.when(s == 0)
    def _():
        step(scan=True, readout=False)

    @pl.when((s > 0) & (s < last))
    def _():
        step(scan=True, readout=True)

    @pl.when(s == last)
    def _():
        step(scan=False, readout=True)


def _s5_branch(h, consts, layer):
    b, l, d = h.shape
    tb = min(S5_TB, l)
    rows = tb // S5_BLK
    nblk = l // tb
    total = b * nblk
    n, w = S5_NSTATE, S5_WIDTH
    handoff = [
        pltpu.VMEM((2, rows, S5_BLK * w), F32),
        pltpu.VMEM((2, rows, S5_BLK * w), BF16),
        pltpu.VMEM((2, rows, 2 * n), BF16),
    ]
    out = pl.pallas_call(
        functools.partial(_s5_kernel, rows=rows, nblk=nblk),
        grid=(total + 1,),
        in_specs=[pl.BlockSpec((tb, d), lambda s: (jnp.minimum(s, total - 1), 0))]
                 + [_layer_spec(c, layer) for c in consts],
        out_specs=pl.BlockSpec((tb, w), lambda s: (jnp.maximum(s - 1, 0), 0)),
        out_shape=jax.ShapeDtypeStruct((b * l, w), BF16),
        scratch_shapes=[
            pltpu.VMEM((2, S5_HALF_LANES, S5_HALF_LANES), BF16),
            pltpu.VMEM((2, S5_HALF_LANES, S5_HALF_LANES), BF16),
            pltpu.VMEM((2, S5_HALF_LANES, S5_HALF_LANES), BF16),
        ] + handoff + [
            pltpu.VMEM((w // LANES, tb, LANES), F32),
            pltpu.VMEM((2 * n // LANES, rows + SUBLANES, LANES), F32),
            pltpu.VMEM((2 * n // LANES, rows, LANES), F32),
            pltpu.VMEM((SUBLANES, 2 * n), F32),
            pltpu.VMEM((w // LANES, tb, LANES), F32),
        ],
        compiler_params=_params(("arbitrary",)),
        name="s5_branch",
    )(h.reshape(b * l, d), *consts)
    return out.reshape(b, l, w)


def _s5_prepare(lam_re, lam_im, log_step, b_re, b_im, c_re, c_im, n_tiles):
    g, p, c, t8 = S5_GROUPS, S5_STATE, S5_GROUP_CH, S5_BLK
    lr, li = lam_re.astype(F32), lam_im.astype(F32)
    dt = jnp.exp(log_step.astype(F32))[:, None]
    ar, ai = lr * dt, li * dt

    def lam_pow(k):
        k = k.astype(F32)[..., None, None]
        mag = jnp.exp(k * ar)
        return mag * jnp.cos(k * ai), mag * jnp.sin(k * ai)

    br1, bi1 = lam_pow(jnp.ones((), jnp.int32))
    xr_, xi_ = br1 - 1.0, bi1
    den = lr * lr + li * li
    cr, ci = (xr_ * lr + xi_ * li) / den, (xi_ * lr - xr_ * li) / den
    bre, bim = b_re.astype(F32), b_im.astype(F32)
    bbr = cr[..., None] * bre - ci[..., None] * bim
    bbi = cr[..., None] * bim + ci[..., None] * bre
    steps = jnp.arange(t8)
    pr, pi = lam_pow(t8 - 1 - steps)
    wsr = pr[..., None] * bbr[None] - pi[..., None] * bbi[None]
    wsi = pr[..., None] * bbi[None] + pi[..., None] * bbr[None]

    def interleave(re, im, axis):
        shape = list(re.shape)
        shape[axis:axis + 1] = [re.shape[axis] // LANES, 1, LANES]
        both = jnp.concatenate([re.reshape(shape), im.reshape(shape)], axis=axis + 1)
        shape[axis:axis + 3] = [2 * re.shape[axis]]
        return both.reshape(shape)

    wc = interleave(*[part.transpose(0, 3, 1, 2).reshape(t8 * c, g * p) for part in (wsr, wsi)], axis=1)
    cre, cim = c_re.astype(F32), c_im.astype(F32)
    pr, pi = lam_pow(steps + 1)
    vor = cre[None] * pr[:, :, None, :] - cim[None] * pi[:, :, None, :]
    voi = cre[None] * pi[:, :, None, :] + cim[None] * pr[:, :, None, :]
    vc = interleave(*[part.transpose(1, 3, 0, 2).reshape(g * p, t8 * c) for part in (vor, -voi)], axis=0)
    ct = interleave(*[jnp.tile(part.transpose(0, 2, 1).reshape(g * p, c), (1, g)) for part in (cre, -cim)], axis=0)

    def re_im(pair):
        return interleave(pair[0].reshape(-1, g * p), pair[1].reshape(-1, g * p), axis=1)

    lvl = re_im(lam_pow(t8 * 2 ** jnp.arange(SUBLANES)))
    pw = re_im(lam_pow(t8 * (steps + 1)))
    pwt = re_im(lam_pow(t8 * SUBLANES * (jnp.arange(n_tiles) + 1)))
    return wc, vc, ct, lvl, pw, pwt


def _expand_heads(cols, grp, head_lane):
    base = SSD_HPG * grp
    out = jnp.broadcast_to(cols[:, base:base + 1], (cols.shape[0], SSD_GW))
    for hh in range(1, SSD_HPG):
        out = jnp.where(head_lane == hh, cols[:, base + hh:base + hh + 1], out)
    return out


def _causal_conv(pad_ref, w_ref, b_ref, rows, cols):
    acc = b_ref[:, cols] + w_ref[CONV_K - 1:CONV_K, cols] * pad_ref[SUBLANES:SUBLANES + rows, cols]
    for k in range(CONV_K - 1):
        off = SUBLANES - (CONV_K - 1) + k
        acc = acc + w_ref[k:k + 1, cols] * pad_ref[off:off + rows, cols]
    return acc


_ProjBuf = collections.namedtuple("_ProjBuf", "z xpad dt rpad gr")
_MixW = collections.namedtuple(
    "_MixW", "wz wxbc wdt wxr wgr cw cb dtb aneg dvec nw rcw rcb wa ba wx bx rlam wout g b")
_MixScratch = collections.namedtuple("_MixScratch", "state ascr bscr cexp rcar")


def _col_tiles(width):
    return [slice(c, min(c + MXU_TILE, width)) for c in range(0, width, MXU_TILE)]


def _mixer_output_jobs(h_ref, ycat, w, o_ref):
    def tile_job(cols):
        def run():
            y = jnp.dot(ycat[...], w.wout[:, cols], preferred_element_type=F32)
            o_ref[:, cols] = ALPHA * h_ref[:, cols] + y
        return run

    def norm_job():
        o_ref[...] = _layer_norm(o_ref[...], w.g[...], w.b[...])

    return [tile_job(cols) for cols in _col_tiles(D_MODEL)] + [norm_job]


def _mixer_project_jobs(h_ref, h16, w, nxt, cur, first, tq):
    def cast_job():
        h16[...] = h_ref[...].astype(BF16)

    def tile_job(dst, w_ref, cols, row0):
        def run():
            dst[row0:row0 + tq, cols] = jnp.dot(h16[...], w_ref[:, cols], preferred_element_type=F32)
        return run

    def tail_job(pad_n, pad_c):
        def run():
            pad_n[0:SUBLANES, :] = jnp.where(first, 0.0, pad_c[tq:tq + SUBLANES, :])
        return run

    jobs = [cast_job]
    for dst, w_ref, row0 in ((nxt.z, w.wz, 0), (nxt.xpad, w.wxbc, SUBLANES), (nxt.dt, w.wdt, 0),
                             (nxt.rpad, w.wxr, SUBLANES), (nxt.gr, w.wgr, 0)):
        jobs += [tile_job(dst, w_ref, cols, row0) for cols in _col_tiles(dst.shape[1])]
    jobs += [tail_job(nxt.xpad, cur.xpad), tail_job(nxt.rpad, cur.rpad)]
    return jobs


def _mixer_compute(cur, ys5_ref, w, scr, ycat, first, tq, chunk, side):
    cw_ref, cb_ref, dtb_ref, aneg_ref, dvec_ref, nw_ref = w.cw, w.cb, w.dtb, w.aneg, w.dvec, w.nw
    state = scr.state
    z = cur.z
    slabs = []
    for cols in _col_tiles(SSD_XBC):
        side(1)
        slabs.append(_silu(_causal_conv(cur.xpad, cw_ref, cb_ref, tq, cols)))
    xact = jnp.concatenate(slabs, axis=-1)
    side(2)
    dt_all = _softplus(cur.dt[...] + dtb_ref[...])
    q = chunk
    tri = (lax.broadcasted_iota(jnp.int32, (q, q), 0) >= lax.broadcasted_iota(jnp.int32, (q, q), 1))
    row_id = lax.broadcasted_iota(jnp.int32, (q, LANES), 0)
    head_lane = lax.broadcasted_iota(jnp.int32, (1, SSD_GW), 1) // SSD_HEAD_DIM
    for ci in range(tq // q):
        r0 = ci * q
        dt = dt_all[r0:r0 + q, :]
        cs = dt * aneg_ref[...]
        dist = 1
        while dist < q:
            cs = cs + jnp.where(row_id >= dist, pltpu.roll(cs, dist, 0), 0.0)
            dist *= 2
        cs_t = cs.T
        cs_last = cs[q - 1:q, :]
        chunk_decay = jnp.exp(cs_last)
        dec_t = jnp.exp(cs_t[:, q - 1:q] - cs_t)
        ecs = jnp.exp(cs)
        zc = z[r0:r0 + q, :]
        halves = []
        for grp in range(SSD_GROUPS):
            side(2)
            lo = grp * SSD_GW
            xs = xact[r0:r0 + q, lo:lo + SSD_GW]
            bm = xact[r0:r0 + q, SSD_WIDTH + grp * SSD_STATE:SSD_WIDTH + (grp + 1) * SSD_STATE]
            cm = xact[r0:r0 + q, SSD_WIDTH + (SSD_GROUPS + grp) * SSD_STATE:
                      SSD_WIDTH + (SSD_GROUPS + grp + 1) * SSD_STATE]
            cm16 = cm.astype(BF16)
            xdt16 = (xs * _expand_heads(dt, grp, head_lane)).astype(BF16)
            cb = _dot_nt(cm16, bm)
            bm_t = bm.T
            s_prev = state[grp]
            if ci == 0:
                s_prev = jnp.where(first, 0.0, s_prev)
            y_off = jnp.dot(cm16, s_prev.astype(BF16), preferred_element_type=F32)
            intra, to_state, xdt_heads = [], [], []
            for hh in range(SSD_HPG):
                hd = grp * SSD_HPG + hh
                seg = cs[:, hd:hd + 1] - cs_t[hd:hd + 1, :]
                lmat = jnp.exp(jnp.where(tri, seg, NEG_BIG))
                intra.append((cb * lmat).astype(BF16))
                to_state.append((bm_t * dec_t[hd:hd + 1, :]).astype(BF16))
                xdt_heads.append(jnp.where(head_lane == hh, xdt16, jnp.zeros_like(xdt16)))
            lhs = jnp.concatenate([jnp.concatenate(intra, axis=1), jnp.concatenate(to_state, axis=1)], axis=0)
            both = jnp.dot(lhs, jnp.concatenate(xdt_heads, axis=0), preferred_element_type=F32)
            y_diag = both[:q]
            st_new = both[q:]
            state[grp] = s_prev * _expand_heads(chunk_decay, grp, head_lane) + st_new
            halves.append(y_diag + y_off * _expand_heads(ecs, grp, head_lane) + xs * dvec_ref[:, lo:lo + SSD_GW])
        y = jnp.concatenate(halves, axis=-1) * _silu(zc)
        y = y * lax.rsqrt(jnp.mean(y * y, axis=-1, keepdims=True) + LN_EPS) * nw_ref[...]
        ycat[r0:r0 + q, 0:SSD_WIDTH] = y.astype(BF16)

    ycat[:, SSD_WIDTH:SSD_WIDTH + S5_WIDTH] = ys5_ref[...]
    _mixer_rglru(cur, w, scr, ycat, first, tq, side)


def _mixer_rglru(cur, w, scr, ycat, first, tq, side):
    side(1)
    xc = _causal_conv(cur.rpad, w.rcw, w.rcb, tq, slice(0, RG_WIDTH))
    xc16 = xc.astype(BF16)
    rgate = _sigmoid(jnp.dot(xc16, w.wa[...], preferred_element_type=F32) + w.ba[...])
    igate = _sigmoid(jnp.dot(xc16, w.wx[...], preferred_element_type=F32) + w.bx[...])
    log_a = (-RG_C) * rgate * _softplus(-w.rlam[...])
    a = jnp.exp(log_a)
    bt = jnp.sqrt(1.0 - jnp.exp(2.0 * log_a)) * (igate * xc)
    row8 = lax.broadcasted_iota(jnp.int32, (tq, RG_WIDTH), 0) & (SUBLANES - 1)
    side(1)
    for dist in (1, 2, 4):
        keep = row8 >= dist
        a_sh = jnp.where(keep, pltpu.roll(a, dist, 0), 1.0)
        b_sh = jnp.where(keep, pltpu.roll(bt, dist, 0), 0.0)
        bt = bt + a * b_sh
        a = a * a_sh
    nt = tq // SUBLANES
    lane_halves = RG_WIDTH // LANES
    for hf in range(lane_halves):
        scr.ascr[hf] = a[:, hf * LANES:(hf + 1) * LANES]
        scr.bscr[hf] = bt[:, hf * LANES:(hf + 1) * LANES]
    ae = jnp.concatenate([scr.ascr[hf, pl.ds(SUBLANES - 1, nt, stride=SUBLANES), :] for hf in range(lane_halves)], -1)
    be = jnp.concatenate([scr.bscr[hf, pl.ds(SUBLANES - 1, nt, stride=SUBLANES), :] for hf in range(lane_halves)], -1)
    tile_id = lax.broadcasted_iota(jnp.int32, (nt, RG_WIDTH), 0)
    dist = 1
    while dist < nt:
        keep = tile_id >= dist
        a_sh = jnp.where(keep, pltpu.roll(ae, dist, 0), 1.0)
        b_sh = jnp.where(keep, pltpu.roll(be, dist, 0), 0.0)
        be = be + ae * b_sh
        ae = ae * a_sh
        dist *= 2
    carry_in = jnp.where(first, 0.0, scr.rcar[0:1, :])
    tile_end = be + ae * carry_in
    scr.rcar[0:1, :] = tile_end[nt - 1:nt, :]
    tile_in = jnp.where(tile_id >= 1, pltpu.roll(tile_end, 1, 0), carry_in)
    for hf in range(lane_halves):
        for k in range(SUBLANES):
            scr.cexp[hf, pl.ds(k, nt, stride=SUBLANES), :] = tile_in[:, hf * LANES:(hf + 1) * LANES]
    hscan = bt + a * jnp.concatenate([scr.cexp[hf] for hf in range(lane_halves)], axis=-1)
    ycat[:, SSD_WIDTH + S5_WIDTH:] = (hscan * _gelu(cur.gr[...])).astype(BF16)


_MIX_W16 = ("wz", "wxbc", "wdt", "wxr", "wgr", "wout")


def _mixer_load_weights(w_in_ref, w_out_ref, w):
    o_dt = SSD_WIDTH + SSD_XBC
    o_xr = o_dt + SSD_HEADS + S5_WIDTH
    o_gr = o_xr + RG_WIDTH
    w.wz[...] = w_in_ref[:, 0:SSD_WIDTH].astype(BF16)
    w.wxbc[...] = w_in_ref[:, SSD_WIDTH:o_dt].astype(BF16)
    head_cols = lax.broadcasted_iota(jnp.int32, (1, LANES), 1) < SSD_HEADS
    w.wdt[...] = jnp.where(head_cols, w_in_ref[:, o_dt:o_dt + LANES], 0.0).astype(BF16)
    w.wxr[...] = w_in_ref[:, o_xr:o_gr].astype(BF16)
    w.wgr[...] = w_in_ref[:, o_gr:o_gr + RG_WIDTH].astype(BF16)
    w.wout[...] = w_out_ref[...].astype(BF16)


def _mixer_kernel(hp_ref, ho_ref, ys5_ref, w_in_ref, w_out_ref, *rest, tq, chunk, nblk, last):
    small = [f for f in _MixW._fields if f not in _MIX_W16]
    o_ref = rest[len(small)]
    bufs = rest[len(small) + 1:]
    w16 = bufs[len(bufs) - len(_MIX_W16):]
    bufs = bufs[:len(bufs) - len(_MIX_W16)]
    w = _MixW(**dict(zip(small, rest[:len(small)])), **dict(zip(_MIX_W16, w16)))
    npb = len(_ProjBuf._fields)
    sets = (_ProjBuf(*bufs[:npb]), _ProjBuf(*bufs[npb:2 * npb]))
    ycats = bufs[2 * npb:2 * npb + 2]
    scr = _MixScratch(*bufs[2 * npb + 2:2 * npb + 2 + len(_MixScratch._fields)])
    h16 = bufs[-1]
    s = pl.program_id(0)

    @pl.when(s == 0)
    def _():
        _mixer_load_weights(w_in_ref, w_out_ref, w)
        for ref in bufs:
            ref[...] = jnp.zeros(ref.shape, ref.dtype)

    first_p = (s % nblk) == 0
    first_c = ((s + nblk - 1) % nblk) == 0

    def step(parity, project=True, mix=True, output=True):
        nxt, cur = sets[parity], sets[1 - parity]
        ycat_c, ycat_o = ycats[1 - parity], ycats[parity]
        jobs = _mixer_output_jobs(ho_ref, ycat_o, w, o_ref) if output else []
        if project:
            jobs += _mixer_project_jobs(hp_ref, h16, w, nxt, cur, first_p, tq)

        def side(n):
            for _ in range(min(n, len(jobs))):
                jobs.pop(0)()

        if mix:
            _mixer_compute(cur, ys5_ref, w, scr, ycat_c, first_c, tq, chunk, side)
        side(len(jobs))

    pl.when(s == 0)(lambda: step(0, mix=False, output=False))
    pl.when(s == 1)(lambda: step(1, output=False))
    steady = (s >= 2) & (s <= last - 2)
    pl.when(steady & (s % 2 == 0))(lambda: step(0))
    pl.when(steady & (s % 2 == 1))(lambda: step(1))
    pl.when(s == last - 1)(lambda: step((last - 1) % 2, project=False))
    pl.when(s == last)(lambda: step(last % 2, project=False, mix=False))


def _mixer(h, ys5, w_in, w_out, consts, layer):
    b, l, d = h.shape
    tq = min(MIX_TQ, l)
    chunk = min(SSD_CHUNK, tq)
    nblk = l // tq
    total = b * nblk
    h2d = h.reshape(b * l, d)
    proj_buf = [
        pltpu.VMEM((tq, SSD_WIDTH), F32),
        pltpu.VMEM((tq + SUBLANES, SSD_XBC), F32),
        pltpu.VMEM((tq, LANES), F32),
        pltpu.VMEM((tq + SUBLANES, RG_WIDTH), F32),
        pltpu.VMEM((tq, RG_WIDTH), F32),
    ]
    rg_plane = pltpu.VMEM((RG_WIDTH // LANES, tq, LANES), F32)
    out = pl.pallas_call(
        functools.partial(_mixer_kernel, tq=tq, chunk=chunk, nblk=nblk, last=total + 1),
        grid=(total + 2,),
        in_specs=[pl.BlockSpec((tq, d), lambda s: (jnp.minimum(s, total - 1), 0)),
                  pl.BlockSpec((tq, d), lambda s: (jnp.maximum(s - 2, 0), 0)),
                  pl.BlockSpec((tq, S5_WIDTH), lambda s: (jnp.clip(s - 1, 0, total - 1), 0))]
                 + [_layer_spec(c, layer) for c in (w_in, w_out) + tuple(consts)],
        out_specs=pl.BlockSpec((tq, d), lambda s: (jnp.maximum(s - 2, 0), 0)),
        out_shape=jax.ShapeDtypeStruct((b * l, d), F32),
        scratch_shapes=proj_buf + proj_buf + [
            pltpu.VMEM((tq, D_MODEL), BF16),
            pltpu.VMEM((tq, D_MODEL), BF16),
            pltpu.VMEM((SSD_GROUPS, SSD_STATE, SSD_GW), F32),
            rg_plane, rg_plane, rg_plane,
            pltpu.VMEM((SUBLANES, RG_WIDTH), F32),
            pltpu.VMEM((tq, d), BF16),
            pltpu.VMEM((d, SSD_WIDTH), BF16),
            pltpu.VMEM((d, SSD_XBC), BF16),
            pltpu.VMEM((d, LANES), BF16),
            pltpu.VMEM((d, RG_WIDTH), BF16),
            pltpu.VMEM((d, RG_WIDTH), BF16),
            pltpu.VMEM((d, d), BF16),
        ],
        compiler_params=_params(("arbitrary",)),
        name="mixer",
    )(h2d, h2d, ys5.reshape(b * l, S5_WIDTH), w_in, w_out, *consts)
    return out.reshape(b, l, d)


def _block_diag(w):
    nl, nb, bi, bo = w.shape
    return jnp.einsum('lhij,hk->lhikj', w, jnp.eye(nb, dtype=w.dtype)).reshape(nl, nb * bi, nb * bo)


def _xattn_kernel(h_ref, m_ref, wk32_ref, wv32_ref, wq32_ref, wo32_ref, g_ref, b_ref, o_ref,
                  wq_ref, wo_ref, k_ref, v_ref):
    @pl.when((pl.program_id(0) == 0) & (pl.program_id(1) == 0))
    def _():
        wq_ref[...] = wq32_ref[...].astype(BF16)
        wo_ref[...] = wo32_ref[...].astype(BF16)

    @pl.when(pl.program_id(1) == 0)
    def _():
        m16 = m_ref[...].astype(BF16)
        k_ref[...] = jnp.dot(m16, wk32_ref[...].astype(BF16), preferred_element_type=F32).astype(BF16)
        v_ref[...] = jnp.dot(m16, wv32_ref[...].astype(BF16), preferred_element_type=F32).astype(BF16)

    hb = h_ref[...]
    qall = jnp.dot(hb.astype(BF16), wq_ref[...], preferred_element_type=F32) * (1.0 / math.sqrt(XA_HEAD_DIM))
    outs = []
    for hd in range(XA_HEADS):
        sl = slice(hd * XA_HEAD_DIM, (hd + 1) * XA_HEAD_DIM)
        s = _dot_nt(qall[:, sl], k_ref[:, sl])
        p = jnp.exp(s - jnp.max(s, axis=-1, keepdims=True))
        o = jnp.dot(p.astype(BF16), v_ref[:, sl], preferred_element_type=F32)
        outs.append(o / jnp.sum(p, axis=-1, keepdims=True))
    o16 = jnp.concatenate(outs, axis=-1).astype(BF16)
    half = o16.shape[0] // 2
    for rows in (slice(0, half), slice(half, 2 * half)):
        y = jnp.dot(o16[rows], wo_ref[...], preferred_element_type=F32)
        o_ref[rows, :] = _layer_norm(ALPHA * hb[rows] + y, g_ref[...], b_ref[...])


def _xattn(h, mem, wk, wv, wq, wo, g, bb, layer):
    b, l, d = h.shape
    m = mem.shape[1]
    tm = min(TOK_TM, l)
    tok = pl.BlockSpec((None, tm, d), lambda i, j: (i, j, 0))
    mems = pl.BlockSpec((None, m, d), lambda i, j: (i, 0, 0))
    return pl.pallas_call(
        _xattn_kernel,
        grid=(b, l // tm),
        in_specs=[tok, mems] + [_layer_spec(c, layer) for c in (wk, wv, wq, wo, g, bb)],
        out_specs=tok,
        out_shape=jax.ShapeDtypeStruct((b, l, d), F32),
        scratch_shapes=[pltpu.VMEM((d, d), BF16), pltpu.VMEM((d, d), BF16),
                        pltpu.VMEM((m, d), BF16), pltpu.VMEM((m, d), BF16)],
        compiler_params=_params(("arbitrary", "arbitrary")),
        name="xattn",
    )(h, mem, wk, wv, wq, wo, g, bb)


def _mlp_kernel(h_ref, w1_ref, w2_ref, g_ref, b_ref, o_ref):
    hb = h_ref[...]
    h16 = hb.astype(BF16)
    acc = ALPHA * hb
    for f in range(D_FF // MLP_FC):
        sl = slice(f * MLP_FC, (f + 1) * MLP_FC)
        t = jnp.maximum(jnp.dot(h16, w1_ref[:, sl], preferred_element_type=F32), 0.0)
        acc = acc + jnp.dot((t * t).astype(BF16), w2_ref[sl, :], preferred_element_type=F32)
    o_ref[...] = _layer_norm(acc, g_ref[...], b_ref[...])


def _mlp(h2d, w1, w2, g, bb, layer):
    t, d = h2d.shape
    tm = min(MLP_TM, t)
    tok = pl.BlockSpec((tm, d), lambda i: (i, 0))
    return pl.pallas_call(
        _mlp_kernel,
        grid=(t // tm,),
        in_specs=[tok] + [_layer_spec(c, layer) for c in (w1, w2, g, bb)],
        out_specs=tok,
        out_shape=jax.ShapeDtypeStruct((t, d), F32),
        compiler_params=_params(("parallel",)),
        name="mlp",
    )(h2d, w1, w2, g, bb)


def _rows(v):
    return v.astype(F32).reshape(v.shape[0], 1, -1)


def _pad_lanes(v3d):
    return jnp.pad(v3d, ((0, 0), (0, 0), (0, LANES - v3d.shape[2])))


def prepare_params(seq_len, w_in, w_out, ssd_conv_w, ssd_conv_b, ssd_dt_bias, ssd_a_log, ssd_d, ssd_norm_w,
                   s5_lam_re, s5_lam_im, s5_log_step, s5_b_re, s5_b_im, s5_c_re, s5_c_im, s5_d, s5_glu_w, s5_glu_b,
                   rg_conv_w, rg_conv_b, rg_wa, rg_ba, rg_wx, rg_bx, rg_lambda, ln1_g, ln1_b,
                   xa_wq, xa_wk, xa_wv, xa_wo, ln2_g, ln2_b, mlp_w1, mlp_w2, ln3_g, ln3_b):
    o_u = SSD_WIDTH + SSD_XBC + SSD_HEADS
    n_tiles = min(S5_TB, seq_len) // (S5_BLK * SUBLANES)
    assert n_tiles <= 2 ** (SUBLANES - 3), "S5 scan-level table holds 3 in-tile + 5 tile levels"
    s5_ops = jax.vmap(functools.partial(_s5_prepare, n_tiles=n_tiles))(
        s5_lam_re, s5_lam_im, s5_log_step, s5_b_re, s5_b_im, s5_c_re, s5_c_im)
    s5 = (w_in[:, :, o_u:o_u + S5_WIDTH].astype(BF16), *s5_ops, _rows(s5_d), s5_glu_w.astype(BF16), _rows(s5_glu_b))
    mix_small = dict(
        cw=ssd_conv_w.astype(F32), cb=_rows(ssd_conv_b), dtb=_pad_lanes(_rows(ssd_dt_bias)),
        aneg=_pad_lanes(_rows(-jnp.exp(ssd_a_log.astype(F32)))),
        dvec=_rows(jnp.repeat(ssd_d.astype(F32), SSD_HEAD_DIM, axis=1)), nw=_rows(ssd_norm_w),
        rcw=rg_conv_w.astype(F32), rcb=_rows(rg_conv_b),
        wa=_block_diag(rg_wa.astype(F32)).astype(BF16), ba=_rows(rg_ba.reshape(rg_ba.shape[0], -1)),
        wx=_block_diag(rg_wx.astype(F32)).astype(BF16), bx=_rows(rg_bx.reshape(rg_bx.shape[0], -1)),
        rlam=_rows(rg_lambda), g=_rows(ln1_g), b=_rows(ln1_b))
    mix = (w_in.astype(F32), w_out.astype(F32), tuple(mix_small[f] for f in _MixW._fields if f not in _MIX_W16))
    xa = (xa_wk.astype(F32), xa_wv.astype(F32), xa_wq.astype(F32), xa_wo.astype(F32), _rows(ln2_g), _rows(ln2_b))
    mlp = (mlp_w1.astype(BF16), mlp_w2.astype(BF16), _rows(ln3_g), _rows(ln3_b))
    return dict(s5=s5, mix=mix, xa=xa, mlp=mlp)


def mixer_sublayer(h, prm, layer):
    ys5 = _s5_branch(h, prm["s5"], layer)
    w_in, w_out, consts = prm["mix"]
    return _mixer(h, ys5, w_in, w_out, consts, layer)


def xattn_sublayer(h, mem, prm, layer):
    return _xattn(h, mem, *prm["xa"], layer)


def mlp_sublayer(h, prm, layer):
    b, l, d = h.shape
    return _mlp(h.reshape(b * l, d), *prm["mlp"], layer).reshape(b, l, d)


def kernel(x, mem, w_in, w_out, ssd_conv_w, ssd_conv_b, ssd_dt_bias, ssd_a_log, ssd_d, ssd_norm_w, s5_lam_re, s5_lam_im, s5_log_step, s5_b_re, s5_b_im, s5_c_re, s5_c_im, s5_d, s5_glu_w, s5_glu_b, rg_conv_w, rg_conv_b, rg_wa, rg_ba, rg_wx, rg_bx, rg_lambda, ln1_g, ln1_b, xa_wq, xa_wk, xa_wv, xa_wo, ln2_g, ln2_b, mlp_w1, mlp_w2, ln3_g, ln3_b):
    h = x.astype(F32)
    memf = mem.astype(F32)
    prm = prepare_params(x.shape[1], w_in, w_out, ssd_conv_w, ssd_conv_b, ssd_dt_bias, ssd_a_log, ssd_d, ssd_norm_w,
                         s5_lam_re, s5_lam_im, s5_log_step, s5_b_re, s5_b_im, s5_c_re, s5_c_im, s5_d, s5_glu_w,
                         s5_glu_b, rg_conv_w, rg_conv_b, rg_wa, rg_ba, rg_wx, rg_bx, rg_lambda, ln1_g, ln1_b,
                         xa_wq, xa_wk, xa_wv, xa_wo, ln2_g, ln2_b, mlp_w1, mlp_w2, ln3_g, ln3_b)
    for layer in range(DEPTH):
        h = mixer_sublayer(h, prm, layer)
        h = xattn_sublayer(h, memf, prm, layer)
        h = mlp_sublayer(h, prm, layer)
    return h.astype(x.dtype)
```

```python
import collections
import functools
import math

import jax
import jax.numpy as jnp
from jax import lax
from jax.experimental import pallas as pl
from jax.experimental.pallas import tpu as pltpu

F32 = jnp.float32
BF16 = jnp.bfloat16

D_MODEL = 1024
DEPTH = 2
SSD_WIDTH = 512
SSD_HEAD_DIM = 64
SSD_HEADS = 8
SSD_GROUPS = 2
SSD_HPG = SSD_HEADS // SSD_GROUPS
SSD_GW = SSD_HPG * SSD_HEAD_DIM
SSD_STATE = 128
SSD_XBC = 1024
CONV_K = 4
S5_WIDTH = 256
S5_GROUPS = 16
S5_GROUP_CH = 16
S5_STATE = 64
S5_NSTATE = S5_GROUPS * S5_STATE
S5_BLK = 8
RG_WIDTH = 256
RG_BLOCKS = 4
RG_C = 8.0
XA_HEADS = 4
XA_HEAD_DIM = 256
D_FF = 4096
ALPHA = (2.0 * DEPTH) ** 0.25
LN_EPS = 1e-5

LANES = 128
SUBLANES = 8
MXU_TILE = 256
NEG_BIG = -1e30
VMEM_LIMIT = 56 * 1024 * 1024

SSD_CHUNK = 128
MIX_TQ = 256
S5_TB = 2048
TOK_TM = 1024
MLP_TM = 1024
MLP_FC = 1024


def _dot(a, b):
    return jnp.dot(a.astype(BF16), b.astype(BF16), preferred_element_type=F32)


def _dot_nt(a, b):
    return lax.dot_general(a.astype(BF16), b.astype(BF16), (((1,), (1,)), ((), ())),
                           preferred_element_type=F32)


def _layer_norm(x, g, b):
    mu = jnp.mean(x, axis=-1, keepdims=True)
    xc = x - mu
    var = jnp.mean(xc * xc, axis=-1, keepdims=True)
    return xc * lax.rsqrt(var + LN_EPS) * g + b


def _gelu(x):
    c = math.sqrt(2.0 / math.pi)
    return 0.5 * x * (1.0 + jnp.tanh(c * (x + 0.044715 * (x * x * x))))


def _sigmoid(x):
    return 0.5 + 0.5 * jnp.tanh(0.5 * x)


def _silu(x):
    hx = 0.5 * x
    return hx + hx * jnp.tanh(hx)


def _softplus(x):
    return jnp.maximum(x, 0.0) + jnp.log1p(jnp.exp(-jnp.abs(x)))


def _const_spec(shape):
    nd = len(shape)
    return pl.BlockSpec(shape, lambda *_: (0,) * nd, pipeline_mode=pl.Buffered(1))


def _layer_spec(stacked, layer):
    nd = stacked.ndim - 1
    return pl.BlockSpec((None,) + stacked.shape[1:], lambda *_: (layer,) + (0,) * nd, pipeline_mode=pl.Buffered(1))


def _params(sem):
    return pltpu.CompilerParams(dimension_semantics=sem, vmem_limit_bytes=VMEM_LIMIT)


def _re_lanes(p):
    return slice(2 * p * LANES, (2 * p + 1) * LANES)


def _im_lanes(p):
    return slice((2 * p + 1) * LANES, (2 * p + 2) * LANES)


def _s5_state_group(idx):
    state = (idx // (2 * LANES)) * LANES + idx % LANES
    return state // S5_STATE


def _s5_split(x):
    planes = S5_NSTATE // LANES
    return (jnp.concatenate([x[:, _re_lanes(p)] for p in range(planes)], axis=-1),
            jnp.concatenate([x[:, _im_lanes(p)] for p in range(planes)], axis=-1))


S5_HALF_CH = S5_WIDTH // 2
S5_HALF_LANES = S5_BLK * S5_HALF_CH


def _s5_expand_operators(wc_ref, vc_ref, ct_ref, ms_scr, toep_scr, mo_scr):
    n, w, c = S5_NSTATE, S5_WIDTH, S5_GROUP_CH
    hw, hl = S5_HALF_CH, S5_HALF_LANES
    grp_half = S5_GROUPS // 2
    reps = hw // c
    grp_r = _s5_state_group(lax.broadcasted_iota(jnp.int32, (2 * n, w), 0))
    same_o = grp_r == lax.broadcasted_iota(jnp.int32, (2 * n, w), 1) // c
    mo0 = jnp.where(same_o, ct_ref[...], 0.0)
    kc = jnp.dot(wc_ref[...], mo0, preferred_element_type=F32, precision=lax.Precision.HIGHEST)
    toep_scr[...] = jnp.zeros(toep_scr.shape, toep_scr.dtype)
    ch_grp = lax.broadcasted_iota(jnp.int32, (hw, hl), 0) // c
    same_k = (lax.broadcasted_iota(jnp.int32, (hw, hw), 0) // c) == (lax.broadcasted_iota(jnp.int32, (hw, hw), 1) // c)
    pick_r = lax.broadcasted_iota(jnp.int32, (S5_BLK * c, hl), 0)
    pick_c = lax.broadcasted_iota(jnp.int32, (S5_BLK * c, hl), 1)
    pick = ((pick_r // c == pick_c // hw) & (pick_r % c == pick_c % c)).astype(BF16)
    for hf in range(2):
        lanes = slice(hf * hl, (hf + 1) * hl)
        st_grp = _s5_state_group(lax.broadcasted_iota(jnp.int32, (hw, hl), 1) + hf * hl) - hf * grp_half
        same_s = ch_grp == st_grp
        for t in range(S5_BLK):
            blk = jnp.where(same_s, jnp.tile(wc_ref[t * c:(t + 1) * c, lanes], (reps, 1)), 0.0)
            ms_scr[hf, t * hw:(t + 1) * hw, :] = blk.astype(BF16)
        st_grp = _s5_state_group(lax.broadcasted_iota(jnp.int32, (hl, hl), 0) + hf * hl) - hf * grp_half
        same_r = st_grp == (lax.broadcasted_iota(jnp.int32, (hl, hl), 1) % hw) // c
        spread = jnp.dot(vc_ref[lanes, :].astype(BF16), pick, preferred_element_type=F32)
        mo_scr[hf] = jnp.where(same_r, spread, 0.0).astype(BF16)
        for t_in in range(S5_BLK):
            for t_out in range(t_in, S5_BLK):
                k = S5_BLK - 1 - (t_out - t_in)
                blk = jnp.where(same_k, jnp.tile(kc[k * c:(k + 1) * c, hf * hw:(hf + 1) * hw], (reps, 1)), 0.0)
                toep_scr[hf, t_in * hw:(t_in + 1) * hw, t_out * hw:(t_out + 1) * hw] = blk.astype(BF16)


_S5W = collections.namedtuple("_S5W", "wu wc vc ct lvl pw pwt d gluw glub")
_S5Ops = collections.namedtuple("_S5Ops", "ms toep mo")
_S5Buf = collections.namedtuple("_S5Buf", "u32 u16 hprev16")
_S5Scratch = collections.namedtuple("_S5Scratch", "u_tok hplane cexp carry yg")


def _s5_scan_stage(h_ref, w, ops, scr, out, first, rows, side):
    n, wd = S5_NSTATE, S5_WIDTH
    lane_halves = wd // LANES
    planes = n // LANES
    nt = rows // SUBLANES
    tb = rows * S5_BLK
    for r0 in range(0, tb, tb // 4):
        rs = slice(r0, r0 + tb // 4)
        u_tok = jnp.dot(h_ref[rs, :].astype(BF16), w.wu[...], preferred_element_type=F32)
        for hf in range(lane_halves):
            scr.u_tok[hf, rs, :] = u_tok[:, hf * LANES:(hf + 1) * LANES]
    u = jnp.concatenate([scr.u_tok[hf, pl.ds(k, rows, stride=S5_BLK), :]
                         for hf in range(lane_halves) for k in range(S5_BLK)], axis=-1)
    out.u32[...] = u
    u16 = u.astype(BF16)
    out.u16[...] = u16
    row8 = lax.broadcasted_iota(jnp.int32, (nt, SUBLANES, LANES), 1)
    planes_per_half = planes // lane_halves
    for p in range(planes):
        hf, pl_in_half = divmod(p, planes_per_half)
        s = jnp.dot(u16[:, hf * S5_HALF_LANES:(hf + 1) * S5_HALF_LANES],
                    ops.ms[hf, :, 2 * pl_in_half * LANES:(2 * pl_in_half + 2) * LANES],
                    preferred_element_type=F32)
        if p < planes - 2:
            side(1)
        xr = s[:, :LANES].reshape(nt, SUBLANES, LANES)
        xi = s[:, LANES:].reshape(nt, SUBLANES, LANES)
        for lv, dist in enumerate((1, 2, 4)):
            lr = w.lvl[lv:lv + 1, _re_lanes(p)]
            li = w.lvl[lv:lv + 1, _im_lanes(p)]
            keep = row8 >= dist
            sr = jnp.where(keep, pltpu.roll(xr, dist, 1), 0.0)
            si = jnp.where(keep, pltpu.roll(xi, dist, 1), 0.0)
            xr, xi = xr + lr * sr - li * si, xi + lr * si + li * sr
        scr.hplane[2 * p, SUBLANES:SUBLANES + rows, :] = xr.reshape(rows, LANES)
        scr.hplane[2 * p + 1, SUBLANES:SUBLANES + rows, :] = xi.reshape(rows, LANES)
    side(1)
    last = 2 * SUBLANES - 1
    er = jnp.concatenate([scr.hplane[2 * p, pl.ds(last, nt, stride=SUBLANES), :] for p in range(planes)], axis=-1)
    ei = jnp.concatenate([scr.hplane[2 * p + 1, pl.ds(last, nt, stride=SUBLANES), :] for p in range(planes)], -1)
    tile_id = lax.broadcasted_iota(jnp.int32, (nt, n), 0)
    lv, dist = 3, 1
    while dist < nt:
        lr, li = _s5_split(w.lvl[lv:lv + 1, :])
        keep = tile_id >= dist
        sr = jnp.where(keep, pltpu.roll(er, dist, 0), 0.0)
        si = jnp.where(keep, pltpu.roll(ei, dist, 0), 0.0)
        er, ei = er + lr * sr - li * si, ei + lr * si + li * sr
        lv, dist = lv + 1, dist * 2
    cr, ci = _s5_split(jnp.where(first, 0.0, scr.carry[0:1, :]))
    ptr, pti = _s5_split(w.pwt[...])
    er, ei = er + ptr * cr - pti * ci, ei + ptr * ci + pti * cr
    for p in range(planes):
        scr.carry[0:1, _re_lanes(p)] = er[nt - 1:nt, p * LANES:(p + 1) * LANES]
        scr.carry[0:1, _im_lanes(p)] = ei[nt - 1:nt, p * LANES:(p + 1) * LANES]
    tin_r = jnp.where(tile_id >= 1, pltpu.roll(er, 1, 0), cr)
    tin_i = jnp.where(tile_id >= 1, pltpu.roll(ei, 1, 0), ci)
    side(1)
    for p in range(planes):
        for k in range(SUBLANES):
            scr.cexp[2 * p, pl.ds(k, nt, stride=SUBLANES), :] = tin_r[:, p * LANES:(p + 1) * LANES]
            scr.cexp[2 * p + 1, pl.ds(k, nt, stride=SUBLANES), :] = tin_i[:, p * LANES:(p + 1) * LANES]
    for p in range(planes):
        if p % 2 == 0:
            side(1)
        pr = jnp.tile(w.pw[:, _re_lanes(p)], (nt, 1))
        pi = jnp.tile(w.pw[:, _im_lanes(p)], (nt, 1))
        tr = scr.cexp[2 * p]
        ti = scr.cexp[2 * p + 1]
        hr = scr.hplane[2 * p, SUBLANES:SUBLANES + rows, :] + pr * tr - pi * ti
        hi = scr.hplane[2 * p + 1, SUBLANES:SUBLANES + rows, :] + pr * ti + pi * tr
        scr.hplane[2 * p, SUBLANES:SUBLANES + rows, :] = hr
        scr.hplane[2 * p + 1, SUBLANES:SUBLANES + rows, :] = hi
        scr.hplane[2 * p, SUBLANES - 1:SUBLANES, :] = cr[:, p * LANES:(p + 1) * LANES]
        scr.hplane[2 * p + 1, SUBLANES - 1:SUBLANES, :] = ci[:, p * LANES:(p + 1) * LANES]
    for p in range(2 * planes):
        out.hprev16[:, p * LANES:(p + 1) * LANES] = scr.hplane[p, SUBLANES - 1:SUBLANES - 1 + rows, :].astype(BF16)


def _s5_readout_jobs(src, w, ops, scr, o_ref, rows):
    hw, hl = S5_HALF_CH, S5_HALF_LANES
    lane_halves = S5_WIDTH // LANES
    steps_per_tile = MXU_TILE // hw

    def tile_job(hf, j):
        def run():
            k_rows = (j + 1) * MXU_TILE
            cols = slice(j * MXU_TILE, (j + 1) * MXU_TILE)
            y = jnp.dot(src.u16[:, hf * hl:hf * hl + k_rows], ops.toep[hf, :k_rows, cols],
                        preferred_element_type=F32)
            y = y + jnp.dot(src.hprev16[:, hf * hl:(hf + 1) * hl], ops.mo[hf, :, cols], preferred_element_type=F32)
            d_half = w.d[:, hf * hw:(hf + 1) * hw]
            y = _gelu(y + jnp.concatenate([d_half] * steps_per_tile, axis=1) * src.u32[:, hf * hl + j * MXU_TILE:
                                                                                           hf * hl + (j + 1) * MXU_TILE])
            for i in range(steps_per_tile):
                scr.yg[hf, pl.ds(j * steps_per_tile + i, rows, stride=S5_BLK), :] = y[:, i * hw:(i + 1) * hw]
        return run

    def glu_job():
        yg = jnp.concatenate([scr.yg[hf] for hf in range(lane_halves)], axis=-1)
        glu = _sigmoid(_dot(yg, w.gluw[...]) + w.glub[...])
        o_ref[...] = (yg * glu).astype(o_ref.dtype)

    return [tile_job(hf, j) for hf in range(lane_halves) for j in range(hl // MXU_TILE)] + [glu_job]


def _s5_kernel(h_ref, *rest, rows, nblk):
    nw = len(_S5W._fields)
    w = _S5W(*rest[:nw])
    o_ref = rest[nw]
    bufs = rest[nw + 1:]
    ops = _S5Ops(*bufs[:3])
    nb = len(_S5Buf._fields)
    scr = _S5Scratch(*bufs[3 + nb:])
    s = pl.program_id(0)
    parity = s % 2

    @pl.when(s == 0)
    def _():
        _s5_expand_operators(w.wc, w.vc, w.ct, *ops)
        for ref in bufs[3:]:
            ref[...] = jnp.zeros(ref.shape, ref.dtype)

    first = (s % nblk) == 0

    def step(out, src):
        jobs = _s5_readout_jobs(src, w, ops, scr, o_ref, rows)

        def side(k):
            for _ in range(min(k, len(jobs))):
                jobs.pop(0)()

        _s5_scan_stage(h_ref, w, ops, scr, out, first, rows, side)
        side(len(jobs))

    step(_S5Buf(*(r.at[parity] for r in bufs[3:3 + nb])), _S5Buf(*(r.at[1 - parity] for r in bufs[3:3 + nb])))


def _s5_branch(h, consts, layer):
    b, l, d = h.shape
    tb = min(S5_TB, l)
    rows = tb // S5_BLK
    nblk = l // tb
    total = b * nblk
    n, w = S5_NSTATE, S5_WIDTH
    handoff = [
        pltpu.VMEM((2, rows, S5_BLK * w), F32),
        pltpu.VMEM((2, rows, S5_BLK * w), BF16),
        pltpu.VMEM((2, rows, 2 * n), BF16),
    ]
    out = pl.pallas_call(
        functools.partial(_s5_kernel, rows=rows, nblk=nblk),
        grid=(total + 1,),
        in_specs=[pl.BlockSpec((tb, d), lambda s: (jnp.minimum(s, total - 1), 0))]
                 + [_layer_spec(c, layer) for c in consts],
        out_specs=pl.BlockSpec((tb, w), lambda s: (jnp.maximum(s - 1, 0), 0)),
        out_shape=jax.ShapeDtypeStruct((b * l, w), BF16),
        scratch_shapes=[
            pltpu.VMEM((2, S5_HALF_LANES, S5_HALF_LANES), BF16),
            pltpu.VMEM((2, S5_HALF_LANES, S5_HALF_LANES), BF16),
            pltpu.VMEM((2, S5_HALF_LANES, S5_HALF_LANES), BF16),
        ] + handoff + [
            pltpu.VMEM((w // LANES, tb, LANES), F32),
            pltpu.VMEM((2 * n // LANES, rows + SUBLANES, LANES), F32),
            pltpu.VMEM((2 * n // LANES, rows, LANES), F32),
            pltpu.VMEM((SUBLANES, 2 * n), F32),
            pltpu.VMEM((w // LANES, tb, LANES), F32),
        ],
        compiler_params=_params(("arbitrary",)),
        name="s5_branch",
    )(h.reshape(b * l, d), *consts)
    return out.reshape(b, l, w)


def _s5_prepare(lam_re, lam_im, log_step, b_re, b_im, c_re, c_im, n_tiles):
    g, p, c, t8 = S5_GROUPS, S5_STATE, S5_GROUP_CH, S5_BLK
    lr, li = lam_re.astype(F32), lam_im.astype(F32)
    dt = jnp.exp(log_step.astype(F32))[:, None]
    ar, ai = lr * dt, li * dt

    def lam_pow(k):
        k = k.astype(F32)[..., None, None]
        mag = jnp.exp(k * ar)
        return mag * jnp.cos(k * ai), mag * jnp.sin(k * ai)

    br1, bi1 = lam_pow(jnp.ones((), jnp.int32))
    xr_, xi_ = br1 - 1.0, bi1
    den = lr * lr + li * li
    cr, ci = (xr_ * lr + xi_ * li) / den, (xi_ * lr - xr_ * li) / den
    bre, bim = b_re.astype(F32), b_im.astype(F32)
    bbr = cr[..., None] * bre - ci[..., None] * bim
    bbi = cr[..., None] * bim + ci[..., None] * bre
    steps = jnp.arange(t8)
    pr, pi = lam_pow(t8 - 1 - steps)
    wsr = pr[..., None] * bbr[None] - pi[..., None] * bbi[None]
    wsi = pr[..., None] * bbi[None] + pi[..., None] * bbr[None]

    def interleave(re, im, axis):
        shape = list(re.shape)
        shape[axis:axis + 1] = [re.shape[axis] // LANES, 1, LANES]
        both = jnp.concatenate([re.reshape(shape), im.reshape(shape)], axis=axis + 1)
        shape[axis:axis + 3] = [2 * re.shape[axis]]
        return both.reshape(shape)

    wc = interleave(*[part.transpose(0, 3, 1, 2).reshape(t8 * c, g * p) for part in (wsr, wsi)], axis=1)
    cre, cim = c_re.astype(F32), c_im.astype(F32)
    pr, pi = lam_pow(steps + 1)
    vor = cre[None] * pr[:, :, None, :] - cim[None] * pi[:, :, None, :]
    voi = cre[None] * pi[:, :, None, :] + cim[None] * pr[:, :, None, :]
    vc = interleave(*[part.transpose(1, 3, 0, 2).reshape(g * p, t8 * c) for part in (vor, -voi)], axis=0)
    ct = interleave(*[jnp.tile(part.transpose(0, 2, 1).reshape(g * p, c), (1, g)) for part in (cre, -cim)], axis=0)

    def re_im(pair):
        return interleave(pair[0].reshape(-1, g * p), pair[1].reshape(-1, g * p), axis=1)

    lvl = re_im(lam_pow(t8 * 2 ** jnp.arange(SUBLANES)))
    pw = re_im(lam_pow(t8 * (steps + 1)))
    pwt = re_im(lam_pow(t8 * SUBLANES * (jnp.arange(n_tiles) + 1)))
    return wc, vc, ct, lvl, pw, pwt


def _expand_heads(cols, grp, head_lane):
    base = SSD_HPG * grp
    out = jnp.broadcast_to(cols[:, base:base + 1], (cols.shape[0], SSD_GW))
    for hh in range(1, SSD_HPG):
        out = jnp.where(head_lane == hh, cols[:, base + hh:base + hh + 1], out)
    return out


def _causal_conv(pad_ref, w_ref, b_ref, rows, cols):
    acc = b_ref[:, cols] + w_ref[CONV_K - 1:CONV_K, cols] * pad_ref[SUBLANES:SUBLANES + rows, cols]
    for k in range(CONV_K - 1):
        off = SUBLANES - (CONV_K - 1) + k
        acc = acc + w_ref[k:k + 1, cols] * pad_ref[off:off + rows, cols]
    return acc


_ProjBuf = collections.namedtuple("_ProjBuf", "z xpad dt rpad gr")
_MixW = collections.namedtuple(
    "_MixW", "wz wxbc wdt wxr wgr cw cb dtb aneg dvec nw rcw rcb wa ba wx bx rlam wout g b")
_MixScratch = collections.namedtuple("_MixScratch", "state ascr bscr cexp rcar")


def _col_tiles(width):
    return [slice(c, min(c + MXU_TILE, width)) for c in range(0, width, MXU_TILE)]


def _mixer_output_jobs(h_ref, ycat, w, o_ref):
    def tile_job(cols):
        def run():
            y = jnp.dot(ycat[...], w.wout[:, cols], preferred_element_type=F32)
            o_ref[:, cols] = ALPHA * h_ref[:, cols] + y
        return run

    def norm_job():
        o_ref[...] = _layer_norm(o_ref[...], w.g[...], w.b[...])

    return [tile_job(cols) for cols in _col_tiles(D_MODEL)] + [norm_job]


def _mixer_project_jobs(h_ref, h16, w, nxt, cur, first, tq):
    def cast_job():
        h16[...] = h_ref[...].astype(BF16)

    def tile_job(dst, w_ref, cols, row0):
        def run():
            dst[row0:row0 + tq, cols] = jnp.dot(h16[...], w_ref[:, cols], preferred_element_type=F32)
        return run

    def tail_job(pad_n, pad_c):
        def run():
            pad_n[0:SUBLANES, :] = jnp.where(first, 0.0, pad_c[tq:tq + SUBLANES, :])
        return run

    jobs = [cast_job]
    for dst, w_ref, row0 in ((nxt.z, w.wz, 0), (nxt.xpad, w.wxbc, SUBLANES), (nxt.dt, w.wdt, 0),
                             (nxt.rpad, w.wxr, SUBLANES), (nxt.gr, w.wgr, 0)):
        jobs += [tile_job(dst, w_ref, cols, row0) for cols in _col_tiles(dst.shape[1])]
    jobs += [tail_job(nxt.xpad, cur.xpad), tail_job(nxt.rpad, cur.rpad)]
    return jobs


def _mixer_compute(cur, ys5_ref, w, scr, ycat, first, tq, chunk, side):
    cw_ref, cb_ref, dtb_ref, aneg_ref, dvec_ref, nw_ref = w.cw, w.cb, w.dtb, w.aneg, w.dvec, w.nw
    state = scr.state
    z = cur.z
    slabs = []
    for cols in _col_tiles(SSD_XBC):
        side(1)
        slabs.append(_silu(_causal_conv(cur.xpad, cw_ref, cb_ref, tq, cols)))
    xact = jnp.concatenate(slabs, axis=-1)
    side(2)
    dt_all = _softplus(cur.dt[...] + dtb_ref[...])
    q = chunk
    tri = (lax.broadcasted_iota(jnp.int32, (q, q), 0) >= lax.broadcasted_iota(jnp.int32, (q, q), 1))
    row_id = lax.broadcasted_iota(jnp.int32, (q, LANES), 0)
    head_lane = lax.broadcasted_iota(jnp.int32, (1, SSD_GW), 1) // SSD_HEAD_DIM
    for ci in range(tq // q):
        r0 = ci * q
        dt = dt_all[r0:r0 + q, :]
        cs = dt * aneg_ref[...]
        dist = 1
        while dist < q:
            cs = cs + jnp.where(row_id >= dist, pltpu.roll(cs, dist, 0), 0.0)
            dist *= 2
        cs_t = cs.T
        cs_last = cs[q - 1:q, :]
        chunk_decay = jnp.exp(cs_last)
        dec_t = jnp.exp(cs_t[:, q - 1:q] - cs_t)
        ecs = jnp.exp(cs)
        zc = z[r0:r0 + q, :]
        halves = []
        for grp in range(SSD_GROUPS):
            side(2)
            lo = grp * SSD_GW
            xs = xact[r0:r0 + q, lo:lo + SSD_GW]
            bm = xact[r0:r0 + q, SSD_WIDTH + grp * SSD_STATE:SSD_WIDTH + (grp + 1) * SSD_STATE]
            cm = xact[r0:r0 + q, SSD_WIDTH + (SSD_GROUPS + grp) * SSD_STATE:
                      SSD_WIDTH + (SSD_GROUPS + grp + 1) * SSD_STATE]
            cm16 = cm.astype(BF16)
            xdt16 = (xs * _expand_heads(dt, grp, head_lane)).astype(BF16)
            cb = _dot_nt(cm16, bm)
            bm_t = bm.T
            s_prev = state[grp]
            if ci == 0:
                s_prev = jnp.where(first, 0.0, s_prev)
            y_off = jnp.dot(cm16, s_prev.astype(BF16), preferred_element_type=F32)
            intra, to_state, xdt_heads = [], [], []
            for hh in range(SSD_HPG):
                hd = grp * SSD_HPG + hh
                seg = cs[:, hd:hd + 1] - cs_t[hd:hd + 1, :]
                lmat = jnp.exp(jnp.where(tri, seg, NEG_BIG))
                intra.append((cb * lmat).astype(BF16))
                to_state.append((bm_t * dec_t[hd:hd + 1, :]).astype(BF16))
                xdt_heads.append(jnp.where(head_lane == hh, xdt16, jnp.zeros_like(xdt16)))
            lhs = jnp.concatenate([jnp.concatenate(intra, axis=1), jnp.concatenate(to_state, axis=1)], axis=0)
            both = jnp.dot(lhs, jnp.concatenate(xdt_heads, axis=0), preferred_element_type=F32)
            y_diag = both[:q]
            st_new = both[q:]
            state[grp] = s_prev * _expand_heads(chunk_decay, grp, head_lane) + st_new
            halves.append(y_diag + y_off * _expand_heads(ecs, grp, head_lane) + xs * dvec_ref[:, lo:lo + SSD_GW])
        y = jnp.concatenate(halves, axis=-1) * _silu(zc)
        y = y * lax.rsqrt(jnp.mean(y * y, axis=-1, keepdims=True) + LN_EPS) * nw_ref[...]
        ycat[r0:r0 + q, 0:SSD_WIDTH] = y.astype(BF16)

    ycat[:, SSD_WIDTH:SSD_WIDTH + S5_WIDTH] = ys5_ref[...]
    _mixer_rglru(cur, w, scr, ycat, first, tq, side)


def _mixer_rglru(cur, w, scr, ycat, first, tq, side):
    side(1)
    xc = _causal_conv(cur.rpad, w.rcw, w.rcb, tq, slice(0, RG_WIDTH))
    xc16 = xc.astype(BF16)
    rgate = _sigmoid(jnp.dot(xc16, w.wa[...], preferred_element_type=F32) + w.ba[...])
    igate = _sigmoid(jnp.dot(xc16, w.wx[...], preferred_element_type=F32) + w.bx[...])
    log_a = (-RG_C) * rgate * _softplus(-w.rlam[...])
    a = jnp.exp(log_a)
    bt = jnp.sqrt(1.0 - jnp.exp(2.0 * log_a)) * (igate * xc)
    row8 = lax.broadcasted_iota(jnp.int32, (tq, RG_WIDTH), 0) & (SUBLANES - 1)
    side(1)
    for dist in (1, 2, 4):
        keep = row8 >= dist
        a_sh = jnp.where(keep, pltpu.roll(a, dist, 0), 1.0)
        b_sh = jnp.where(keep, pltpu.roll(bt, dist, 0), 0.0)
        bt = bt + a * b_sh
        a = a * a_sh
    nt = tq // SUBLANES
    lane_halves = RG_WIDTH // LANES
    for hf in range(lane_halves):
        scr.ascr[hf] = a[:, hf * LANES:(hf + 1) * LANES]
        scr.bscr[hf] = bt[:, hf * LANES:(hf + 1) * LANES]
    ae = jnp.concatenate([scr.ascr[hf, pl.ds(SUBLANES - 1, nt, stride=SUBLANES), :] for hf in range(lane_halves)], -1)
    be = jnp.concatenate([scr.bscr[hf, pl.ds(SUBLANES - 1, nt, stride=SUBLANES), :] for hf in range(lane_halves)], -1)
    tile_id = lax.broadcasted_iota(jnp.int32, (nt, RG_WIDTH), 0)
    dist = 1
    while dist < nt:
        keep = tile_id >= dist
        a_sh = jnp.where(keep, pltpu.roll(ae, dist, 0), 1.0)
        b_sh = jnp.where(keep, pltpu.roll(be, dist, 0), 0.0)
        be = be + ae * b_sh
        ae = ae * a_sh
        dist *= 2
    carry_in = jnp.where(first, 0.0, scr.rcar[0:1, :])
    tile_end = be + ae * carry_in
    scr.rcar[0:1, :] = tile_end[nt - 1:nt, :]
    tile_in = jnp.where(tile_id >= 1, pltpu.roll(tile_end, 1, 0), carry_in)
    for hf in range(lane_halves):
        for k in range(SUBLANES):
            scr.cexp[hf, pl.ds(k, nt, stride=SUBLANES), :] = tile_in[:, hf * LANES:(hf + 1) * LANES]
    hscan = bt + a * jnp.concatenate([scr.cexp[hf] for hf in range(lane_halves)], axis=-1)
    ycat[:, SSD_WIDTH + S5_WIDTH:] = (hscan * _gelu(cur.gr[...])).astype(BF16)


_MIX_W16 = ("wz", "wxbc", "wdt", "wxr", "wgr", "wout")


def _mixer_load_weights(w_in_ref, w_out_ref, w):
    o_dt = SSD_WIDTH + SSD_XBC
    o_xr = o_dt + SSD_HEADS + S5_WIDTH
    o_gr = o_xr + RG_WIDTH
    w.wz[...] = w_in_ref[:, 0:SSD_WIDTH].astype(BF16)
    w.wxbc[...] = w_in_ref[:, SSD_WIDTH:o_dt].astype(BF16)
    head_cols = lax.broadcasted_iota(jnp.int32, (1, LANES), 1) < SSD_HEADS
    w.wdt[...] = jnp.where(head_cols, w_in_ref[:, o_dt:o_dt + LANES], 0.0).astype(BF16)
    w.wxr[...] = w_in_ref[:, o_xr:o_gr].astype(BF16)
    w.wgr[...] = w_in_ref[:, o_gr:o_gr + RG_WIDTH].astype(BF16)
    w.wout[...] = w_out_ref[...].astype(BF16)


def _mixer_kernel(hp_ref, ho_ref, ys5_ref, w_in_ref, w_out_ref, *rest, tq, chunk, nblk, last):
    small = [f for f in _MixW._fields if f not in _MIX_W16]
    o_ref = rest[len(small)]
    bufs = rest[len(small) + 1:]
    w16 = bufs[len(bufs) - len(_MIX_W16):]
    bufs = bufs[:len(bufs) - len(_MIX_W16)]
    w = _MixW(**dict(zip(small, rest[:len(small)])), **dict(zip(_MIX_W16, w16)))
    npb = len(_ProjBuf._fields)
    sets = (_ProjBuf(*bufs[:npb]), _ProjBuf(*bufs[npb:2 * npb]))
    ycats = bufs[2 * npb:2 * npb + 2]
    scr = _MixScratch(*bufs[2 * npb + 2:2 * npb + 2 + len(_MixScratch._fields)])
    h16 = bufs[-1]
    s = pl.program_id(0)

    @pl.when(s == 0)
    def _():
        _mixer_load_weights(w_in_ref, w_out_ref, w)
        for ref in bufs:
            ref[...] = jnp.zeros(ref.shape, ref.dtype)

    first_p = (s % nblk) == 0
    first_c = ((s + nblk - 1) % nblk) == 0

    def step(parity, project=True, mix=True, output=True):
        nxt, cur = sets[parity], sets[1 - parity]
        ycat_c, ycat_o = ycats[1 - parity], ycats[parity]
        jobs = _mixer_output_jobs(ho_ref, ycat_o, w, o_ref) if output else []
        if project:
            jobs += _mixer_project_jobs(hp_ref, h16, w, nxt, cur, first_p, tq)

        def side(n):
            for _ in range(min(n, len(jobs))):
                jobs.pop(0)()

        if mix:
            _mixer_compute(cur, ys5_ref, w, scr, ycat_c, first_c, tq, chunk, side)
        side(len(jobs))

    pl.when(s == 0)(lambda: step(0, mix=False, output=False))
    pl.when(s == 1)(lambda: step(1, output=False))
    steady = (s >= 2) & (s <= last - 2)
    pl.when(steady & (s % 2 == 0))(lambda: step(0))
    pl.when(steady & (s % 2 == 1))(lambda: step(1))
    pl.when(s == last - 1)(lambda: step((last - 1) % 2, project=False))
    pl.when(s == last)(lambda: step(last % 2, project=False, mix=False))


def _mixer(h, ys5, w_in, w_out, consts, layer):
    b, l, d = h.shape
    tq = min(MIX_TQ, l)
    chunk = min(SSD_CHUNK, tq)
    nblk = l // tq
    total = b * nblk
    h2d = h.reshape(b * l, d)
    proj_buf = [
        pltpu.VMEM((tq, SSD_WIDTH), F32),
        pltpu.VMEM((tq + SUBLANES, SSD_XBC), F32),
        pltpu.VMEM((tq, LANES), F32),
        pltpu.VMEM((tq + SUBLANES, RG_WIDTH), F32),
        pltpu.VMEM((tq, RG_WIDTH), F32),
    ]
    rg_plane = pltpu.VMEM((RG_WIDTH // LANES, tq, LANES), F32)
    out = pl.pallas_call(
        functools.partial(_mixer_kernel, tq=tq, chunk=chunk, nblk=nblk, last=total + 1),
        grid=(total + 2,),
        in_specs=[pl.BlockSpec((tq, d), lambda s: (jnp.minimum(s, total - 1), 0)),
                  pl.BlockSpec((tq, d), lambda s: (jnp.maximum(s - 2, 0), 0)),
                  pl.BlockSpec((tq, S5_WIDTH), lambda s: (jnp.clip(s - 1, 0, total - 1), 0))]
                 + [_layer_spec(c, layer) for c in (w_in, w_out) + tuple(consts)],
        out_specs=pl.BlockSpec((tq, d), lambda s: (jnp.maximum(s - 2, 0), 0)),
        out_shape=jax.ShapeDtypeStruct((b * l, d), F32),
        scratch_shapes=proj_buf + proj_buf + [
            pltpu.VMEM((tq, D_MODEL), BF16),
            pltpu.VMEM((tq, D_MODEL), BF16),
            pltpu.VMEM((SSD_GROUPS, SSD_STATE, SSD_GW), F32),
            rg_plane, rg_plane, rg_plane,
            pltpu.VMEM((SUBLANES, RG_WIDTH), F32),
            pltpu.VMEM((tq, d), BF16),
            pltpu.VMEM((d, SSD_WIDTH), BF16),
            pltpu.VMEM((d, SSD_XBC), BF16),
            pltpu.VMEM((d, LANES), BF16),
            pltpu.VMEM((d, RG_WIDTH), BF16),
            pltpu.VMEM((d, RG_WIDTH), BF16),
            pltpu.VMEM((d, d), BF16),
        ],
        compiler_params=_params(("arbitrary",)),
        name="mixer",
    )(h2d, h2d, ys5.reshape(b * l, S5_WIDTH), w_in, w_out, *consts)
    return out.reshape(b, l, d)


def _block_diag(w):
    nl, nb, bi, bo = w.shape
    return jnp.einsum('lhij,hk->lhikj', w, jnp.eye(nb, dtype=w.dtype)).reshape(nl, nb * bi, nb * bo)


def _xattn_kernel(h_ref, m_ref, wk32_ref, wv32_ref, wq32_ref, wo32_ref, g_ref, b_ref, o_ref,
                  wq_ref, wo_ref, k_ref, v_ref):
    @pl.when((pl.program_id(0) == 0) & (pl.program_id(1) == 0))
    def _():
        wq_ref[...] = wq32_ref[...].astype(BF16)
        wo_ref[...] = wo32_ref[...].astype(BF16)

    @pl.when(pl.program_id(1) == 0)
    def _():
        m16 = m_ref[...].astype(BF16)
        k_ref[...] = jnp.dot(m16, wk32_ref[...].astype(BF16), preferred_element_type=F32).astype(BF16)
        v_ref[...] = jnp.dot(m16, wv32_ref[...].astype(BF16), preferred_element_type=F32).astype(BF16)

    hb = h_ref[...]
    qall = jnp.dot(hb.astype(BF16), wq_ref[...], preferred_element_type=F32) * (1.0 / math.sqrt(XA_HEAD_DIM))
    outs = []
    for hd in range(XA_HEADS):
        sl = slice(hd * XA_HEAD_DIM, (hd + 1) * XA_HEAD_DIM)
        s = _dot_nt(qall[:, sl], k_ref[:, sl])
        p = jnp.exp(s - jnp.max(s, axis=-1, keepdims=True))
        o = jnp.dot(p.astype(BF16), v_ref[:, sl], preferred_element_type=F32)
        outs.append(o / jnp.sum(p, axis=-1, keepdims=True))
    o16 = jnp.concatenate(outs, axis=-1).astype(BF16)
    half = o16.shape[0] // 2
    for rows in (slice(0, half), slice(half, 2 * half)):
        y = jnp.dot(o16[rows], wo_ref[...], preferred_element_type=F32)
        o_ref[rows, :] = _layer_norm(ALPHA * hb[rows] + y, g_ref[...], b_ref[...])


def _xattn(h, mem, wk, wv, wq, wo, g, bb, layer):
    b, l, d = h.shape
    m = mem.shape[1]
    tm = min(TOK_TM, l)
    tok = pl.BlockSpec((None, tm, d), lambda i, j: (i, j, 0))
    mems = pl.BlockSpec((None, m, d), lambda i, j: (i, 0, 0))
    return pl.pallas_call(
        _xattn_kernel,
        grid=(b, l // tm),
        in_specs=[tok, mems] + [_layer_spec(c, layer) for c in (wk, wv, wq, wo, g, bb)],
        out_specs=tok,
        out_shape=jax.ShapeDtypeStruct((b, l, d), F32),
        scratch_shapes=[pltpu.VMEM((d, d), BF16), pltpu.VMEM((d, d), BF16),
                        pltpu.VMEM((m, d), BF16), pltpu.VMEM((m, d), BF16)],
        compiler_params=_params(("arbitrary", "arbitrary")),
        name="xattn",
    )(h, mem, wk, wv, wq, wo, g, bb)


def _mlp_kernel(h_ref, w1_ref, w2_ref, g_ref, b_ref, o_ref):
    hb = h_ref[...]
    h16 = hb.astype(BF16)
    acc = ALPHA * hb
    for f in range(D_FF // MLP_FC):
        sl = slice(f * MLP_FC, (f + 1) * MLP_FC)
        t = jnp.maximum(jnp.dot(h16, w1_ref[:, sl], preferred_element_type=F32), 0.0)
        acc = acc + jnp.dot((t * t).astype(BF16), w2_ref[sl, :], preferred_element_type=F32)
    o_ref[...] = _layer_norm(acc, g_ref[...], b_ref[...])


def _mlp(h2d, w1, w2, g, bb, layer):
    t, d = h2d.shape
    tm = min(MLP_TM, t)
    tok = pl.BlockSpec((tm, d), lambda i: (i, 0))
    return pl.pallas_call(
        _mlp_kernel,
        grid=(t // tm,),
        in_specs=[tok] + [_layer_spec(c, layer) for c in (w1, w2, g, bb)],
        out_specs=tok,
        out_shape=jax.ShapeDtypeStruct((t, d), F32),
        compiler_params=_params(("parallel",)),
        name="mlp",
    )(h2d, w1, w2, g, bb)


def _rows(v):
    return v.astype(F32).reshape(v.shape[0], 1, -1)


def _pad_lanes(v3d):
    return jnp.pad(v3d, ((0, 0), (0, 0), (0, LANES - v3d.shape[2])))


def prepare_params(seq_len, w_in, w_out, ssd_conv_w, ssd_conv_b, ssd_dt_bias, ssd_a_log, ssd_d, ssd_norm_w,
                   s5_lam_re, s5_lam_im, s5_log_step, s5_b_re, s5_b_im, s5_c_re, s5_c_im, s5_d, s5_glu_w, s5_glu_b,
                   rg_conv_w, rg_conv_b, rg_wa, rg_ba, rg_wx, rg_bx, rg_lambda, ln1_g, ln1_b,
                   xa_wq, xa_wk, xa_wv, xa_wo, ln2_g, ln2_b, mlp_w1, mlp_w2, ln3_g, ln3_b):
    o_u = SSD_WIDTH + SSD_XBC + SSD_HEADS
    n_tiles = min(S5_TB, seq_len) // (S5_BLK * SUBLANES)
    assert n_tiles <= 2 ** (SUBLANES - 3), "S5 scan-level table holds 3 in-tile + 5 tile levels"
    s5_ops = jax.vmap(functools.partial(_s5_prepare, n_tiles=n_tiles))(
        s5_lam_re, s5_lam_im, s5_log_step, s5_b_re, s5_b_im, s5_c_re, s5_c_im)
    s5 = (w_in[:, :, o_u:o_u + S5_WIDTH].astype(BF16), *s5_ops, _rows(s5_d), s5_glu_w.astype(BF16), _rows(s5_glu_b))
    mix_small = dict(
        cw=ssd_conv_w.astype(F32), cb=_rows(ssd_conv_b), dtb=_pad_lanes(_rows(ssd_dt_bias)),
        aneg=_pad_lanes(_rows(-jnp.exp(ssd_a_log.astype(F32)))),
        dvec=_rows(jnp.repeat(ssd_d.astype(F32), SSD_HEAD_DIM, axis=1)), nw=_rows(ssd_norm_w),
        rcw=rg_conv_w.astype(F32), rcb=_rows(rg_conv_b),
        wa=_block_diag(rg_wa.astype(F32)).astype(BF16), ba=_rows(rg_ba.reshape(rg_ba.shape[0], -1)),
        wx=_block_diag(rg_wx.astype(F32)).astype(BF16), bx=_rows(rg_bx.reshape(rg_bx.shape[0], -1)),
        rlam=_rows(rg_lambda), g=_rows(ln1_g), b=_rows(ln1_b))
    mix = (w_in.astype(F32), w_out.astype(F32), tuple(mix_small[f] for f in _MixW._fields if f not in _MIX_W16))
    xa = (xa_wk.astype(F32), xa_wv.astype(F32), xa_wq.astype(F32), xa_wo.astype(F32), _rows(ln2_g), _rows(ln2_b))
    mlp = (mlp_w1.astype(BF16), mlp_w2.astype(BF16), _rows(ln3_g), _rows(ln3_b))
    return dict(s5=s5, mix=mix, xa=xa, mlp=mlp)


def mixer_sublayer(h, prm, layer):
    ys5 = _s5_branch(h, prm["s5"], layer)
    w_in, w_out, consts = prm["mix"]
    return _mixer(h, ys5, w_in, w_out, consts, layer)


def xattn_sublayer(h, mem, prm, layer):
    return _xattn(h, mem, *prm["xa"], layer)


def mlp_sublayer(h, prm, layer):
    b, l, d = h.shape
    return _mlp(h.reshape(b * l, d), *prm["mlp"], layer).reshape(b, l, d)


def kernel(x, mem, w_in, w_out, ssd_conv_w, ssd_conv_b, ssd_dt_bias, ssd_a_log, ssd_d, ssd_norm_w, s5_lam_re, s5_lam_im, s5_log_step, s5_b_re, s5_b_im, s5_c_re, s5_c_im, s5_d, s5_glu_w, s5_glu_b, rg_conv_w, rg_conv_b, rg_wa, rg_ba, rg_wx, rg_bx, rg_lambda, ln1_g, ln1_b, xa_wq, xa_wk, xa_wv, xa_wo, ln2_g, ln2_b, mlp_w1, mlp_w2, ln3_g, ln3_b):
    h = x.astype(F32)
    memf = mem.astype(F32)
    prm = prepare_params(x.shape[1], w_in, w_out, ssd_conv_w, ssd_conv_b, ssd_dt_bias, ssd_a_log, ssd_d, ssd_norm_w,
                         s5_lam_re, s5_lam_im, s5_log_step, s5_b_re, s5_b_im, s5_c_re, s5_c_im, s5_d, s5_glu_w,
                         s5_glu_b, rg_conv_w, rg_conv_b, rg_wa, rg_ba, rg_wx, rg_bx, rg_lambda, ln1_g, ln1_b,
                         xa_wq, xa_wk, xa_wv, xa_wo, ln2_g, ln2_b, mlp_w1, mlp_w2, ln3_g, ln3_b)
    for layer in range(DEPTH):
        h = mixer_sublayer(h, prm, layer)
        h = xattn_sublayer(h, memf, prm, layer)
        h = mlp_sublayer(h, prm, layer)
    return h.astype(x.dtype)
```

```python
import collections
import functools
import math

import jax
import jax.numpy as jnp
from jax import lax
from jax.experimental import pallas as pl
from jax.experimental.pallas import tpu as pltpu

F32 = jnp.float32
BF16 = jnp.bfloat16

D_MODEL = 1024
DEPTH = 2
SSD_WIDTH = 512
SSD_HEAD_DIM = 64
SSD_HEADS = 8
SSD_GROUPS = 2
SSD_HPG = SSD_HEADS // SSD_GROUPS
SSD_GW = SSD_HPG * SSD_HEAD_DIM
SSD_STATE = 128
SSD_XBC = 1024
CONV_K = 4
S5_WIDTH = 256
S5_GROUPS = 16
S5_GROUP_CH = 16
S5_STATE = 64
S5_NSTATE = S5_GROUPS * S5_STATE
S5_BLK = 8
RG_WIDTH = 256
RG_BLOCKS = 4
RG_C = 8.0
XA_HEADS = 4
XA_HEAD_DIM = 256
D_FF = 4096
ALPHA = (2.0 * DEPTH) ** 0.25
LN_EPS = 1e-5

LANES = 128
SUBLANES = 8
MXU_TILE = 256
NEG_BIG = -1e30
VMEM_LIMIT = 56 * 1024 * 1024

SSD_CHUNK = 128
MIX_TQ = 256
S5_TB = 2048
TOK_TM = 1024
XA_OUT_PARTS = 4
MLP_TM = 1024
MLP_FC = 1024


def _dot(a, b):
    return jnp.dot(a.astype(BF16), b.astype(BF16), preferred_element_type=F32)


def _dot_nt(a, b):
    return lax.dot_general(a.astype(BF16), b.astype(BF16), (((1,), (1,)), ((), ())),
                           preferred_element_type=F32)


def _layer_norm(x, g, b):
    mu = jnp.mean(x, axis=-1, keepdims=True)
    xc = x - mu
    var = jnp.mean(xc * xc, axis=-1, keepdims=True)
    return xc * lax.rsqrt(var + LN_EPS) * g + b


def _gelu(x):
    c = math.sqrt(2.0 / math.pi)
    return 0.5 * x * (1.0 + jnp.tanh(c * (x + 0.044715 * (x * x * x))))


def _sigmoid(x):
    return 0.5 + 0.5 * jnp.tanh(0.5 * x)


def _silu(x):
    hx = 0.5 * x
    return hx + hx * jnp.tanh(hx)


def _softplus(x):
    return jnp.maximum(x, 0.0) + jnp.log1p(jnp.exp(-jnp.abs(x)))


def _const_spec(shape):
    nd = len(shape)
    return pl.BlockSpec(shape, lambda *_: (0,) * nd, pipeline_mode=pl.Buffered(1))


def _layer_spec(stacked, layer):
    nd = stacked.ndim - 1
    return pl.BlockSpec((None,) + stacked.shape[1:], lambda *_: (layer,) + (0,) * nd, pipeline_mode=pl.Buffered(1))


def _params(sem):
    return pltpu.CompilerParams(dimension_semantics=sem, vmem_limit_bytes=VMEM_LIMIT)


def _re_lanes(p):
    return slice(2 * p * LANES, (2 * p + 1) * LANES)


def _im_lanes(p):
    return slice((2 * p + 1) * LANES, (2 * p + 2) * LANES)


def _s5_state_group(idx):
    state = (idx // (2 * LANES)) * LANES + idx % LANES
    return state // S5_STATE


def _s5_split(x):
    planes = S5_NSTATE // LANES
    return (jnp.concatenate([x[:, _re_lanes(p)] for p in range(planes)], axis=-1),
            jnp.concatenate([x[:, _im_lanes(p)] for p in range(planes)], axis=-1))


S5_HALF_CH = S5_WIDTH // 2
S5_HALF_LANES = S5_BLK * S5_HALF_CH


def _s5_expand_operators(wc_ref, vc_ref, ct_ref, ms_scr, toep_scr, mo_scr):
    n, w, c = S5_NSTATE, S5_WIDTH, S5_GROUP_CH
    hw, hl = S5_HALF_CH, S5_HALF_LANES
    grp_half = S5_GROUPS // 2
    reps = hw // c
    grp_r = _s5_state_group(lax.broadcasted_iota(jnp.int32, (2 * n, w), 0))
    same_o = grp_r == lax.broadcasted_iota(jnp.int32, (2 * n, w), 1) // c
    mo0 = jnp.where(same_o, ct_ref[...], 0.0)
    kc = jnp.dot(wc_ref[...], mo0, preferred_element_type=F32, precision=lax.Precision.HIGHEST)
    toep_scr[...] = jnp.zeros(toep_scr.shape, toep_scr.dtype)
    ch_grp = lax.broadcasted_iota(jnp.int32, (hw, hl), 0) // c
    same_k = (lax.broadcasted_iota(jnp.int32, (hw, hw), 0) // c) == (lax.broadcasted_iota(jnp.int32, (hw, hw), 1) // c)
    pick_r = lax.broadcasted_iota(jnp.int32, (S5_BLK * c, hl), 0)
    pick_c = lax.broadcasted_iota(jnp.int32, (S5_BLK * c, hl), 1)
    pick = ((pick_r // c == pick_c // hw) & (pick_r % c == pick_c % c)).astype(BF16)
    for hf in range(2):
        lanes = slice(hf * hl, (hf + 1) * hl)
        st_grp = _s5_state_group(lax.broadcasted_iota(jnp.int32, (hw, hl), 1) + hf * hl) - hf * grp_half
        same_s = ch_grp == st_grp
        for t in range(S5_BLK):
            blk = jnp.where(same_s, jnp.tile(wc_ref[t * c:(t + 1) * c, lanes], (reps, 1)), 0.0)
            ms_scr[hf, t * hw:(t + 1) * hw, :] = blk.astype(BF16)
        st_grp = _s5_state_group(lax.broadcasted_iota(jnp.int32, (hl, hl), 0) + hf * hl) - hf * grp_half
        same_r = st_grp == (lax.broadcasted_iota(jnp.int32, (hl, hl), 1) % hw) // c
        spread = jnp.dot(vc_ref[lanes, :].astype(BF16), pick, preferred_element_type=F32)
        mo_scr[hf] = jnp.where(same_r, spread, 0.0).astype(BF16)
        for t_in in range(S5_BLK):
            for t_out in range(t_in, S5_BLK):
                k = S5_BLK - 1 - (t_out - t_in)
                blk = jnp.where(same_k, jnp.tile(kc[k * c:(k + 1) * c, hf * hw:(hf + 1) * hw], (reps, 1)), 0.0)
                toep_scr[hf, t_in * hw:(t_in + 1) * hw, t_out * hw:(t_out + 1) * hw] = blk.astype(BF16)


_S5W = collections.namedtuple("_S5W", "wu wc vc ct lvl pw pwt d gluw glub")
_S5Ops = collections.namedtuple("_S5Ops", "ms toep mo")
_S5Buf = collections.namedtuple("_S5Buf", "u32 u16 hprev16")
_S5Scratch = collections.namedtuple("_S5Scratch", "u_tok hplane cexp carry yg")


def _s5_scan_stage(h_ref, w, ops, scr, out, first, rows, side):
    n, wd = S5_NSTATE, S5_WIDTH
    lane_halves = wd // LANES
    planes = n // LANES
    nt = rows // SUBLANES
    tb = rows * S5_BLK
    for r0 in range(0, tb, tb // 4):
        rs = slice(r0, r0 + tb // 4)
        u_tok = jnp.dot(h_ref[rs, :].astype(BF16), w.wu[...], preferred_element_type=F32)
        for hf in range(lane_halves):
            scr.u_tok[hf, rs, :] = u_tok[:, hf * LANES:(hf + 1) * LANES]
    u = jnp.concatenate([scr.u_tok[hf, pl.ds(k, rows, stride=S5_BLK), :]
                         for hf in range(lane_halves) for k in range(S5_BLK)], axis=-1)
    out.u32[...] = u
    u16 = u.astype(BF16)
    out.u16[...] = u16
    row8 = lax.broadcasted_iota(jnp.int32, (nt, SUBLANES, LANES), 1)
    planes_per_half = planes // lane_halves
    for p in range(planes):
        hf, pl_in_half = divmod(p, planes_per_half)
        s = jnp.dot(u16[:, hf * S5_HALF_LANES:(hf + 1) * S5_HALF_LANES],
                    ops.ms[hf, :, 2 * pl_in_half * LANES:(2 * pl_in_half + 2) * LANES],
                    preferred_element_type=F32)
        if p < planes - 2:
            side(1)
        xr = s[:, :LANES].reshape(nt, SUBLANES, LANES)
        xi = s[:, LANES:].reshape(nt, SUBLANES, LANES)
        for lv, dist in enumerate((1, 2, 4)):
            lr = w.lvl[lv:lv + 1, _re_lanes(p)]
            li = w.lvl[lv:lv + 1, _im_lanes(p)]
            keep = row8 >= dist
            sr = jnp.where(keep, pltpu.roll(xr, dist, 1), 0.0)
            si = jnp.where(keep, pltpu.roll(xi, dist, 1), 0.0)
            xr, xi = xr + lr * sr - li * si, xi + lr * si + li * sr
        scr.hplane[2 * p, SUBLANES:SUBLANES + rows, :] = xr.reshape(rows, LANES)
        scr.hplane[2 * p + 1, SUBLANES:SUBLANES + rows, :] = xi.reshape(rows, LANES)
    side(1)
    last = 2 * SUBLANES - 1
    er = jnp.concatenate([scr.hplane[2 * p, pl.ds(last, nt, stride=SUBLANES), :] for p in range(planes)], axis=-1)
    ei = jnp.concatenate([scr.hplane[2 * p + 1, pl.ds(last, nt, stride=SUBLANES), :] for p in range(planes)], -1)
    tile_id = lax.broadcasted_iota(jnp.int32, (nt, n), 0)
    lv, dist = 3, 1
    while dist < nt:
        lr, li = _s5_split(w.lvl[lv:lv + 1, :])
        keep = tile_id >= dist
        sr = jnp.where(keep, pltpu.roll(er, dist, 0), 0.0)
        si = jnp.where(keep, pltpu.roll(ei, dist, 0), 0.0)
        er, ei = er + lr * sr - li * si, ei + lr * si + li * sr
        lv, dist = lv + 1, dist * 2
    cr, ci = _s5_split(jnp.where(first, 0.0, scr.carry[0:1, :]))
    ptr, pti = _s5_split(w.pwt[...])
    er, ei = er + ptr * cr - pti * ci, ei + ptr * ci + pti * cr
    for p in range(planes):
        scr.carry[0:1, _re_lanes(p)] = er[nt - 1:nt, p * LANES:(p + 1) * LANES]
        scr.carry[0:1, _im_lanes(p)] = ei[nt - 1:nt, p * LANES:(p + 1) * LANES]
    tin_r = jnp.where(tile_id >= 1, pltpu.roll(er, 1, 0), cr)
    tin_i = jnp.where(tile_id >= 1, pltpu.roll(ei, 1, 0), ci)
    side(1)
    for p in range(planes):
        for k in range(SUBLANES):
            scr.cexp[2 * p, pl.ds(k, nt, stride=SUBLANES), :] = tin_r[:, p * LANES:(p + 1) * LANES]
            scr.cexp[2 * p + 1, pl.ds(k, nt, stride=SUBLANES), :] = tin_i[:, p * LANES:(p + 1) * LANES]
    for p in range(planes):
        if p % 2 == 0:
            side(1)
        pr = jnp.tile(w.pw[:, _re_lanes(p)], (nt, 1))
        pi = jnp.tile(w.pw[:, _im_lanes(p)], (nt, 1))
        tr = scr.cexp[2 * p]
        ti = scr.cexp[2 * p + 1]
        hr = scr.hplane[2 * p, SUBLANES:SUBLANES + rows, :] + pr * tr - pi * ti
        hi = scr.hplane[2 * p + 1, SUBLANES:SUBLANES + rows, :] + pr * ti + pi * tr
        scr.hplane[2 * p, SUBLANES:SUBLANES + rows, :] = hr
        scr.hplane[2 * p + 1, SUBLANES:SUBLANES + rows, :] = hi
        scr.hplane[2 * p, SUBLANES - 1:SUBLANES, :] = cr[:, p * LANES:(p + 1) * LANES]
        scr.hplane[2 * p + 1, SUBLANES - 1:SUBLANES, :] = ci[:, p * LANES:(p + 1) * LANES]
    for p in range(2 * planes):
        out.hprev16[:, p * LANES:(p + 1) * LANES] = scr.hplane[p, SUBLANES - 1:SUBLANES - 1 + rows, :].astype(BF16)


def _s5_readout_jobs(src, w, ops, scr, o_ref, rows):
    hw, hl = S5_HALF_CH, S5_HALF_LANES
    lane_halves = S5_WIDTH // LANES
    steps_per_tile = MXU_TILE // hw

    def tile_job(hf, j):
        def run():
            k_rows = (j + 1) * MXU_TILE
            cols = slice(j * MXU_TILE, (j + 1) * MXU_TILE)
            y = jnp.dot(src.u16[:, hf * hl:hf * hl + k_rows], ops.toep[hf, :k_rows, cols],
                        preferred_element_type=F32)
            y = y + jnp.dot(src.hprev16[:, hf * hl:(hf + 1) * hl], ops.mo[hf, :, cols], preferred_element_type=F32)
            d_half = w.d[:, hf * hw:(hf + 1) * hw]
            y = _gelu(y + jnp.concatenate([d_half] * steps_per_tile, axis=1) * src.u32[:, hf * hl + j * MXU_TILE:
                                                                                           hf * hl + (j + 1) * MXU_TILE])
            for i in range(steps_per_tile):
                scr.yg[hf, pl.ds(j * steps_per_tile + i, rows, stride=S5_BLK), :] = y[:, i * hw:(i + 1) * hw]
        return run

    def glu_job():
        yg = jnp.concatenate([scr.yg[hf] for hf in range(lane_halves)], axis=-1)
        glu = _sigmoid(_dot(yg, w.gluw[...]) + w.glub[...])
        o_ref[...] = (yg * glu).astype(o_ref.dtype)

    return [tile_job(hf, j) for hf in range(lane_halves) for j in range(hl // MXU_TILE)] + [glu_job]


def _s5_kernel(h_ref, *rest, rows, nblk):
    nw = len(_S5W._fields)
    w = _S5W(*rest[:nw])
    o_ref = rest[nw]
    bufs = rest[nw + 1:]
    ops = _S5Ops(*bufs[:3])
    nb = len(_S5Buf._fields)
    scr = _S5Scratch(*bufs[3 + nb:])
    s = pl.program_id(0)
    parity = s % 2

    @pl.when(s == 0)
    def _():
        _s5_expand_operators(w.wc, w.vc, w.ct, *ops)
        for ref in bufs[3:]:
            ref[...] = jnp.zeros(ref.shape, ref.dtype)

    first = (s % nblk) == 0

    def step(out, src):
        jobs = _s5_readout_jobs(src, w, ops, scr, o_ref, rows)

        def side(k):
            for _ in range(min(k, len(jobs))):
                jobs.pop(0)()

        _s5_scan_stage(h_ref, w, ops, scr, out, first, rows, side)
        side(len(jobs))

    step(_S5Buf(*(r.at[parity] for r in bufs[3:3 + nb])), _S5Buf(*(r.at[1 - parity] for r in bufs[3:3 + nb])))


def _s5_branch(h, consts, layer):
    b, l, d = h.shape
    tb = min(S5_TB, l)
    rows = tb // S5_BLK
    nblk = l // tb
    total = b * nblk
    n, w = S5_NSTATE, S5_WIDTH
    handoff = [
        pltpu.VMEM((2, rows, S5_BLK * w), F32),
        pltpu.VMEM((2, rows, S5_BLK * w), BF16),
        pltpu.VMEM((2, rows, 2 * n), BF16),
    ]
    out = pl.pallas_call(
        functools.partial(_s5_kernel, rows=rows, nblk=nblk),
        grid=(total + 1,),
        in_specs=[pl.BlockSpec((tb, d), lambda s: (jnp.minimum(s, total - 1), 0))]
                 + [_layer_spec(c, layer) for c in consts],
        out_specs=pl.BlockSpec((tb, w), lambda s: (jnp.maximum(s - 1, 0), 0)),
        out_shape=jax.ShapeDtypeStruct((b * l, w), BF16),
        scratch_shapes=[
            pltpu.VMEM((2, S5_HALF_LANES, S5_HALF_LANES), BF16),
            pltpu.VMEM((2, S5_HALF_LANES, S5_HALF_LANES), BF16),
            pltpu.VMEM((2, S5_HALF_LANES, S5_HALF_LANES), BF16),
        ] + handoff + [
            pltpu.VMEM((w // LANES, tb, LANES), F32),
            pltpu.VMEM((2 * n // LANES, rows + SUBLANES, LANES), F32),
            pltpu.VMEM((2 * n // LANES, rows, LANES), F32),
            pltpu.VMEM((SUBLANES, 2 * n), F32),
            pltpu.VMEM((w // LANES, tb, LANES), F32),
        ],
        compiler_params=_params(("arbitrary",)),
        name="s5_branch",
    )(h.reshape(b * l, d), *consts)
    return out.reshape(b, l, w)


def _s5_prepare(lam_re, lam_im, log_step, b_re, b_im, c_re, c_im, n_tiles):
    g, p, c, t8 = S5_GROUPS, S5_STATE, S5_GROUP_CH, S5_BLK
    lr, li = lam_re.astype(F32), lam_im.astype(F32)
    dt = jnp.exp(log_step.astype(F32))[:, None]
    ar, ai = lr * dt, li * dt

    def lam_pow(k):
        k = k.astype(F32)[..., None, None]
        mag = jnp.exp(k * ar)
        return mag * jnp.cos(k * ai), mag * jnp.sin(k * ai)

    br1, bi1 = lam_pow(jnp.ones((), jnp.int32))
    xr_, xi_ = br1 - 1.0, bi1
    den = lr * lr + li * li
    cr, ci = (xr_ * lr + xi_ * li) / den, (xi_ * lr - xr_ * li) / den
    bre, bim = b_re.astype(F32), b_im.astype(F32)
    bbr = cr[..., None] * bre - ci[..., None] * bim
    bbi = cr[..., None] * bim + ci[..., None] * bre
    steps = jnp.arange(t8)
    pr, pi = lam_pow(t8 - 1 - steps)
    wsr = pr[..., None] * bbr[None] - pi[..., None] * bbi[None]
    wsi = pr[..., None] * bbi[None] + pi[..., None] * bbr[None]

    def interleave(re, im, axis):
        shape = list(re.shape)
        shape[axis:axis + 1] = [re.shape[axis] // LANES, 1, LANES]
        both = jnp.concatenate([re.reshape(shape), im.reshape(shape)], axis=axis + 1)
        shape[axis:axis + 3] = [2 * re.shape[axis]]
        return both.reshape(shape)

    wc = interleave(*[part.transpose(0, 3, 1, 2).reshape(t8 * c, g * p) for part in (wsr, wsi)], axis=1)
    cre, cim = c_re.astype(F32), c_im.astype(F32)
    pr, pi = lam_pow(steps + 1)
    vor = cre[None] * pr[:, :, None, :] - cim[None] * pi[:, :, None, :]
    voi = cre[None] * pi[:, :, None, :] + cim[None] * pr[:, :, None, :]
    vc = interleave(*[part.transpose(1, 3, 0, 2).reshape(g * p, t8 * c) for part in (vor, -voi)], axis=0)
    ct = interleave(*[jnp.tile(part.transpose(0, 2, 1).reshape(g * p, c), (1, g)) for part in (cre, -cim)], axis=0)

    def re_im(pair):
        return interleave(pair[0].reshape(-1, g * p), pair[1].reshape(-1, g * p), axis=1)

    lvl = re_im(lam_pow(t8 * 2 ** jnp.arange(SUBLANES)))
    pw = re_im(lam_pow(t8 * (steps + 1)))
    pwt = re_im(lam_pow(t8 * SUBLANES * (jnp.arange(n_tiles) + 1)))
    return wc, vc, ct, lvl, pw, pwt


def _expand_heads(cols, grp, head_lane):
    base = SSD_HPG * grp
    out = jnp.broadcast_to(cols[:, base:base + 1], (cols.shape[0], SSD_GW))
    for hh in range(1, SSD_HPG):
        out = jnp.where(head_lane == hh, cols[:, base + hh:base + hh + 1], out)
    return out


def _causal_conv(pad_ref, w_ref, b_ref, rows, cols):
    acc = b_ref[:, cols] + w_ref[CONV_K - 1:CONV_K, cols] * pad_ref[SUBLANES:SUBLANES + rows, cols]
    for k in range(CONV_K - 1):
        off = SUBLANES - (CONV_K - 1) + k
        acc = acc + w_ref[k:k + 1, cols] * pad_ref[off:off + rows, cols]
    return acc


_ProjBuf = collections.namedtuple("_ProjBuf", "z xpad dt rpad gr")
_MixW = collections.namedtuple(
    "_MixW", "wz wxbc wdt wxr wgr cw cb dtb aneg dvec nw rcw rcb wa ba wx bx rlam wout g b")
_MixScratch = collections.namedtuple("_MixScratch", "state ascr bscr cexp rcar")


def _col_tiles(width):
    return [slice(c, min(c + MXU_TILE, width)) for c in range(0, width, MXU_TILE)]


def _mixer_output_jobs(h_ref, ycat, w, o_ref):
    def tile_job(cols):
        def run():
            y = jnp.dot(ycat[...], w.wout[:, cols], preferred_element_type=F32)
            o_ref[:, cols] = ALPHA * h_ref[:, cols] + y
        return run

    def norm_job():
        o_ref[...] = _layer_norm(o_ref[...], w.g[...], w.b[...])

    return [tile_job(cols) for cols in _col_tiles(D_MODEL)] + [norm_job]


def _mixer_project_jobs(h_ref, h16, w, nxt, cur, first, tq):
    def cast_job():
        h16[...] = h_ref[...].astype(BF16)

    def tile_job(dst, w_ref, cols, row0):
        def run():
            dst[row0:row0 + tq, cols] = jnp.dot(h16[...], w_ref[:, cols], preferred_element_type=F32)
        return run

    def tail_job(pad_n, pad_c):
        def run():
            pad_n[0:SUBLANES, :] = jnp.where(first, 0.0, pad_c[tq:tq + SUBLANES, :])
        return run

    jobs = [cast_job]
    for dst, w_ref, row0 in ((nxt.z, w.wz, 0), (nxt.xpad, w.wxbc, SUBLANES), (nxt.dt, w.wdt, 0),
                             (nxt.rpad, w.wxr, SUBLANES), (nxt.gr, w.wgr, 0)):
        jobs += [tile_job(dst, w_ref, cols, row0) for cols in _col_tiles(dst.shape[1])]
    jobs += [tail_job(nxt.xpad, cur.xpad), tail_job(nxt.rpad, cur.rpad)]
    return jobs


def _mixer_compute(cur, ys5_ref, w, scr, ycat, first, tq, chunk, side):
    cw_ref, cb_ref, dtb_ref, aneg_ref, dvec_ref, nw_ref = w.cw, w.cb, w.dtb, w.aneg, w.dvec, w.nw
    state = scr.state
    z = cur.z
    slabs = []
    for cols in _col_tiles(SSD_XBC):
        side(1)
        slabs.append(_silu(_causal_conv(cur.xpad, cw_ref, cb_ref, tq, cols)))
    xact = jnp.concatenate(slabs, axis=-1)
    side(2)
    dt_all = _softplus(cur.dt[...] + dtb_ref[...])
    q = chunk
    tri = (lax.broadcasted_iota(jnp.int32, (q, q), 0) >= lax.broadcasted_iota(jnp.int32, (q, q), 1))
    row_id = lax.broadcasted_iota(jnp.int32, (q, LANES), 0)
    head_lane = lax.broadcasted_iota(jnp.int32, (1, SSD_GW), 1) // SSD_HEAD_DIM
    for ci in range(tq // q):
        r0 = ci * q
        dt = dt_all[r0:r0 + q, :]
        cs = dt * aneg_ref[...]
        dist = 1
        while dist < q:
            cs = cs + jnp.where(row_id >= dist, pltpu.roll(cs, dist, 0), 0.0)
            dist *= 2
        cs_t = cs.T
        cs_last = cs[q - 1:q, :]
        chunk_decay = jnp.exp(cs_last)
        dec_t = jnp.exp(cs_t[:, q - 1:q] - cs_t)
        ecs = jnp.exp(cs)
        zc = z[r0:r0 + q, :]
        halves = []
        for grp in range(SSD_GROUPS):
            side(2)
            lo = grp * SSD_GW
            xs = xact[r0:r0 + q, lo:lo + SSD_GW]
            bm = xact[r0:r0 + q, SSD_WIDTH + grp * SSD_STATE:SSD_WIDTH + (grp + 1) * SSD_STATE]
            cm = xact[r0:r0 + q, SSD_WIDTH + (SSD_GROUPS + grp) * SSD_STATE:
                      SSD_WIDTH + (SSD_GROUPS + grp + 1) * SSD_STATE]
            cm16 = cm.astype(BF16)
            xdt16 = (xs * _expand_heads(dt, grp, head_lane)).astype(BF16)
            cb = _dot_nt(cm16, bm)
            bm_t = bm.T
            s_prev = state[grp]
            if ci == 0:
                s_prev = jnp.where(first, 0.0, s_prev)
            y_off = jnp.dot(cm16, s_prev.astype(BF16), preferred_element_type=F32)
            intra, to_state, xdt_heads = [], [], []
            for hh in range(SSD_HPG):
                hd = grp * SSD_HPG + hh
                seg = cs[:, hd:hd + 1] - cs_t[hd:hd + 1, :]
                lmat = jnp.exp(jnp.where(tri, seg, NEG_BIG))
                intra.append((cb * lmat).astype(BF16))
                to_state.append((bm_t * dec_t[hd:hd + 1, :]).astype(BF16))
                xdt_heads.append(jnp.where(head_lane == hh, xdt16, jnp.zeros_like(xdt16)))
            lhs = jnp.concatenate([jnp.concatenate(intra, axis=1), jnp.concatenate(to_state, axis=1)], axis=0)
            both = jnp.dot(lhs, jnp.concatenate(xdt_heads, axis=0), preferred_element_type=F32)
            y_diag = both[:q]
            st_new = both[q:]
            state[grp] = s_prev * _expand_heads(chunk_decay, grp, head_lane) + st_new
            halves.append(y_diag + y_off * _expand_heads(ecs, grp, head_lane) + xs * dvec_ref[:, lo:lo + SSD_GW])
        y = jnp.concatenate(halves, axis=-1) * _silu(zc)
        y = y * lax.rsqrt(jnp.mean(y * y, axis=-1, keepdims=True) + LN_EPS) * nw_ref[...]
        ycat[r0:r0 + q, 0:SSD_WIDTH] = y.astype(BF16)

    ycat[:, SSD_WIDTH:SSD_WIDTH + S5_WIDTH] = ys5_ref[...]
    _mixer_rglru(cur, w, scr, ycat, first, tq, side)


def _mixer_rglru(cur, w, scr, ycat, first, tq, side):
    side(1)
    xc = _causal_conv(cur.rpad, w.rcw, w.rcb, tq, slice(0, RG_WIDTH))
    xc16 = xc.astype(BF16)
    rgate = _sigmoid(jnp.dot(xc16, w.wa[...], preferred_element_type=F32) + w.ba[...])
    igate = _sigmoid(jnp.dot(xc16, w.wx[...], preferred_element_type=F32) + w.bx[...])
    log_a = (-RG_C) * rgate * _softplus(-w.rlam[...])
    a = jnp.exp(log_a)
    bt = jnp.sqrt(1.0 - jnp.exp(2.0 * log_a)) * (igate * xc)
    row8 = lax.broadcasted_iota(jnp.int32, (tq, RG_WIDTH), 0) & (SUBLANES - 1)
    side(1)
    for dist in (1, 2, 4):
        keep = row8 >= dist
        a_sh = jnp.where(keep, pltpu.roll(a, dist, 0), 1.0)
        b_sh = jnp.where(keep, pltpu.roll(bt, dist, 0), 0.0)
        bt = bt + a * b_sh
        a = a * a_sh
    nt = tq // SUBLANES
    lane_halves = RG_WIDTH // LANES
    for hf in range(lane_halves):
        scr.ascr[hf] = a[:, hf * LANES:(hf + 1) * LANES]
        scr.bscr[hf] = bt[:, hf * LANES:(hf + 1) * LANES]
    ae = jnp.concatenate([scr.ascr[hf, pl.ds(SUBLANES - 1, nt, stride=SUBLANES), :] for hf in range(lane_halves)], -1)
    be = jnp.concatenate([scr.bscr[hf, pl.ds(SUBLANES - 1, nt, stride=SUBLANES), :] for hf in range(lane_halves)], -1)
    tile_id = lax.broadcasted_iota(jnp.int32, (nt, RG_WIDTH), 0)
    dist = 1
    while dist < nt:
        keep = tile_id >= dist
        a_sh = jnp.where(keep, pltpu.roll(ae, dist, 0), 1.0)
        b_sh = jnp.where(keep, pltpu.roll(be, dist, 0), 0.0)
        be = be + ae * b_sh
        ae = ae * a_sh
        dist *= 2
    carry_in = jnp.where(first, 0.0, scr.rcar[0:1, :])
    tile_end = be + ae * carry_in
    scr.rcar[0:1, :] = tile_end[nt - 1:nt, :]
    tile_in = jnp.where(tile_id >= 1, pltpu.roll(tile_end, 1, 0), carry_in)
    for hf in range(lane_halves):
        for k in range(SUBLANES):
            scr.cexp[hf, pl.ds(k, nt, stride=SUBLANES), :] = tile_in[:, hf * LANES:(hf + 1) * LANES]
    hscan = bt + a * jnp.concatenate([scr.cexp[hf] for hf in range(lane_halves)], axis=-1)
    ycat[:, SSD_WIDTH + S5_WIDTH:] = (hscan * _gelu(cur.gr[...])).astype(BF16)


_MIX_W16 = ("wz", "wxbc", "wdt", "wxr", "wgr", "wout")


def _mixer_load_weights(w_in_ref, w_out_ref, w):
    o_dt = SSD_WIDTH + SSD_XBC
    o_xr = o_dt + SSD_HEADS + S5_WIDTH
    o_gr = o_xr + RG_WIDTH
    w.wz[...] = w_in_ref[:, 0:SSD_WIDTH].astype(BF16)
    w.wxbc[...] = w_in_ref[:, SSD_WIDTH:o_dt].astype(BF16)
    head_cols = lax.broadcasted_iota(jnp.int32, (1, LANES), 1) < SSD_HEADS
    w.wdt[...] = jnp.where(head_cols, w_in_ref[:, o_dt:o_dt + LANES], 0.0).astype(BF16)
    w.wxr[...] = w_in_ref[:, o_xr:o_gr].astype(BF16)
    w.wgr[...] = w_in_ref[:, o_gr:o_gr + RG_WIDTH].astype(BF16)
    w.wout[...] = w_out_ref[...].astype(BF16)


def _mixer_kernel(hp_ref, ho_ref, ys5_ref, w_in_ref, w_out_ref, *rest, tq, chunk, nblk, last):
    small = [f for f in _MixW._fields if f not in _MIX_W16]
    o_ref = rest[len(small)]
    bufs = rest[len(small) + 1:]
    w16 = bufs[len(bufs) - len(_MIX_W16):]
    bufs = bufs[:len(bufs) - len(_MIX_W16)]
    w = _MixW(**dict(zip(small, rest[:len(small)])), **dict(zip(_MIX_W16, w16)))
    npb = len(_ProjBuf._fields)
    sets = (_ProjBuf(*bufs[:npb]), _ProjBuf(*bufs[npb:2 * npb]))
    ycats = bufs[2 * npb:2 * npb + 2]
    scr = _MixScratch(*bufs[2 * npb + 2:2 * npb + 2 + len(_MixScratch._fields)])
    h16 = bufs[-1]
    s = pl.program_id(0)

    @pl.when(s == 0)
    def _():
        _mixer_load_weights(w_in_ref, w_out_ref, w)
        for ref in bufs:
            ref[...] = jnp.zeros(ref.shape, ref.dtype)

    first_p = (s % nblk) == 0
    first_c = ((s + nblk - 1) % nblk) == 0

    def step(parity, project=True, mix=True, output=True):
        nxt, cur = sets[parity], sets[1 - parity]
        ycat_c, ycat_o = ycats[1 - parity], ycats[parity]
        jobs = _mixer_output_jobs(ho_ref, ycat_o, w, o_ref) if output else []
        if project:
            jobs += _mixer_project_jobs(hp_ref, h16, w, nxt, cur, first_p, tq)

        def side(n):
            for _ in range(min(n, len(jobs))):
                jobs.pop(0)()

        if mix:
            _mixer_compute(cur, ys5_ref, w, scr, ycat_c, first_c, tq, chunk, side)
        side(len(jobs))

    pl.when(s == 0)(lambda: step(0, mix=False, output=False))
    pl.when(s == 1)(lambda: step(1, output=False))
    steady = (s >= 2) & (s <= last - 2)
    pl.when(steady & (s % 2 == 0))(lambda: step(0))
    pl.when(steady & (s % 2 == 1))(lambda: step(1))
    pl.when(s == last - 1)(lambda: step((last - 1) % 2, project=False))
    pl.when(s == last)(lambda: step(last % 2, project=False, mix=False))


def _mixer(h, ys5, w_in, w_out, consts, layer):
    b, l, d = h.shape
    tq = min(MIX_TQ, l)
    chunk = min(SSD_CHUNK, tq)
    nblk = l // tq
    total = b * nblk
    h2d = h.reshape(b * l, d)
    proj_buf = [
        pltpu.VMEM((tq, SSD_WIDTH), F32),
        pltpu.VMEM((tq + SUBLANES, SSD_XBC), F32),
        pltpu.VMEM((tq, LANES), F32),
        pltpu.VMEM((tq + SUBLANES, RG_WIDTH), F32),
        pltpu.VMEM((tq, RG_WIDTH), F32),
    ]
    rg_plane = pltpu.VMEM((RG_WIDTH // LANES, tq, LANES), F32)
    out = pl.pallas_call(
        functools.partial(_mixer_kernel, tq=tq, chunk=chunk, nblk=nblk, last=total + 1),
        grid=(total + 2,),
        in_specs=[pl.BlockSpec((tq, d), lambda s: (jnp.minimum(s, total - 1), 0)),
                  pl.BlockSpec((tq, d), lambda s: (jnp.maximum(s - 2, 0), 0)),
                  pl.BlockSpec((tq, S5_WIDTH), lambda s: (jnp.clip(s - 1, 0, total - 1), 0))]
                 + [_layer_spec(c, layer) for c in (w_in, w_out) + tuple(consts)],
        out_specs=pl.BlockSpec((tq, d), lambda s: (jnp.maximum(s - 2, 0), 0)),
        out_shape=jax.ShapeDtypeStruct((b * l, d), F32),
        scratch_shapes=proj_buf + proj_buf + [
            pltpu.VMEM((tq, D_MODEL), BF16),
            pltpu.VMEM((tq, D_MODEL), BF16),
            pltpu.VMEM((SSD_GROUPS, SSD_STATE, SSD_GW), F32),
            rg_plane, rg_plane, rg_plane,
            pltpu.VMEM((SUBLANES, RG_WIDTH), F32),
            pltpu.VMEM((tq, d), BF16),
            pltpu.VMEM((d, SSD_WIDTH), BF16),
            pltpu.VMEM((d, SSD_XBC), BF16),
            pltpu.VMEM((d, LANES), BF16),
            pltpu.VMEM((d, RG_WIDTH), BF16),
            pltpu.VMEM((d, RG_WIDTH), BF16),
            pltpu.VMEM((d, d), BF16),
        ],
        compiler_params=_params(("arbitrary",)),
        name="mixer",
    )(h2d, h2d, ys5.reshape(b * l, S5_WIDTH), w_in, w_out, *consts)
    return out.reshape(b, l, d)


def _block_diag(w):
    nl, nb, bi, bo = w.shape
    return jnp.einsum('lhij,hk->lhikj', w, jnp.eye(nb, dtype=w.dtype)).reshape(nl, nb * bi, nb * bo)


def _xattn_kernel(h_ref, m_ref, wk32_ref, wv32_ref, wq32_ref, wo32_ref, g_ref, b_ref, o_ref,
                  wq_ref, wo_ref, k_ref, v_ref):
    @pl.when((pl.program_id(0) == 0) & (pl.program_id(1) == 0))
    def _():
        wq_ref[...] = wq32_ref[...].astype(BF16)
        wo_ref[...] = wo32_ref[...].astype(BF16)

    @pl.when(pl.program_id(1) == 0)
    def _():
        m16 = m_ref[...].astype(BF16)
        k_ref[...] = jnp.dot(m16, wk32_ref[...].astype(BF16), preferred_element_type=F32).astype(BF16)
        v_ref[...] = jnp.dot(m16, wv32_ref[...].astype(BF16), preferred_element_type=F32).astype(BF16)

    hb = h_ref[...]
    qall = jnp.dot(hb.astype(BF16), wq_ref[...], preferred_element_type=F32) * (1.0 / math.sqrt(XA_HEAD_DIM))
    outs = []
    for hd in range(XA_HEADS):
        sl = slice(hd * XA_HEAD_DIM, (hd + 1) * XA_HEAD_DIM)
        s = _dot_nt(qall[:, sl], k_ref[:, sl])
        p = jnp.exp(s - jnp.max(s, axis=-1, keepdims=True))
        o = jnp.dot(p.astype(BF16), v_ref[:, sl], preferred_element_type=F32)
        outs.append(o / jnp.sum(p, axis=-1, keepdims=True))
    o16 = jnp.concatenate(outs, axis=-1).astype(BF16)
    part = o16.shape[0] // XA_OUT_PARTS
    for r0 in range(0, o16.shape[0], part):
        rows = slice(r0, r0 + part)
        y = jnp.dot(o16[rows], wo_ref[...], preferred_element_type=F32)
        o_ref[rows, :] = _layer_norm(ALPHA * hb[rows] + y, g_ref[...], b_ref[...])


def _xattn(h, mem, wk, wv, wq, wo, g, bb, layer):
    b, l, d = h.shape
    m = mem.shape[1]
    tm = min(TOK_TM, l)
    tok = pl.BlockSpec((None, tm, d), lambda i, j: (i, j, 0))
    mems = pl.BlockSpec((None, m, d), lambda i, j: (i, 0, 0))
    return pl.pallas_call(
        _xattn_kernel,
        grid=(b, l // tm),
        in_specs=[tok, mems] + [_layer_spec(c, layer) for c in (wk, wv, wq, wo, g, bb)],
        out_specs=tok,
        out_shape=jax.ShapeDtypeStruct((b, l, d), F32),
        scratch_shapes=[pltpu.VMEM((d, d), BF16), pltpu.VMEM((d, d), BF16),
                        pltpu.VMEM((m, d), BF16), pltpu.VMEM((m, d), BF16)],
        compiler_params=_params(("arbitrary", "arbitrary")),
        name="xattn",
    )(h, mem, wk, wv, wq, wo, g, bb)


def _mlp_kernel(h_ref, w1_ref, w2_ref, g_ref, b_ref, o_ref):
    hb = h_ref[...]
    h16 = hb.astype(BF16)
    acc = ALPHA * hb
    for f in range(D_FF // MLP_FC):
        sl = slice(f * MLP_FC, (f + 1) * MLP_FC)
        t = jnp.maximum(jnp.dot(h16, w1_ref[:, sl], preferred_element_type=F32), 0.0)
        acc = acc + jnp.dot((t * t).astype(BF16), w2_ref[sl, :], preferred_element_type=F32)
    o_ref[...] = _layer_norm(acc, g_ref[...], b_ref[...])


def _mlp(h2d, w1, w2, g, bb, layer):
    t, d = h2d.shape
    tm = min(MLP_TM, t)
    tok = pl.BlockSpec((tm, d), lambda i: (i, 0))
    return pl.pallas_call(
        _mlp_kernel,
        grid=(t // tm,),
        in_specs=[tok] + [_layer_spec(c, layer) for c in (w1, w2, g, bb)],
        out_specs=tok,
        out_shape=jax.ShapeDtypeStruct((t, d), F32),
        compiler_params=_params(("parallel",)),
        name="mlp",
    )(h2d, w1, w2, g, bb)


def _rows(v):
    return v.astype(F32).reshape(v.shape[0], 1, -1)


def _pad_lanes(v3d):
    return jnp.pad(v3d, ((0, 0), (0, 0), (0, LANES - v3d.shape[2])))


def prepare_params(seq_len, w_in, w_out, ssd_conv_w, ssd_conv_b, ssd_dt_bias, ssd_a_log, ssd_d, ssd_norm_w,
                   s5_lam_re, s5_lam_im, s5_log_step, s5_b_re, s5_b_im, s5_c_re, s5_c_im, s5_d, s5_glu_w, s5_glu_b,
                   rg_conv_w, rg_conv_b, rg_wa, rg_ba, rg_wx, rg_bx, rg_lambda, ln1_g, ln1_b,
                   xa_wq, xa_wk, xa_wv, xa_wo, ln2_g, ln2_b, mlp_w1, mlp_w2, ln3_g, ln3_b):
    o_u = SSD_WIDTH + SSD_XBC + SSD_HEADS
    n_tiles = min(S5_TB, seq_len) // (S5_BLK * SUBLANES)
    assert n_tiles <= 2 ** (SUBLANES - 3), "S5 scan-level table holds 3 in-tile + 5 tile levels"
    s5_ops = jax.vmap(functools.partial(_s5_prepare, n_tiles=n_tiles))(
        s5_lam_re, s5_lam_im, s5_log_step, s5_b_re, s5_b_im, s5_c_re, s5_c_im)
    s5 = (w_in[:, :, o_u:o_u + S5_WIDTH].astype(BF16), *s5_ops, _rows(s5_d), s5_glu_w.astype(BF16), _rows(s5_glu_b))
    mix_small = dict(
        cw=ssd_conv_w.astype(F32), cb=_rows(ssd_conv_b), dtb=_pad_lanes(_rows(ssd_dt_bias)),
        aneg=_pad_lanes(_rows(-jnp.exp(ssd_a_log.astype(F32)))),
        dvec=_rows(jnp.repeat(ssd_d.astype(F32), SSD_HEAD_DIM, axis=1)), nw=_rows(ssd_norm_w),
        rcw=rg_conv_w.astype(F32), rcb=_rows(rg_conv_b),
        wa=_block_diag(rg_wa.astype(F32)).astype(BF16), ba=_rows(rg_ba.reshape(rg_ba.shape[0], -1)),
        wx=_block_diag(rg_wx.astype(F32)).astype(BF16), bx=_rows(rg_bx.reshape(rg_bx.shape[0], -1)),
        rlam=_rows(rg_lambda), g=_rows(ln1_g), b=_rows(ln1_b))
    w_in_aligned = jnp.pad(w_in.astype(F32), ((0, 0), (0, 0), (0, -w_in.shape[2] % LANES)))
    mix = (w_in_aligned, w_out.astype(F32), tuple(mix_small[f] for f in _MixW._fields if f not in _MIX_W16))
    xa = (xa_wk.astype(F32), xa_wv.astype(F32), xa_wq.astype(F32), xa_wo.astype(F32), _rows(ln2_g), _rows(ln2_b))
    mlp = (mlp_w1.astype(BF16), mlp_w2.astype(BF16), _rows(ln3_g), _rows(ln3_b))
    return dict(s5=s5, mix=mix, xa=xa, mlp=mlp)


def mixer_sublayer(h, prm, layer):
    ys5 = _s5_branch(h, prm["s5"], layer)
    w_in, w_out, consts = prm["mix"]
    return _mixer(h, ys5, w_in, w_out, consts, layer)


def xattn_sublayer(h, mem, prm, layer):
    return _xattn(h, mem, *prm["xa"], layer)


def mlp_sublayer(h, prm, layer):
    b, l, d = h.shape
    return _mlp(h.reshape(b * l, d), *prm["mlp"], layer).reshape(b, l, d)


def kernel(x, mem, w_in, w_out, ssd_conv_w, ssd_conv_b, ssd_dt_bias, ssd_a_log, ssd_d, ssd_norm_w, s5_lam_re, s5_lam_im, s5_log_step, s5_b_re, s5_b_im, s5_c_re, s5_c_im, s5_d, s5_glu_w, s5_glu_b, rg_conv_w, rg_conv_b, rg_wa, rg_ba, rg_wx, rg_bx, rg_lambda, ln1_g, ln1_b, xa_wq, xa_wk, xa_wv, xa_wo, ln2_g, ln2_b, mlp_w1, mlp_w2, ln3_g, ln3_b):
    h = x.astype(F32)
    memf = mem.astype(F32)
    prm = prepare_params(x.shape[1], w_in, w_out, ssd_conv_w, ssd_conv_b, ssd_dt_bias, ssd_a_log, ssd_d, ssd_norm_w,
                         s5_lam_re, s5_lam_im, s5_log_step, s5_b_re, s5_b_im, s5_c_re, s5_c_im, s5_d, s5_glu_w,
                         s5_glu_b, rg_conv_w, rg_conv_b, rg_wa, rg_ba, rg_wx, rg_bx, rg_lambda, ln1_g, ln1_b,
                         xa_wq, xa_wk, xa_wv, xa_wo, ln2_g, ln2_b, mlp_w1, mlp_w2, ln3_g, ln3_b)
    for layer in range(DEPTH):
        h = mixer_sublayer(h, prm, layer)
        h = xattn_sublayer(h, memf, prm, layer)
        h = mlp_sublayer(h, prm, layer)
    return h.astype(x.dtype)
```

```python
import collections
import functools
import math

import jax
import jax.numpy as jnp
from jax import lax
from jax.experimental import pallas as pl
from jax.experimental.pallas import tpu as pltpu

F32 = jnp.float32
BF16 = jnp.bfloat16

D_MODEL = 1024
DEPTH = 2
SSD_WIDTH = 512
SSD_HEAD_DIM = 64
SSD_HEADS = 8
SSD_GROUPS = 2
SSD_HPG = SSD_HEADS // SSD_GROUPS
SSD_GW = SSD_HPG * SSD_HEAD_DIM
SSD_STATE = 128
SSD_XBC = 1024
CONV_K = 4
S5_WIDTH = 256
S5_GROUPS = 16
S5_GROUP_CH = 16
S5_STATE = 64
S5_NSTATE = S5_GROUPS * S5_STATE
S5_BLK = 8
RG_WIDTH = 256
RG_BLOCKS = 4
RG_C = 8.0
XA_HEADS = 4
XA_HEAD_DIM = 256
D_FF = 4096
ALPHA = (2.0 * DEPTH) ** 0.25
LN_EPS = 1e-5

LANES = 128
SUBLANES = 8
MXU_TILE = 256
NEG_BIG = -1e30
VMEM_LIMIT = 56 * 1024 * 1024

SSD_CHUNK = 128
MIX_TQ = 256
S5_TB = 2048
TOK_TM = 1024
XA_OUT_PARTS = 4
MLP_TM = 1024
MLP_FC = 1024


def _dot(a, b):
    return jnp.dot(a.astype(BF16), b.astype(BF16), preferred_element_type=F32)


def _dot_nt(a, b):
    return lax.dot_general(a.astype(BF16), b.astype(BF16), (((1,), (1,)), ((), ())),
                           preferred_element_type=F32)


def _layer_norm(x, g, b):
    mu = jnp.mean(x, axis=-1, keepdims=True)
    xc = x - mu
    var = jnp.mean(xc * xc, axis=-1, keepdims=True)
    return xc * lax.rsqrt(var + LN_EPS) * g + b


def _gelu(x):
    c = math.sqrt(2.0 / math.pi)
    return 0.5 * x * (1.0 + jnp.tanh(c * (x + 0.044715 * (x * x * x))))


def _sigmoid(x):
    return 0.5 + 0.5 * jnp.tanh(0.5 * x)


def _silu(x):
    hx = 0.5 * x
    return hx + hx * jnp.tanh(hx)


def _softplus(x):
    return jnp.maximum(x, 0.0) + jnp.log1p(jnp.exp(-jnp.abs(x)))


def _const_spec(shape):
    nd = len(shape)
    return pl.BlockSpec(shape, lambda *_: (0,) * nd, pipeline_mode=pl.Buffered(1))


def _layer_spec(stacked, layer):
    nd = stacked.ndim - 1
    return pl.BlockSpec((None,) + stacked.shape[1:], lambda *_: (layer,) + (0,) * nd, pipeline_mode=pl.Buffered(1))


def _params(sem):
    return pltpu.CompilerParams(dimension_semantics=sem, vmem_limit_bytes=VMEM_LIMIT)


def _re_lanes(p):
    return slice(2 * p * LANES, (2 * p + 1) * LANES)


def _im_lanes(p):
    return slice((2 * p + 1) * LANES, (2 * p + 2) * LANES)


def _s5_state_group(idx):
    state = (idx // (2 * LANES)) * LANES + idx % LANES
    return state // S5_STATE


def _s5_split(x):
    planes = S5_NSTATE // LANES
    return (jnp.concatenate([x[:, _re_lanes(p)] for p in range(planes)], axis=-1),
            jnp.concatenate([x[:, _im_lanes(p)] for p in range(planes)], axis=-1))


S5_HALF_CH = S5_WIDTH // 2
S5_HALF_LANES = S5_BLK * S5_HALF_CH


def _s5_expand_operators(wc_ref, vc_ref, ct_ref, ms_scr, toep_scr, mo_scr):
    n, w, c = S5_NSTATE, S5_WIDTH, S5_GROUP_CH
    hw, hl = S5_HALF_CH, S5_HALF_LANES
    grp_half = S5_GROUPS // 2
    reps = hw // c
    grp_r = _s5_state_group(lax.broadcasted_iota(jnp.int32, (2 * n, w), 0))
    same_o = grp_r == lax.broadcasted_iota(jnp.int32, (2 * n, w), 1) // c
    mo0 = jnp.where(same_o, ct_ref[...], 0.0)
    kc = jnp.dot(wc_ref[...], mo0, preferred_element_type=F32, precision=lax.Precision.HIGHEST)
    toep_scr[...] = jnp.zeros(toep_scr.shape, toep_scr.dtype)
    ch_grp = lax.broadcasted_iota(jnp.int32, (hw, hl), 0) // c
    same_k = (lax.broadcasted_iota(jnp.int32, (hw, hw), 0) // c) == (lax.broadcasted_iota(jnp.int32, (hw, hw), 1) // c)
    pick_r = lax.broadcasted_iota(jnp.int32, (S5_BLK * c, hl), 0)
    pick_c = lax.broadcasted_iota(jnp.int32, (S5_BLK * c, hl), 1)
    pick = ((pick_r // c == pick_c // hw) & (pick_r % c == pick_c % c)).astype(BF16)
    for hf in range(2):
        lanes = slice(hf * hl, (hf + 1) * hl)
        st_grp = _s5_state_group(lax.broadcasted_iota(jnp.int32, (hw, hl), 1) + hf * hl) - hf * grp_half
        same_s = ch_grp == st_grp
        for t in range(S5_BLK):
            blk = jnp.where(same_s, jnp.tile(wc_ref[t * c:(t + 1) * c, lanes], (reps, 1)), 0.0)
            ms_scr[hf, t * hw:(t + 1) * hw, :] = blk.astype(BF16)
        st_grp = _s5_state_group(lax.broadcasted_iota(jnp.int32, (hl, hl), 0) + hf * hl) - hf * grp_half
        same_r = st_grp == (lax.broadcasted_iota(jnp.int32, (hl, hl), 1) % hw) // c
        spread = jnp.dot(vc_ref[lanes, :].astype(BF16), pick, preferred_element_type=F32)
        mo_scr[hf] = jnp.where(same_r, spread, 0.0).astype(BF16)
        for t_in in range(S5_BLK):
            for t_out in range(t_in, S5_BLK):
                k = S5_BLK - 1 - (t_out - t_in)
                blk = jnp.where(same_k, jnp.tile(kc[k * c:(k + 1) * c, hf * hw:(hf + 1) * hw], (reps, 1)), 0.0)
                toep_scr[hf, t_in * hw:(t_in + 1) * hw, t_out * hw:(t_out + 1) * hw] = blk.astype(BF16)


_S5W = collections.namedtuple("_S5W", "wu wc vc ct lvl pw pwt d gluw glub")
_S5Ops = collections.namedtuple("_S5Ops", "ms toep mo")
_S5Buf = collections.namedtuple("_S5Buf", "u32 u16 hprev16")
_S5Scratch = collections.namedtuple("_S5Scratch", "u_tok hplane cexp carry yg")


def _s5_scan_stage(h_ref, w, ops, scr, out, first, rows, side):
    n, wd = S5_NSTATE, S5_WIDTH
    lane_halves = wd // LANES
    planes = n // LANES
    nt = rows // SUBLANES
    tb = rows * S5_BLK
    for r0 in range(0, tb, tb // 4):
        rs = slice(r0, r0 + tb // 4)
        u_tok = jnp.dot(h_ref[rs, :].astype(BF16), w.wu[...], preferred_element_type=F32)
        for hf in range(lane_halves):
            scr.u_tok[hf, rs, :] = u_tok[:, hf * LANES:(hf + 1) * LANES]
    u = jnp.concatenate([scr.u_tok[hf, pl.ds(k, rows, stride=S5_BLK), :]
                         for hf in range(lane_halves) for k in range(S5_BLK)], axis=-1)
    out.u32[...] = u
    u16 = u.astype(BF16)
    out.u16[...] = u16
    row8 = lax.broadcasted_iota(jnp.int32, (nt, SUBLANES, LANES), 1)
    planes_per_half = planes // lane_halves
    for p in range(planes):
        hf, pl_in_half = divmod(p, planes_per_half)
        s = jnp.dot(u16[:, hf * S5_HALF_LANES:(hf + 1) * S5_HALF_LANES],
                    ops.ms[hf, :, 2 * pl_in_half * LANES:(2 * pl_in_half + 2) * LANES],
                    preferred_element_type=F32)
        if p < planes - 2:
            side(1)
        xr = s[:, :LANES].reshape(nt, SUBLANES, LANES)
        xi = s[:, LANES:].reshape(nt, SUBLANES, LANES)
        for lv, dist in enumerate((1, 2, 4)):
            lr = w.lvl[lv:lv + 1, _re_lanes(p)]
            li = w.lvl[lv:lv + 1, _im_lanes(p)]
            keep = row8 >= dist
            sr = jnp.where(keep, pltpu.roll(xr, dist, 1), 0.0)
            si = jnp.where(keep, pltpu.roll(xi, dist, 1), 0.0)
            xr, xi = xr + lr * sr - li * si, xi + lr * si + li * sr
        scr.hplane[2 * p, SUBLANES:SUBLANES + rows, :] = xr.reshape(rows, LANES)
        scr.hplane[2 * p + 1, SUBLANES:SUBLANES + rows, :] = xi.reshape(rows, LANES)
    side(1)
    last = 2 * SUBLANES - 1
    er = jnp.concatenate([scr.hplane[2 * p, pl.ds(last, nt, stride=SUBLANES), :] for p in range(planes)], axis=-1)
    ei = jnp.concatenate([scr.hplane[2 * p + 1, pl.ds(last, nt, stride=SUBLANES), :] for p in range(planes)], -1)
    tile_id = lax.broadcasted_iota(jnp.int32, (nt, n), 0)
    lv, dist = 3, 1
    while dist < nt:
        lr, li = _s5_split(w.lvl[lv:lv + 1, :])
        keep = tile_id >= dist
        sr = jnp.where(keep, pltpu.roll(er, dist, 0), 0.0)
        si = jnp.where(keep, pltpu.roll(ei, dist, 0), 0.0)
        er, ei = er + lr * sr - li * si, ei + lr * si + li * sr
        lv, dist = lv + 1, dist * 2
    cr, ci = _s5_split(jnp.where(first, 0.0, scr.carry[0:1, :]))
    ptr, pti = _s5_split(w.pwt[...])
    er, ei = er + ptr * cr - pti * ci, ei + ptr * ci + pti * cr
    for p in range(planes):
        scr.carry[0:1, _re_lanes(p)] = er[nt - 1:nt, p * LANES:(p + 1) * LANES]
        scr.carry[0:1, _im_lanes(p)] = ei[nt - 1:nt, p * LANES:(p + 1) * LANES]
    tin_r = jnp.where(tile_id >= 1, pltpu.roll(er, 1, 0), cr)
    tin_i = jnp.where(tile_id >= 1, pltpu.roll(ei, 1, 0), ci)
    side(1)
    for p in range(planes):
        for k in range(SUBLANES):
            scr.cexp[2 * p, pl.ds(k, nt, stride=SUBLANES), :] = tin_r[:, p * LANES:(p + 1) * LANES]
            scr.cexp[2 * p + 1, pl.ds(k, nt, stride=SUBLANES), :] = tin_i[:, p * LANES:(p + 1) * LANES]
    for p in range(planes):
        if p % 2 == 0:
            side(1)
        pr = jnp.tile(w.pw[:, _re_lanes(p)], (nt, 1))
        pi = jnp.tile(w.pw[:, _im_lanes(p)], (nt, 1))
        tr = scr.cexp[2 * p]
        ti = scr.cexp[2 * p + 1]
        hr = scr.hplane[2 * p, SUBLANES:SUBLANES + rows, :] + pr * tr - pi * ti
        hi = scr.hplane[2 * p + 1, SUBLANES:SUBLANES + rows, :] + pr * ti + pi * tr
        scr.hplane[2 * p, SUBLANES:SUBLANES + rows, :] = hr
        scr.hplane[2 * p + 1, SUBLANES:SUBLANES + rows, :] = hi
        scr.hplane[2 * p, SUBLANES - 1:SUBLANES, :] = cr[:, p * LANES:(p + 1) * LANES]
        scr.hplane[2 * p + 1, SUBLANES - 1:SUBLANES, :] = ci[:, p * LANES:(p + 1) * LANES]
    for p in range(2 * planes):
        out.hprev16[:, p * LANES:(p + 1) * LANES] = scr.hplane[p, SUBLANES - 1:SUBLANES - 1 + rows, :].astype(BF16)


def _s5_readout_jobs(src, w, ops, scr, o_ref, rows):
    hw, hl = S5_HALF_CH, S5_HALF_LANES
    lane_halves = S5_WIDTH // LANES
    steps_per_tile = MXU_TILE // hw

    def tile_job(hf, j):
        def run():
            k_rows = (j + 1) * MXU_TILE
            cols = slice(j * MXU_TILE, (j + 1) * MXU_TILE)
            y = jnp.dot(src.u16[:, hf * hl:hf * hl + k_rows], ops.toep[hf, :k_rows, cols],
                        preferred_element_type=F32)
            y = y + jnp.dot(src.hprev16[:, hf * hl:(hf + 1) * hl], ops.mo[hf, :, cols], preferred_element_type=F32)
            d_half = w.d[:, hf * hw:(hf + 1) * hw]
            y = _gelu(y + jnp.concatenate([d_half] * steps_per_tile, axis=1) * src.u32[:, hf * hl + j * MXU_TILE:
                                                                                           hf * hl + (j + 1) * MXU_TILE])
            for i in range(steps_per_tile):
                scr.yg[hf, pl.ds(j * steps_per_tile + i, rows, stride=S5_BLK), :] = y[:, i * hw:(i + 1) * hw]
        return run

    def glu_job():
        yg = jnp.concatenate([scr.yg[hf] for hf in range(lane_halves)], axis=-1)
        glu = _sigmoid(_dot(yg, w.gluw[...]) + w.glub[...])
        o_ref[...] = (yg * glu).astype(o_ref.dtype)

    return [tile_job(hf, j) for hf in range(lane_halves) for j in range(hl // MXU_TILE)] + [glu_job]


def _s5_kernel(h_ref, *rest, rows, nblk):
    nw = len(_S5W._fields)
    w = _S5W(*rest[:nw])
    o_ref = rest[nw]
    bufs = rest[nw + 1:]
    ops = _S5Ops(*bufs[:3])
    nb = len(_S5Buf._fields)
    scr = _S5Scratch(*bufs[3 + nb:])
    s = pl.program_id(0)
    parity = s % 2

    @pl.when(s == 0)
    def _():
        _s5_expand_operators(w.wc, w.vc, w.ct, *ops)
        for ref in bufs[3:]:
            ref[...] = jnp.zeros(ref.shape, ref.dtype)

    first = (s % nblk) == 0

    def step(out, src):
        jobs = _s5_readout_jobs(src, w, ops, scr, o_ref, rows)

        def side(k):
            for _ in range(min(k, len(jobs))):
                jobs.pop(0)()

        _s5_scan_stage(h_ref, w, ops, scr, out, first, rows, side)
        side(len(jobs))

    step(_S5Buf(*(r.at[parity] for r in bufs[3:3 + nb])), _S5Buf(*(r.at[1 - parity] for r in bufs[3:3 + nb])))


def _s5_branch(h, consts, layer):
    b, l, d = h.shape
    tb = min(S5_TB, l)
    rows = tb // S5_BLK
    nblk = l // tb
    total = b * nblk
    n, w = S5_NSTATE, S5_WIDTH
    handoff = [
        pltpu.VMEM((2, rows, S5_BLK * w), F32),
        pltpu.VMEM((2, rows, S5_BLK * w), BF16),
        pltpu.VMEM((2, rows, 2 * n), BF16),
    ]
    out = pl.pallas_call(
        functools.partial(_s5_kernel, rows=rows, nblk=nblk),
        grid=(total + 1,),
        in_specs=[pl.BlockSpec((tb, d), lambda s: (jnp.minimum(s, total - 1), 0))]
                 + [_layer_spec(c, layer) for c in consts],
        out_specs=pl.BlockSpec((tb, w), lambda s: (jnp.maximum(s - 1, 0), 0)),
        out_shape=jax.ShapeDtypeStruct((b * l, w), BF16),
        scratch_shapes=[
            pltpu.VMEM((2, S5_HALF_LANES, S5_HALF_LANES), BF16),
            pltpu.VMEM((2, S5_HALF_LANES, S5_HALF_LANES), BF16),
            pltpu.VMEM((2, S5_HALF_LANES, S5_HALF_LANES), BF16),
        ] + handoff + [
            pltpu.VMEM((w // LANES, tb, LANES), F32),
            pltpu.VMEM((2 * n // LANES, rows + SUBLANES, LANES), F32),
            pltpu.VMEM((2 * n // LANES, rows, LANES), F32),
            pltpu.VMEM((SUBLANES, 2 * n), F32),
            pltpu.VMEM((w // LANES, tb, LANES), F32),
        ],
        compiler_params=_params(("arbitrary",)),
        name="s5_branch",
    )(h.reshape(b * l, d), *consts)
    return out.reshape(b, l, w)


def _s5_prepare(lam_re, lam_im, log_step, b_re, b_im, c_re, c_im, n_tiles):
    g, p, c, t8 = S5_GROUPS, S5_STATE, S5_GROUP_CH, S5_BLK
    lr, li = lam_re.astype(F32), lam_im.astype(F32)
    dt = jnp.exp(log_step.astype(F32))[:, None]
    ar, ai = lr * dt, li * dt

    def lam_pow(k):
        k = k.astype(F32)[..., None, None]
        mag = jnp.exp(k * ar)
        return mag * jnp.cos(k * ai), mag * jnp.sin(k * ai)

    br1, bi1 = lam_pow(jnp.ones((), jnp.int32))
    xr_, xi_ = br1 - 1.0, bi1
    den = lr * lr + li * li
    cr, ci = (xr_ * lr + xi_ * li) / den, (xi_ * lr - xr_ * li) / den
    bre, bim = b_re.astype(F32), b_im.astype(F32)
    bbr = cr[..., None] * bre - ci[..., None] * bim
    bbi = cr[..., None] * bim + ci[..., None] * bre
    steps = jnp.arange(t8)
    pr, pi = lam_pow(t8 - 1 - steps)
    wsr = pr[..., None] * bbr[None] - pi[..., None] * bbi[None]
    wsi = pr[..., None] * bbi[None] + pi[..., None] * bbr[None]

    def interleave(re, im, axis):
        shape = list(re.shape)
        shape[axis:axis + 1] = [re.shape[axis] // LANES, 1, LANES]
        both = jnp.concatenate([re.reshape(shape), im.reshape(shape)], axis=axis + 1)
        shape[axis:axis + 3] = [2 * re.shape[axis]]
        return both.reshape(shape)

    wc = interleave(*[part.transpose(0, 3, 1, 2).reshape(t8 * c, g * p) for part in (wsr, wsi)], axis=1)
    cre, cim = c_re.astype(F32), c_im.astype(F32)
    pr, pi = lam_pow(steps + 1)
    vor = cre[None] * pr[:, :, None, :] - cim[None] * pi[:, :, None, :]
    voi = cre[None] * pi[:, :, None, :] + cim[None] * pr[:, :, None, :]
    vc = interleave(*[part.transpose(1, 3, 0, 2).reshape(g * p, t8 * c) for part in (vor, -voi)], axis=0)
    ct = interleave(*[jnp.tile(part.transpose(0, 2, 1).reshape(g * p, c), (1, g)) for part in (cre, -cim)], axis=0)

    def re_im(pair):
        return interleave(pair[0].reshape(-1, g * p), pair[1].reshape(-1, g * p), axis=1)

    lvl = re_im(lam_pow(t8 * 2 ** jnp.arange(SUBLANES)))
    pw = re_im(lam_pow(t8 * (steps + 1)))
    pwt = re_im(lam_pow(t8 * SUBLANES * (jnp.arange(n_tiles) + 1)))
    return wc, vc, ct, lvl, pw, pwt


def _expand_heads(cols, grp, head_lane):
    base = SSD_HPG * grp
    out = jnp.broadcast_to(cols[:, base:base + 1], (cols.shape[0], SSD_GW))
    for hh in range(1, SSD_HPG):
        out = jnp.where(head_lane == hh, cols[:, base + hh:base + hh + 1], out)
    return out


def _causal_conv(pad_ref, w_ref, b_ref, rows, cols):
    acc = b_ref[:, cols] + w_ref[CONV_K - 1:CONV_K, cols] * pad_ref[SUBLANES:SUBLANES + rows, cols]
    for k in range(CONV_K - 1):
        off = SUBLANES - (CONV_K - 1) + k
        acc = acc + w_ref[k:k + 1, cols] * pad_ref[off:off + rows, cols]
    return acc


_ProjBuf = collections.namedtuple("_ProjBuf", "z xpad dt rpad gr")
_MixW = collections.namedtuple(
    "_MixW", "wz wxbc wdt wxr wgr cw cb dtb aneg dvec nw rcw rcb wa ba wx bx rlam wout g b")
_MixScratch = collections.namedtuple("_MixScratch", "state ascr bscr cexp rcar")


def _col_tiles(width):
    return [slice(c, min(c + MXU_TILE, width)) for c in range(0, width, MXU_TILE)]


def _mixer_output_jobs(h_ref, ycat, w, o_ref):
    def tile_job(cols):
        def run():
            y = jnp.dot(ycat[...], w.wout[:, cols], preferred_element_type=F32)
            o_ref[:, cols] = ALPHA * h_ref[:, cols] + y
        return run

    def norm_job():
        o_ref[...] = _layer_norm(o_ref[...], w.g[...], w.b[...])

    return [tile_job(cols) for cols in _col_tiles(D_MODEL)] + [norm_job]


def _mixer_project_jobs(h_ref, h16, w, nxt, cur, first, tq):
    def cast_job():
        h16[...] = h_ref[...].astype(BF16)

    def tile_job(dst, w_ref, cols, row0):
        def run():
            dst[row0:row0 + tq, cols] = jnp.dot(h16[...], w_ref[:, cols], preferred_element_type=F32)
        return run

    def tail_job(pad_n, pad_c):
        def run():
            pad_n[0:SUBLANES, :] = jnp.where(first, 0.0, pad_c[tq:tq + SUBLANES, :])
        return run

    jobs = [cast_job]
    for dst, w_ref, row0 in ((nxt.z, w.wz, 0), (nxt.xpad, w.wxbc, SUBLANES), (nxt.dt, w.wdt, 0),
                             (nxt.rpad, w.wxr, SUBLANES), (nxt.gr, w.wgr, 0)):
        jobs += [tile_job(dst, w_ref, cols, row0) for cols in _col_tiles(dst.shape[1])]
    jobs += [tail_job(nxt.xpad, cur.xpad), tail_job(nxt.rpad, cur.rpad)]
    return jobs


def _mixer_compute(cur, ys5_ref, w, scr, ycat, first, tq, chunk, side):
    cw_ref, cb_ref, dtb_ref, aneg_ref, dvec_ref, nw_ref = w.cw, w.cb, w.dtb, w.aneg, w.dvec, w.nw
    state = scr.state
    z = cur.z
    slabs = []
    for cols in _col_tiles(SSD_XBC):
        side(1)
        slabs.append(_silu(_causal_conv(cur.xpad, cw_ref, cb_ref, tq, cols)))
    xact = jnp.concatenate(slabs, axis=-1)
    side(2)
    dt_all = _softplus(cur.dt[...] + dtb_ref[...])
    q = chunk
    tri = (lax.broadcasted_iota(jnp.int32, (q, q), 0) >= lax.broadcasted_iota(jnp.int32, (q, q), 1))
    row_id = lax.broadcasted_iota(jnp.int32, (q, LANES), 0)
    head_lane = lax.broadcasted_iota(jnp.int32, (1, SSD_GW), 1) // SSD_HEAD_DIM
    for ci in range(tq // q):
        r0 = ci * q
        dt = dt_all[r0:r0 + q, :]
        cs = dt * aneg_ref[...]
        dist = 1
        while dist < q:
            cs = cs + jnp.where(row_id >= dist, pltpu.roll(cs, dist, 0), 0.0)
            dist *= 2
        cs_t = cs.T
        cs_last = cs[q - 1:q, :]
        chunk_decay = jnp.exp(cs_last)
        dec_t = jnp.exp(cs_t[:, q - 1:q] - cs_t)
        ecs = jnp.exp(cs)
        zc = z[r0:r0 + q, :]
        halves = []
        for grp in range(SSD_GROUPS):
            side(2)
            lo = grp * SSD_GW
            xs = xact[r0:r0 + q, lo:lo + SSD_GW]
            bm = xact[r0:r0 + q, SSD_WIDTH + grp * SSD_STATE:SSD_WIDTH + (grp + 1) * SSD_STATE]
            cm = xact[r0:r0 + q, SSD_WIDTH + (SSD_GROUPS + grp) * SSD_STATE:
                      SSD_WIDTH + (SSD_GROUPS + grp + 1) * SSD_STATE]
            cm16 = cm.astype(BF16)
            xdt16 = (xs * _expand_heads(dt, grp, head_lane)).astype(BF16)
            cb = _dot_nt(cm16, bm)
            bm_t = bm.T
            s_prev = state[grp]
            if ci == 0:
                s_prev = jnp.where(first, 0.0, s_prev)
            y_off = jnp.dot(cm16, s_prev.astype(BF16), preferred_element_type=F32)
            intra, to_state, xdt_heads = [], [], []
            for hh in range(SSD_HPG):
                hd = grp * SSD_HPG + hh
                seg = cs[:, hd:hd + 1] - cs_t[hd:hd + 1, :]
                lmat = jnp.exp(jnp.where(tri, seg, NEG_BIG))
                intra.append((cb * lmat).astype(BF16))
                to_state.append((bm_t * dec_t[hd:hd + 1, :]).astype(BF16))
                xdt_heads.append(jnp.where(head_lane == hh, xdt16, jnp.zeros_like(xdt16)))
            lhs = jnp.concatenate([jnp.concatenate(intra, axis=1), jnp.concatenate(to_state, axis=1)], axis=0)
            both = jnp.dot(lhs, jnp.concatenate(xdt_heads, axis=0), preferred_element_type=F32)
            y_diag = both[:q]
            st_new = both[q:]
            state[grp] = s_prev * _expand_heads(chunk_decay, grp, head_lane) + st_new
            halves.append(y_diag + y_off * _expand_heads(ecs, grp, head_lane) + xs * dvec_ref[:, lo:lo + SSD_GW])
        y = jnp.concatenate(halves, axis=-1) * _silu(zc)
        y = y * lax.rsqrt(jnp.mean(y * y, axis=-1, keepdims=True) + LN_EPS) * nw_ref[...]
        ycat[r0:r0 + q, 0:SSD_WIDTH] = y.astype(BF16)

    ycat[:, SSD_WIDTH:SSD_WIDTH + S5_WIDTH] = ys5_ref[...]
    _mixer_rglru(cur, w, scr, ycat, first, tq, side)


def _mixer_rglru(cur, w, scr, ycat, first, tq, side):
    side(1)
    xc = _causal_conv(cur.rpad, w.rcw, w.rcb, tq, slice(0, RG_WIDTH))
    xc16 = xc.astype(BF16)
    rgate = _sigmoid(jnp.dot(xc16, w.wa[...], preferred_element_type=F32) + w.ba[...])
    igate = _sigmoid(jnp.dot(xc16, w.wx[...], preferred_element_type=F32) + w.bx[...])
    log_a = (-RG_C) * rgate * _softplus(-w.rlam[...])
    a = jnp.exp(log_a)
    bt = jnp.sqrt(1.0 - jnp.exp(2.0 * log_a)) * (igate * xc)
    row8 = lax.broadcasted_iota(jnp.int32, (tq, RG_WIDTH), 0) & (SUBLANES - 1)
    side(1)
    for dist in (1, 2, 4):
        keep = row8 >= dist
        a_sh = jnp.where(keep, pltpu.roll(a, dist, 0), 1.0)
        b_sh = jnp.where(keep, pltpu.roll(bt, dist, 0), 0.0)
        bt = bt + a * b_sh
        a = a * a_sh
    nt = tq // SUBLANES
    lane_halves = RG_WIDTH // LANES
    for hf in range(lane_halves):
        scr.ascr[hf] = a[:, hf * LANES:(hf + 1) * LANES]
        scr.bscr[hf] = bt[:, hf * LANES:(hf + 1) * LANES]
    ae = jnp.concatenate([scr.ascr[hf, pl.ds(SUBLANES - 1, nt, stride=SUBLANES), :] for hf in range(lane_halves)], -1)
    be = jnp.concatenate([scr.bscr[hf, pl.ds(SUBLANES - 1, nt, stride=SUBLANES), :] for hf in range(lane_halves)], -1)
    tile_id = lax.broadcasted_iota(jnp.int32, (nt, RG_WIDTH), 0)
    dist = 1
    while dist < nt:
        keep = tile_id >= dist
        a_sh = jnp.where(keep, pltpu.roll(ae, dist, 0), 1.0)
        b_sh = jnp.where(keep, pltpu.roll(be, dist, 0), 0.0)
        be = be + ae * b_sh
        ae = ae * a_sh
        dist *= 2
    carry_in = jnp.where(first, 0.0, scr.rcar[0:1, :])
    tile_end = be + ae * carry_in
    scr.rcar[0:1, :] = tile_end[nt - 1:nt, :]
    tile_in = jnp.where(tile_id >= 1, pltpu.roll(tile_end, 1, 0), carry_in)
    for hf in range(lane_halves):
        for k in range(SUBLANES):
            scr.cexp[hf, pl.ds(k, nt, stride=SUBLANES), :] = tile_in[:, hf * LANES:(hf + 1) * LANES]
    hscan = bt + a * jnp.concatenate([scr.cexp[hf] for hf in range(lane_halves)], axis=-1)
    ycat[:, SSD_WIDTH + S5_WIDTH:] = (hscan * _gelu(cur.gr[...])).astype(BF16)


_MIX_W16 = ("wz", "wxbc", "wdt", "wxr", "wgr", "wout")


def _mixer_load_weights(w_ssd_ref, w_tail_ref, w_out_ref, w):
    o_xr = SSD_HEADS + S5_WIDTH
    o_gr = o_xr + RG_WIDTH
    w.wz[...] = w_ssd_ref[:, 0:SSD_WIDTH].astype(BF16)
    w.wxbc[...] = w_ssd_ref[:, SSD_WIDTH:].astype(BF16)
    head_cols = lax.broadcasted_iota(jnp.int32, (1, LANES), 1) < SSD_HEADS
    w.wdt[...] = jnp.where(head_cols, w_tail_ref[:, 0:LANES], 0.0).astype(BF16)
    w.wxr[...] = w_tail_ref[:, o_xr:o_gr].astype(BF16)
    w.wgr[...] = w_tail_ref[:, o_gr:o_gr + RG_WIDTH].astype(BF16)
    w.wout[...] = w_out_ref[...].astype(BF16)


def _mixer_kernel(hp_ref, ho_ref, ys5_ref, w_ssd_ref, w_tail_ref, w_out_ref, *rest, tq, chunk, nblk, last):
    small = [f for f in _MixW._fields if f not in _MIX_W16]
    o_ref = rest[len(small)]
    bufs = rest[len(small) + 1:]
    w16 = bufs[len(bufs) - len(_MIX_W16):]
    bufs = bufs[:len(bufs) - len(_MIX_W16)]
    w = _MixW(**dict(zip(small, rest[:len(small)])), **dict(zip(_MIX_W16, w16)))
    npb = len(_ProjBuf._fields)
    sets = (_ProjBuf(*bufs[:npb]), _ProjBuf(*bufs[npb:2 * npb]))
    ycats = bufs[2 * npb:2 * npb + 2]
    scr = _MixScratch(*bufs[2 * npb + 2:2 * npb + 2 + len(_MixScratch._fields)])
    h16 = bufs[-1]
    s = pl.program_id(0)

    @pl.when(s == 0)
    def _():
        _mixer_load_weights(w_ssd_ref, w_tail_ref, w_out_ref, w)
        for ref in bufs:
            ref[...] = jnp.zeros(ref.shape, ref.dtype)

    first_p = (s % nblk) == 0
    first_c = ((s + nblk - 1) % nblk) == 0

    def step(parity, project=True, mix=True, output=True):
        nxt, cur = sets[parity], sets[1 - parity]
        ycat_c, ycat_o = ycats[1 - parity], ycats[parity]
        jobs = _mixer_output_jobs(ho_ref, ycat_o, w, o_ref) if output else []
        if project:
            jobs += _mixer_project_jobs(hp_ref, h16, w, nxt, cur, first_p, tq)

        def side(n):
            for _ in range(min(n, len(jobs))):
                jobs.pop(0)()

        if mix:
            _mixer_compute(cur, ys5_ref, w, scr, ycat_c, first_c, tq, chunk, side)
        side(len(jobs))

    pl.when(s == 0)(lambda: step(0, mix=False, output=False))
    pl.when(s == 1)(lambda: step(1, output=False))
    steady = (s >= 2) & (s <= last - 2)
    pl.when(steady & (s % 2 == 0))(lambda: step(0))
    pl.when(steady & (s % 2 == 1))(lambda: step(1))
    pl.when(s == last - 1)(lambda: step((last - 1) % 2, project=False))
    pl.when(s == last)(lambda: step(last % 2, project=False, mix=False))


def _mixer(h, ys5, w_in, w_out, consts, layer):
    b, l, d = h.shape
    tq = min(MIX_TQ, l)
    chunk = min(SSD_CHUNK, tq)
    nblk = l // tq
    total = b * nblk
    h2d = h.reshape(b * l, d)
    proj_buf = [
        pltpu.VMEM((tq, SSD_WIDTH), F32),
        pltpu.VMEM((tq + SUBLANES, SSD_XBC), F32),
        pltpu.VMEM((tq, LANES), F32),
        pltpu.VMEM((tq + SUBLANES, RG_WIDTH), F32),
        pltpu.VMEM((tq, RG_WIDTH), F32),
    ]
    rg_plane = pltpu.VMEM((RG_WIDTH // LANES, tq, LANES), F32)
    out = pl.pallas_call(
        functools.partial(_mixer_kernel, tq=tq, chunk=chunk, nblk=nblk, last=total + 1),
        grid=(total + 2,),
        in_specs=[pl.BlockSpec((tq, d), lambda s: (jnp.minimum(s, total - 1), 0)),
                  pl.BlockSpec((tq, d), lambda s: (jnp.maximum(s - 2, 0), 0)),
                  pl.BlockSpec((tq, S5_WIDTH), lambda s: (jnp.clip(s - 1, 0, total - 1), 0))]
                 + [_layer_spec(c, layer) for c in tuple(w_in) + (w_out,) + tuple(consts)],
        out_specs=pl.BlockSpec((tq, d), lambda s: (jnp.maximum(s - 2, 0), 0)),
        out_shape=jax.ShapeDtypeStruct((b * l, d), F32),
        scratch_shapes=proj_buf + proj_buf + [
            pltpu.VMEM((tq, D_MODEL), BF16),
            pltpu.VMEM((tq, D_MODEL), BF16),
            pltpu.VMEM((SSD_GROUPS, SSD_STATE, SSD_GW), F32),
            rg_plane, rg_plane, rg_plane,
            pltpu.VMEM((SUBLANES, RG_WIDTH), F32),
            pltpu.VMEM((tq, d), BF16),
            pltpu.VMEM((d, SSD_WIDTH), BF16),
            pltpu.VMEM((d, SSD_XBC), BF16),
            pltpu.VMEM((d, LANES), BF16),
            pltpu.VMEM((d, RG_WIDTH), BF16),
            pltpu.VMEM((d, RG_WIDTH), BF16),
            pltpu.VMEM((d, d), BF16),
        ],
        compiler_params=_params(("arbitrary",)),
        name="mixer",
    )(h2d, h2d, ys5.reshape(b * l, S5_WIDTH), *w_in, w_out, *consts)
    return out.reshape(b, l, d)


def _block_diag(w):
    nl, nb, bi, bo = w.shape
    return jnp.einsum('lhij,hk->lhikj', w, jnp.eye(nb, dtype=w.dtype)).reshape(nl, nb * bi, nb * bo)


def _xattn_kernel(h_ref, m_ref, wk32_ref, wv32_ref, wq32_ref, wo32_ref, g_ref, b_ref, o_ref,
                  wq_ref, wo_ref, k_ref, v_ref):
    @pl.when((pl.program_id(0) == 0) & (pl.program_id(1) == 0))
    def _():
        wq_ref[...] = wq32_ref[...].astype(BF16)
        wo_ref[...] = wo32_ref[...].astype(BF16)

    @pl.when(pl.program_id(1) == 0)
    def _():
        m16 = m_ref[...].astype(BF16)
        k_ref[...] = jnp.dot(m16, wk32_ref[...].astype(BF16), preferred_element_type=F32).astype(BF16)
        v_ref[...] = jnp.dot(m16, wv32_ref[...].astype(BF16), preferred_element_type=F32).astype(BF16)

    hb = h_ref[...]
    qall = jnp.dot(hb.astype(BF16), wq_ref[...], preferred_element_type=F32) * (1.0 / math.sqrt(XA_HEAD_DIM))
    outs = []
    for hd in range(XA_HEADS):
        sl = slice(hd * XA_HEAD_DIM, (hd + 1) * XA_HEAD_DIM)
        s = _dot_nt(qall[:, sl], k_ref[:, sl])
        p = jnp.exp(s - jnp.max(s, axis=-1, keepdims=True))
        o = jnp.dot(p.astype(BF16), v_ref[:, sl], preferred_element_type=F32)
        outs.append(o / jnp.sum(p, axis=-1, keepdims=True))
    o16 = jnp.concatenate(outs, axis=-1).astype(BF16)
    part = o16.shape[0] // XA_OUT_PARTS
    for r0 in range(0, o16.shape[0], part):
        rows = slice(r0, r0 + part)
        y = jnp.dot(o16[rows], wo_ref[...], preferred_element_type=F32)
        o_ref[rows, :] = _layer_norm(ALPHA * hb[rows] + y, g_ref[...], b_ref[...])


def _xattn(h, mem, wk, wv, wq, wo, g, bb, layer):
    b, l, d = h.shape
    m = mem.shape[1]
    tm = min(TOK_TM, l)
    tok = pl.BlockSpec((None, tm, d), lambda i, j: (i, j, 0))
    mems = pl.BlockSpec((None, m, d), lambda i, j: (i, 0, 0))
    return pl.pallas_call(
        _xattn_kernel,
        grid=(b, l // tm),
        in_specs=[tok, mems] + [_layer_spec(c, layer) for c in (wk, wv, wq, wo, g, bb)],
        out_specs=tok,
        out_shape=jax.ShapeDtypeStruct((b, l, d), F32),
        scratch_shapes=[pltpu.VMEM((d, d), BF16), pltpu.VMEM((d, d), BF16),
                        pltpu.VMEM((m, d), BF16), pltpu.VMEM((m, d), BF16)],
        compiler_params=_params(("arbitrary", "arbitrary")),
        name="xattn",
    )(h, mem, wk, wv, wq, wo, g, bb)


def _mlp_kernel(h_ref, w1_ref, w2_ref, g_ref, b_ref, o_ref):
    hb = h_ref[...]
    h16 = hb.astype(BF16)
    acc = ALPHA * hb
    for f in range(D_FF // MLP_FC):
        sl = slice(f * MLP_FC, (f + 1) * MLP_FC)
        t = jnp.maximum(jnp.dot(h16, w1_ref[:, sl], preferred_element_type=F32), 0.0)
        acc = acc + jnp.dot((t * t).astype(BF16), w2_ref[sl, :], preferred_element_type=F32)
    o_ref[...] = _layer_norm(acc, g_ref[...], b_ref[...])


def _mlp(h2d, w1, w2, g, bb, layer):
    t, d = h2d.shape
    tm = min(MLP_TM, t)
    tok = pl.BlockSpec((tm, d), lambda i: (i, 0))
    return pl.pallas_call(
        _mlp_kernel,
        grid=(t // tm,),
        in_specs=[tok] + [_layer_spec(c, layer) for c in (w1, w2, g, bb)],
        out_specs=tok,
        out_shape=jax.ShapeDtypeStruct((t, d), F32),
        compiler_params=_params(("parallel",)),
        name="mlp",
    )(h2d, w1, w2, g, bb)


def _rows(v):
    return v.astype(F32).reshape(v.shape[0], 1, -1)


def _pad_lanes(v3d):
    return jnp.pad(v3d, ((0, 0), (0, 0), (0, LANES - v3d.shape[2])))


def prepare_params(seq_len, w_in, w_out, ssd_conv_w, ssd_conv_b, ssd_dt_bias, ssd_a_log, ssd_d, ssd_norm_w,
                   s5_lam_re, s5_lam_im, s5_log_step, s5_b_re, s5_b_im, s5_c_re, s5_c_im, s5_d, s5_glu_w, s5_glu_b,
                   rg_conv_w, rg_conv_b, rg_wa, rg_ba, rg_wx, rg_bx, rg_lambda, ln1_g, ln1_b,
                   xa_wq, xa_wk, xa_wv, xa_wo, ln2_g, ln2_b, mlp_w1, mlp_w2, ln3_g, ln3_b):
    o_u = SSD_WIDTH + SSD_XBC + SSD_HEADS
    n_tiles = min(S5_TB, seq_len) // (S5_BLK * SUBLANES)
    assert n_tiles <= 2 ** (SUBLANES - 3), "S5 scan-level table holds 3 in-tile + 5 tile levels"
    s5_ops = jax.vmap(functools.partial(_s5_prepare, n_tiles=n_tiles))(
        s5_lam_re, s5_lam_im, s5_log_step, s5_b_re, s5_b_im, s5_c_re, s5_c_im)
    s5 = (w_in[:, :, o_u:o_u + S5_WIDTH].astype(BF16), *s5_ops, _rows(s5_d), s5_glu_w.astype(BF16), _rows(s5_glu_b))
    mix_small = dict(
        cw=ssd_conv_w.astype(F32), cb=_rows(ssd_conv_b), dtb=_pad_lanes(_rows(ssd_dt_bias)),
        aneg=_pad_lanes(_rows(-jnp.exp(ssd_a_log.astype(F32)))),
        dvec=_rows(jnp.repeat(ssd_d.astype(F32), SSD_HEAD_DIM, axis=1)), nw=_rows(ssd_norm_w),
        rcw=rg_conv_w.astype(F32), rcb=_rows(rg_conv_b),
        wa=_block_diag(rg_wa.astype(F32)).astype(BF16), ba=_rows(rg_ba.reshape(rg_ba.shape[0], -1)),
        wx=_block_diag(rg_wx.astype(F32)).astype(BF16), bx=_rows(rg_bx.reshape(rg_bx.shape[0], -1)),
        rlam=_rows(rg_lambda), g=_rows(ln1_g), b=_rows(ln1_b))
    o_dt = SSD_WIDTH + SSD_XBC
    w_in_parts = (w_in[:, :, :o_dt].astype(F32), w_in[:, :, o_dt:].astype(F32))
    mix = (w_in_parts, w_out.astype(F32), tuple(mix_small[f] for f in _MixW._fields if f not in _MIX_W16))
    xa = (xa_wk.astype(F32), xa_wv.astype(F32), xa_wq.astype(F32), xa_wo.astype(F32), _rows(ln2_g), _rows(ln2_b))
    mlp = (mlp_w1.astype(BF16), mlp_w2.astype(BF16), _rows(ln3_g), _rows(ln3_b))
    return dict(s5=s5, mix=mix, xa=xa, mlp=mlp)


def mixer_sublayer(h, prm, layer):
    ys5 = _s5_branch(h, prm["s5"], layer)
    w_in, w_out, consts = prm["mix"]
    return _mixer(h, ys5, w_in, w_out, consts, layer)


def xattn_sublayer(h, mem, prm, layer):
    return _xattn(h, mem, *prm["xa"], layer)


def mlp_sublayer(h, prm, layer):
    b, l, d = h.shape
    return _mlp(h.reshape(b * l, d), *prm["mlp"], layer).reshape(b, l, d)


def kernel(x, mem, w_in, w_out, ssd_conv_w, ssd_conv_b, ssd_dt_bias, ssd_a_log, ssd_d, ssd_norm_w, s5_lam_re, s5_lam_im, s5_log_step, s5_b_re, s5_b_im, s5_c_re, s5_c_im, s5_d, s5_glu_w, s5_glu_b, rg_conv_w, rg_conv_b, rg_wa, rg_ba, rg_wx, rg_bx, rg_lambda, ln1_g, ln1_b, xa_wq, xa_wk, xa_wv, xa_wo, ln2_g, ln2_b, mlp_w1, mlp_w2, ln3_g, ln3_b):
    h = x.astype(F32)
    memf = mem.astype(F32)
    prm = prepare_params(x.shape[1], w_in, w_out, ssd_conv_w, ssd_conv_b, ssd_dt_bias, ssd_a_log, ssd_d, ssd_norm_w,
                         s5_lam_re, s5_lam_im, s5_log_step, s5_b_re, s5_b_im, s5_c_re, s5_c_im, s5_d, s5_glu_w,
                         s5_glu_b, rg_conv_w, rg_conv_b, rg_wa, rg_ba, rg_wx, rg_bx, rg_lambda, ln1_g, ln1_b,
                         xa_wq, xa_wk, xa_wv, xa_wo, ln2_g, ln2_b, mlp_w1, mlp_w2, ln3_g, ln3_b)
    for layer in range(DEPTH):
        h = mixer_sublayer(h, prm, layer)
        h = xattn_sublayer(h, memf, prm, layer)
        h = mlp_sublayer(h, prm, layer)
    return h.astype(x.dtype)
```

```python
import collections
import functools
import math

import jax
import jax.numpy as jnp
from jax import lax
from jax.experimental import pallas as pl
from jax.experimental.pallas import tpu as pltpu

F32 = jnp.float32
BF16 = jnp.bfloat16

D_MODEL = 1024
DEPTH = 2
SSD_WIDTH = 512
SSD_HEAD_DIM = 64
SSD_HEADS = 8
SSD_GROUPS = 2
SSD_HPG = SSD_HEADS // SSD_GROUPS
SSD_GW = SSD_HPG * SSD_HEAD_DIM
SSD_STATE = 128
SSD_XBC = 1024
CONV_K = 4
S5_WIDTH = 256
S5_GROUPS = 16
S5_GROUP_CH = 16
S5_STATE = 64
S5_NSTATE = S5_GROUPS * S5_STATE
S5_BLK = 8
RG_WIDTH = 256
RG_BLOCKS = 4
RG_C = 8.0
XA_HEADS = 4
XA_HEAD_DIM = 256
D_FF = 4096
ALPHA = (2.0 * DEPTH) ** 0.25
LN_EPS = 1e-5

LANES = 128
SUBLANES = 8
MXU_TILE = 256
NEG_BIG = -1e30
VMEM_LIMIT = 56 * 1024 * 1024

SSD_CHUNK = 128
MIX_TQ = 256
S5_TB = 2048
TOK_TM = 1024
XA_OUT_PARTS = 4
MLP_TM = 1024
MLP_FC = 1024


def _dot(a, b):
    return jnp.dot(a.astype(BF16), b.astype(BF16), preferred_element_type=F32)


def _dot_nt(a, b):
    return lax.dot_general(a.astype(BF16), b.astype(BF16), (((1,), (1,)), ((), ())),
                           preferred_element_type=F32)


def _layer_norm(x, g, b):
    mu = jnp.mean(x, axis=-1, keepdims=True)
    xc = x - mu
    var = jnp.mean(xc * xc, axis=-1, keepdims=True)
    return xc * lax.rsqrt(var + LN_EPS) * g + b


def _gelu(x):
    c = math.sqrt(2.0 / math.pi)
    return 0.5 * x * (1.0 + jnp.tanh(c * (x + 0.044715 * (x * x * x))))


def _sigmoid(x):
    return 0.5 + 0.5 * jnp.tanh(0.5 * x)


def _silu(x):
    hx = 0.5 * x
    return hx + hx * jnp.tanh(hx)


def _softplus(x):
    return jnp.maximum(x, 0.0) + jnp.log1p(jnp.exp(-jnp.abs(x)))


def _const_spec(shape):
    nd = len(shape)
    return pl.BlockSpec(shape, lambda *_: (0,) * nd, pipeline_mode=pl.Buffered(1))


def _layer_spec(stacked, layer):
    nd = stacked.ndim - 1
    return pl.BlockSpec((None,) + stacked.shape[1:], lambda *_: (layer,) + (0,) * nd, pipeline_mode=pl.Buffered(1))


def _params(sem):
    return pltpu.CompilerParams(dimension_semantics=sem, vmem_limit_bytes=VMEM_LIMIT)


def _re_lanes(p):
    return slice(2 * p * LANES, (2 * p + 1) * LANES)


def _im_lanes(p):
    return slice((2 * p + 1) * LANES, (2 * p + 2) * LANES)


def _s5_state_group(idx):
    state = (idx // (2 * LANES)) * LANES + idx % LANES
    return state // S5_STATE


def _s5_split(x):
    planes = S5_NSTATE // LANES
    return (jnp.concatenate([x[:, _re_lanes(p)] for p in range(planes)], axis=-1),
            jnp.concatenate([x[:, _im_lanes(p)] for p in range(planes)], axis=-1))


S5_HALF_CH = S5_WIDTH // 2
S5_HALF_LANES = S5_BLK * S5_HALF_CH


def _s5_expand_operators(wc_ref, vc_ref, ct_ref, ms_scr, toep_scr, mo_scr):
    n, w, c = S5_NSTATE, S5_WIDTH, S5_GROUP_CH
    hw, hl = S5_HALF_CH, S5_HALF_LANES
    grp_half = S5_GROUPS // 2
    reps = hw // c
    grp_r = _s5_state_group(lax.broadcasted_iota(jnp.int32, (2 * n, w), 0))
    same_o = grp_r == lax.broadcasted_iota(jnp.int32, (2 * n, w), 1) // c
    mo0 = jnp.where(same_o, ct_ref[...], 0.0)
    kc = jnp.dot(wc_ref[...], mo0, preferred_element_type=F32, precision=lax.Precision.HIGHEST)
    toep_scr[...] = jnp.zeros(toep_scr.shape, toep_scr.dtype)
    ch_grp = lax.broadcasted_iota(jnp.int32, (hw, hl), 0) // c
    same_k = (lax.broadcasted_iota(jnp.int32, (hw, hw), 0) // c) == (lax.broadcasted_iota(jnp.int32, (hw, hw), 1) // c)
    pick_r = lax.broadcasted_iota(jnp.int32, (S5_BLK * c, hl), 0)
    pick_c = lax.broadcasted_iota(jnp.int32, (S5_BLK * c, hl), 1)
    pick = ((pick_r // c == pick_c // hw) & (pick_r % c == pick_c % c)).astype(BF16)
    for hf in range(2):
        lanes = slice(hf * hl, (hf + 1) * hl)
        st_grp = _s5_state_group(lax.broadcasted_iota(jnp.int32, (hw, hl), 1) + hf * hl) - hf * grp_half
        same_s = ch_grp == st_grp
        for t in range(S5_BLK):
            blk = jnp.where(same_s, jnp.tile(wc_ref[t * c:(t + 1) * c, lanes], (reps, 1)), 0.0)
            ms_scr[hf, t * hw:(t + 1) * hw, :] = blk.astype(BF16)
        st_grp = _s5_state_group(lax.broadcasted_iota(jnp.int32, (hl, hl), 0) + hf * hl) - hf * grp_half
        same_r = st_grp == (lax.broadcasted_iota(jnp.int32, (hl, hl), 1) % hw) // c
        spread = jnp.dot(vc_ref[lanes, :].astype(BF16), pick, preferred_element_type=F32)
        mo_scr[hf] = jnp.where(same_r, spread, 0.0).astype(BF16)
        for t_in in range(S5_BLK):
            for t_out in range(t_in, S5_BLK):
                k = S5_BLK - 1 - (t_out - t_in)
                blk = jnp.where(same_k, jnp.tile(kc[k * c:(k + 1) * c, hf * hw:(hf + 1) * hw], (reps, 1)), 0.0)
                toep_scr[hf, t_in * hw:(t_in + 1) * hw, t_out * hw:(t_out + 1) * hw] = blk.astype(BF16)


_S5W = collections.namedtuple("_S5W", "wu wc vc ct lvl pw pwt d gluw glub")
_S5Ops = collections.namedtuple("_S5Ops", "ms toep mo")
_S5Buf = collections.namedtuple("_S5Buf", "u32 u16 hprev16")
_S5Scratch = collections.namedtuple("_S5Scratch", "u_tok hplane cexp carry yg")


def _s5_scan_stage(h_ref, w, ops, scr, out, first, rows, side):
    n, wd = S5_NSTATE, S5_WIDTH
    lane_halves = wd // LANES
    planes = n // LANES
    nt = rows // SUBLANES
    tb = rows * S5_BLK
    for r0 in range(0, tb, tb // 4):
        rs = slice(r0, r0 + tb // 4)
        u_tok = jnp.dot(h_ref[rs, :].astype(BF16), w.wu[...], preferred_element_type=F32)
        for hf in range(lane_halves):
            scr.u_tok[hf, rs, :] = u_tok[:, hf * LANES:(hf + 1) * LANES]
    u = jnp.concatenate([scr.u_tok[hf, pl.ds(k, rows, stride=S5_BLK), :]
                         for hf in range(lane_halves) for k in range(S5_BLK)], axis=-1)
    out.u32[...] = u
    u16 = u.astype(BF16)
    out.u16[...] = u16
    row8 = lax.broadcasted_iota(jnp.int32, (nt, SUBLANES, LANES), 1)
    planes_per_half = planes // lane_halves
    for p in range(planes):
        hf, pl_in_half = divmod(p, planes_per_half)
        s = jnp.dot(u16[:, hf * S5_HALF_LANES:(hf + 1) * S5_HALF_LANES],
                    ops.ms[hf, :, 2 * pl_in_half * LANES:(2 * pl_in_half + 2) * LANES],
                    preferred_element_type=F32)
        if p < planes - 2:
            side(1)
        xr = s[:, :LANES].reshape(nt, SUBLANES, LANES)
        xi = s[:, LANES:].reshape(nt, SUBLANES, LANES)
        for lv, dist in enumerate((1, 2, 4)):
            lr = w.lvl[lv:lv + 1, _re_lanes(p)]
            li = w.lvl[lv:lv + 1, _im_lanes(p)]
            keep = row8 >= dist
            sr = jnp.where(keep, pltpu.roll(xr, dist, 1), 0.0)
            si = jnp.where(keep, pltpu.roll(xi, dist, 1), 0.0)
            xr, xi = xr + lr * sr - li * si, xi + lr * si + li * sr
        scr.hplane[2 * p, SUBLANES:SUBLANES + rows, :] = xr.reshape(rows, LANES)
        scr.hplane[2 * p + 1, SUBLANES:SUBLANES + rows, :] = xi.reshape(rows, LANES)
    side(1)
    last = 2 * SUBLANES - 1
    er = jnp.concatenate([scr.hplane[2 * p, pl.ds(last, nt, stride=SUBLANES), :] for p in range(planes)], axis=-1)
    ei = jnp.concatenate([scr.hplane[2 * p + 1, pl.ds(last, nt, stride=SUBLANES), :] for p in range(planes)], -1)
    tile_id = lax.broadcasted_iota(jnp.int32, (nt, n), 0)
    lv, dist = 3, 1
    while dist < nt:
        lr, li = _s5_split(w.lvl[lv:lv + 1, :])
        keep = tile_id >= dist
        sr = jnp.where(keep, pltpu.roll(er, dist, 0), 0.0)
        si = jnp.where(keep, pltpu.roll(ei, dist, 0), 0.0)
        er, ei = er + lr * sr - li * si, ei + lr * si + li * sr
        lv, dist = lv + 1, dist * 2
    cr, ci = _s5_split(jnp.where(first, 0.0, scr.carry[0:1, :]))
    ptr, pti = _s5_split(w.pwt[...])
    er, ei = er + ptr * cr - pti * ci, ei + ptr * ci + pti * cr
    for p in range(planes):
        scr.carry[0:1, _re_lanes(p)] = er[nt - 1:nt, p * LANES:(p + 1) * LANES]
        scr.carry[0:1, _im_lanes(p)] = ei[nt - 1:nt, p * LANES:(p + 1) * LANES]
    tin_r = jnp.where(tile_id >= 1, pltpu.roll(er, 1, 0), cr)
    tin_i = jnp.where(tile_id >= 1, pltpu.roll(ei, 1, 0), ci)
    side(1)
    for p in range(planes):
        for k in range(SUBLANES):
            scr.cexp[2 * p, pl.ds(k, nt, stride=SUBLANES), :] = tin_r[:, p * LANES:(p + 1) * LANES]
            scr.cexp[2 * p + 1, pl.ds(k, nt, stride=SUBLANES), :] = tin_i[:, p * LANES:(p + 1) * LANES]
    for p in range(planes):
        if p % 2 == 0:
            side(1)
        pr = jnp.tile(w.pw[:, _re_lanes(p)], (nt, 1))
        pi = jnp.tile(w.pw[:, _im_lanes(p)], (nt, 1))
        tr = scr.cexp[2 * p]
        ti = scr.cexp[2 * p + 1]
        hr = scr.hplane[2 * p, SUBLANES:SUBLANES + rows, :] + pr * tr - pi * ti
        hi = scr.hplane[2 * p + 1, SUBLANES:SUBLANES + rows, :] + pr * ti + pi * tr
        scr.hplane[2 * p, SUBLANES:SUBLANES + rows, :] = hr
        scr.hplane[2 * p + 1, SUBLANES:SUBLANES + rows, :] = hi
        scr.hplane[2 * p, SUBLANES - 1:SUBLANES, :] = cr[:, p * LANES:(p + 1) * LANES]
        scr.hplane[2 * p + 1, SUBLANES - 1:SUBLANES, :] = ci[:, p * LANES:(p + 1) * LANES]
    for p in range(2 * planes):
        out.hprev16[:, p * LANES:(p + 1) * LANES] = scr.hplane[p, SUBLANES - 1:SUBLANES - 1 + rows, :].astype(BF16)


def _s5_readout_jobs(src, w, ops, scr, o_ref, rows):
    hw, hl = S5_HALF_CH, S5_HALF_LANES
    lane_halves = S5_WIDTH // LANES
    steps_per_tile = MXU_TILE // hw

    def tile_job(hf, j):
        def run():
            k_rows = (j + 1) * MXU_TILE
            cols = slice(j * MXU_TILE, (j + 1) * MXU_TILE)
            y = jnp.dot(src.u16[:, hf * hl:hf * hl + k_rows], ops.toep[hf, :k_rows, cols],
                        preferred_element_type=F32)
            y = y + jnp.dot(src.hprev16[:, hf * hl:(hf + 1) * hl], ops.mo[hf, :, cols], preferred_element_type=F32)
            d_half = w.d[:, hf * hw:(hf + 1) * hw]
            y = _gelu(y + jnp.concatenate([d_half] * steps_per_tile, axis=1) * src.u32[:, hf * hl + j * MXU_TILE:
                                                                                           hf * hl + (j + 1) * MXU_TILE])
            for i in range(steps_per_tile):
                scr.yg[hf, pl.ds(j * steps_per_tile + i, rows, stride=S5_BLK), :] = y[:, i * hw:(i + 1) * hw]
        return run

    def glu_job():
        yg = jnp.concatenate([scr.yg[hf] for hf in range(lane_halves)], axis=-1)
        glu = _sigmoid(_dot(yg, w.gluw[...]) + w.glub[...])
        o_ref[...] = (yg * glu).astype(o_ref.dtype)

    return [tile_job(hf, j) for hf in range(lane_halves) for j in range(hl // MXU_TILE)] + [glu_job]


def _s5_kernel(h_ref, *rest, rows, nblk):
    nw = len(_S5W._fields)
    w = _S5W(*rest[:nw])
    o_ref = rest[nw]
    bufs = rest[nw + 1:]
    ops = _S5Ops(*bufs[:3])
    nb = len(_S5Buf._fields)
    scr = _S5Scratch(*bufs[3 + nb:])
    s = pl.program_id(0)
    parity = s % 2

    @pl.when(s == 0)
    def _():
        _s5_expand_operators(w.wc, w.vc, w.ct, *ops)
        for ref in bufs[3:]:
            ref[...] = jnp.zeros(ref.shape, ref.dtype)

    first = (s % nblk) == 0

    def step(out, src):
        jobs = _s5_readout_jobs(src, w, ops, scr, o_ref, rows)

        def side(k):
            for _ in range(min(k, len(jobs))):
                jobs.pop(0)()

        _s5_scan_stage(h_ref, w, ops, scr, out, first, rows, side)
        side(len(jobs))

    step(_S5Buf(*(r.at[parity] for r in bufs[3:3 + nb])), _S5Buf(*(r.at[1 - parity] for r in bufs[3:3 + nb])))


def _s5_branch(h, consts, layer):
    b, l, d = h.shape
    tb = min(S5_TB, l)
    rows = tb // S5_BLK
    nblk = l // tb
    total = b * nblk
    n, w = S5_NSTATE, S5_WIDTH
    handoff = [
        pltpu.VMEM((2, rows, S5_BLK * w), F32),
        pltpu.VMEM((2, rows, S5_BLK * w), BF16),
        pltpu.VMEM((2, rows, 2 * n), BF16),
    ]
    out = pl.pallas_call(
        functools.partial(_s5_kernel, rows=rows, nblk=nblk),
        grid=(total + 1,),
        in_specs=[pl.BlockSpec((tb, d), lambda s: (jnp.minimum(s, total - 1), 0))]
                 + [_layer_spec(c, layer) for c in consts],
        out_specs=pl.BlockSpec((tb, w), lambda s: (jnp.maximum(s - 1, 0), 0)),
        out_shape=jax.ShapeDtypeStruct((b * l, w), BF16),
        scratch_shapes=[
            pltpu.VMEM((2, S5_HALF_LANES, S5_HALF_LANES), BF16),
            pltpu.VMEM((2, S5_HALF_LANES, S5_HALF_LANES), BF16),
            pltpu.VMEM((2, S5_HALF_LANES, S5_HALF_LANES), BF16),
        ] + handoff + [
            pltpu.VMEM((w // LANES, tb, LANES), F32),
            pltpu.VMEM((2 * n // LANES, rows + SUBLANES, LANES), F32),
            pltpu.VMEM((2 * n // LANES, rows, LANES), F32),
            pltpu.VMEM((SUBLANES, 2 * n), F32),
            pltpu.VMEM((w // LANES, tb, LANES), F32),
        ],
        compiler_params=_params(("arbitrary",)),
        name="s5_branch",
    )(h.reshape(b * l, d), *consts)
    return out.reshape(b, l, w)


def _s5_prepare(lam_re, lam_im, log_step, b_re, b_im, c_re, c_im, n_tiles):
    g, p, c, t8 = S5_GROUPS, S5_STATE, S5_GROUP_CH, S5_BLK
    lr, li = lam_re.astype(F32), lam_im.astype(F32)
    dt = jnp.exp(log_step.astype(F32))[:, None]
    ar, ai = lr * dt, li * dt

    def lam_pow(k):
        k = k.astype(F32)[..., None, None]
        mag = jnp.exp(k * ar)
        return mag * jnp.cos(k * ai), mag * jnp.sin(k * ai)

    br1, bi1 = lam_pow(jnp.ones((), jnp.int32))
    xr_, xi_ = br1 - 1.0, bi1
    den = lr * lr + li * li
    cr, ci = (xr_ * lr + xi_ * li) / den, (xi_ * lr - xr_ * li) / den
    bre, bim = b_re.astype(F32), b_im.astype(F32)
    bbr = cr[..., None] * bre - ci[..., None] * bim
    bbi = cr[..., None] * bim + ci[..., None] * bre
    steps = jnp.arange(t8)
    pr, pi = lam_pow(t8 - 1 - steps)
    wsr = pr[..., None] * bbr[None] - pi[..., None] * bbi[None]
    wsi = pr[..., None] * bbi[None] + pi[..., None] * bbr[None]

    def interleave(re, im, axis):
        shape = list(re.shape)
        shape[axis:axis + 1] = [re.shape[axis] // LANES, 1, LANES]
        both = jnp.concatenate([re.reshape(shape), im.reshape(shape)], axis=axis + 1)
        shape[axis:axis + 3] = [2 * re.shape[axis]]
        return both.reshape(shape)

    wc = interleave(*[part.transpose(0, 3, 1, 2).reshape(t8 * c, g * p) for part in (wsr, wsi)], axis=1)
    cre, cim = c_re.astype(F32), c_im.astype(F32)
    pr, pi = lam_pow(steps + 1)
    vor = cre[None] * pr[:, :, None, :] - cim[None] * pi[:, :, None, :]
    voi = cre[None] * pi[:, :, None, :] + cim[None] * pr[:, :, None, :]
    vc = interleave(*[part.transpose(1, 3, 0, 2).reshape(g * p, t8 * c) for part in (vor, -voi)], axis=0)
    ct = interleave(*[jnp.tile(part.transpose(0, 2, 1).reshape(g * p, c), (1, g)) for part in (cre, -cim)], axis=0)

    def re_im(pair):
        return interleave(pair[0].reshape(-1, g * p), pair[1].reshape(-1, g * p), axis=1)

    lvl = re_im(lam_pow(t8 * 2 ** jnp.arange(SUBLANES)))
    pw = re_im(lam_pow(t8 * (steps + 1)))
    pwt = re_im(lam_pow(t8 * SUBLANES * (jnp.arange(n_tiles) + 1)))
    return wc, vc, ct, lvl, pw, pwt


def _expand_heads(cols, grp, head_lane):
    base = SSD_HPG * grp
    out = jnp.broadcast_to(cols[:, base:base + 1], (cols.shape[0], SSD_GW))
    for hh in range(1, SSD_HPG):
        out = jnp.where(head_lane == hh, cols[:, base + hh:base + hh + 1], out)
    return out


def _causal_conv(pad_ref, w_ref, b_ref, rows, cols):
    acc = b_ref[:, cols] + w_ref[CONV_K - 1:CONV_K, cols] * pad_ref[SUBLANES:SUBLANES + rows, cols]
    for k in range(CONV_K - 1):
        off = SUBLANES - (CONV_K - 1) + k
        acc = acc + w_ref[k:k + 1, cols] * pad_ref[off:off + rows, cols]
    return acc


_ProjBuf = collections.namedtuple("_ProjBuf", "z xpad dt rpad gr")
_MixW = collections.namedtuple(
    "_MixW", "wz wxbc wdt wxr wgr cw cb dtb aneg dvec nw rcw rcb wa ba wx bx rlam wout g b")
_MixScratch = collections.namedtuple("_MixScratch", "state ascr bscr cexp rcar")


def _col_tiles(width):
    return [slice(c, min(c + MXU_TILE, width)) for c in range(0, width, MXU_TILE)]


def _mixer_output_jobs(h_ref, ycat, w, o_ref):
    def tile_job(cols):
        def run():
            y = jnp.dot(ycat[...], w.wout[:, cols], preferred_element_type=F32)
            o_ref[:, cols] = ALPHA * h_ref[:, cols] + y
        return run

    def norm_job():
        o_ref[...] = _layer_norm(o_ref[...], w.g[...], w.b[...])

    return [tile_job(cols) for cols in _col_tiles(D_MODEL)] + [norm_job]


def _mixer_project_jobs(h_ref, h16, w, nxt, cur, first, tq):
    def cast_job():
        h16[...] = h_ref[...].astype(BF16)

    def tile_job(dst, w_ref, cols, row0):
        def run():
            dst[row0:row0 + tq, cols] = jnp.dot(h16[...], w_ref[:, cols], preferred_element_type=F32)
        return run

    def tail_job(pad_n, pad_c):
        def run():
            pad_n[0:SUBLANES, :] = jnp.where(first, 0.0, pad_c[tq:tq + SUBLANES, :])
        return run

    jobs = [cast_job]
    for dst, w_ref, row0 in ((nxt.z, w.wz, 0), (nxt.xpad, w.wxbc, SUBLANES), (nxt.dt, w.wdt, 0),
                             (nxt.rpad, w.wxr, SUBLANES), (nxt.gr, w.wgr, 0)):
        jobs += [tile_job(dst, w_ref, cols, row0) for cols in _col_tiles(dst.shape[1])]
    jobs += [tail_job(nxt.xpad, cur.xpad), tail_job(nxt.rpad, cur.rpad)]
    return jobs


def _mixer_compute(cur, ys5_ref, w, scr, ycat, first, tq, chunk, side):
    cw_ref, cb_ref, dtb_ref, aneg_ref, dvec_ref, nw_ref = w.cw, w.cb, w.dtb, w.aneg, w.dvec, w.nw
    state = scr.state
    z = cur.z
    slabs = []
    for cols in _col_tiles(SSD_XBC):
        side(1)
        slabs.append(_silu(_causal_conv(cur.xpad, cw_ref, cb_ref, tq, cols)))
    xact = jnp.concatenate(slabs, axis=-1)
    side(2)
    dt_all = _softplus(cur.dt[...] + dtb_ref[...])
    q = chunk
    tri = (lax.broadcasted_iota(jnp.int32, (q, q), 0) >= lax.broadcasted_iota(jnp.int32, (q, q), 1))
    row_id = lax.broadcasted_iota(jnp.int32, (q, LANES), 0)
    head_lane = lax.broadcasted_iota(jnp.int32, (1, SSD_GW), 1) // SSD_HEAD_DIM
    for ci in range(tq // q):
        r0 = ci * q
        dt = dt_all[r0:r0 + q, :]
        cs = dt * aneg_ref[...]
        dist = 1
        while dist < q:
            cs = cs + jnp.where(row_id >= dist, pltpu.roll(cs, dist, 0), 0.0)
            dist *= 2
        cs_t = cs.T
        cs_last = cs[q - 1:q, :]
        chunk_decay = jnp.exp(cs_last)
        dec_t = jnp.exp(cs_t[:, q - 1:q] - cs_t)
        ecs = jnp.exp(cs)
        zc = z[r0:r0 + q, :]
        halves = []
        for grp in range(SSD_GROUPS):
            side(2)
            lo = grp * SSD_GW
            xs = xact[r0:r0 + q, lo:lo + SSD_GW]
            bm = xact[r0:r0 + q, SSD_WIDTH + grp * SSD_STATE:SSD_WIDTH + (grp + 1) * SSD_STATE]
            cm = xact[r0:r0 + q, SSD_WIDTH + (SSD_GROUPS + grp) * SSD_STATE:
                      SSD_WIDTH + (SSD_GROUPS + grp + 1) * SSD_STATE]
            cm16 = cm.astype(BF16)
            xdt16 = (xs * _expand_heads(dt, grp, head_lane)).astype(BF16)
            cb = _dot_nt(cm16, bm)
            bm_t = bm.T
            s_prev = state[grp]
            if ci == 0:
                s_prev = jnp.where(first, 0.0, s_prev)
            y_off = jnp.dot(cm16, s_prev.astype(BF16), preferred_element_type=F32)
            intra, to_state, xdt_heads = [], [], []
            for hh in range(SSD_HPG):
                hd = grp * SSD_HPG + hh
                seg = cs[:, hd:hd + 1] - cs_t[hd:hd + 1, :]
                lmat = jnp.exp(jnp.where(tri, seg, NEG_BIG))
                intra.append((cb * lmat).astype(BF16))
                to_state.append((bm_t * dec_t[hd:hd + 1, :]).astype(BF16))
                xdt_heads.append(jnp.where(head_lane == hh, xdt16, jnp.zeros_like(xdt16)))
            lhs = jnp.concatenate([jnp.concatenate(intra, axis=1), jnp.concatenate(to_state, axis=1)], axis=0)
            both = jnp.dot(lhs, jnp.concatenate(xdt_heads, axis=0), preferred_element_type=F32)
            y_diag = both[:q]
            st_new = both[q:]
            state[grp] = s_prev * _expand_heads(chunk_decay, grp, head_lane) + st_new
            halves.append(y_diag + y_off * _expand_heads(ecs, grp, head_lane) + xs * dvec_ref[:, lo:lo + SSD_GW])
        y = jnp.concatenate(halves, axis=-1) * _silu(zc)
        y = y * lax.rsqrt(jnp.mean(y * y, axis=-1, keepdims=True) + LN_EPS) * nw_ref[...]
        ycat[r0:r0 + q, 0:SSD_WIDTH] = y.astype(BF16)

    ycat[:, SSD_WIDTH:SSD_WIDTH + S5_WIDTH] = ys5_ref[...]
    _mixer_rglru(cur, w, scr, ycat, first, tq, side)


def _mixer_rglru(cur, w, scr, ycat, first, tq, side):
    side(1)
    xc = _causal_conv(cur.rpad, w.rcw, w.rcb, tq, slice(0, RG_WIDTH))
    xc16 = xc.astype(BF16)
    rgate = _sigmoid(jnp.dot(xc16, w.wa[...], preferred_element_type=F32) + w.ba[...])
    igate = _sigmoid(jnp.dot(xc16, w.wx[...], preferred_element_type=F32) + w.bx[...])
    log_a = (-RG_C) * rgate * _softplus(-w.rlam[...])
    a = jnp.exp(log_a)
    bt = jnp.sqrt(1.0 - jnp.exp(2.0 * log_a)) * (igate * xc)
    row8 = lax.broadcasted_iota(jnp.int32, (tq, RG_WIDTH), 0) & (SUBLANES - 1)
    side(1)
    for dist in (1, 2, 4):
        keep = row8 >= dist
        a_sh = jnp.where(keep, pltpu.roll(a, dist, 0), 1.0)
        b_sh = jnp.where(keep, pltpu.roll(bt, dist, 0), 0.0)
        bt = bt + a * b_sh
        a = a * a_sh
    nt = tq // SUBLANES
    lane_halves = RG_WIDTH // LANES
    for hf in range(lane_halves):
        scr.ascr[hf] = a[:, hf * LANES:(hf + 1) * LANES]
        scr.bscr[hf] = bt[:, hf * LANES:(hf + 1) * LANES]
    ae = jnp.concatenate([scr.ascr[hf, pl.ds(SUBLANES - 1, nt, stride=SUBLANES), :] for hf in range(lane_halves)], -1)
    be = jnp.concatenate([scr.bscr[hf, pl.ds(SUBLANES - 1, nt, stride=SUBLANES), :] for hf in range(lane_halves)], -1)
    tile_id = lax.broadcasted_iota(jnp.int32, (nt, RG_WIDTH), 0)
    dist = 1
    while dist < nt:
        keep = tile_id >= dist
        a_sh = jnp.where(keep, pltpu.roll(ae, dist, 0), 1.0)
        b_sh = jnp.where(keep, pltpu.roll(be, dist, 0), 0.0)
        be = be + ae * b_sh
        ae = ae * a_sh
        dist *= 2
    carry_in = jnp.where(first, 0.0, scr.rcar[0:1, :])
    tile_end = be + ae * carry_in
    scr.rcar[0:1, :] = tile_end[nt - 1:nt, :]
    tile_in = jnp.where(tile_id >= 1, pltpu.roll(tile_end, 1, 0), carry_in)
    for hf in range(lane_halves):
        for k in range(SUBLANES):
            scr.cexp[hf, pl.ds(k, nt, stride=SUBLANES), :] = tile_in[:, hf * LANES:(hf + 1) * LANES]
    hscan = bt + a * jnp.concatenate([scr.cexp[hf] for hf in range(lane_halves)], axis=-1)
    ycat[:, SSD_WIDTH + S5_WIDTH:] = (hscan * _gelu(cur.gr[...])).astype(BF16)


_MIX_W16 = ("wz", "wxbc", "wdt", "wxr", "wgr", "wout")


def _mixer_load_weights(w_in_ref, w_out_ref, w):
    o_dt = SSD_WIDTH + SSD_XBC
    o_xr = o_dt + SSD_HEADS + S5_WIDTH
    o_gr = o_xr + RG_WIDTH
    w.wz[...] = w_in_ref[:, 0:SSD_WIDTH].astype(BF16)
    w.wxbc[...] = w_in_ref[:, SSD_WIDTH:o_dt].astype(BF16)
    head_cols = lax.broadcasted_iota(jnp.int32, (1, LANES), 1) < SSD_HEADS
    w.wdt[...] = jnp.where(head_cols, w_in_ref[:, o_dt:o_dt + LANES], 0.0).astype(BF16)
    w.wxr[...] = w_in_ref[:, o_xr:o_gr].astype(BF16)
    w.wgr[...] = w_in_ref[:, o_gr:o_gr + RG_WIDTH].astype(BF16)
    w.wout[...] = w_out_ref[...].astype(BF16)


def _mixer_kernel(hp_ref, ho_ref, ys5_ref, w_in_ref, w_out_ref, *rest, tq, chunk, nblk, last):
    small = [f for f in _MixW._fields if f not in _MIX_W16]
    o_ref = rest[len(small)]
    bufs = rest[len(small) + 1:]
    w16 = bufs[len(bufs) - len(_MIX_W16):]
    bufs = bufs[:len(bufs) - len(_MIX_W16)]
    w = _MixW(**dict(zip(small, rest[:len(small)])), **dict(zip(_MIX_W16, w16)))
    npb = len(_ProjBuf._fields)
    sets = (_ProjBuf(*bufs[:npb]), _ProjBuf(*bufs[npb:2 * npb]))
    ycats = bufs[2 * npb:2 * npb + 2]
    scr = _MixScratch(*bufs[2 * npb + 2:2 * npb + 2 + len(_MixScratch._fields)])
    h16 = bufs[-1]
    s = pl.program_id(0)

    @pl.when(s == 0)
    def _():
        _mixer_load_weights(w_in_ref, w_out_ref, w)
        for ref in bufs:
            ref[...] = jnp.zeros(ref.shape, ref.dtype)

    first_p = (s % nblk) == 0
    first_c = ((s + nblk - 1) % nblk) == 0

    def step(parity, project=True, mix=True, output=True):
        nxt, cur = sets[parity], sets[1 - parity]
        ycat_c, ycat_o = ycats[1 - parity], ycats[parity]
        jobs = _mixer_output_jobs(ho_ref, ycat_o, w, o_ref) if output else []
        if project:
            jobs += _mixer_project_jobs(hp_ref, h16, w, nxt, cur, first_p, tq)

        def side(n):
            for _ in range(min(n, len(jobs))):
                jobs.pop(0)()

        if mix:
            _mixer_compute(cur, ys5_ref, w, scr, ycat_c, first_c, tq, chunk, side)
        side(len(jobs))

    pl.when(s == 0)(lambda: step(0, mix=False, output=False))
    pl.when(s == 1)(lambda: step(1, output=False))
    steady = (s >= 2) & (s <= last - 2)
    pl.when(steady & (s % 2 == 0))(lambda: step(0))
    pl.when(steady & (s % 2 == 1))(lambda: step(1))
    pl.when(s == last - 1)(lambda: step((last - 1) % 2, project=False))
    pl.when(s == last)(lambda: step(last % 2, project=False, mix=False))


def _mixer(h, ys5, w_in, w_out, consts, layer):
    b, l, d = h.shape
    tq = min(MIX_TQ, l)
    chunk = min(SSD_CHUNK, tq)
    nblk = l // tq
    total = b * nblk
    h2d = h.reshape(b * l, d)
    proj_buf = [
        pltpu.VMEM((tq, SSD_WIDTH), F32),
        pltpu.VMEM((tq + SUBLANES, SSD_XBC), F32),
        pltpu.VMEM((tq, LANES), F32),
        pltpu.VMEM((tq + 2 * SUBLANES, RG_WIDTH), F32),
        pltpu.VMEM((tq, RG_WIDTH), F32),
    ]
    rg_plane = pltpu.VMEM((RG_WIDTH // LANES, tq, LANES), F32)
    out = pl.pallas_call(
        functools.partial(_mixer_kernel, tq=tq, chunk=chunk, nblk=nblk, last=total + 1),
        grid=(total + 2,),
        in_specs=[pl.BlockSpec((tq, d), lambda s: (jnp.minimum(s, total - 1), 0)),
                  pl.BlockSpec((tq, d), lambda s: (jnp.maximum(s - 2, 0), 0)),
                  pl.BlockSpec((tq, S5_WIDTH), lambda s: (jnp.clip(s - 1, 0, total - 1), 0))]
                 + [_layer_spec(c, layer) for c in (w_in, w_out) + tuple(consts)],
        out_specs=pl.BlockSpec((tq, d), lambda s: (jnp.maximum(s - 2, 0), 0)),
        out_shape=jax.ShapeDtypeStruct((b * l, d), F32),
        scratch_shapes=proj_buf + proj_buf + [
            pltpu.VMEM((tq, D_MODEL), BF16),
            pltpu.VMEM((tq, D_MODEL), BF16),
            pltpu.VMEM((SSD_GROUPS, SSD_STATE, SSD_GW), F32),
            rg_plane, rg_plane, rg_plane,
            pltpu.VMEM((2 * SUBLANES, RG_WIDTH), F32),
            pltpu.VMEM((tq, d), BF16),
            pltpu.VMEM((d, SSD_WIDTH), BF16),
            pltpu.VMEM((d, SSD_XBC), BF16),
            pltpu.VMEM((d, LANES), BF16),
            pltpu.VMEM((d, RG_WIDTH), BF16),
            pltpu.VMEM((d, RG_WIDTH), BF16),
            pltpu.VMEM((d, d), BF16),
        ],
        compiler_params=_params(("arbitrary",)),
        name="mixer",
    )(h2d, h2d, ys5.reshape(b * l, S5_WIDTH), w_in, w_out, *consts)
    return out.reshape(b, l, d)


def _block_diag(w):
    nl, nb, bi, bo = w.shape
    return jnp.einsum('lhij,hk->lhikj', w, jnp.eye(nb, dtype=w.dtype)).reshape(nl, nb * bi, nb * bo)


def _xattn_kernel(h_ref, m_ref, wk32_ref, wv32_ref, wq32_ref, wo32_ref, g_ref, b_ref, o_ref,
                  wq_ref, wo_ref, k_ref, v_ref):
    @pl.when((pl.program_id(0) == 0) & (pl.program_id(1) == 0))
    def _():
        wq_ref[...] = wq32_ref[...].astype(BF16)
        wo_ref[...] = wo32_ref[...].astype(BF16)

    @pl.when(pl.program_id(1) == 0)
    def _():
        m16 = m_ref[...].astype(BF16)
        k_ref[...] = jnp.dot(m16, wk32_ref[...].astype(BF16), preferred_element_type=F32).astype(BF16)
        v_ref[...] = jnp.dot(m16, wv32_ref[...].astype(BF16), preferred_element_type=F32).astype(BF16)

    hb = h_ref[...]
    qall = jnp.dot(hb.astype(BF16), wq_ref[...], preferred_element_type=F32) * (1.0 / math.sqrt(XA_HEAD_DIM))
    outs = []
    for hd in range(XA_HEADS):
        sl = slice(hd * XA_HEAD_DIM, (hd + 1) * XA_HEAD_DIM)
        s = _dot_nt(qall[:, sl], k_ref[:, sl])
        p = jnp.exp(s - jnp.max(s, axis=-1, keepdims=True))
        o = jnp.dot(p.astype(BF16), v_ref[:, sl], preferred_element_type=F32)
        outs.append(o / jnp.sum(p, axis=-1, keepdims=True))
    o16 = jnp.concatenate(outs, axis=-1).astype(BF16)
    part = o16.shape[0] // XA_OUT_PARTS
    for r0 in range(0, o16.shape[0], part):
        rows = slice(r0, r0 + part)
        y = jnp.dot(o16[rows], wo_ref[...], preferred_element_type=F32)
        o_ref[rows, :] = _layer_norm(ALPHA * hb[rows] + y, g_ref[...], b_ref[...])


def _xattn(h, mem, wk, wv, wq, wo, g, bb, layer):
    b, l, d = h.shape
    m = mem.shape[1]
    tm = min(TOK_TM, l)
    tok = pl.BlockSpec((None, tm, d), lambda i, j: (i, j, 0))
    mems = pl.BlockSpec((None, m, d), lambda i, j: (i, 0, 0))
    return pl.pallas_call(
        _xattn_kernel,
        grid=(b, l // tm),
        in_specs=[tok, mems] + [_layer_spec(c, layer) for c in (wk, wv, wq, wo, g, bb)],
        out_specs=tok,
        out_shape=jax.ShapeDtypeStruct((b, l, d), F32),
        scratch_shapes=[pltpu.VMEM((d, d), BF16), pltpu.VMEM((d, d), BF16),
                        pltpu.VMEM((m, d), BF16), pltpu.VMEM((m, d), BF16)],
        compiler_params=_params(("arbitrary", "arbitrary")),
        name="xattn",
    )(h, mem, wk, wv, wq, wo, g, bb)


def _mlp_kernel(h_ref, w1_ref, w2_ref, g_ref, b_ref, o_ref):
    hb = h_ref[...]
    h16 = hb.astype(BF16)
    acc = ALPHA * hb
    for f in range(D_FF // MLP_FC):
        sl = slice(f * MLP_FC, (f + 1) * MLP_FC)
        t = jnp.maximum(jnp.dot(h16, w1_ref[:, sl], preferred_element_type=F32), 0.0)
        acc = acc + jnp.dot((t * t).astype(BF16), w2_ref[sl, :], preferred_element_type=F32)
    o_ref[...] = _layer_norm(acc, g_ref[...], b_ref[...])


def _mlp(h2d, w1, w2, g, bb, layer):
    t, d = h2d.shape
    tm = min(MLP_TM, t)
    tok = pl.BlockSpec((tm, d), lambda i: (i, 0))
    return pl.pallas_call(
        _mlp_kernel,
        grid=(t // tm,),
        in_specs=[tok] + [_layer_spec(c, layer) for c in (w1, w2, g, bb)],
        out_specs=tok,
        out_shape=jax.ShapeDtypeStruct((t, d), F32),
        compiler_params=_params(("parallel",)),
        name="mlp",
    )(h2d, w1, w2, g, bb)


def _rows(v):
    return v.astype(F32).reshape(v.shape[0], 1, -1)


def _pad_lanes(v3d):
    return jnp.pad(v3d, ((0, 0), (0, 0), (0, LANES - v3d.shape[2])))


def prepare_params(seq_len, w_in, w_out, ssd_conv_w, ssd_conv_b, ssd_dt_bias, ssd_a_log, ssd_d, ssd_norm_w,
                   s5_lam_re, s5_lam_im, s5_log_step, s5_b_re, s5_b_im, s5_c_re, s5_c_im, s5_d, s5_glu_w, s5_glu_b,
                   rg_conv_w, rg_conv_b, rg_wa, rg_ba, rg_wx, rg_bx, rg_lambda, ln1_g, ln1_b,
                   xa_wq, xa_wk, xa_wv, xa_wo, ln2_g, ln2_b, mlp_w1, mlp_w2, ln3_g, ln3_b):
    o_u = SSD_WIDTH + SSD_XBC + SSD_HEADS
    n_tiles = min(S5_TB, seq_len) // (S5_BLK * SUBLANES)
    assert n_tiles <= 2 ** (SUBLANES - 3), "S5 scan-level table holds 3 in-tile + 5 tile levels"
    s5_ops = jax.vmap(functools.partial(_s5_prepare, n_tiles=n_tiles))(
        s5_lam_re, s5_lam_im, s5_log_step, s5_b_re, s5_b_im, s5_c_re, s5_c_im)
    s5 = (w_in[:, :, o_u:o_u + S5_WIDTH].astype(BF16), *s5_ops, _rows(s5_d), s5_glu_w.astype(BF16), _rows(s5_glu_b))
    mix_small = dict(
        cw=ssd_conv_w.astype(F32), cb=_rows(ssd_conv_b), dtb=_pad_lanes(_rows(ssd_dt_bias)),
        aneg=_pad_lanes(_rows(-jnp.exp(ssd_a_log.astype(F32)))),
        dvec=_rows(jnp.repeat(ssd_d.astype(F32), SSD_HEAD_DIM, axis=1)), nw=_rows(ssd_norm_w),
        rcw=rg_conv_w.astype(F32), rcb=_rows(rg_conv_b),
        wa=_block_diag(rg_wa.astype(F32)).astype(BF16), ba=_rows(rg_ba.reshape(rg_ba.shape[0], -1)),
        wx=_block_diag(rg_wx.astype(F32)).astype(BF16), bx=_rows(rg_bx.reshape(rg_bx.shape[0], -1)),
        rlam=_rows(rg_lambda), g=_rows(ln1_g), b=_rows(ln1_b))
    mix = (w_in.astype(F32), w_out.astype(F32), tuple(mix_small[f] for f in _MixW._fields if f not in _MIX_W16))
    xa = (xa_wk.astype(F32), xa_wv.astype(F32), xa_wq.astype(F32), xa_wo.astype(F32), _rows(ln2_g), _rows(ln2_b))
    mlp = (mlp_w1.astype(BF16), mlp_w2.astype(BF16), _rows(ln3_g), _rows(ln3_b))
    return dict(s5=s5, mix=mix, xa=xa, mlp=mlp)


def mixer_sublayer(h, prm, layer):
    ys5 = _s5_branch(h, prm["s5"], layer)
    w_in, w_out, consts = prm["mix"]
    return _mixer(h, ys5, w_in, w_out, consts, layer)


def xattn_sublayer(h, mem, prm, layer):
    return _xattn(h, mem, *prm["xa"], layer)


def mlp_sublayer(h, prm, layer):
    b, l, d = h.shape
    return _mlp(h.reshape(b * l, d), *prm["mlp"], layer).reshape(b, l, d)


def kernel(x, mem, w_in, w_out, ssd_conv_w, ssd_conv_b, ssd_dt_bias, ssd_a_log, ssd_d, ssd_norm_w, s5_lam_re, s5_lam_im, s5_log_step, s5_b_re, s5_b_im, s5_c_re, s5_c_im, s5_d, s5_glu_w, s5_glu_b, rg_conv_w, rg_conv_b, rg_wa, rg_ba, rg_wx, rg_bx, rg_lambda, ln1_g, ln1_b, xa_wq, xa_wk, xa_wv, xa_wo, ln2_g, ln2_b, mlp_w1, mlp_w2, ln3_g, ln3_b):
    h = x.astype(F32)
    memf = mem.astype(F32)
    prm = prepare_params(x.shape[1], w_in, w_out, ssd_conv_w, ssd_conv_b, ssd_dt_bias, ssd_a_log, ssd_d, ssd_norm_w,
                         s5_lam_re, s5_lam_im, s5_log_step, s5_b_re, s5_b_im, s5_c_re, s5_c_im, s5_d, s5_glu_w,
                         s5_glu_b, rg_conv_w, rg_conv_b, rg_wa, rg_ba, rg_wx, rg_bx, rg_lambda, ln1_g, ln1_b,
                         xa_wq, xa_wk, xa_wv, xa_wo, ln2_g, ln2_b, mlp_w1, mlp_w2, ln3_g, ln3_b)
    for layer in range(DEPTH):
        h = mixer_sublayer(h, prm, layer)
        h = xattn_sublayer(h, memf, prm, layer)
        h = mlp_sublayer(h, prm, layer)
    return h.astype(x.dtype)
```
